```python
import math
import jax
import jax.numpy as jnp
from jax import lax
import numpy as np

D_MODEL = 2048
BATCH = 8
SEQ = 2048
DEPTH = 2

GRID_W = 64
CTX_LEN = 256
EPS = 1e-6

A_HEADS = 8
A_DK = 128
A_DV = 128
A_WIDTH = A_HEADS * A_DK
HGRN_CHUNK = 32

B_HEADS = 8
B_DH = 64
B_DV = 2 * B_DH
B_QK_WIDTH = B_HEADS * 2 * B_DH
B_WIDTH = B_HEADS * B_DV
Q_BLOCK = 128
ROPE_BASE = 10000.0

EVEN_SPLITS = (A_WIDTH, A_WIDTH, A_WIDTH, A_WIDTH, A_WIDTH, B_QK_WIDTH, B_QK_WIDTH, B_WIDTH)
EVEN_IN = 5 * A_WIDTH + 2 * B_QK_WIDTH + B_WIDTH
MIX_OUT = A_WIDTH + B_WIDTH

RG_WIDTH = D_MODEL
RG_HEADS = 16
RG_BLOCK = RG_WIDTH // RG_HEADS
CONV_W = 4
RG_C = 8.0

N_GROUPS = 4
EXPERTS_PER_GROUP = 4
N_EXPERTS = N_GROUPS * EXPERTS_PER_GROUP
TOPK_IN_GROUP = 2
D_EXPERT = 512

kernel_name = 'hybrid_hgrn2_diffattn_rglru_hmoe_dit'


def rms_norm(x, gain):
    xf = x.astype(jnp.float32)
    y = xf * lax.rsqrt(jnp.mean(xf * xf, axis=-1, keepdims=True) + EPS)
    return (y * gain.astype(jnp.float32)).astype(x.dtype)


def modulate(h, shift, scale):
    return h * (1.0 + scale) + shift


def ada(cvec, w, b):
    return jax.nn.silu(cvec) @ w + b


def to_heads(t, n_heads):
    b, n, w = t.shape
    return t.reshape(b, n, n_heads, w // n_heads).transpose(0, 2, 1, 3)


def merge_heads(t):
    b, h, n, d = t.shape
    return t.transpose(0, 2, 1, 3).reshape(b, n, h * d)


def axial_rope(n):
    n_rows = n // GRID_W
    row = jnp.repeat(jnp.arange(n_rows), GRID_W).astype(jnp.float32)
    col = jnp.tile(jnp.arange(GRID_W), n_rows).astype(jnp.float32)
    pairs = B_DH // 4
    inv = ROPE_BASE ** (-jnp.arange(pairs, dtype=jnp.float32) / pairs)
    ang = jnp.concatenate([row[:, None] * inv, col[:, None] * inv], axis=-1)
    return jnp.cos(ang)[:, None, None, :], jnp.sin(ang)[:, None, None, :]


def apply_rope(t, cos, sin):
    tp = t.reshape(t.shape[:-1] + (-1, 2))
    t0, t1 = tp[..., 0], tp[..., 1]
    out = jnp.stack([t0 * cos - t1 * sin, t0 * sin + t1 * cos], axis=-1)
    return out.reshape(t.shape).astype(t.dtype)


def diff_qk(t, gain, rope):
    b, n, _ = t.shape
    t = rms_norm(t.reshape(b, n, B_HEADS, 2, B_DH), gain)
    if rope is not None:
        t = apply_rope(t, rope[0], rope[1])
    return t.transpose(0, 2, 3, 1, 4)


def diff_attention(q, k, v, lam):
    s = jnp.einsum('bhmqd,bhmkd->bhmqk', q, k).astype(jnp.float32) * (B_DH ** -0.5)
    p = jax.nn.softmax(s, axis=-1)
    w = p[:, :, 0] - lam * p[:, :, 1]
    return jnp.einsum('bhqk,bhkv->bhqv', w.astype(v.dtype), v)


def diff_attention_blocked(q, k, v, lam):
    b, h, _, n, dh = q.shape
    nb = n // Q_BLOCK
    qb = jnp.moveaxis(q.reshape(b, h, 2, nb, Q_BLOCK, dh), 3, 0)
    ob = lax.map(lambda blk: diff_attention(blk, k, v, lam), qb)
    return jnp.moveaxis(ob, 0, 2).reshape(b, h, n, -1)


def hgrn2_scan(q, k, v, log_f, s0):
    b, h, n, dk = q.shape
    dv = v.shape[-1]
    nc = n // HGRN_CHUNK

    def to_chunks(t):
        return jnp.moveaxis(t.reshape(b, h, nc, HGRN_CHUNK, t.shape[-1]), 2, 0)

    mask = jnp.tril(jnp.ones((HGRN_CHUNK, HGRN_CHUNK), dtype=bool))

    def step(s, xs):
        qc, kc, vc, lf = xs
        cum = jnp.cumsum(lf, axis=2)
        cum_last = cum[:, :, -1:, :]
        q_t = qc * jnp.exp(cum)
        k_t = kc * jnp.exp(-cum)
        att = jnp.where(mask, jnp.einsum('bhtk,bhsk->bhts', q_t, k_t), 0.0)
        o = jnp.einsum('bhts,bhsv->bhtv', att, vc) + jnp.einsum('bhtk,bhkv->bhtv', q_t, s)
        s_new = jnp.exp(cum_last[:, :, 0, :])[..., None] * s + jnp.einsum('bhsk,bhsv->bhkv', kc * jnp.exp(cum_last - cum), vc)
        return s_new, o

    s_fin, o = lax.scan(step, s0, (to_chunks(q), to_chunks(k), to_chunks(v), to_chunks(log_f)))
    return jnp.moveaxis(o, 0, 2).reshape(b, h, n, dv), s_fin


def hgrn2_dir(q, v, f_logit, lb, s0, reverse):
    f = lb + (1.0 - lb) * jax.nn.sigmoid(f_logit)
    k = to_heads(1.0 - f, A_HEADS)
    log_f = to_heads(jnp.log(f), A_HEADS)
    if reverse:
        q, k, v, log_f = (jnp.flip(t, axis=2) for t in (q, k, v, log_f))
    o, s = hgrn2_scan(q, k, v, log_f, s0)
    if reverse:
        o = jnp.flip(o, axis=2)
    return o, s


def linear_scan(a, bx, h0):
    bx = bx.at[:, 0].add(a[:, 0] * h0)

    def comb(e1, e2):
        a1, b1 = e1
        a2, b2 = e2
        return a1 * a2, a2 * b1 + b2

    _, h = lax.associative_scan(comb, (a, bx), axis=1)
    return h, h[:, -1]


def rglru_dir(u, gate_w, gate_b, lam, h0, reverse):
    b, n, _ = u.shape
    uf = u.astype(jnp.float32)
    gates = jnp.einsum('bthi,ghij->gbthj', uf.reshape(b, n, RG_HEADS, RG_BLOCK), gate_w.astype(jnp.float32))
    gates = gates.reshape(2, b, n, RG_WIDTH) + gate_b[:, None, None, :]
    r = jax.nn.sigmoid(gates[0])
    i = jax.nn.sigmoid(gates[1])
    log_a = -RG_C * jax.nn.softplus(-lam.astype(jnp.float32)) * r
    a = jnp.exp(log_a)
    bx = jnp.sqrt(-jnp.expm1(2.0 * log_a)) * (i * uf)
    if reverse:
        a, bx = jnp.flip(a, axis=1), jnp.flip(bx, axis=1)
    h, h_last = linear_scan(a, bx, h0)
    if reverse:
        h = jnp.flip(h, axis=1)
    return h, h_last


def dwconv(u, w, b):
    out = lax.conv_general_dilated(u, w[:, None, :].astype(u.dtype), window_strides=(1,),
                                   padding=[(CONV_W // 2, CONV_W - 1 - CONV_W // 2)],
                                   dimension_numbers=('NWC', 'WIO', 'NWC'), feature_group_count=u.shape[-1])
    return out + b


def even_mixer(h_lat, h_ctx, w_in, w_out, lb_f, lb_b, hgrn_gain, qk_gain, lam_vec, diff_gain, lam_init, rope, with_ctx):
    idx = np.cumsum(EVEN_SPLITS)[:-1].tolist()
    pl = jnp.split(h_lat @ w_in, idx, axis=-1)
    pc = jnp.split(h_ctx @ w_in, idx, axis=-1)
    bsz = h_lat.shape[0]

    def a_prep(parts):
        q = to_heads(jax.nn.silu(parts[0].astype(jnp.float32)), A_HEADS)
        v = to_heads(parts[3].astype(jnp.float32), A_HEADS)
        return q, v, parts[1].astype(jnp.float32), parts[2].astype(jnp.float32)

    def a_out(o, g):
        return merge_heads(rms_norm(o, hgrn_gain)).astype(g.dtype) * jax.nn.silu(g)

    s0 = jnp.zeros((bsz, A_HEADS, A_DK, A_DV), jnp.float32)
    qc, vc, ffc, fbc = a_prep(pc)
    oc_f, sc_f = hgrn2_dir(qc, vc, ffc, lb_f, s0, False)
    oc_b, sc_b = hgrn2_dir(qc, vc, fbc, lb_b, s0, True)
    ql, vl, ffl, fbl = a_prep(pl)
    ol_f, _ = hgrn2_dir(ql, vl, ffl, lb_f, sc_f, False)
    ol_b, _ = hgrn2_dir(ql, vl, fbl, lb_b, sc_b, True)

    lv = lam_vec.astype(jnp.float32)
    lam = jnp.exp(jnp.sum(lv[0] * lv[1])) - jnp.exp(jnp.sum(lv[2] * lv[3])) + lam_init
    q_l = diff_qk(pl[5], qk_gain[0], rope)
    k_l = diff_qk(pl[6], qk_gain[1], rope)
    v_l = to_heads(pl[7], B_HEADS)
    k_c = diff_qk(pc[6], qk_gain[1], None)
    v_c = to_heads(pc[7], B_HEADS)
    k_all = jnp.concatenate([k_l, k_c], axis=3)
    v_all = jnp.concatenate([v_l, v_c], axis=2)
    ob_l = rms_norm(diff_attention_blocked(q_l, k_all, v_all, lam), diff_gain) * (1.0 - lam_init)
    out_lat = jnp.concatenate([a_out(ol_f + ol_b, pl[4]), merge_heads(ob_l).astype(h_lat.dtype)], axis=-1) @ w_out
    if not with_ctx:
        return out_lat, None
    q_c = diff_qk(pc[5], qk_gain[0], None)
    ob_c = rms_norm(diff_attention(q_c, k_c, v_c, lam), diff_gain) * (1.0 - lam_init)
    out_ctx = jnp.concatenate([a_out(oc_f + oc_b, pc[4]), merge_heads(ob_c).astype(h_ctx.dtype)], axis=-1) @ w_out
    return out_lat, out_ctx


def odd_mixer(h_lat, h_ctx, w_in, conv_w, conv_b, gate_w, gate_b, lam, w_out, with_ctx):
    y_l, u_l = jnp.split(h_lat @ w_in, 2, axis=-1)
    y_c, u_c = jnp.split(h_ctx @ w_in, 2, axis=-1)
    u_l = dwconv(u_l, conv_w, conv_b)
    u_c = dwconv(u_c, conv_w, conv_b)
    h0 = jnp.zeros((h_lat.shape[0], RG_WIDTH), jnp.float32)
    hc_f, sc_f = rglru_dir(u_c, gate_w[0], gate_b[0], lam[0], h0, False)
    hc_b, sc_b = rglru_dir(u_c, gate_w[1], gate_b[1], lam[1], h0, True)
    hl_f, _ = rglru_dir(u_l, gate_w[0], gate_b[0], lam[0], sc_f, False)
    hl_b, _ = rglru_dir(u_l, gate_w[1], gate_b[1], lam[1], sc_b, True)
    out_lat = ((hl_f + hl_b).astype(y_l.dtype) * jax.nn.gelu(y_l)) @ w_out
    if not with_ctx:
        return out_lat, None
    out_ctx = ((hc_f + hc_b).astype(y_c.dtype) * jax.nn.gelu(y_c)) @ w_out
    return out_lat, out_ctx


def hier_moe(h, w_grp, b_grp, w_exp, b_exp, w1, w3, w2):
    n_tok = h.shape[0]
    g_logit = (h @ w_grp).astype(jnp.float32) + b_grp
    g_prob = jax.nn.softmax(g_logit, axis=-1)
    g_idx = jnp.argmax(g_logit, axis=-1)
    g_w = jnp.take_along_axis(g_prob, g_idx[:, None], axis=-1)
    e_logit = ((h @ w_exp).astype(jnp.float32) + b_exp).reshape(n_tok, N_GROUPS, EXPERTS_PER_GROUP)
    e_logit = jnp.take_along_axis(e_logit, g_idx[:, None, None], axis=1)[:, 0]
    top_v, top_i = lax.top_k(e_logit, TOPK_IN_GROUP)
    top_w = jax.nn.softmax(top_v, axis=-1) * g_w
    expert_id = g_idx[:, None] * EXPERTS_PER_GROUP + top_i
    combine = jnp.einsum('nk,nke->ne', top_w, jax.nn.one_hot(expert_id, N_EXPERTS, dtype=jnp.float32))
    hid = jax.nn.silu(jnp.einsum('nd,edf->nef', h, w1)) * jnp.einsum('nd,edf->nef', h, w3)
    return jnp.einsum('nef,efd->nd', hid * combine[:, :, None].astype(hid.dtype), w2)


def setup_inputs(seed: int = 0) -> dict:
    key = jax.random.key(seed)
    ks = jax.random.split(key, 40)
    d = D_MODEL
    n_even = (DEPTH + 1) // 2
    n_odd = DEPTH // 2

    def nrm(k, shape, scale):
        return jax.random.normal(k, shape, jnp.float32) * scale

    a_tgt = jax.random.uniform(ks[20], (n_odd, 2, RG_WIDTH), jnp.float32, minval=0.9, maxval=0.999)
    a_root = a_tgt ** (1.0 / RG_C)
    rg_lambda = jnp.log(a_root) - jnp.log1p(-a_root)
    return {
        'x': nrm(ks[0], (BATCH, SEQ, d), 1.0),
        'c': nrm(ks[1], (BATCH, d), 1.0),
        'ctx': nrm(ks[2], (BATCH, CTX_LEN, d), 1.0),
        'c_ctx': nrm(ks[3], (d,), 1.0),
        'ada_w': nrm(ks[4], (DEPTH, d, 6 * d), 0.5 * d ** -0.5),
        'ada_b': nrm(ks[5], (DEPTH, 6 * d), 0.02),
        'norm_mix': 1.0 + nrm(ks[6], (DEPTH, d), 0.02),
        'norm_ffn': 1.0 + nrm(ks[7], (DEPTH, d), 0.02),
        'even_w_in': nrm(ks[8], (n_even, d, EVEN_IN), d ** -0.5),
        'even_w_out': nrm(ks[9], (n_even, MIX_OUT, d), MIX_OUT ** -0.5),
        'hgrn_lb_logits': nrm(ks[10], (2, n_even + 1, A_WIDTH), 0.1),
        'hgrn_out_norm': 1.0 + nrm(ks[11], (n_even, A_DV), 0.02),
        'diff_qk_norm': 1.0 + nrm(ks[12], (n_even, 2, B_DH), 0.02),
        'diff_lambda': nrm(ks[13], (n_even, 4, B_DH), 0.1),
        'diff_out_norm': 1.0 + nrm(ks[14], (n_even, B_DV), 0.02),
        'odd_w_in': nrm(ks[15], (n_odd, d, 2 * RG_WIDTH), d ** -0.5),
        'odd_conv_w': nrm(ks[16], (n_odd, CONV_W, RG_WIDTH), CONV_W ** -0.5),
        'odd_conv_b': nrm(ks[17], (n_odd, RG_WIDTH), 0.01),
        'rg_gate_w': nrm(ks[18], (n_odd, 2, 2, RG_HEADS, RG_BLOCK, RG_BLOCK), RG_BLOCK ** -0.5),
        'rg_gate_b': nrm(ks[19], (n_odd, 2, 2, RG_WIDTH), 0.01),
        'rg_lambda': rg_lambda,
        'odd_w_out': nrm(ks[21], (n_odd, RG_WIDTH, d), RG_WIDTH ** -0.5),
        'moe_w_grp': nrm(ks[22], (DEPTH, d, N_GROUPS), d ** -0.5),
        'moe_b_grp': nrm(ks[23], (DEPTH, N_GROUPS), 0.01),
        'moe_w_exp': nrm(ks[24], (DEPTH, d, N_EXPERTS), d ** -0.5),
        'moe_b_exp': nrm(ks[25], (DEPTH, N_EXPERTS), 0.01),
        'moe_w1': nrm(ks[26], (DEPTH, N_EXPERTS, d, D_EXPERT), d ** -0.5),
        'moe_w3': nrm(ks[27], (DEPTH, N_EXPERTS, d, D_EXPERT), d ** -0.5),
        'moe_w2': nrm(ks[28], (DEPTH, N_EXPERTS, D_EXPERT, d), D_EXPERT ** -0.5),
    }


def reference(x, c, ctx, c_ctx, ada_w, ada_b, norm_mix, norm_ffn,
              even_w_in, even_w_out, hgrn_lb_logits, hgrn_out_norm,
              diff_qk_norm, diff_lambda, diff_out_norm,
              odd_w_in, odd_conv_w, odd_conv_b, rg_gate_w, rg_gate_b, rg_lambda, odd_w_out,
              moe_w_grp, moe_b_grp, moe_w_exp, moe_b_exp, moe_w1, moe_w3, moe_w2):
    bsz, n, d = x.shape
    rope = axial_rope(n)
    lb_all = jnp.cumsum(jax.nn.softmax(hgrn_lb_logits.astype(jnp.float32), axis=1), axis=1)
    for l in range(DEPTH):
        with_ctx = l < DEPTH - 1
        li = l // 2
        m_lat = jnp.split(ada(c, ada_w[l], ada_b[l])[:, None, :], 6, axis=-1)
        m_ctx = jnp.split(ada(c_ctx, ada_w[l], ada_b[l])[None, None, :], 6, axis=-1)
        h_lat = modulate(rms_norm(x, norm_mix[l]), m_lat[0], m_lat[1])
        h_ctx = modulate(rms_norm(ctx, norm_mix[l]), m_ctx[0], m_ctx[1])
        if l % 2 == 0:
            mix_lat, mix_ctx = even_mixer(h_lat, h_ctx, even_w_in[li], even_w_out[li], lb_all[0, li], lb_all[1, li],
                                          hgrn_out_norm[li], diff_qk_norm[li], diff_lambda[li], diff_out_norm[li],
                                          0.8 - 0.6 * math.exp(-0.3 * l), rope, with_ctx)
        else:
            mix_lat, mix_ctx = odd_mixer(h_lat, h_ctx, odd_w_in[li], odd_conv_w[li], odd_conv_b[li],
                                         rg_gate_w[li], rg_gate_b[li], rg_lambda[li], odd_w_out[li], with_ctx)
        x = x + m_lat[2] * mix_lat.astype(x.dtype)
        f_lat = modulate(rms_norm(x, norm_ffn[l]), m_lat[3], m_lat[4]).reshape(-1, d)
        if with_ctx:
            ctx = ctx + m_ctx[2] * mix_ctx.astype(ctx.dtype)
            f_ctx = modulate(rms_norm(ctx, norm_ffn[l]), m_ctx[3], m_ctx[4]).reshape(-1, d)
            y = hier_moe(jnp.concatenate([f_lat, f_ctx], axis=0), moe_w_grp[l], moe_b_grp[l], moe_w_exp[l],
                         moe_b_exp[l], moe_w1[l], moe_w3[l], moe_w2[l])
            ctx = ctx + m_ctx[5] * y[bsz * n:].reshape(ctx.shape).astype(ctx.dtype)
            y = y[:bsz * n]
        else:
            y = hier_moe(f_lat, moe_w_grp[l], moe_b_grp[l], moe_w_exp[l], moe_b_exp[l],
                         moe_w1[l], moe_w3[l], moe_w2[l])
        x = x + m_lat[5] * y.reshape(x.shape).astype(x.dtype)
    return x
```

```python
import functools
import math

import jax
import jax.numpy as jnp
import numpy as np
from jax import lax
from jax.experimental import pallas as pl
from jax.experimental.pallas import tpu as pltpu

F32 = jnp.float32
BF16 = jnp.bfloat16

EPS = 1e-6
GRID_W = 64
A_HEADS = 8
A_DK = 128
A_WIDTH = 1024
HGRN_CHUNK = 32
B_HEADS = 8
B_DH = 64
B_QK_WIDTH = 1024
B_WIDTH = 1024
Q_BLOCK = 128
ROPE_BASE = 10000.0
EVEN_SPLITS = (A_WIDTH, A_WIDTH, A_WIDTH, A_WIDTH, A_WIDTH, B_QK_WIDTH, B_QK_WIDTH, B_WIDTH)
RG_HEADS = 16
RG_BLOCK = 128
CONV_W = 4
RG_C = 8.0
N_GROUPS = 4
EXPERTS_PER_GROUP = 4
N_EXPERTS = 16
D_EXPERT = 512
N_MOD = 6
MOD_ROWS = 16

VMEM_LIMIT = 56 * 1024 * 1024


def _cparams(sem):
    return pltpu.CompilerParams(dimension_semantics=sem, vmem_limit_bytes=VMEM_LIMIT)


def _ada_kernel(c_ref, w_ref, b_ref, o_ref):
    c = c_ref[...]
    a = (c * jax.nn.sigmoid(c)).astype(BF16)
    o_ref[...] = jnp.dot(a, w_ref[...].astype(BF16), preferred_element_type=F32) + b_ref[...]


def _ada_all(cvec, ada_w, ada_b, tn=1024):
    depth, d, n = ada_w.shape
    return pl.pallas_call(
        _ada_kernel,
        grid=(depth, n // tn),
        in_specs=[pl.BlockSpec((MOD_ROWS, d), lambda l, j: (0, 0)),
                  pl.BlockSpec((None, d, tn), lambda l, j: (l, 0, j)),
                  pl.BlockSpec((None, 1, tn), lambda l, j: (l, 0, j))],
        out_specs=pl.BlockSpec((None, MOD_ROWS, tn), lambda l, j: (l, 0, j)),
        out_shape=jax.ShapeDtypeStruct((depth, MOD_ROWS, n), F32),
        compiler_params=_cparams(("arbitrary", "arbitrary")),
        name="ada_mod",
    )(cvec, ada_w, ada_b.reshape(depth, 1, n))


def _norm_mod_mm_kernel(x_ref, g_ref, sh_ref, sc_ref, w_ref, o_ref, h_scr):
    @pl.when(pl.program_id(1) == 0)
    def _():
        x = x_ref[...]
        y = x * lax.rsqrt(jnp.mean(x * x, axis=-1, keepdims=True) + EPS) * g_ref[...]
        h_scr[...] = (y * (1.0 + sc_ref[0]) + sh_ref[0]).astype(BF16)

    o_ref[...] = jnp.dot(h_scr[...], w_ref[...], preferred_element_type=F32).astype(o_ref.dtype)


def _norm_mod_mm(xc, gain, mods, w, rows_per_mod, n_lat_mods, out_dtype, tm=1024, tn=512):
    r, d = xc.shape
    n = w.shape[1]
    tpm = rows_per_mod // tm

    def mrow(i):
        return jnp.minimum(i // tpm, n_lat_mods) * N_MOD

    return pl.pallas_call(
        _norm_mod_mm_kernel,
        grid=(r // tm, n // tn),
        in_specs=[pl.BlockSpec((tm, d), lambda i, j: (i, 0)),
                  pl.BlockSpec((1, d), lambda i, j: (0, 0)),
                  pl.BlockSpec((1, 1, d), lambda i, j: (mrow(i) + 0, 0, 0)),
                  pl.BlockSpec((1, 1, d), lambda i, j: (mrow(i) + 1, 0, 0)),
                  pl.BlockSpec((d, tn), lambda i, j: (0, j))],
        out_specs=pl.BlockSpec((tm, tn), lambda i, j: (i, j)),
        out_shape=jax.ShapeDtypeStruct((r, n), out_dtype),
        scratch_shapes=[pltpu.VMEM((tm, d), BF16)],
        compiler_params=_cparams(("parallel", "arbitrary")),
        name="norm_mod_mm",
    )(xc, gain.reshape(1, d), mods, mods, w)


def _out_proj_kernel(m_ref, w_ref, x_ref, gate_ref, g_ref, sh_ref, sc_ref, wr_hi_ref, wr_lo_ref,
                     xo_ref, f_ref, lg_ref):
    mix = jnp.dot(m_ref[...], w_ref[...], preferred_element_type=F32)
    x = x_ref[...] + gate_ref[0] * mix
    xo_ref[...] = x
    y = x * lax.rsqrt(jnp.mean(x * x, axis=-1, keepdims=True) + EPS) * g_ref[...]
    f = y * (1.0 + sc_ref[0]) + sh_ref[0]
    f_ref[...] = f.astype(BF16)
    f_hi = f.astype(BF16)
    f_lo = (f - f_hi.astype(F32)).astype(BF16)
    lg = jnp.dot(f_hi, wr_hi_ref[...], preferred_element_type=F32)
    lg += jnp.dot(f_lo, wr_hi_ref[...], preferred_element_type=F32)
    lg += jnp.dot(f_hi, wr_lo_ref[...], preferred_element_type=F32)
    lg_ref[...] = lg


def _out_proj(mixc, w_out, xc, gain, mods, w_router, rows_per_mod, n_lat_mods, tm=256):
    r, k = mixc.shape
    d = w_out.shape[1]
    tpm = rows_per_mod // tm
    wr_hi = w_router.astype(BF16)
    wr_lo = (w_router - wr_hi.astype(F32)).astype(BF16)
    nr = w_router.shape[1]

    def mrow(i):
        return jnp.minimum(i // tpm, n_lat_mods) * N_MOD

    return pl.pallas_call(
        _out_proj_kernel,
        grid=(r // tm,),
        in_specs=[pl.BlockSpec((tm, k), lambda i: (i, 0)),
                  pl.BlockSpec((k, d), lambda i: (0, 0)),
                  pl.BlockSpec((tm, d), lambda i: (i, 0)),
                  pl.BlockSpec((1, 1, d), lambda i: (mrow(i) + 2, 0, 0)),
                  pl.BlockSpec((1, d), lambda i: (0, 0)),
                  pl.BlockSpec((1, 1, d), lambda i: (mrow(i) + 3, 0, 0)),
                  pl.BlockSpec((1, 1, d), lambda i: (mrow(i) + 4, 0, 0)),
                  pl.BlockSpec((d, nr), lambda i: (0, 0)),
                  pl.BlockSpec((d, nr), lambda i: (0, 0))],
        out_specs=[pl.BlockSpec((tm, d), lambda i: (i, 0)),
                   pl.BlockSpec((tm, d), lambda i: (i, 0)),
                   pl.BlockSpec((tm, nr), lambda i: (i, 0))],
        out_shape=[jax.ShapeDtypeStruct((r, d), F32),
                   jax.ShapeDtypeStruct((r, d), BF16),
                   jax.ShapeDtypeStruct((r, nr), F32)],
        compiler_params=_cparams(("parallel",)),
        name="out_proj",
    )(mixc, w_out, xc, mods, gain.reshape(1, d), mods, mods, wr_hi, wr_lo)


def _moe_kernel(tg_ref, tv_ref, x_ref, cw_ref, w1_ref, w3_ref, w2_ref, o_ref, acc_ref):
    t = pl.program_id(0)
    e = pl.program_id(1)

    @pl.when(e == 0)
    def _():
        acc_ref[...] = jnp.zeros_like(acc_ref)

    @pl.when(tv_ref[t] > 0)
    def _():
        x = x_ref[...]
        h1 = jnp.dot(x, w1_ref[...], preferred_element_type=F32)
        h3 = jnp.dot(x, w3_ref[...], preferred_element_type=F32)
        cw = cw_ref[...]
        lane = lax.broadcasted_iota(jnp.int32, cw.shape, 1)
        cwe = jnp.sum(jnp.where(lane == e, cw, 0.0), axis=-1, keepdims=True)
        hid = (h1 * jax.nn.sigmoid(h1) * h3 * cwe).astype(BF16)
        acc_ref[...] += jnp.dot(hid, w2_ref[...], preferred_element_type=F32)

    @pl.when(e == EXPERTS_PER_GROUP - 1)
    def _():
        o_ref[...] = acc_ref[...]


def _moe_ffn(tile_group, tile_valid, f_sorted, cw_sorted, w1, w3, w2, tm):
    p, d = f_sorted.shape
    fe = w1.shape[2]
    n_tiles = p // tm
    epg = EXPERTS_PER_GROUP
    grid_spec = pltpu.PrefetchScalarGridSpec(
        num_scalar_prefetch=2,
        grid=(n_tiles, epg),
        in_specs=[pl.BlockSpec((tm, d), lambda t, e, tg, tv: (t, 0)),
                  pl.BlockSpec((tm, 128), lambda t, e, tg, tv: (t, 0)),
                  pl.BlockSpec((None, d, fe), lambda t, e, tg, tv: (tg[t] * epg + e, 0, 0)),
                  pl.BlockSpec((None, d, fe), lambda t, e, tg, tv: (tg[t] * epg + e, 0, 0)),
                  pl.BlockSpec((None, fe, d), lambda t, e, tg, tv: (tg[t] * epg + e, 0, 0))],
        out_specs=pl.BlockSpec((tm, d), lambda t, e, tg, tv: (t, 0)),
        scratch_shapes=[pltpu.VMEM((tm, d), F32)],
    )
    return pl.pallas_call(
        _moe_kernel,
        grid_spec=grid_spec,
        out_shape=jax.ShapeDtypeStruct((p, d), F32),
        compiler_params=_cparams(("arbitrary", "arbitrary")),
        name="moe_ffn",
    )(tile_group, tile_valid, f_sorted, cw_sorted, w1, w3, w2)


def _route(logits, b_grp, b_exp):
    g_logit = logits[:, :N_GROUPS] + b_grp
    g_prob = jax.nn.softmax(g_logit, axis=-1)
    g_idx = jnp.argmax(g_logit, axis=-1)
    g_w = jnp.take_along_axis(g_prob, g_idx[:, None], axis=-1)
    e_logit = (logits[:, N_GROUPS:N_GROUPS + N_EXPERTS] + b_exp).reshape(-1, N_GROUPS, EXPERTS_PER_GROUP)
    e_logit = jnp.take_along_axis(e_logit, g_idx[:, None, None], axis=1)[:, 0]
    top_v, top_i = lax.top_k(e_logit, 2)
    top_w = jax.nn.softmax(top_v, axis=-1) * g_w
    cw = jnp.einsum('nk,nke->ne', top_w, jax.nn.one_hot(top_i, EXPERTS_PER_GROUP, dtype=F32))
    return g_idx.astype(jnp.int32), cw


def _moe(f, logits, b_grp, b_exp, w1, w3, w2, tm=512):
    r, d = f.shape
    g_idx, cw = _route(logits, b_grp, b_exp)
    onehot = (g_idx[:, None] == jnp.arange(N_GROUPS, dtype=jnp.int32)[None, :]).astype(jnp.int32)
    counts = jnp.sum(onehot, axis=0)
    rank = jnp.sum((jnp.cumsum(onehot, axis=0) - 1) * onehot, axis=1)
    padded = ((counts + tm - 1) // tm) * tm
    ends = jnp.cumsum(padded)
    starts = ends - padded
    pos = starts[g_idx] + rank
    p = r + N_GROUPS * tm
    n_tiles = p // tm
    src = jnp.zeros((p,), jnp.int32).at[pos].set(jnp.arange(r, dtype=jnp.int32))
    cw_sorted = jnp.zeros((p, 128), F32).at[pos, :EXPERTS_PER_GROUP].set(cw)
    tile_start = jnp.arange(n_tiles, dtype=jnp.int32) * tm
    tile_group = jnp.minimum(jnp.sum((tile_start[:, None] >= ends[None, :]).astype(jnp.int32), axis=1),
                             N_GROUPS - 1).astype(jnp.int32)
    tile_valid = (tile_start < ends[-1]).astype(jnp.int32)
    f_sorted = f[src]
    y_sorted = _moe_ffn(tile_group, tile_valid, f_sorted, cw_sorted, w1, w3, w2, tm)
    return y_sorted[pos]


def _rms_norm(x, gain):
    xf = x.astype(F32)
    y = xf * lax.rsqrt(jnp.mean(xf * xf, axis=-1, keepdims=True) + EPS)
    return (y * gain.astype(F32)).astype(x.dtype)


def _to_heads(t, n_heads):
    b, n, w = t.shape
    return t.reshape(b, n, n_heads, w // n_heads).transpose(0, 2, 1, 3)


def _merge_heads(t):
    b, h, n, d = t.shape
    return t.transpose(0, 2, 1, 3).reshape(b, n, h * d)


def _axial_rope(n):
    n_rows = n // GRID_W
    row = jnp.repeat(jnp.arange(n_rows), GRID_W).astype(F32)
    col = jnp.tile(jnp.arange(GRID_W), n_rows).astype(F32)
    pairs = B_DH // 4
    inv = ROPE_BASE ** (-jnp.arange(pairs, dtype=F32) / pairs)
    ang = jnp.concatenate([row[:, None] * inv, col[:, None] * inv], axis=-1)
    return jnp.cos(ang)[:, None, None, :], jnp.sin(ang)[:, None, None, :]


def _apply_rope(t, cos, sin):
    tp = t.reshape(t.shape[:-1] + (-1, 2))
    t0, t1 = tp[..., 0], tp[..., 1]
    out = jnp.stack([t0 * cos - t1 * sin, t0 * sin + t1 * cos], axis=-1)
    return out.reshape(t.shape).astype(t.dtype)


def _diff_qk(t, gain, rope):
    b, n, _ = t.shape
    t = _rms_norm(t.reshape(b, n, B_HEADS, 2, B_DH), gain)
    if rope is not None:
        t = _apply_rope(t, rope[0], rope[1])
    return t.transpose(0, 2, 3, 1, 4)


def _diff_attention(q, k, v, lam):
    s = jnp.einsum('bhmqd,bhmkd->bhmqk', q, k).astype(F32) * (B_DH ** -0.5)
    p = jax.nn.softmax(s, axis=-1)
    w = p[:, :, 0] - lam * p[:, :, 1]
    return jnp.einsum('bhqk,bhkv->bhqv', w.astype(v.dtype), v)


def _diff_attention_blocked(q, k, v, lam):
    b, h, _, n, dh = q.shape
    nb = n // Q_BLOCK
    qb = jnp.moveaxis(q.reshape(b, h, 2, nb, Q_BLOCK, dh), 3, 0)
    ob = lax.map(lambda blk: _diff_attention(blk, k, v, lam), qb)
    return jnp.moveaxis(ob, 0, 2).reshape(b, h, n, -1)


def _hgrn2_scan(q, k, v, log_f, s0):
    b, h, n, dk = q.shape
    dv = v.shape[-1]
    nc = n // HGRN_CHUNK

    def to_chunks(t):
        return jnp.moveaxis(t.reshape(b, h, nc, HGRN_CHUNK, t.shape[-1]), 2, 0)

    mask = jnp.tril(jnp.ones((HGRN_CHUNK, HGRN_CHUNK), dtype=bool))

    def step(s, xs):
        qc, kc, vc, lf = xs
        cum = jnp.cumsum(lf, axis=2)
        cum_last = cum[:, :, -1:, :]
        q_t = qc * jnp.exp(cum)
        k_t = kc * jnp.exp(-cum)
        att = jnp.where(mask, jnp.einsum('bhtk,bhsk->bhts', q_t, k_t), 0.0)
        o = jnp.einsum('bhts,bhsv->bhtv', att, vc) + jnp.einsum('bhtk,bhkv->bhtv', q_t, s)
        s_new = jnp.exp(cum_last[:, :, 0, :])[..., None] * s + jnp.einsum('bhsk,bhsv->bhkv', kc * jnp.exp(cum_last - cum), vc)
        return s_new, o

    s_fin, o = lax.scan(step, s0, (to_chunks(q), to_chunks(k), to_chunks(v), to_chunks(log_f)))
    return jnp.moveaxis(o, 0, 2).reshape(b, h, n, dv), s_fin


def _hgrn2_dir(q, v, f_logit, lb, s0, reverse):
    f = lb + (1.0 - lb) * jax.nn.sigmoid(f_logit)
    k = _to_heads(1.0 - f, A_HEADS)
    log_f = _to_heads(jnp.log(f), A_HEADS)
    if reverse:
        q, k, v, log_f = (jnp.flip(t, axis=2) for t in (q, k, v, log_f))
    o, s = _hgrn2_scan(q, k, v, log_f, s0)
    if reverse:
        o = jnp.flip(o, axis=2)
    return o, s


def _even_core(pl_, pc_, lb_f, lb_b, hgrn_gain, qk_gain, lam_vec, diff_gain, lam_init, rope):
    idx = np.cumsum(EVEN_SPLITS)[:-1].tolist()
    pl_ = jnp.split(pl_, idx, axis=-1)
    pc_ = jnp.split(pc_, idx, axis=-1)
    bsz = pl_[0].shape[0]

    def a_prep(parts):
        q = _to_heads(jax.nn.silu(parts[0].astype(F32)), A_HEADS)
        v = _to_heads(parts[3].astype(F32), A_HEADS)
        return q, v, parts[1].astype(F32), parts[2].astype(F32)

    def a_out(o, g):
        return _merge_heads(_rms_norm(o, hgrn_gain)).astype(g.dtype) * jax.nn.silu(g)

    s0 = jnp.zeros((bsz, A_HEADS, A_DK, A_DK), F32)
    qc, vc, ffc, fbc = a_prep(pc_)
    oc_f, sc_f = _hgrn2_dir(qc, vc, ffc, lb_f, s0, False)
    oc_b, sc_b = _hgrn2_dir(qc, vc, fbc, lb_b, s0, True)
    ql, vl, ffl, fbl = a_prep(pl_)
    ol_f, _ = _hgrn2_dir(ql, vl, ffl, lb_f, sc_f, False)
    ol_b, _ = _hgrn2_dir(ql, vl, fbl, lb_b, sc_b, True)

    lv = lam_vec.astype(F32)
    lam = jnp.exp(jnp.sum(lv[0] * lv[1])) - jnp.exp(jnp.sum(lv[2] * lv[3])) + lam_init
    q_l = _diff_qk(pl_[5], qk_gain[0], rope)
    k_l = _diff_qk(pl_[6], qk_gain[1], rope)
    v_l = _to_heads(pl_[7], B_HEADS)
    k_c = _diff_qk(pc_[6], qk_gain[1], None)
    v_c = _to_heads(pc_[7], B_HEADS)
    k_all = jnp.concatenate([k_l, k_c], axis=3)
    v_all = jnp.concatenate([v_l, v_c], axis=2)
    ob_l = _rms_norm(_diff_attention_blocked(q_l, k_all, v_all, lam), diff_gain) * (1.0 - lam_init)
    cat_lat = jnp.concatenate([a_out(ol_f + ol_b, pl_[4]), _merge_heads(ob_l)], axis=-1)
    q_c = _diff_qk(pc_[5], qk_gain[0], None)
    ob_c = _rms_norm(_diff_attention(q_c, k_c, v_c, lam), diff_gain) * (1.0 - lam_init)
    cat_ctx = jnp.concatenate([a_out(oc_f + oc_b, pc_[4]), _merge_heads(ob_c)], axis=-1)
    return cat_lat, cat_ctx


def _linear_scan(a, bx, h0):
    bx = bx.at[:, 0].add(a[:, 0] * h0)

    def comb(e1, e2):
        a1, b1 = e1
        a2, b2 = e2
        return a1 * a2, a2 * b1 + b2

    _, h = lax.associative_scan(comb, (a, bx), axis=1)
    return h, h[:, -1]


def _rglru_dir(u, gate_w, gate_b, lam, h0, reverse):
    b, n, w = u.shape
    uf = u.astype(F32)
    gates = jnp.einsum('bthi,ghij->gbthj', uf.reshape(b, n, RG_HEADS, RG_BLOCK), gate_w.astype(F32))
    gates = gates.reshape(2, b, n, w) + gate_b[:, None, None, :]
    r = jax.nn.sigmoid(gates[0])
    i = jax.nn.sigmoid(gates[1])
    log_a = -RG_C * jax.nn.softplus(-lam.astype(F32)) * r
    a = jnp.exp(log_a)
    bx = jnp.sqrt(-jnp.expm1(2.0 * log_a)) * (i * uf)
    if reverse:
        a, bx = jnp.flip(a, axis=1), jnp.flip(bx, axis=1)
    h, h_last = _linear_scan(a, bx, h0)
    if reverse:
        h = jnp.flip(h, axis=1)
    return h, h_last


def _dwconv(u, w, b):
    out = lax.conv_general_dilated(u, w[:, None, :].astype(u.dtype), window_strides=(1,),
                                   padding=[(CONV_W // 2, CONV_W - 1 - CONV_W // 2)],
                                   dimension_numbers=('NWC', 'WIO', 'NWC'), feature_group_count=u.shape[-1])
    return out + b


def _odd_core(pl_, pc_, conv_w, conv_b, gate_w, gate_b, lam):
    y_l, u_l = jnp.split(pl_, 2, axis=-1)
    _, u_c = jnp.split(pc_, 2, axis=-1)
    u_l = _dwconv(u_l, conv_w, conv_b)
    u_c = _dwconv(u_c, conv_w, conv_b)
    h0 = jnp.zeros((pl_.shape[0], u_l.shape[-1]), F32)
    _, sc_f = _rglru_dir(u_c, gate_w[0], gate_b[0], lam[0], h0, False)
    _, sc_b = _rglru_dir(u_c, gate_w[1], gate_b[1], lam[1], h0, True)
    hl_f, _ = _rglru_dir(u_l, gate_w[0], gate_b[0], lam[0], sc_f, False)
    hl_b, _ = _rglru_dir(u_l, gate_w[1], gate_b[1], lam[1], sc_b, True)
    return (hl_f + hl_b) * jax.nn.gelu(y_l)


def kernel(x, c, ctx, c_ctx, ada_w, ada_b, norm_mix, norm_ffn, even_w_in, even_w_out, hgrn_lb_logits, hgrn_out_norm, diff_qk_norm, diff_lambda, diff_out_norm, odd_w_in, odd_conv_w, odd_conv_b, rg_gate_w, rg_gate_b, rg_lambda, odd_w_out, moe_w_grp, moe_b_grp, moe_w_exp, moe_b_exp, moe_w1, moe_w3, moe_w2):
    bsz, n, d = x.shape
    n_ctx = ctx.shape[1]
    r_lat = bsz * n
    depth = ada_w.shape[0]

    cvec = jnp.zeros((MOD_ROWS, d), F32).at[:bsz].set(c).at[bsz].set(c_ctx)
    mods_all = _ada_all(cvec, ada_w, ada_b)
    mods_all = mods_all.reshape(depth, MOD_ROWS * N_MOD, 1, d)

    xc = jnp.concatenate([x.reshape(r_lat, d), ctx.reshape(bsz * n_ctx, d)], axis=0)
    rope = _axial_rope(n)
    lb_all = jnp.cumsum(jax.nn.softmax(hgrn_lb_logits.astype(F32), axis=1), axis=1)

    def router_w(l):
        w = jnp.concatenate([moe_w_grp[l], moe_w_exp[l]], axis=1)
        return jnp.pad(w, ((0, 0), (0, 128 - w.shape[1])))

    l = 0
    mods = mods_all[l]
    proj = _norm_mod_mm(xc, norm_mix[l], mods, even_w_in[0].astype(BF16), n, bsz, F32)
    pl_ = proj[:r_lat].reshape(bsz, n, -1)
    pc_ = proj[r_lat:].reshape(bsz, n_ctx, -1)
    cat_lat, cat_ctx = _even_core(pl_, pc_, lb_all[0, 0], lb_all[1, 0], hgrn_out_norm[0], diff_qk_norm[0],
                                  diff_lambda[0], diff_out_norm[0], 0.8 - 0.6 * math.exp(-0.3 * l), rope)
    mixc = jnp.concatenate([cat_lat.reshape(r_lat, -1), cat_ctx.reshape(bsz * n_ctx, -1)], axis=0).astype(BF16)
    xc, f, logits = _out_proj(mixc, even_w_out[0].astype(BF16), xc, norm_ffn[l], mods, router_w(l), n, bsz)
    y = _moe(f, logits, moe_b_grp[l], moe_b_exp[l], moe_w1[l].astype(BF16), moe_w3[l].astype(BF16),
             moe_w2[l].astype(BF16))
    gate = mods[5::N_MOD, 0, :]
    x_lat = xc[:r_lat].reshape(bsz, n, d) + gate[:bsz, None, :] * y[:r_lat].reshape(bsz, n, d)
    x_ctx = xc[r_lat:].reshape(bsz, n_ctx, d) + gate[bsz][None, None, :] * y[r_lat:].reshape(bsz, n_ctx, d)
    xc = jnp.concatenate([x_lat.reshape(r_lat, d), x_ctx.reshape(bsz * n_ctx, d)], axis=0)

    l = 1
    mods = mods_all[l]
    proj = _norm_mod_mm(xc, norm_mix[l], mods, odd_w_in[0].astype(BF16), n, bsz, F32)
    pl_ = proj[:r_lat].reshape(bsz, n, -1)
    pc_ = proj[r_lat:].reshape(bsz, n_ctx, -1)
    gated = _odd_core(pl_, pc_, odd_conv_w[0], odd_conv_b[0], rg_gate_w[0], rg_gate_b[0], rg_lambda[0])
    x_lat2, f, logits = _out_proj(gated.reshape(r_lat, -1).astype(BF16), odd_w_out[0].astype(BF16), xc[:r_lat],
                                  norm_ffn[l], mods, router_w(l), n, bsz)
    y = _moe(f, logits, moe_b_grp[l], moe_b_exp[l], moe_w1[l].astype(BF16), moe_w3[l].astype(BF16),
             moe_w2[l].astype(BF16))
    gate = mods[5::N_MOD, 0, :]
    out = x_lat2.reshape(bsz, n, d) + gate[:bsz, None, :] * y.reshape(bsz, n, d)
    return out
```

```python
import functools
import math

import jax
import jax.numpy as jnp
from jax import lax
from jax.experimental import pallas as pl
from jax.experimental.pallas import tpu as pltpu

F32 = jnp.float32
BF16 = jnp.bfloat16
I32 = jnp.int32

EPS = 1e-6
LANES = 128
SUBLANES = 8
GRID_W = 64
A_HEADS = 8
HEAD_W = 128
HGRN_CHUNK = 32
B_HEADS = 8
B_DH = 64
ROPE_BASE = 10000.0
RG_HEADS = 16
CONV_W = 4
RG_C = 8.0
N_GROUPS = 4
EXPERTS_PER_GROUP = 4
N_EXPERTS = 16
N_MOD = 6
MOD_ROWS = 16
ROUTE_W = LANES

VMEM_LIMIT = 56 * 1024 * 1024


def _cparams(sem):
    return pltpu.CompilerParams(dimension_semantics=sem, vmem_limit_bytes=VMEM_LIMIT)


def _sigmoid(x):
    return 0.5 * jnp.tanh(0.5 * x) + 0.5


def _dot(a, b):
    return jnp.dot(a, b, preferred_element_type=F32)


def _dot_nt(a, b):
    return lax.dot_general(a, b, (((1,), (1,)), ((), ())), preferred_element_type=F32)


def _dot_tn(a, b):
    return lax.dot_general(a, b, (((0,), (0,)), ((), ())), preferred_element_type=F32)


def _split_bf16(x):
    hi = x.astype(BF16)
    return hi, (x - hi.astype(F32)).astype(BF16)


def _ada_kernel(c_ref, w_ref, b_ref, o_ref):
    c = c_ref[...]
    a = (c * _sigmoid(c)).astype(BF16)
    o_ref[...] = _dot(a, w_ref[...].astype(BF16)) + b_ref[...]


def _ada_all(cvec, ada_w, ada_b, tn=1024):
    depth, d, n = ada_w.shape
    return pl.pallas_call(
        _ada_kernel,
        grid=(depth, n // tn),
        in_specs=[pl.BlockSpec((MOD_ROWS, d), lambda l, j: (0, 0)),
                  pl.BlockSpec((None, d, tn), lambda l, j: (l, 0, j)),
                  pl.BlockSpec((None, 1, tn), lambda l, j: (l, 0, j))],
        out_specs=pl.BlockSpec((None, MOD_ROWS, tn), lambda l, j: (l, 0, j)),
        out_shape=jax.ShapeDtypeStruct((depth, MOD_ROWS, n), F32),
        compiler_params=_cparams(("arbitrary", "arbitrary")),
        name="ada_mod",
    )(cvec, ada_w, ada_b.reshape(depth, 1, n))


def _norm_mod_mm_kernel(x_ref, g_ref, sh_ref, sc_ref, w_ref, o_ref, h_scr):
    @pl.when(pl.program_id(1) == 0)
    def _():
        x = x_ref[...]
        y = x * lax.rsqrt(jnp.mean(x * x, axis=-1, keepdims=True) + EPS) * g_ref[...]
        h_scr[...] = (y * (1.0 + sc_ref[0]) + sh_ref[0]).astype(BF16)

    o_ref[...] = _dot(h_scr[...], w_ref[...]).astype(o_ref.dtype)


def _norm_mod_mm(x2, gain, mods, w, rows_per_mod, mod_base, tm, tn=512):
    r, d = x2.shape
    n = w.shape[1]
    tpm = rows_per_mod // tm

    def mrow(i):
        return (mod_base + i // tpm) * N_MOD

    return pl.pallas_call(
        _norm_mod_mm_kernel,
        grid=(r // tm, n // tn),
        in_specs=[pl.BlockSpec((tm, d), lambda i, j: (i, 0)),
                  pl.BlockSpec((1, d), lambda i, j: (0, 0)),
                  pl.BlockSpec((1, 1, d), lambda i, j: (mrow(i) + 0, 0, 0)),
                  pl.BlockSpec((1, 1, d), lambda i, j: (mrow(i) + 1, 0, 0)),
                  pl.BlockSpec((d, tn), lambda i, j: (0, j))],
        out_specs=pl.BlockSpec((tm, tn), lambda i, j: (i, j)),
        out_shape=jax.ShapeDtypeStruct((r, n), F32),
        scratch_shapes=[pltpu.VMEM((tm, d), BF16)],
        compiler_params=_cparams(("parallel", "arbitrary")),
        name="norm_mod_mm",
    )(x2, gain.reshape(1, d), mods, mods, w)


def _hgrn_kernel(ql, ffl, fbl, vl, gl, qc, ffc, fbc, vc, gc, lbf_ref, lbb_ref, gain_ref,
                 ol_ref, oc_ref,
                 oi_l, oi_c, qtf_l, qtb_l, qtf_c, qtb_c, kv_f, kv_b, dec_f, dec_b):
    c_sz = HGRN_CHUNK
    nc_l = ql.shape[0] // c_sz
    nc_c = qc.shape[0] // c_sz
    nc = nc_l + nc_c
    row = lax.broadcasted_iota(I32, (c_sz, c_sz), 0)
    col = lax.broadcasted_iota(I32, (c_sz, c_sz), 1)
    tril = row >= col
    triu = row <= col
    ones_l = jnp.where(tril, 1.0, 0.0).astype(BF16)
    ones_u = jnp.where(triu, 1.0, 0.0).astype(BF16)
    lbf = lbf_ref[...]
    lbb = lbb_ref[...]

    def local_terms(c, q_ref, ff_ref, fb_ref, v_ref, oi_ref, qtf_ref, qtb_ref, gid_f, gid_b):
        rows = pl.ds(pl.multiple_of(c * c_sz, c_sz), c_sz)
        q = q_ref[rows, :]
        qs = q * _sigmoid(q)
        vb = v_ref[rows, :].astype(BF16)
        o_sum = jnp.zeros((c_sz, HEAD_W), F32)
        for f_ref, lb, ones, mask, last, qt_ref, kv_ref, dec_ref, gid in (
                (ff_ref, lbf, ones_l, tril, c_sz - 1, qtf_ref, kv_f, dec_f, gid_f),
                (fb_ref, lbb, ones_u, triu, 0, qtb_ref, kv_b, dec_b, gid_b)):
            f = lb + (1.0 - lb) * _sigmoid(f_ref[rows, :])
            k = 1.0 - f
            lf_hi, lf_lo = _split_bf16(jnp.log(f))
            cum2 = _dot(ones, jnp.concatenate([lf_hi, lf_lo], axis=1))
            cum = cum2[:, :HEAD_W] + cum2[:, HEAD_W:]
            d = jnp.exp(cum[last:last + 1, :])
            qt = (qs * jnp.exp(cum)).astype(BF16)
            kt = k * jnp.exp(-cum)
            att = jnp.where(mask, _dot_nt(qt, kt.astype(BF16)), 0.0)
            o_sum = o_sum + _dot(att.astype(BF16), vb)
            kv_ref[gid] = _dot_tn(vb, (kt * d).astype(BF16))
            dec_ref[gid] = d
            qt_ref[rows, :] = qt
        oi_ref[rows, :] = o_sum

    def local_c(c, carry):
        local_terms(c, qc, ffc, fbc, vc, oi_c, qtf_c, qtb_c, c, nc_l + c)
        return carry

    def local_l(c, carry):
        local_terms(c, ql, ffl, fbl, vl, oi_l, qtf_l, qtb_l, nc_c + c, c)
        return carry

    lax.fori_loop(0, nc_c, local_c, 0, unroll=2)
    lax.fori_loop(0, nc_l, local_l, 0, unroll=2)

    def rec_f(i, s):
        new = s * dec_f[i] + kv_f[i]
        kv_f[i] = s
        return new

    def rec_b(i, s):
        j = nc - 1 - i
        new = s * dec_b[j] + kv_b[j]
        kv_b[j] = s
        return new

    s0 = jnp.zeros((HEAD_W, HEAD_W), F32)
    lax.fori_loop(0, nc, rec_f, s0)
    lax.fori_loop(0, nc, rec_b, s0)

    gain = gain_ref[...]

    def finish(c, g_ref, oi_ref, qtf_ref, qtb_ref, o_ref, gid_f, gid_b):
        rows = pl.ds(pl.multiple_of(c * c_sz, c_sz), c_sz)
        o = (oi_ref[rows, :] + _dot_nt(qtf_ref[rows, :], kv_f[gid_f].astype(BF16))
             + _dot_nt(qtb_ref[rows, :], kv_b[gid_b].astype(BF16)))
        y = o * lax.rsqrt(jnp.mean(o * o, axis=-1, keepdims=True) + EPS) * gain
        g = g_ref[rows, :]
        o_ref[rows, :] = (y * (g * _sigmoid(g))).astype(o_ref.dtype)

    def fin_c(c, carry):
        finish(c, gc, oi_c, qtf_c, qtb_c, oc_ref, c, nc_l + c)
        return carry

    def fin_l(c, carry):
        finish(c, gl, oi_l, qtf_l, qtb_l, ol_ref, nc_c + c, c)
        return carry

    lax.fori_loop(0, nc_c, fin_c, 0, unroll=2)
    lax.fori_loop(0, nc_l, fin_l, 0, unroll=2)


def _hgrn(proj_l, proj_c, lb_f, lb_b, gain, bsz):
    n = proj_l.shape[0] // bsz
    n_ctx = proj_c.shape[0] // bsz
    nc = (n + n_ctx) // HGRN_CHUNK
    w = A_HEADS * HEAD_W

    def col(k):
        return lambda b, h: (b, k * A_HEADS + h)

    in_specs = ([pl.BlockSpec((n, HEAD_W), col(k)) for k in range(5)]
                + [pl.BlockSpec((n_ctx, HEAD_W), col(k)) for k in range(5)]
                + [pl.BlockSpec((1, HEAD_W), lambda b, h: (0, h)),
                   pl.BlockSpec((1, HEAD_W), lambda b, h: (0, h)),
                   pl.BlockSpec((1, HEAD_W), lambda b, h: (0, 0))])
    return pl.pallas_call(
        _hgrn_kernel,
        grid=(bsz, A_HEADS),
        in_specs=in_specs,
        out_specs=[pl.BlockSpec((n, HEAD_W), lambda b, h: (b, h)),
                   pl.BlockSpec((n_ctx, HEAD_W), lambda b, h: (b, h))],
        out_shape=[jax.ShapeDtypeStruct((bsz * n, w), BF16),
                   jax.ShapeDtypeStruct((bsz * n_ctx, w), BF16)],
        scratch_shapes=[pltpu.VMEM((n, HEAD_W), F32), pltpu.VMEM((n_ctx, HEAD_W), F32),
                        pltpu.VMEM((n, HEAD_W), BF16), pltpu.VMEM((n, HEAD_W), BF16),
                        pltpu.VMEM((n_ctx, HEAD_W), BF16), pltpu.VMEM((n_ctx, HEAD_W), BF16),
                        pltpu.VMEM((nc, HEAD_W, HEAD_W), F32), pltpu.VMEM((nc, HEAD_W, HEAD_W), F32),
                        pltpu.VMEM((nc, 1, HEAD_W), F32), pltpu.VMEM((nc, 1, HEAD_W), F32)],
        compiler_params=_cparams(("parallel", "parallel")),
        name="hgrn2",
    )(*([proj_l] * 5 + [proj_c] * 5), lb_f.reshape(1, w), lb_b.reshape(1, w), gain.reshape(1, HEAD_W))


def _half_mean_matrix():
    r = lax.broadcasted_iota(I32, (LANES, LANES), 0) // B_DH
    c = lax.broadcasted_iota(I32, (LANES, LANES), 1) // B_DH
    return (r == c).astype(BF16)


def _qk_prep(t, gain, bd, cos, sin):
    sq_hi, sq_lo = _split_bf16(t * t)
    ms = (_dot(sq_hi, bd) + _dot(sq_lo, bd)) * (1.0 / B_DH)
    y = t * lax.rsqrt(ms + EPS) * gain
    if cos is None:
        return y
    lane = lax.broadcasted_iota(I32, y.shape, 1)
    partner = jnp.where(lane % 2 == 0, pltpu.roll(y, LANES - 1, 1), pltpu.roll(y, 1, 1))
    return y * cos + partner * sin


def _attn_kernel(q_l, q_c, k_l, k_c, v_l, v_c, cosq, sinq, cosk, sink, gq_ref, gk_ref, go_ref, lam_ref,
                 o_l, o_c, kp, vp, *, out_scale):
    qb = pl.program_id(2)
    n_ctx = k_c.shape[0]
    bd = _half_mean_matrix()
    lane = lax.broadcasted_iota(I32, (1, LANES), 1)
    m1 = (lane < B_DH).astype(F32)
    m2 = 1.0 - m1
    lam = lam_ref[:, 0:1]

    @pl.when(qb == 0)
    def _():
        kp[0:n_ctx, :] = _qk_prep(k_c[...], gk_ref[...], bd, None, None).astype(BF16)
        kp[n_ctx:, :] = _qk_prep(k_l[...], gk_ref[...], bd, cosk[...], sink[...]).astype(BF16)
        vp[0:n_ctx, :] = v_c[...].astype(BF16)
        vp[n_ctx:, :] = v_l[...].astype(BF16)

    def attend(q, keys, vals):
        q = q * (B_DH ** -0.5)
        s1 = _dot_nt((q * m1).astype(BF16), keys)
        s2 = _dot_nt((q * m2).astype(BF16), keys)
        e1 = jnp.exp(s1 - jnp.max(s1, axis=-1, keepdims=True))
        e2 = jnp.exp(s2 - jnp.max(s2, axis=-1, keepdims=True))
        w = e1 * (1.0 / jnp.sum(e1, axis=-1, keepdims=True)) - e2 * (lam / jnp.sum(e2, axis=-1, keepdims=True))
        o = _dot(w.astype(BF16), vals)
        y = o * lax.rsqrt(jnp.mean(o * o, axis=-1, keepdims=True) + EPS) * go_ref[...]
        return (y * out_scale).astype(BF16)

    @pl.when(qb == 0)
    def _():
        q = _qk_prep(q_c[...], gq_ref[...], bd, None, None)
        o_c[...] = attend(q, kp[0:n_ctx, :], vp[0:n_ctx, :])

    @pl.when(qb > 0)
    def _():
        q = _qk_prep(q_l[...], gq_ref[...], bd, cosq[...], sinq[...])
        o_l[...] = attend(q, kp[...], vp[...])


def _rope_tables(n):
    n_rows = n // GRID_W
    rowp = jnp.repeat(jnp.arange(n_rows), GRID_W).astype(F32)
    colp = jnp.tile(jnp.arange(GRID_W), n_rows).astype(F32)
    pairs = B_DH // 4
    inv = ROPE_BASE ** (-jnp.arange(pairs, dtype=F32) / pairs)
    ang = jnp.concatenate([rowp[:, None] * inv, colp[:, None] * inv], axis=-1)
    cos = jnp.repeat(jnp.cos(ang), 2, axis=-1)
    sin = jnp.repeat(jnp.sin(ang), 2, axis=-1) * jnp.tile(jnp.array([-1.0, 1.0], F32), B_DH // 2)
    return jnp.tile(cos, (1, 2)), jnp.tile(sin, (1, 2))


def _attn(proj_l, proj_c, qk_gain, out_gain, lam, lam_init, bsz, tq=256):
    n = proj_l.shape[0] // bsz
    n_ctx = proj_c.shape[0] // bsz
    assert n_ctx == tq
    nqb = n // tq
    w = B_HEADS * HEAD_W
    cos, sin = _rope_tables(n)
    qcol, kcol, vcol = 5 * A_HEADS, 5 * A_HEADS + B_HEADS, 5 * A_HEADS + 2 * B_HEADS

    def lat_q(b, h, qb):
        return (b * nqb + jnp.maximum(qb - 1, 0), qcol + h)

    in_specs = [pl.BlockSpec((tq, HEAD_W), lat_q),
                pl.BlockSpec((n_ctx, HEAD_W), lambda b, h, qb: (b, qcol + h)),
                pl.BlockSpec((n, HEAD_W), lambda b, h, qb: (b, kcol + h)),
                pl.BlockSpec((n_ctx, HEAD_W), lambda b, h, qb: (b, kcol + h)),
                pl.BlockSpec((n, HEAD_W), lambda b, h, qb: (b, vcol + h)),
                pl.BlockSpec((n_ctx, HEAD_W), lambda b, h, qb: (b, vcol + h)),
                pl.BlockSpec((tq, LANES), lambda b, h, qb: (jnp.maximum(qb - 1, 0), 0)),
                pl.BlockSpec((tq, LANES), lambda b, h, qb: (jnp.maximum(qb - 1, 0), 0)),
                pl.BlockSpec((n, LANES), lambda b, h, qb: (0, 0)),
                pl.BlockSpec((n, LANES), lambda b, h, qb: (0, 0)),
                pl.BlockSpec((1, LANES), lambda b, h, qb: (0, 0)),
                pl.BlockSpec((1, LANES), lambda b, h, qb: (0, 0)),
                pl.BlockSpec((1, LANES), lambda b, h, qb: (0, 0)),
                pl.BlockSpec((1, LANES), lambda b, h, qb: (0, 0))]
    return pl.pallas_call(
        functools.partial(_attn_kernel, out_scale=1.0 - lam_init),
        grid=(bsz, B_HEADS, nqb + 1),
        in_specs=in_specs,
        out_specs=[pl.BlockSpec((tq, HEAD_W), lambda b, h, qb: (b * nqb + jnp.maximum(qb - 1, 0), h)),
                   pl.BlockSpec((n_ctx, HEAD_W), lambda b, h, qb: (b, h))],
        out_shape=[jax.ShapeDtypeStruct((bsz * n, w), BF16),
                   jax.ShapeDtypeStruct((bsz * n_ctx, w), BF16)],
        scratch_shapes=[pltpu.VMEM((n + n_ctx, HEAD_W), BF16), pltpu.VMEM((n + n_ctx, HEAD_W), BF16)],
        compiler_params=_cparams(("parallel", "parallel", "arbitrary")),
        name="diff_attn",
    )(proj_l, proj_c, proj_l, proj_c, proj_l, proj_c, cos, sin, cos, sin,
      jnp.tile(qk_gain[0], 2).reshape(1, LANES), jnp.tile(qk_gain[1], 2).reshape(1, LANES),
      out_gain.reshape(1, LANES), jnp.full((1, LANES), lam, F32))


def _seg_scan(a, b, reverse, seg):
    n = a.shape[0]
    rowi = lax.broadcasted_iota(I32, a.shape, 0)
    if seg is None:
        seg = n
    else:
        rowi = rowi % seg
    s = 1
    while s < seg:
        if reverse:
            keep = rowi < seg - s
            a_sh = pltpu.roll(a, n - s, 0)
            b_sh = pltpu.roll(b, n - s, 0)
        else:
            keep = rowi >= s
            a_sh = pltpu.roll(a, s, 0)
            b_sh = pltpu.roll(b, s, 0)
        b = jnp.where(keep, a * b_sh + b, b)
        a = jnp.where(keep, a * a_sh, a)
        s *= 2
    return a, b


def _rglru_kernel(y_l, u_l, u_c, cw_ref, cb_ref, gw_ref, gb_ref, lam_ref, o_ref,
                  upad, a_f, b_f, a_b, b_b, eb):
    n = u_l.shape[0]
    n_ctx = u_c.shape[0]
    tot = n + n_ctx
    blk = SUBLANES
    cw = cw_ref[...]
    cb = cb_ref[...]
    pad = SUBLANES

    def conv(u_ref, rows):
        upad[0:pad, :] = jnp.zeros((pad, LANES), F32)
        upad[pad:pad + rows, :] = u_ref[...]
        upad[pad + rows:pad + rows + pad, :] = jnp.zeros((pad, LANES), F32)
        acc = cb + jnp.zeros((rows, LANES), F32)
        for j in range(CONV_W):
            off = pad + j - CONV_W // 2
            acc = acc + cw[j:j + 1, :] * upad[off:off + rows, :]
        return acc

    def gates(uc, rows_f, rows_b):
        ub = uc.astype(BF16)
        for d, (a_ref, b_ref, rows) in enumerate(((a_f, b_f, rows_f), (a_b, b_b, rows_b))):
            lam = lam_ref[d:d + 1, :]
            neg_sp = -(jnp.maximum(-lam, 0.0) + jnp.log(1.0 + jnp.exp(-jnp.abs(lam))))
            r = _sigmoid(_dot(ub, gw_ref[d, 0].astype(BF16)) + gb_ref[d, 0:1, :])
            i = _sigmoid(_dot(ub, gw_ref[d, 1].astype(BF16)) + gb_ref[d, 1:2, :])
            log_a = (RG_C * neg_sp) * r
            a = jnp.exp(log_a)
            a_ref[rows, :] = a
            b_ref[rows, :] = jnp.sqrt((1.0 - a) * (1.0 + a)) * (i * uc)

    gates(conv(u_c, n_ctx), pl.ds(0, n_ctx), pl.ds(n, n_ctx))
    gates(conv(u_l, n), pl.ds(n_ctx, n), pl.ds(0, n))

    n_blk = tot // blk
    h_sum = None
    for a_ref, b_ref, reverse, lat0 in ((a_f, b_f, False, n_ctx), (a_b, b_b, True, 0)):
        chunk = 256
        for c0 in range(0, tot, chunk):
            a1, b1 = _seg_scan(a_ref[c0:c0 + chunk, :], b_ref[c0:c0 + chunk, :], reverse, blk)
            a_ref[c0:c0 + chunk, :] = a1
            b_ref[c0:c0 + chunk, :] = b1
        end = 0 if reverse else blk - 1
        ae = a_ref[pl.ds(end, n_blk, stride=blk), :]
        be = b_ref[pl.ds(end, n_blk, stride=blk), :]
        _, he = _seg_scan(ae, be, reverse, None)
        zero = jnp.zeros((1, LANES), F32)
        if reverse:
            eb[0:n_blk - 1, :] = he[1:, :]
            eb[n_blk - 1:n_blk, :] = zero
        else:
            eb[0:1, :] = zero
            eb[1:n_blk, :] = he[:n_blk - 1, :]

        def apply(j, carry, a_ref=a_ref, b_ref=b_ref):
            rows = pl.ds(pl.multiple_of(j * blk, blk), blk)
            b_ref[rows, :] = b_ref[rows, :] + a_ref[rows, :] * eb[pl.ds(j, 1), :]
            return carry

        lax.fori_loop(0, n_blk, apply, 0, unroll=8)
        h = b_ref[lat0:lat0 + n, :]
        h_sum = h if h_sum is None else h_sum + h

    y = y_l[...]
    gelu = 0.5 * y * (1.0 + jnp.tanh(0.7978845608028654 * (y + 0.044715 * y * y * y)))
    o_ref[...] = (h_sum * gelu).astype(o_ref.dtype)


def _rglru(proj_l, proj_c, conv_w, conv_b, gate_w, gate_b, lam, bsz):
    n = proj_l.shape[0] // bsz
    n_ctx = proj_c.shape[0] // bsz
    w = RG_HEADS * HEAD_W
    tot = n + n_ctx
    return pl.pallas_call(
        _rglru_kernel,
        grid=(bsz, RG_HEADS),
        in_specs=[pl.BlockSpec((n, HEAD_W), lambda b, h: (b, h)),
                  pl.BlockSpec((n, HEAD_W), lambda b, h: (b, RG_HEADS + h)),
                  pl.BlockSpec((n_ctx, HEAD_W), lambda b, h: (b, RG_HEADS + h)),
                  pl.BlockSpec((CONV_W, HEAD_W), lambda b, h: (0, h)),
                  pl.BlockSpec((1, HEAD_W), lambda b, h: (0, h)),
                  pl.BlockSpec((2, 2, None, HEAD_W, HEAD_W), lambda b, h: (0, 0, h, 0, 0)),
                  pl.BlockSpec((2, 2, HEAD_W), lambda b, h: (0, 0, h)),
                  pl.BlockSpec((2, HEAD_W), lambda b, h: (0, h))],
        out_specs=pl.BlockSpec((n, HEAD_W), lambda b, h: (b, h)),
        out_shape=jax.ShapeDtypeStruct((bsz * n, w), BF16),
        scratch_shapes=[pltpu.VMEM((n + 2 * SUBLANES, LANES), F32)]
                       + [pltpu.VMEM((tot, LANES), F32)] * 4
                       + [pltpu.VMEM((tot // SUBLANES, LANES), F32)],
        compiler_params=_cparams(("parallel", "parallel")),
        name="rglru",
    )(proj_l, proj_l, proj_c, conv_w, conv_b.reshape(1, w), gate_w, gate_b, lam)


def _out_proj_kernel(*refs, n_mix, n_groups, epg):
    mix_refs = refs[:n_mix]
    (w_ref, x_ref, gate_ref, g_ref, sh_ref, sc_ref, wr_hi_ref, wr_lo_ref, rb_ref,
     xo_ref, f_ref, gid_ref) = refs[n_mix:]
    d = x_ref.shape[1]
    k0 = 0
    mix = None
    for m_ref in mix_refs:
        kk = m_ref.shape[1]
        part = _dot(m_ref[...], w_ref[k0:k0 + kk, :])
        mix = part if mix is None else mix + part
        k0 += kk
    x = x_ref[...] + gate_ref[0] * mix
    xo_ref[...] = x
    y = x * lax.rsqrt(jnp.mean(x * x, axis=-1, keepdims=True) + EPS) * g_ref[...]
    f = y * (1.0 + sc_ref[0]) + sh_ref[0]
    f_ref[:, 0:d] = f

    f_hi, f_lo = _split_bf16(f)
    lg = _dot(f_hi, wr_hi_ref[...]) + _dot(f_lo, wr_hi_ref[...]) + _dot(f_hi, wr_lo_ref[...]) + rb_ref[...]
    lane = lax.broadcasted_iota(I32, lg.shape, 1).astype(F32)
    neg = -jnp.inf
    big = float(ROUTE_W)
    gl = jnp.where(lane < n_groups, lg, neg)
    gmax = jnp.max(gl, axis=-1, keepdims=True)
    gidx = jnp.min(jnp.where(gl == gmax, lane, big), axis=-1, keepdims=True)
    gw = 1.0 / jnp.sum(jnp.exp(gl - gmax), axis=-1, keepdims=True)
    base = n_groups + gidx * epg
    el = jnp.where((lane >= base) & (lane < base + epg), lg, neg)
    v1 = jnp.max(el, axis=-1, keepdims=True)
    i1 = jnp.min(jnp.where(el == v1, lane, big), axis=-1, keepdims=True)
    el2 = jnp.where(lane == i1, neg, el)
    v2 = jnp.max(el2, axis=-1, keepdims=True)
    i2 = jnp.min(jnp.where(el2 == v2, lane, big), axis=-1, keepdims=True)
    t = jnp.exp(v2 - v1)
    w1 = gw / (1.0 + t)
    w2 = w1 * t
    slot = lane + base
    f_ref[:, d:] = jnp.where(slot == i1, w1, 0.0) + jnp.where(slot == i2, w2, 0.0)
    gid_ref[...] = jnp.broadcast_to(gidx, gid_ref.shape).astype(I32)


def _out_proj(mixes, w_out, x2, gain, mods, w_router, b_router, rows_per_mod, mod_base, tm=256):
    r, d = x2.shape
    k = w_out.shape[0]
    tpm = rows_per_mod // tm
    wr_hi, wr_lo = _split_bf16(w_router)

    def mrow(i):
        return (mod_base + i // tpm) * N_MOD

    const2 = lambda i: (0, 0)
    in_specs = ([pl.BlockSpec((tm, m.shape[1]), lambda i: (i, 0)) for m in mixes]
                + [pl.BlockSpec((k, d), const2),
                   pl.BlockSpec((tm, d), lambda i: (i, 0)),
                   pl.BlockSpec((1, 1, d), lambda i: (mrow(i) + 2, 0, 0)),
                   pl.BlockSpec((1, d), const2),
                   pl.BlockSpec((1, 1, d), lambda i: (mrow(i) + 3, 0, 0)),
                   pl.BlockSpec((1, 1, d), lambda i: (mrow(i) + 4, 0, 0)),
                   pl.BlockSpec((d, ROUTE_W), const2),
                   pl.BlockSpec((d, ROUTE_W), const2),
                   pl.BlockSpec((1, ROUTE_W), const2)])
    return pl.pallas_call(
        functools.partial(_out_proj_kernel, n_mix=len(mixes), n_groups=N_GROUPS, epg=EXPERTS_PER_GROUP),
        grid=(r // tm,),
        in_specs=in_specs,
        out_specs=[pl.BlockSpec((tm, d), lambda i: (i, 0)),
                   pl.BlockSpec((tm, d + ROUTE_W), lambda i: (i, 0)),
                   pl.BlockSpec((tm, ROUTE_W), lambda i: (i, 0))],
        out_shape=[jax.ShapeDtypeStruct((r, d), F32),
                   jax.ShapeDtypeStruct((r, d + ROUTE_W), F32),
                   jax.ShapeDtypeStruct((r, ROUTE_W), I32)],
        compiler_params=_cparams(("parallel",)),
        name="out_proj",
    )(*mixes, w_out, x2, mods, gain.reshape(1, d), mods, mods, wr_hi, wr_lo, b_router)


def _row_copy(src, s, dst, t, sem):
    return pltpu.make_async_copy(src.at[pl.ds(s, 1), :], dst.at[pl.ds(t, 1), :], sem)


def _dispatch_kernel(*refs, n_src, tm):
    pos_refs = refs[:n_src]
    src_refs = refs[n_src:2 * n_src]
    dst = refs[2 * n_src + 1]
    sem = refs[2 * n_src + 2]
    i = pl.program_id(0)
    n_steps = pl.num_programs(0)
    tiles = [s.shape[0] // tm for s in src_refs]

    def wait_tile():
        def body(j, carry):
            _row_copy(src_refs[0], 0, dst, 0, sem).wait()
            return carry
        lax.fori_loop(0, tm, body, 0)

    first = 0
    for pos_ref, src, nt in zip(pos_refs, src_refs, tiles):
        @pl.when((i >= first) & (i < first + nt))
        def _(pos_ref=pos_ref, src=src, first=first):
            base = (i - first) * tm

            def body(j, carry):
                _row_copy(src, base + j, dst, pos_ref[0, 0, j], sem).start()
                return carry
            lax.fori_loop(0, tm, body, 0)
        first += nt

    @pl.when(i > 0)
    def _():
        wait_tile()

    @pl.when(i == n_steps - 1)
    def _():
        wait_tile()


def _dispatch(srcs, poss, zeros_sorted, tm=256):
    n_src = len(srcs)
    tiles = [s.shape[0] // tm for s in srcs]
    n_steps = sum(tiles)
    firsts = [sum(tiles[:k]) for k in range(n_src)]

    def pos_spec(k):
        return pl.BlockSpec((1, 1, tm), lambda i: (jnp.clip(i - firsts[k], 0, tiles[k] - 1), 0, 0),
                            memory_space=pltpu.SMEM)

    return pl.pallas_call(
        functools.partial(_dispatch_kernel, n_src=n_src, tm=tm),
        grid=(n_steps,),
        in_specs=[pos_spec(k) for k in range(n_src)]
                 + [pl.BlockSpec(memory_space=pl.ANY)] * (n_src + 1),
        out_specs=pl.BlockSpec(memory_space=pl.ANY),
        out_shape=jax.ShapeDtypeStruct(zeros_sorted.shape, zeros_sorted.dtype),
        scratch_shapes=[pltpu.SemaphoreType.DMA(())],
        input_output_aliases={2 * n_src: 0},
        compiler_params=_cparams(("arbitrary",)),
        name="moe_dispatch",
    )(*[p.reshape(t, 1, tm) for p, t in zip(poss, tiles)], *srcs, zeros_sorted)


def _moe_kernel(tg_ref, tv_ref, x_ref, w1_ref, w3_ref, w2_ref, o_ref, acc_ref):
    t = pl.program_id(0)
    e = pl.program_id(1)
    d = o_ref.shape[1]

    @pl.when(e == 0)
    def _():
        acc_ref[...] = jnp.zeros_like(acc_ref)

    @pl.when(tv_ref[t] > 0)
    def _():
        x = x_ref[:, 0:d].astype(BF16)
        h1 = _dot(x, w1_ref[...])
        h3 = _dot(x, w3_ref[...])
        cw = x_ref[:, d:]
        lane = lax.broadcasted_iota(I32, cw.shape, 1)
        cwe = jnp.sum(jnp.where(lane == e, cw, 0.0), axis=-1, keepdims=True)
        hid = (h1 * _sigmoid(h1) * h3 * cwe).astype(BF16)
        acc_ref[...] += _dot(hid, w2_ref[...])

    @pl.when(e == EXPERTS_PER_GROUP - 1)
    def _():
        o_ref[...] = acc_ref[...]


def _moe_ffn(tile_group, tile_valid, f_sorted, w1, w3, w2, tm):
    p, dw = f_sorted.shape
    d = dw - ROUTE_W
    fe = w1.shape[2]
    epg = EXPERTS_PER_GROUP
    grid_spec = pltpu.PrefetchScalarGridSpec(
        num_scalar_prefetch=2,
        grid=(p // tm, epg),
        in_specs=[pl.BlockSpec((tm, dw), lambda t, e, tg, tv: (t, 0)),
                  pl.BlockSpec((None, d, fe), lambda t, e, tg, tv: (tg[t] * epg + e, 0, 0)),
                  pl.BlockSpec((None, d, fe), lambda t, e, tg, tv: (tg[t] * epg + e, 0, 0)),
                  pl.BlockSpec((None, fe, d), lambda t, e, tg, tv: (tg[t] * epg + e, 0, 0))],
        out_specs=pl.BlockSpec((tm, d), lambda t, e, tg, tv: (t, 0)),
        scratch_shapes=[pltpu.VMEM((tm, d), F32)],
    )
    return pl.pallas_call(
        _moe_kernel,
        grid_spec=grid_spec,
        out_shape=jax.ShapeDtypeStruct((p, d), F32),
        compiler_params=_cparams(("arbitrary", "arbitrary")),
        name="moe_ffn",
    )(tile_group, tile_valid, f_sorted, w1, w3, w2)


def _combine_kernel(pos_cur, pos_nxt, y_hbm, x_ref, gate_ref, o_ref, ybuf, sem, *, tm):
    i = pl.program_id(0)
    n_steps = pl.num_programs(0)

    def issue(pos_ref, slot):
        def body(j, carry):
            _row_copy(y_hbm, pos_ref[0, 0, j], ybuf.at[slot], j, sem.at[slot]).start()
            return carry
        lax.fori_loop(0, tm, body, 0)

    @pl.when(i == 0)
    def _():
        issue(pos_cur, 0)

    @pl.when(i + 1 < n_steps)
    def _():
        issue(pos_nxt, (i + 1) % 2)

    slot = i % 2

    def wbody(j, carry):
        _row_copy(y_hbm, 0, ybuf.at[slot], 0, sem.at[slot]).wait()
        return carry
    lax.fori_loop(0, tm, wbody, 0)
    o_ref[...] = x_ref[...] + gate_ref[0] * ybuf[slot]


def _combine(y_sorted, pos, x2, mods, rows_per_mod, mod_base, tm=256):
    r, d = x2.shape
    n_steps = r // tm
    tpm = rows_per_mod // tm
    pos3 = pos.reshape(n_steps, 1, tm)
    return pl.pallas_call(
        functools.partial(_combine_kernel, tm=tm),
        grid=(n_steps,),
        in_specs=[pl.BlockSpec((1, 1, tm), lambda i: (i, 0, 0), memory_space=pltpu.SMEM),
                  pl.BlockSpec((1, 1, tm), lambda i: (jnp.minimum(i + 1, n_steps - 1), 0, 0),
                               memory_space=pltpu.SMEM),
                  pl.BlockSpec(memory_space=pl.ANY),
                  pl.BlockSpec((tm, d), lambda i: (i, 0)),
                  pl.BlockSpec((1, 1, d), lambda i: ((mod_base + i // tpm) * N_MOD + 5, 0, 0))],
        out_specs=pl.BlockSpec((tm, d), lambda i: (i, 0)),
        out_shape=jax.ShapeDtypeStruct((r, d), F32),
        scratch_shapes=[pltpu.VMEM((2, tm, d), F32), pltpu.SemaphoreType.DMA((2,))],
        compiler_params=_cparams(("arbitrary",)),
        name="moe_combine",
    )(pos3, pos3, y_sorted, x2, mods)


def _sorted_positions(gids, tm):
    g = jnp.concatenate(gids)
    r = g.shape[0]
    onehot = (g[:, None] == jnp.arange(N_GROUPS, dtype=I32)[None, :]).astype(I32)
    counts = jnp.sum(onehot, axis=0)
    rank = jnp.sum((jnp.cumsum(onehot, axis=0) - 1) * onehot, axis=1)
    padded = ((counts + tm - 1) // tm) * tm
    ends = jnp.cumsum(padded)
    pos = jnp.sum(onehot * (ends - padded)[None, :], axis=1) + rank
    p = r + N_GROUPS * tm
    tile_start = jnp.arange(p // tm, dtype=I32) * tm
    tile_group = jnp.minimum(jnp.sum((tile_start[:, None] >= ends[None, :]).astype(I32), axis=1), N_GROUPS - 1)
    tile_valid = (tile_start < ends[-1]).astype(I32)
    return pos.astype(I32), tile_group.astype(I32), tile_valid, p


def _moe(fexts, gids, w1, w3, w2, tm=512):
    pos, tile_group, tile_valid, p = _sorted_positions([g[:, 0] for g in gids], tm)
    sizes = [f.shape[0] for f in fexts]
    poss, off = [], 0
    for s in sizes:
        poss.append(lax.slice(pos, (off,), (off + s,)))
        off += s
    f_sorted = _dispatch(fexts, poss, jnp.zeros((p, fexts[0].shape[1]), F32))
    y_sorted = _moe_ffn(tile_group, tile_valid, f_sorted, w1, w3, w2, tm)
    return y_sorted, poss


def kernel(x, c, ctx, c_ctx, ada_w, ada_b, norm_mix, norm_ffn, even_w_in, even_w_out, hgrn_lb_logits, hgrn_out_norm, diff_qk_norm, diff_lambda, diff_out_norm, odd_w_in, odd_conv_w, odd_conv_b, rg_gate_w, rg_gate_b, rg_lambda, odd_w_out, moe_w_grp, moe_b_grp, moe_w_exp, moe_b_exp, moe_w1, moe_w3, moe_w2):
    bsz, n, d = x.shape
    n_ctx = ctx.shape[1]
    depth = ada_w.shape[0]
    assert depth == 2

    cvec = jnp.zeros((MOD_ROWS, d), F32).at[:bsz].set(c).at[bsz].set(c_ctx)
    mods_all = _ada_all(cvec, ada_w, ada_b).reshape(depth, MOD_ROWS * N_MOD, 1, d)
    x_l = x.reshape(bsz * n, d)
    x_c = ctx.reshape(bsz * n_ctx, d)
    lb_all = jnp.cumsum(jax.nn.softmax(hgrn_lb_logits.astype(F32), axis=1), axis=1)

    def router(l):
        w = jnp.concatenate([moe_w_grp[l], moe_w_exp[l]], axis=1)
        b = jnp.concatenate([moe_b_grp[l], moe_b_exp[l]])
        padw = ROUTE_W - w.shape[1]
        return jnp.pad(w, ((0, 0), (0, padw))), jnp.pad(b, (0, padw)).reshape(1, ROUTE_W)

    l = 0
    mods = mods_all[l]
    lam_init = 0.8 - 0.6 * math.exp(-0.3 * l)
    lv = diff_lambda[0].astype(F32)
    lam = jnp.exp(jnp.sum(lv[0] * lv[1])) - jnp.exp(jnp.sum(lv[2] * lv[3])) + lam_init
    w_in = even_w_in[0].astype(BF16)
    proj_l = _norm_mod_mm(x_l, norm_mix[l], mods, w_in, n, 0, tm=1024)
    proj_c = _norm_mod_mm(x_c, norm_mix[l], mods, w_in, bsz * n_ctx, bsz, tm=1024)
    a_l, a_c = _hgrn(proj_l, proj_c, lb_all[0, 0], lb_all[1, 0], hgrn_out_norm[0], bsz)
    b_l, b_c = _attn(proj_l, proj_c, diff_qk_norm[0], diff_out_norm[0], lam, lam_init, bsz)
    w_out = even_w_out[0].astype(BF16)
    wr, br = router(l)
    x_l, f_l, g_l = _out_proj([a_l, b_l], w_out, x_l, norm_ffn[l], mods, wr, br, n, 0)
    x_c, f_c, g_c = _out_proj([a_c, b_c], w_out, x_c, norm_ffn[l], mods, wr, br, bsz * n_ctx, bsz)
    y_sorted, (pos_l, pos_c) = _moe([f_l, f_c], [g_l, g_c], moe_w1[l].astype(BF16), moe_w3[l].astype(BF16),
                                    moe_w2[l].astype(BF16))
    x_l = _combine(y_sorted, pos_l, x_l, mods, n, 0)
    x_c = _combine(y_sorted, pos_c, x_c, mods, bsz * n_ctx, bsz)

    l = 1
    mods = mods_all[l]
    w_in = odd_w_in[0].astype(BF16)
    proj_l = _norm_mod_mm(x_l, norm_mix[l], mods, w_in, n, 0, tm=1024)
    proj_c = _norm_mod_mm(x_c, norm_mix[l], mods, w_in, bsz * n_ctx, bsz, tm=1024)
    gated = _rglru(proj_l, proj_c, odd_conv_w[0], odd_conv_b[0], rg_gate_w[0], rg_gate_b[0], rg_lambda[0], bsz)
    wr, br = router(l)
    x_l, f_l, g_l = _out_proj([gated], odd_w_out[0].astype(BF16), x_l, norm_ffn[l], mods, wr, br, n, 0)
    y_sorted, (pos_l,) = _moe([f_l], [g_l], moe_w1[l].astype(BF16), moe_w3[l].astype(BF16),
                              moe_w2[l].astype(BF16))
    return _combine(y_sorted, pos_l, x_l, mods, n, 0).reshape(bsz, n, d)
```

```python
import functools
import math

import jax
import jax.numpy as jnp
from jax import lax
from jax.experimental import pallas as pl
from jax.experimental.pallas import tpu as pltpu

F32 = jnp.float32
BF16 = jnp.bfloat16
I32 = jnp.int32

EPS = 1e-6
LOG2_E = 1.4426950408889634
TINY = 1e-30
LANES = 128
SUBLANES = 8
GRID_W = 64
A_HEADS = 8
HEAD_W = 128
HGRN_CHUNK = 32
HGRN_BLOCK = 256
B_HEADS = 8
B_DH = 64
ROPE_BASE = 10000.0
RG_HEADS = 16
CONV_W = 4
RG_C = 8.0
N_GROUPS = 4
EXPERTS_PER_GROUP = 4
N_EXPERTS = 16
N_MOD = 6
MOD_ROWS = 16
ROUTE_W = LANES
ROW_DMA_UNROLL = 8

VMEM_LIMIT = 56 * 1024 * 1024


def _cparams(sem):
    return pltpu.CompilerParams(dimension_semantics=sem, vmem_limit_bytes=VMEM_LIMIT)


def _sigmoid(x):
    return 0.5 * jnp.tanh(0.5 * x) + 0.5


def _dot(a, b):
    return jnp.dot(a, b, preferred_element_type=F32)


def _dot_nt(a, b):
    return lax.dot_general(a, b, (((1,), (1,)), ((), ())), preferred_element_type=F32)


def _dot_tn(a, b):
    return lax.dot_general(a, b, (((0,), (0,)), ((), ())), preferred_element_type=F32)


def _split_bf16(x):
    hi = x.astype(BF16)
    return hi, (x - hi.astype(F32)).astype(BF16)


def _ada_kernel(c_ref, w_ref, b_ref, o_ref):
    c = c_ref[...]
    a = (c * _sigmoid(c)).astype(BF16)
    o_ref[...] = _dot(a, w_ref[...].astype(BF16)) + b_ref[...]


def _ada_all(cvec, ada_w, ada_b, tn=1024):
    depth, d, n = ada_w.shape
    return pl.pallas_call(
        _ada_kernel,
        grid=(depth, n // tn),
        in_specs=[pl.BlockSpec((MOD_ROWS, d), lambda l, j: (0, 0)),
                  pl.BlockSpec((None, d, tn), lambda l, j: (l, 0, j)),
                  pl.BlockSpec((None, 1, tn), lambda l, j: (l, 0, j))],
        out_specs=pl.BlockSpec((None, MOD_ROWS, tn), lambda l, j: (l, 0, j)),
        out_shape=jax.ShapeDtypeStruct((depth, MOD_ROWS, n), F32),
        compiler_params=_cparams(("arbitrary", "arbitrary")),
        name="ada_mod",
    )(cvec, ada_w, ada_b.reshape(depth, 1, n))


def _norm_mod_mm_kernel(x_ref, g_ref, sh_ref, sc_ref, w_ref, o_ref, h_scr):
    @pl.when(pl.program_id(1) == 0)
    def _():
        x = x_ref[...]
        y = x * lax.rsqrt(jnp.mean(x * x, axis=-1, keepdims=True) + EPS) * g_ref[...]
        h_scr[...] = (y * (1.0 + sc_ref[0]) + sh_ref[0]).astype(BF16)

    o_ref[...] = _dot(h_scr[...], w_ref[...]).astype(o_ref.dtype)


def _norm_mod_mm(x2, gain, mods, w, rows_per_mod, mod_base, tm, tn=512):
    r, d = x2.shape
    n = w.shape[1]
    tpm = rows_per_mod // tm

    def mrow(i):
        return (mod_base + i // tpm) * N_MOD

    return pl.pallas_call(
        _norm_mod_mm_kernel,
        grid=(r // tm, n // tn),
        in_specs=[pl.BlockSpec((tm, d), lambda i, j: (i, 0)),
                  pl.BlockSpec((1, d), lambda i, j: (0, 0)),
                  pl.BlockSpec((1, 1, d), lambda i, j: (mrow(i) + 0, 0, 0)),
                  pl.BlockSpec((1, 1, d), lambda i, j: (mrow(i) + 1, 0, 0)),
                  pl.BlockSpec((d, tn), lambda i, j: (0, j))],
        out_specs=pl.BlockSpec((tm, tn), lambda i, j: (i, j)),
        out_shape=jax.ShapeDtypeStruct((r, n), F32),
        scratch_shapes=[pltpu.VMEM((tm, d), BF16)],
        compiler_params=_cparams(("parallel", "arbitrary")),
        name="norm_mod_mm",
    )(x2, gain.reshape(1, d), mods, mods, w)


def _hgrn_kernel(ql, ffl, fbl, vl, gl, qc, ffc, fbc, vc, gc, lbf_ref, lbb_ref, gain_ref,
                 ol_ref, oc_ref,
                 oi_l, oi_c, qtf_l, qtb_l, qtf_c, qtb_c, kv_f, kv_b, dec_f, dec_b):
    c_sz = HGRN_CHUNK
    blk = HGRN_BLOCK
    cpb = blk // c_sz
    nb_l = ql.shape[0] // blk
    nc_l = ql.shape[0] // c_sz
    nc_c = qc.shape[0] // c_sz
    nc = nc_l + nc_c
    assert qc.shape[0] == blk
    row = lax.broadcasted_iota(I32, (blk, blk), 0)
    col = lax.broadcasted_iota(I32, (blk, blk), 1)
    same = (row // c_sz) == (col // c_sz)
    tril = same & (row >= col)
    triu = same & (row <= col)
    lbf = lbf_ref[...]
    lbb = lbb_ref[...]

    exp_mask = (lax.broadcasted_iota(I32, (blk, cpb * HEAD_W), 0) // c_sz
                == lax.broadcasted_iota(I32, (blk, cpb * HEAD_W), 1) // HEAD_W)

    def local_terms(r0, q_ref, ff_ref, fb_ref, v_ref, oi_ref, qtf_ref, qtb_ref, gid_f, gid_b):
        rows = pl.ds(r0, blk)
        q = q_ref[rows, :]
        qs = q * _sigmoid(q)
        vb = v_ref[rows, :].astype(BF16)
        vt = v_ref[rows, :].T.astype(BF16)
        o_sum = jnp.zeros((blk, HEAD_W), F32)
        for f_ref, lb, mask, last, qt_ref, kv_ref, dec_ref, gid in (
                (ff_ref, lbf, tril, c_sz - 1, qtf_ref, kv_f, dec_f, gid_f),
                (fb_ref, lbb, triu, 0, qtb_ref, kv_b, dec_b, gid_b)):
            f = lb + (1.0 - lb) * _sigmoid(f_ref[rows, :])
            k = 1.0 - f
            lf_hi, lf_lo = _split_bf16(jnp.log(f))
            lf2 = jnp.concatenate([lf_hi, lf_lo], axis=1)
            cum2 = _dot(jnp.where(mask, 1.0, 0.0).astype(BF16), lf2)
            cum = cum2[:, :HEAD_W] + cum2[:, HEAD_W:]
            tot = jnp.concatenate(
                [jnp.broadcast_to(cum[c * c_sz + last:c * c_sz + last + 1, :], (c_sz, HEAD_W))
                 for c in range(cpb)], axis=0)
            e = jnp.exp(cum)
            qt = (qs * e).astype(BF16)
            kt = (k * jnp.exp(-cum)).astype(BF16)
            att = jnp.where(mask, _dot_nt(qt, kt), 0.0)
            o_sum = o_sum + _dot(att.astype(BF16), vb)
            k2 = (k * jnp.exp(tot - cum)).astype(BF16)
            k2x = jnp.where(exp_mask, jnp.concatenate([k2] * cpb, axis=1), jnp.zeros((), BF16))
            kvs = _dot(vt, k2x)
            for c in range(cpb):
                kv_ref[gid + c] = kvs[:, c * HEAD_W:(c + 1) * HEAD_W]
                dec_ref[gid + c] = e[c * c_sz + last:c * c_sz + last + 1, :]
            qt_ref[rows, :] = qt
        oi_ref[rows, :] = o_sum

    local_terms(0, qc, ffc, fbc, vc, oi_c, qtf_c, qtb_c, 0, nc_l)

    def local_l(i, carry):
        local_terms(pl.multiple_of(i * blk, blk), ql, ffl, fbl, vl, oi_l, qtf_l, qtb_l,
                    nc_c + i * cpb, i * cpb)
        return carry

    lax.fori_loop(0, nb_l, local_l, 0, unroll=2)

    def rec_f(i, s):
        new = s * dec_f[i] + kv_f[i]
        kv_f[i] = s
        return new

    def rec_b(i, s):
        j = nc - 1 - i
        new = s * dec_b[j] + kv_b[j]
        kv_b[j] = s
        return new

    s0 = jnp.zeros((HEAD_W, HEAD_W), F32)
    lax.fori_loop(0, nc, rec_f, s0, unroll=2)
    lax.fori_loop(0, nc, rec_b, s0, unroll=2)

    gain = gain_ref[...]

    def finish(r0, g_ref, oi_ref, qtf_ref, qtb_ref, o_ref, gid_f, gid_b):
        parts = []
        for c in range(cpb):
            rows_c = pl.ds(r0 + c * c_sz, c_sz)
            parts.append(_dot_nt(qtf_ref[rows_c, :], kv_f[gid_f + c].astype(BF16))
                         + _dot_nt(qtb_ref[rows_c, :], kv_b[gid_b + c].astype(BF16)))
        rows = pl.ds(r0, blk)
        o = oi_ref[rows, :] + jnp.concatenate(parts, axis=0)
        y = o * lax.rsqrt(jnp.mean(o * o, axis=-1, keepdims=True) + EPS) * gain
        g = g_ref[rows, :]
        o_ref[rows, :] = (y * (g * _sigmoid(g))).astype(o_ref.dtype)

    finish(0, gc, oi_c, qtf_c, qtb_c, oc_ref, 0, nc_l)

    def fin_l(i, carry):
        finish(pl.multiple_of(i * blk, blk), gl, oi_l, qtf_l, qtb_l, ol_ref, nc_c + i * cpb, i * cpb)
        return carry

    lax.fori_loop(0, nb_l, fin_l, 0, unroll=4)


def _hgrn(proj_l, proj_c, lb_f, lb_b, gain, bsz):
    n = proj_l.shape[0] // bsz
    n_ctx = proj_c.shape[0] // bsz
    nc = (n + n_ctx) // HGRN_CHUNK
    w = A_HEADS * HEAD_W

    def col(k):
        return lambda b, h: (b, k * A_HEADS + h)

    in_specs = ([pl.BlockSpec((n, HEAD_W), col(k)) for k in range(5)]
                + [pl.BlockSpec((n_ctx, HEAD_W), col(k)) for k in range(5)]
                + [pl.BlockSpec((1, HEAD_W), lambda b, h: (0, h)),
                   pl.BlockSpec((1, HEAD_W), lambda b, h: (0, h)),
                   pl.BlockSpec((1, HEAD_W), lambda b, h: (0, 0))])
    return pl.pallas_call(
        _hgrn_kernel,
        grid=(bsz, A_HEADS),
        in_specs=in_specs,
        out_specs=[pl.BlockSpec((n, HEAD_W), lambda b, h: (b, h)),
                   pl.BlockSpec((n_ctx, HEAD_W), lambda b, h: (b, h))],
        out_shape=[jax.ShapeDtypeStruct((bsz * n, w), BF16),
                   jax.ShapeDtypeStruct((bsz * n_ctx, w), BF16)],
        scratch_shapes=[pltpu.VMEM((n, HEAD_W), F32), pltpu.VMEM((n_ctx, HEAD_W), F32),
                        pltpu.VMEM((n, HEAD_W), BF16), pltpu.VMEM((n, HEAD_W), BF16),
                        pltpu.VMEM((n_ctx, HEAD_W), BF16), pltpu.VMEM((n_ctx, HEAD_W), BF16),
                        pltpu.VMEM((nc, HEAD_W, HEAD_W), F32), pltpu.VMEM((nc, HEAD_W, HEAD_W), F32),
                        pltpu.VMEM((nc, 1, HEAD_W), F32), pltpu.VMEM((nc, 1, HEAD_W), F32)],
        compiler_params=_cparams(("parallel", "parallel")),
        name="hgrn2",
    )(*([proj_l] * 5 + [proj_c] * 5), lb_f.reshape(1, w), lb_b.reshape(1, w), gain.reshape(1, HEAD_W))


def _half_mean_matrix():
    r = lax.broadcasted_iota(I32, (LANES, LANES), 0) // B_DH
    c = lax.broadcasted_iota(I32, (LANES, LANES), 1) // B_DH
    return (r == c).astype(BF16)


def _qk_prep(t, gain, bd, cos, sin):
    sq_hi, sq_lo = _split_bf16(t * t)
    ms = (_dot(sq_hi, bd) + _dot(sq_lo, bd)) * (1.0 / B_DH)
    y = t * lax.rsqrt(ms + EPS) * gain
    if cos is None:
        return y
    lane = lax.broadcasted_iota(I32, y.shape, 1)
    partner = jnp.where(lane % 2 == 0, pltpu.roll(y, LANES - 1, 1), pltpu.roll(y, 1, 1))
    return y * cos + partner * sin


def _attn_kernel(q_l, q_c, k_l, k_c, v_l, v_c, cosq, sinq, cosk, sink, gq_ref, gk_ref, go_ref, lam_ref,
                 o_l, o_c, kp, vp, *, out_scale):
    qb = pl.program_id(2)
    n_ctx = k_c.shape[0]
    bd = _half_mean_matrix()
    lane = lax.broadcasted_iota(I32, (1, LANES), 1)
    m1 = (lane < B_DH).astype(F32)
    m2 = 1.0 - m1
    lam = lam_ref[:, 0:1]

    @pl.when(qb == 0)
    def _():
        kp[0:n_ctx, :] = _qk_prep(k_c[...], gk_ref[...], bd, None, None).astype(BF16)
        kp[n_ctx:, :] = _qk_prep(k_l[...], gk_ref[...], bd, cosk[...], sink[...]).astype(BF16)
        ones_col = (lax.broadcasted_iota(I32, (1, LANES), 1) == 0).astype(BF16)
        vp[0:n_ctx, 0:HEAD_W] = v_c[...].astype(BF16)
        vp[n_ctx:, 0:HEAD_W] = v_l[...].astype(BF16)
        vp[:, HEAD_W:] = jnp.broadcast_to(ones_col, (vp.shape[0], LANES))

    def attend(q, k_ref, v_ref, n_keys):
        q = q * (B_DH ** -0.5 * LOG2_E)
        o = None
        for msk, coef in ((m1, 1.0), (m2, -lam)):
            s = _dot_nt((q * msk).astype(BF16), k_ref[0:n_keys, :])
            e = jnp.exp2(s - jnp.max(s, axis=-1, keepdims=True)).astype(BF16)
            r = _dot(e, v_ref[0:n_keys, :])
            part = r[:, 0:HEAD_W] * (coef / r[:, HEAD_W:HEAD_W + 1])
            o = part if o is None else o + part
        y = o * lax.rsqrt(jnp.mean(o * o, axis=-1, keepdims=True) + EPS) * go_ref[...]
        return (y * out_scale).astype(BF16)

    @pl.when(qb == 0)
    def _():
        q = _qk_prep(q_c[...], gq_ref[...], bd, None, None)
        o_c[...] = attend(q, kp, vp, n_ctx)

    @pl.when(qb > 0)
    def _():
        q = _qk_prep(q_l[...], gq_ref[...], bd, cosq[...], sinq[...])
        o_l[...] = attend(q, kp, vp, kp.shape[0])


def _rope_tables(n):
    n_rows = n // GRID_W
    rowp = jnp.repeat(jnp.arange(n_rows), GRID_W).astype(F32)
    colp = jnp.tile(jnp.arange(GRID_W), n_rows).astype(F32)
    pairs = B_DH // 4
    inv = ROPE_BASE ** (-jnp.arange(pairs, dtype=F32) / pairs)
    ang = jnp.concatenate([rowp[:, None] * inv, colp[:, None] * inv], axis=-1)
    cos = jnp.repeat(jnp.cos(ang), 2, axis=-1)
    sin = jnp.repeat(jnp.sin(ang), 2, axis=-1) * jnp.tile(jnp.array([-1.0, 1.0], F32), B_DH // 2)
    return jnp.tile(cos, (1, 2)), jnp.tile(sin, (1, 2))


def _attn(proj_l, proj_c, qk_gain, out_gain, lam, lam_init, bsz, tq=256):
    n = proj_l.shape[0] // bsz
    n_ctx = proj_c.shape[0] // bsz
    assert n_ctx == tq
    nqb = n // tq
    w = B_HEADS * HEAD_W
    cos, sin = _rope_tables(n)
    qcol, kcol, vcol = 5 * A_HEADS, 5 * A_HEADS + B_HEADS, 5 * A_HEADS + 2 * B_HEADS

    def lat_q(b, h, qb):
        return (b * nqb + jnp.maximum(qb - 1, 0), qcol + h)

    in_specs = [pl.BlockSpec((tq, HEAD_W), lat_q),
                pl.BlockSpec((n_ctx, HEAD_W), lambda b, h, qb: (b, qcol + h)),
                pl.BlockSpec((n, HEAD_W), lambda b, h, qb: (b, kcol + h)),
                pl.BlockSpec((n_ctx, HEAD_W), lambda b, h, qb: (b, kcol + h)),
                pl.BlockSpec((n, HEAD_W), lambda b, h, qb: (b, vcol + h)),
                pl.BlockSpec((n_ctx, HEAD_W), lambda b, h, qb: (b, vcol + h)),
                pl.BlockSpec((tq, LANES), lambda b, h, qb: (jnp.maximum(qb - 1, 0), 0)),
                pl.BlockSpec((tq, LANES), lambda b, h, qb: (jnp.maximum(qb - 1, 0), 0)),
                pl.BlockSpec((n, LANES), lambda b, h, qb: (0, 0)),
                pl.BlockSpec((n, LANES), lambda b, h, qb: (0, 0)),
                pl.BlockSpec((1, LANES), lambda b, h, qb: (0, 0)),
                pl.BlockSpec((1, LANES), lambda b, h, qb: (0, 0)),
                pl.BlockSpec((1, LANES), lambda b, h, qb: (0, 0)),
                pl.BlockSpec((1, LANES), lambda b, h, qb: (0, 0))]
    return pl.pallas_call(
        functools.partial(_attn_kernel, out_scale=1.0 - lam_init),
        grid=(bsz, B_HEADS, nqb + 1),
        in_specs=in_specs,
        out_specs=[pl.BlockSpec((tq, HEAD_W), lambda b, h, qb: (b * nqb + jnp.maximum(qb - 1, 0), h)),
                   pl.BlockSpec((n_ctx, HEAD_W), lambda b, h, qb: (b, h))],
        out_shape=[jax.ShapeDtypeStruct((bsz * n, w), BF16),
                   jax.ShapeDtypeStruct((bsz * n_ctx, w), BF16)],
        scratch_shapes=[pltpu.VMEM((n + n_ctx, HEAD_W), BF16), pltpu.VMEM((n + n_ctx, HEAD_W + LANES), BF16)],
        compiler_params=_cparams(("parallel", "parallel", "arbitrary")),
        name="diff_attn",
    )(proj_l, proj_c, proj_l, proj_c, proj_l, proj_c, cos, sin, cos, sin,
      jnp.tile(qk_gain[0], 2).reshape(1, LANES), jnp.tile(qk_gain[1], 2).reshape(1, LANES),
      out_gain.reshape(1, LANES), jnp.full((1, LANES), lam, F32))


def _scan_steps(a, b, reverse, axis):
    n = a.shape[axis]
    pos = lax.broadcasted_iota(I32, a.shape, axis)
    s = 1
    while s < n:
        keep = (pos < n - s) if reverse else (pos >= s)
        shift = n - s if reverse else s
        a_sh = jnp.where(keep, pltpu.roll(a, shift, axis), 1.0)
        b_sh = jnp.where(keep, pltpu.roll(b, shift, axis), 0.0)
        b = a * b_sh + b
        a = a * a_sh
        s *= 2
    return a, b


def _block_scan(a, b, reverse):
    return _scan_steps(a, b, reverse, 1)


def _rglru_kernel(y_l, u_l, u_c, cw_ref, cb_ref, gw_ref, gb_ref, lam_ref, o_ref,
                  upad, a_f, b_f, a_b, b_b, eb):
    n = u_l.shape[0]
    n_ctx = u_c.shape[0]
    tot = n + n_ctx
    blk = SUBLANES
    cw = cw_ref[...]
    cb = cb_ref[...]
    pad = SUBLANES

    def conv(u_ref, rows):
        upad[0:pad, :] = jnp.zeros((pad, LANES), F32)
        upad[pad:pad + rows, :] = u_ref[...]
        upad[pad + rows:pad + rows + pad, :] = jnp.zeros((pad, LANES), F32)
        acc = cb + jnp.zeros((rows, LANES), F32)
        for j in range(CONV_W):
            off = pad + j - CONV_W // 2
            acc = acc + cw[j:j + 1, :] * upad[off:off + rows, :]
        return acc

    def gates(uc, rows_f, rows_b):
        ub = uc.astype(BF16)
        for d, (a_ref, b_ref, rows) in enumerate(((a_f, b_f, rows_f), (a_b, b_b, rows_b))):
            lam = lam_ref[d:d + 1, :]
            neg_sp = -(jnp.maximum(-lam, 0.0) + jnp.log(1.0 + jnp.exp(-jnp.abs(lam))))
            r = _sigmoid(_dot(ub, gw_ref[d, 0].astype(BF16)) + gb_ref[d, 0:1, :])
            i = _sigmoid(_dot(ub, gw_ref[d, 1].astype(BF16)) + gb_ref[d, 1:2, :])
            a = jnp.exp((RG_C * neg_sp) * r)
            a_ref[rows, :] = a
            x = (1.0 - a) * (1.0 + a)
            b_ref[rows, :] = (x * lax.rsqrt(jnp.maximum(x, TINY))) * (i * uc)

    gates(conv(u_c, n_ctx), pl.ds(0, n_ctx), pl.ds(n, n_ctx))
    gates(conv(u_l, n), pl.ds(n_ctx, n), pl.ds(0, n))

    n_blk = tot // blk
    h_sum = None
    for a_ref, b_ref, reverse, lat0 in ((a_f, b_f, False, n_ctx), (a_b, b_b, True, 0)):
        chunk = 256
        for c0 in range(0, tot, chunk):
            a1, b1 = _block_scan(a_ref[c0:c0 + chunk, :].reshape(chunk // blk, blk, LANES),
                                 b_ref[c0:c0 + chunk, :].reshape(chunk // blk, blk, LANES), reverse)
            a_ref[c0:c0 + chunk, :] = a1.reshape(chunk, LANES)
            b_ref[c0:c0 + chunk, :] = b1.reshape(chunk, LANES)
        end = 0 if reverse else blk - 1
        ae = a_ref[pl.ds(end, n_blk, stride=blk), :]
        be = b_ref[pl.ds(end, n_blk, stride=blk), :]
        _, he = _scan_steps(ae, be, reverse, 0)
        zero = jnp.zeros((1, LANES), F32)
        if reverse:
            eb[0:n_blk - 1, :] = he[1:, :]
            eb[n_blk - 1:n_blk, :] = zero
        else:
            eb[0:1, :] = zero
            eb[1:n_blk, :] = he[:n_blk - 1, :]

        def apply(j, carry, a_ref=a_ref, b_ref=b_ref):
            rows = pl.ds(pl.multiple_of(j * blk, blk), blk)
            b_ref[rows, :] = b_ref[rows, :] + a_ref[rows, :] * eb[pl.ds(j, 1), :]
            return carry

        lax.fori_loop(0, n_blk, apply, 0, unroll=8)
        h = b_ref[lat0:lat0 + n, :]
        h_sum = h if h_sum is None else h_sum + h

    y = y_l[...]
    gelu = 0.5 * y * (1.0 + jnp.tanh(0.7978845608028654 * (y + 0.044715 * y * y * y)))
    o_ref[...] = (h_sum * gelu).astype(o_ref.dtype)


def _rglru(proj_l, proj_c, conv_w, conv_b, gate_w, gate_b, lam, bsz):
    n = proj_l.shape[0] // bsz
    n_ctx = proj_c.shape[0] // bsz
    w = RG_HEADS * HEAD_W
    tot = n + n_ctx
    return pl.pallas_call(
        _rglru_kernel,
        grid=(bsz, RG_HEADS),
        in_specs=[pl.BlockSpec((n, HEAD_W), lambda b, h: (b, h)),
                  pl.BlockSpec((n, HEAD_W), lambda b, h: (b, RG_HEADS + h)),
                  pl.BlockSpec((n_ctx, HEAD_W), lambda b, h: (b, RG_HEADS + h)),
                  pl.BlockSpec((CONV_W, HEAD_W), lambda b, h: (0, h)),
                  pl.BlockSpec((1, HEAD_W), lambda b, h: (0, h)),
                  pl.BlockSpec((2, 2, None, HEAD_W, HEAD_W), lambda b, h: (0, 0, h, 0, 0)),
                  pl.BlockSpec((2, 2, HEAD_W), lambda b, h: (0, 0, h)),
                  pl.BlockSpec((2, HEAD_W), lambda b, h: (0, h))],
        out_specs=pl.BlockSpec((n, HEAD_W), lambda b, h: (b, h)),
        out_shape=jax.ShapeDtypeStruct((bsz * n, w), BF16),
        scratch_shapes=[pltpu.VMEM((n + 2 * SUBLANES, LANES), F32)]
                       + [pltpu.VMEM((tot, LANES), F32)] * 4
                       + [pltpu.VMEM((tot // SUBLANES, LANES), F32)],
        compiler_params=_cparams(("parallel", "parallel")),
        name="rglru",
    )(proj_l, proj_l, proj_c, conv_w, conv_b.reshape(1, w), gate_w, gate_b, lam)


def _out_proj_kernel(*refs, n_mix, n_groups, epg):
    mix_refs = refs[:n_mix]
    (w_ref, x_ref, gate_ref, g_ref, sh_ref, sc_ref, wr_hi_ref, wr_lo_ref, rb_ref,
     xo_ref, f_ref, gid_ref) = refs[n_mix:]
    d = x_ref.shape[1]
    k0 = 0
    mix = None
    for m_ref in mix_refs:
        kk = m_ref.shape[1]
        part = _dot(m_ref[...], w_ref[k0:k0 + kk, :])
        mix = part if mix is None else mix + part
        k0 += kk
    x = x_ref[...] + gate_ref[0] * mix
    xo_ref[...] = x
    y = x * lax.rsqrt(jnp.mean(x * x, axis=-1, keepdims=True) + EPS) * g_ref[...]
    f = y * (1.0 + sc_ref[0]) + sh_ref[0]
    f_ref[:, 0:d] = f

    f_hi, f_lo = _split_bf16(f)
    lg = _dot(f_hi, wr_hi_ref[...]) + _dot(f_lo, wr_hi_ref[...]) + _dot(f_hi, wr_lo_ref[...]) + rb_ref[...]
    lane = lax.broadcasted_iota(I32, lg.shape, 1).astype(F32)
    neg = -jnp.inf
    big = float(ROUTE_W)
    gl = jnp.where(lane < n_groups, lg, neg)
    gmax = jnp.max(gl, axis=-1, keepdims=True)
    gidx = jnp.min(jnp.where(gl == gmax, lane, big), axis=-1, keepdims=True)
    gw = 1.0 / jnp.sum(jnp.exp(gl - gmax), axis=-1, keepdims=True)
    base = n_groups + gidx * epg
    el = jnp.where((lane >= base) & (lane < base + epg), lg, neg)
    v1 = jnp.max(el, axis=-1, keepdims=True)
    i1 = jnp.min(jnp.where(el == v1, lane, big), axis=-1, keepdims=True)
    el2 = jnp.where(lane == i1, neg, el)
    v2 = jnp.max(el2, axis=-1, keepdims=True)
    i2 = jnp.min(jnp.where(el2 == v2, lane, big), axis=-1, keepdims=True)
    t = jnp.exp(v2 - v1)
    w1 = gw / (1.0 + t)
    w2 = w1 * t
    slot = lane + base
    f_ref[:, d:] = jnp.where(slot == i1, w1, 0.0) + jnp.where(slot == i2, w2, 0.0)
    gid_ref[...] = jnp.broadcast_to(gidx, gid_ref.shape).astype(I32)


def _out_proj(mixes, w_out, x2, gain, mods, w_router, b_router, rows_per_mod, mod_base, tm=256):
    r, d = x2.shape
    k = w_out.shape[0]
    tpm = rows_per_mod // tm
    wr_hi, wr_lo = _split_bf16(w_router)

    def mrow(i):
        return (mod_base + i // tpm) * N_MOD

    const2 = lambda i: (0, 0)
    in_specs = ([pl.BlockSpec((tm, m.shape[1]), lambda i: (i, 0)) for m in mixes]
                + [pl.BlockSpec((k, d), const2),
                   pl.BlockSpec((tm, d), lambda i: (i, 0)),
                   pl.BlockSpec((1, 1, d), lambda i: (mrow(i) + 2, 0, 0)),
                   pl.BlockSpec((1, d), const2),
                   pl.BlockSpec((1, 1, d), lambda i: (mrow(i) + 3, 0, 0)),
                   pl.BlockSpec((1, 1, d), lambda i: (mrow(i) + 4, 0, 0)),
                   pl.BlockSpec((d, ROUTE_W), const2),
                   pl.BlockSpec((d, ROUTE_W), const2),
                   pl.BlockSpec((1, ROUTE_W), const2)])
    return pl.pallas_call(
        functools.partial(_out_proj_kernel, n_mix=len(mixes), n_groups=N_GROUPS, epg=EXPERTS_PER_GROUP),
        grid=(r // tm,),
        in_specs=in_specs,
        out_specs=[pl.BlockSpec((tm, d), lambda i: (i, 0)),
                   pl.BlockSpec((tm, d + ROUTE_W), lambda i: (i, 0)),
                   pl.BlockSpec((tm, ROUTE_W), lambda i: (i, 0))],
        out_shape=[jax.ShapeDtypeStruct((r, d), F32),
                   jax.ShapeDtypeStruct((r, d + ROUTE_W), F32),
                   jax.ShapeDtypeStruct((r, ROUTE_W), I32)],
        compiler_params=_cparams(("parallel",)),
        name="out_proj",
    )(*mixes, w_out, x2, mods, gain.reshape(1, d), mods, mods, wr_hi, wr_lo, b_router)


def _row_copy(src, s, dst, t, sem):
    return pltpu.make_async_copy(src.at[pl.ds(s, 1), :], dst.at[pl.ds(t, 1), :], sem)


def _dispatch_kernel(pos_ref, src_ref, dst_in, dst, sem, *, tm):
    del dst_in

    def body(j, carry):
        _row_copy(src_ref, j, dst, pos_ref[0, 0, j], sem).start()
        return carry
    lax.fori_loop(0, tm, body, 0, unroll=ROW_DMA_UNROLL)

    def wbody(j, carry):
        _row_copy(src_ref, 0, dst, 0, sem).wait()
        return carry
    lax.fori_loop(0, tm, wbody, 0, unroll=True)


def _dispatch(src, pos, sorted_in, tm=512):
    r, w = src.shape
    return pl.pallas_call(
        functools.partial(_dispatch_kernel, tm=tm),
        grid=(r // tm,),
        in_specs=[pl.BlockSpec((1, 1, tm), lambda i: (i, 0, 0), memory_space=pltpu.SMEM),
                  pl.BlockSpec((tm, w), lambda i: (i, 0)),
                  pl.BlockSpec(memory_space=pl.ANY)],
        out_specs=pl.BlockSpec(memory_space=pl.ANY),
        out_shape=jax.ShapeDtypeStruct(sorted_in.shape, sorted_in.dtype),
        scratch_shapes=[pltpu.SemaphoreType.DMA(())],
        input_output_aliases={2: 0},
        compiler_params=_cparams(("arbitrary",)),
        name="moe_dispatch",
    )(pos.reshape(r // tm, 1, tm), src, sorted_in)


def _moe_kernel(tg_ref, tv_ref, x_ref, w1_ref, w3_ref, w2_ref, o_ref, acc_ref):
    t = pl.program_id(0)
    e = pl.program_id(1)
    d = o_ref.shape[1]

    @pl.when(e == 0)
    def _():
        acc_ref[...] = jnp.zeros_like(acc_ref)

    @pl.when(tv_ref[t] > 0)
    def _():
        x = x_ref[:, 0:d].astype(BF16)
        h1 = _dot(x, w1_ref[...])
        h3 = _dot(x, w3_ref[...])
        cw = x_ref[:, d:]
        lane = lax.broadcasted_iota(I32, cw.shape, 1)
        cwe = jnp.sum(jnp.where(lane == e, cw, 0.0), axis=-1, keepdims=True)
        hid = (h1 * _sigmoid(h1) * h3 * cwe).astype(BF16)
        acc_ref[...] += _dot(hid, w2_ref[...])

    @pl.when(e == EXPERTS_PER_GROUP - 1)
    def _():
        o_ref[...] = acc_ref[...]


def _moe_ffn(tile_group, tile_valid, f_sorted, w1, w3, w2, tm):
    p, dw = f_sorted.shape
    d = dw - ROUTE_W
    fe = w1.shape[2]
    epg = EXPERTS_PER_GROUP
    grid_spec = pltpu.PrefetchScalarGridSpec(
        num_scalar_prefetch=2,
        grid=(p // tm, epg),
        in_specs=[pl.BlockSpec((tm, dw), lambda t, e, tg, tv: (t, 0)),
                  pl.BlockSpec((None, d, fe), lambda t, e, tg, tv: (tg[t] * epg + e, 0, 0)),
                  pl.BlockSpec((None, d, fe), lambda t, e, tg, tv: (tg[t] * epg + e, 0, 0)),
                  pl.BlockSpec((None, fe, d), lambda t, e, tg, tv: (tg[t] * epg + e, 0, 0))],
        out_specs=pl.BlockSpec((tm, d), lambda t, e, tg, tv: (t, 0)),
        scratch_shapes=[pltpu.VMEM((tm, d), F32)],
    )
    return pl.pallas_call(
        _moe_kernel,
        grid_spec=grid_spec,
        out_shape=jax.ShapeDtypeStruct((p, d), F32),
        compiler_params=_cparams(("arbitrary", "arbitrary")),
        name="moe_ffn",
    )(tile_group, tile_valid, f_sorted, w1, w3, w2)


def _combine_kernel(pos_cur, pos_nxt, y_hbm, x_ref, gate_ref, o_ref, ybuf, sem, *, tm):
    i = pl.program_id(0)
    n_steps = pl.num_programs(0)

    def issue(pos_ref, slot):
        def body(j, carry):
            _row_copy(y_hbm, pos_ref[0, 0, j], ybuf.at[slot], j, sem.at[slot]).start()
            return carry
        lax.fori_loop(0, tm, body, 0, unroll=ROW_DMA_UNROLL)

    @pl.when(i == 0)
    def _():
        issue(pos_cur, 0)

    @pl.when(i + 1 < n_steps)
    def _():
        issue(pos_nxt, (i + 1) % 2)

    slot = i % 2

    def wbody(j, carry):
        _row_copy(y_hbm, 0, ybuf.at[slot], 0, sem.at[slot]).wait()
        return carry
    lax.fori_loop(0, tm, wbody, 0, unroll=True)
    o_ref[...] = x_ref[...] + gate_ref[0] * ybuf[slot]


def _combine(y_sorted, pos, x2, mods, rows_per_mod, mod_base, tm=256):
    r, d = x2.shape
    n_steps = r // tm
    tpm = rows_per_mod // tm
    pos3 = pos.reshape(n_steps, 1, tm)
    return pl.pallas_call(
        functools.partial(_combine_kernel, tm=tm),
        grid=(n_steps,),
        in_specs=[pl.BlockSpec((1, 1, tm), lambda i: (i, 0, 0), memory_space=pltpu.SMEM),
                  pl.BlockSpec((1, 1, tm), lambda i: (jnp.minimum(i + 1, n_steps - 1), 0, 0),
                               memory_space=pltpu.SMEM),
                  pl.BlockSpec(memory_space=pl.ANY),
                  pl.BlockSpec((tm, d), lambda i: (i, 0)),
                  pl.BlockSpec((1, 1, d), lambda i: ((mod_base + i // tpm) * N_MOD + 5, 0, 0))],
        out_specs=pl.BlockSpec((tm, d), lambda i: (i, 0)),
        out_shape=jax.ShapeDtypeStruct((r, d), F32),
        scratch_shapes=[pltpu.VMEM((2, tm, d), F32), pltpu.SemaphoreType.DMA((2,))],
        compiler_params=_cparams(("arbitrary",)),
        name="moe_combine",
    )(pos3, pos3, y_sorted, x2, mods)


def _sorted_positions(gids, tm):
    g = jnp.concatenate(gids)
    r = g.shape[0]
    onehot = (g[:, None] == jnp.arange(N_GROUPS, dtype=I32)[None, :]).astype(I32)
    counts = jnp.sum(onehot, axis=0)
    rank = jnp.sum((jnp.cumsum(onehot, axis=0) - 1) * onehot, axis=1)
    padded = ((counts + tm - 1) // tm) * tm
    ends = jnp.cumsum(padded)
    pos = jnp.sum(onehot * (ends - padded)[None, :], axis=1) + rank
    p = r + N_GROUPS * tm
    tile_start = jnp.arange(p // tm, dtype=I32) * tm
    tile_group = jnp.minimum(jnp.sum((tile_start[:, None] >= ends[None, :]).astype(I32), axis=1), N_GROUPS - 1)
    tile_valid = (tile_start < ends[-1]).astype(I32)
    return pos.astype(I32), tile_group.astype(I32), tile_valid, p


def _moe(fexts, gids, w1, w3, w2, tm=512):
    pos, tile_group, tile_valid, p = _sorted_positions([g[:, 0] for g in gids], tm)
    sizes = [f.shape[0] for f in fexts]
    poss, off = [], 0
    for s in sizes:
        poss.append(lax.slice(pos, (off,), (off + s,)))
        off += s
    f_sorted = jnp.zeros((p, fexts[0].shape[1]), F32)
    for fext, ps in zip(fexts, poss):
        f_sorted = _dispatch(fext, ps, f_sorted)
    y_sorted = _moe_ffn(tile_group, tile_valid, f_sorted, w1, w3, w2, tm)
    return y_sorted, poss


def kernel(x, c, ctx, c_ctx, ada_w, ada_b, norm_mix, norm_ffn, even_w_in, even_w_out, hgrn_lb_logits, hgrn_out_norm, diff_qk_norm, diff_lambda, diff_out_norm, odd_w_in, odd_conv_w, odd_conv_b, rg_gate_w, rg_gate_b, rg_lambda, odd_w_out, moe_w_grp, moe_b_grp, moe_w_exp, moe_b_exp, moe_w1, moe_w3, moe_w2):
    bsz, n, d = x.shape
    n_ctx = ctx.shape[1]
    depth = ada_w.shape[0]
    assert depth == 2

    cvec = jnp.zeros((MOD_ROWS, d), F32).at[:bsz].set(c).at[bsz].set(c_ctx)
    mods_all = _ada_all(cvec, ada_w, ada_b).reshape(depth, MOD_ROWS * N_MOD, 1, d)
    x_l = x.reshape(bsz * n, d)
    x_c = ctx.reshape(bsz * n_ctx, d)
    lb_all = jnp.cumsum(jax.nn.softmax(hgrn_lb_logits.astype(F32), axis=1), axis=1)

    def router(l):
        w = jnp.concatenate([moe_w_grp[l], moe_w_exp[l]], axis=1)
        b = jnp.concatenate([moe_b_grp[l], moe_b_exp[l]])
        padw = ROUTE_W - w.shape[1]
        return jnp.pad(w, ((0, 0), (0, padw))), jnp.pad(b, (0, padw)).reshape(1, ROUTE_W)

    l = 0
    mods = mods_all[l]
    lam_init = 0.8 - 0.6 * math.exp(-0.3 * l)
    lv = diff_lambda[0].astype(F32)
    lam = jnp.exp(jnp.sum(lv[0] * lv[1])) - jnp.exp(jnp.sum(lv[2] * lv[3])) + lam_init
    w_in = even_w_in[0].astype(BF16)
    proj_l = _norm_mod_mm(x_l, norm_mix[l], mods, w_in, n, 0, tm=1024)
    proj_c = _norm_mod_mm(x_c, norm_mix[l], mods, w_in, bsz * n_ctx, bsz, tm=1024)
    a_l, a_c = _hgrn(proj_l, proj_c, lb_all[0, 0], lb_all[1, 0], hgrn_out_norm[0], bsz)
    b_l, b_c = _attn(proj_l, proj_c, diff_qk_norm[0], diff_out_norm[0], lam, lam_init, bsz)
    w_out = even_w_out[0].astype(BF16)
    wr, br = router(l)
    x_l, f_l, g_l = _out_proj([a_l, b_l], w_out, x_l, norm_ffn[l], mods, wr, br, n, 0)
    x_c, f_c, g_c = _out_proj([a_c, b_c], w_out, x_c, norm_ffn[l], mods, wr, br, bsz * n_ctx, bsz)
    y_sorted, (pos_l, pos_c) = _moe([f_l, f_c], [g_l, g_c], moe_w1[l].astype(BF16), moe_w3[l].astype(BF16),
                                    moe_w2[l].astype(BF16))
    x_l = _combine(y_sorted, pos_l, x_l, mods, n, 0)
    x_c = _combine(y_sorted, pos_c, x_c, mods, bsz * n_ctx, bsz)

    l = 1
    mods = mods_all[l]
    w_in = odd_w_in[0].astype(BF16)
    proj_l = _norm_mod_mm(x_l, norm_mix[l], mods, w_in, n, 0, tm=1024)
    proj_c = _norm_mod_mm(x_c, norm_mix[l], mods, w_in, bsz * n_ctx, bsz, tm=1024)
    gated = _rglru(proj_l, proj_c, odd_conv_w[0], odd_conv_b[0], rg_gate_w[0], rg_gate_b[0], rg_lambda[0], bsz)
    wr, br = router(l)
    x_l, f_l, g_l = _out_proj([gated], odd_w_out[0].astype(BF16), x_l, norm_ffn[l], mods, wr, br, n, 0)
    y_sorted, (pos_l,) = _moe([f_l], [g_l], moe_w1[l].astype(BF16), moe_w3[l].astype(BF16),
                              moe_w2[l].astype(BF16))
    return _combine(y_sorted, pos_l, x_l, mods, n, 0).reshape(bsz, n, d)
```

```python
import functools
import math

import jax
import jax.numpy as jnp
from jax import lax
from jax.experimental import pallas as pl
from jax.experimental.pallas import tpu as pltpu

F32 = jnp.float32
BF16 = jnp.bfloat16
I32 = jnp.int32

EPS = 1e-6
LOG2_E = 1.4426950408889634
TINY = 1e-30
LANES = 128
SUBLANES = 8
GRID_W = 64
A_HEADS = 8
HEAD_W = 128
HGRN_CHUNK = 32
HGRN_BLOCK = 256
B_HEADS = 8
B_DH = 64
ROPE_BASE = 10000.0
RG_HEADS = 16
CONV_W = 4
RG_C = 8.0
N_GROUPS = 4
EXPERTS_PER_GROUP = 4
N_EXPERTS = 16
N_MOD = 6
MOD_ROWS = 16
ROUTE_W = LANES
ROW_DMA_UNROLL = 8

VMEM_LIMIT = 56 * 1024 * 1024


def _cparams(sem):
    return pltpu.CompilerParams(dimension_semantics=sem, vmem_limit_bytes=VMEM_LIMIT)


def _sigmoid(x):
    return 0.5 * jnp.tanh(0.5 * x) + 0.5


def _dot(a, b):
    return jnp.dot(a, b, preferred_element_type=F32)


def _dot_nt(a, b):
    return lax.dot_general(a, b, (((1,), (1,)), ((), ())), preferred_element_type=F32)


def _dot_tn(a, b):
    return lax.dot_general(a, b, (((0,), (0,)), ((), ())), preferred_element_type=F32)


def _split_bf16(x):
    hi = x.astype(BF16)
    return hi, (x - hi.astype(F32)).astype(BF16)


def _ada_kernel(c_ref, w_ref, b_ref, o_ref):
    c = c_ref[...]
    a = (c * _sigmoid(c)).astype(BF16)
    o_ref[...] = _dot(a, w_ref[...].astype(BF16)) + b_ref[...]


def _ada_all(cvec, ada_w, ada_b, tn=1024):
    depth, d, n = ada_w.shape
    return pl.pallas_call(
        _ada_kernel,
        grid=(depth, n // tn),
        in_specs=[pl.BlockSpec((MOD_ROWS, d), lambda l, j: (0, 0)),
                  pl.BlockSpec((None, d, tn), lambda l, j: (l, 0, j)),
                  pl.BlockSpec((None, 1, tn), lambda l, j: (l, 0, j))],
        out_specs=pl.BlockSpec((None, MOD_ROWS, tn), lambda l, j: (l, 0, j)),
        out_shape=jax.ShapeDtypeStruct((depth, MOD_ROWS, n), F32),
        compiler_params=_cparams(("arbitrary", "arbitrary")),
        name="ada_mod",
    )(cvec, ada_w, ada_b.reshape(depth, 1, n))


def _norm_mod_mm_kernel(x_ref, g_ref, sh_ref, sc_ref, w_ref, o_ref, h_scr):
    @pl.when(pl.program_id(1) == 0)
    def _():
        x = x_ref[...]
        y = x * lax.rsqrt(jnp.mean(x * x, axis=-1, keepdims=True) + EPS) * g_ref[...]
        h_scr[...] = (y * (1.0 + sc_ref[0]) + sh_ref[0]).astype(BF16)

    o_ref[...] = _dot(h_scr[...], w_ref[...]).astype(o_ref.dtype)


def _norm_mod_mm(x2, gain, mods, w, rows_per_mod, mod_base, tm, tn=512):
    r, d = x2.shape
    n = w.shape[1]
    tpm = rows_per_mod // tm

    def mrow(i):
        return (mod_base + i // tpm) * N_MOD

    return pl.pallas_call(
        _norm_mod_mm_kernel,
        grid=(r // tm, n // tn),
        in_specs=[pl.BlockSpec((tm, d), lambda i, j: (i, 0)),
                  pl.BlockSpec((1, d), lambda i, j: (0, 0)),
                  pl.BlockSpec((1, 1, d), lambda i, j: (mrow(i) + 0, 0, 0)),
                  pl.BlockSpec((1, 1, d), lambda i, j: (mrow(i) + 1, 0, 0)),
                  pl.BlockSpec((d, tn), lambda i, j: (0, j))],
        out_specs=pl.BlockSpec((tm, tn), lambda i, j: (i, j)),
        out_shape=jax.ShapeDtypeStruct((r, n), F32),
        scratch_shapes=[pltpu.VMEM((tm, d), BF16)],
        compiler_params=_cparams(("parallel", "arbitrary")),
        name="norm_mod_mm",
    )(x2, gain.reshape(1, d), mods, mods, w)


def _hgrn_kernel(ql, ffl, fbl, vl, gl, qc, ffc, fbc, vc, gc, lbf_ref, lbb_ref, gain_ref,
                 ol_ref, oc_ref,
                 oi_l, oi_c, qtf_l, qtb_l, qtf_c, qtb_c, kv_f, kv_b, dec_f, dec_b):
    c_sz = HGRN_CHUNK
    blk = HGRN_BLOCK
    cpb = blk // c_sz
    nb_l = ql.shape[0] // blk
    nc_l = ql.shape[0] // c_sz
    nc_c = qc.shape[0] // c_sz
    nc = nc_l + nc_c
    assert qc.shape[0] == blk
    row = lax.broadcasted_iota(I32, (blk, blk), 0)
    col = lax.broadcasted_iota(I32, (blk, blk), 1)
    same = (row // c_sz) == (col // c_sz)
    tril = same & (row >= col)
    triu = same & (row <= col)
    lbf = lbf_ref[...]
    lbb = lbb_ref[...]

    exp_mask = (lax.broadcasted_iota(I32, (blk, cpb * HEAD_W), 0) // c_sz
                == lax.broadcasted_iota(I32, (blk, cpb * HEAD_W), 1) // HEAD_W)

    def local_terms(r0, q_ref, ff_ref, fb_ref, v_ref, oi_ref, qtf_ref, qtb_ref, gid_f, gid_b):
        rows = pl.ds(r0, blk)
        q = q_ref[rows, :]
        qs = q * _sigmoid(q)
        vb = v_ref[rows, :].astype(BF16)
        vt = v_ref[rows, :].T.astype(BF16)
        two = (0, 1)
        mask = (tril, triu)
        last = (c_sz - 1, 0)
        f = [lb + (1.0 - lb) * _sigmoid(ref[rows, :]) for ref, lb in ((ff_ref, lbf), (fb_ref, lbb))]
        k = [1.0 - f[d] for d in two]
        lf2 = [jnp.concatenate(_split_bf16(jnp.log(f[d])), axis=1) for d in two]
        cum2 = [_dot(jnp.where(mask[d], 1.0, 0.0).astype(BF16), lf2[d]) for d in two]
        cum = [cum2[d][:, :HEAD_W] + cum2[d][:, HEAD_W:] for d in two]
        tot = [jnp.concatenate(
            [jnp.broadcast_to(cum[d][c * c_sz + last[d]:c * c_sz + last[d] + 1, :], (c_sz, HEAD_W))
             for c in range(cpb)], axis=0) for d in two]
        e = [jnp.exp(cum[d]) for d in two]
        qt = [(qs * e[d]).astype(BF16) for d in two]
        kt = [(k[d] * jnp.exp(-cum[d])).astype(BF16) for d in two]
        att = [jnp.where(mask[d], _dot_nt(qt[d], kt[d]), 0.0).astype(BF16) for d in two]
        oi_ref[rows, :] = _dot(att[0], vb) + _dot(att[1], vb)
        k2 = [(k[d] * jnp.exp(tot[d] - cum[d])).astype(BF16) for d in two]
        k2x = [jnp.where(exp_mask, jnp.concatenate([k2[d]] * cpb, axis=1), jnp.zeros((), BF16)) for d in two]
        kvs = [_dot(vt, k2x[d]) for d in two]
        for d, (qt_ref, kv_ref, dec_ref, gid) in enumerate(((qtf_ref, kv_f, dec_f, gid_f),
                                                            (qtb_ref, kv_b, dec_b, gid_b))):
            for c in range(cpb):
                kv_ref[gid + c] = kvs[d][:, c * HEAD_W:(c + 1) * HEAD_W]
                dec_ref[gid + c] = e[d][c * c_sz + last[d]:c * c_sz + last[d] + 1, :]
            qt_ref[rows, :] = qt[d]

    local_terms(0, qc, ffc, fbc, vc, oi_c, qtf_c, qtb_c, 0, nc_l)

    def local_l(i, carry):
        local_terms(pl.multiple_of(i * blk, blk), ql, ffl, fbl, vl, oi_l, qtf_l, qtb_l,
                    nc_c + i * cpb, i * cpb)
        return carry

    lax.fori_loop(0, nb_l, local_l, 0, unroll=2)

    def rec_f(i, s):
        new = s * dec_f[i] + kv_f[i]
        kv_f[i] = s
        return new

    def rec_b(i, s):
        j = nc - 1 - i
        new = s * dec_b[j] + kv_b[j]
        kv_b[j] = s
        return new

    s0 = jnp.zeros((HEAD_W, HEAD_W), F32)
    lax.fori_loop(0, nc, rec_f, s0, unroll=2)
    lax.fori_loop(0, nc, rec_b, s0, unroll=2)

    gain = gain_ref[...]

    def finish(r0, g_ref, oi_ref, qtf_ref, qtb_ref, o_ref, gid_f, gid_b):
        parts = []
        for c in range(cpb):
            rows_c = pl.ds(r0 + c * c_sz, c_sz)
            parts.append(_dot_nt(qtf_ref[rows_c, :], kv_f[gid_f + c].astype(BF16))
                         + _dot_nt(qtb_ref[rows_c, :], kv_b[gid_b + c].astype(BF16)))
        rows = pl.ds(r0, blk)
        o = oi_ref[rows, :] + jnp.concatenate(parts, axis=0)
        y = o * lax.rsqrt(jnp.mean(o * o, axis=-1, keepdims=True) + EPS) * gain
        g = g_ref[rows, :]
        o_ref[rows, :] = (y * (g * _sigmoid(g))).astype(o_ref.dtype)

    finish(0, gc, oi_c, qtf_c, qtb_c, oc_ref, 0, nc_l)

    def fin_l(i, carry):
        finish(pl.multiple_of(i * blk, blk), gl, oi_l, qtf_l, qtb_l, ol_ref, nc_c + i * cpb, i * cpb)
        return carry

    lax.fori_loop(0, nb_l, fin_l, 0, unroll=True)


def _hgrn(proj_l, proj_c, lb_f, lb_b, gain, bsz):
    n = proj_l.shape[0] // bsz
    n_ctx = proj_c.shape[0] // bsz
    nc = (n + n_ctx) // HGRN_CHUNK
    w = A_HEADS * HEAD_W

    def col(k):
        return lambda b, h: (b, k * A_HEADS + h)

    in_specs = ([pl.BlockSpec((n, HEAD_W), col(k)) for k in range(5)]
                + [pl.BlockSpec((n_ctx, HEAD_W), col(k)) for k in range(5)]
                + [pl.BlockSpec((1, HEAD_W), lambda b, h: (0, h)),
                   pl.BlockSpec((1, HEAD_W), lambda b, h: (0, h)),
                   pl.BlockSpec((1, HEAD_W), lambda b, h: (0, 0))])
    return pl.pallas_call(
        _hgrn_kernel,
        grid=(bsz, A_HEADS),
        in_specs=in_specs,
        out_specs=[pl.BlockSpec((n, HEAD_W), lambda b, h: (b, h)),
                   pl.BlockSpec((n_ctx, HEAD_W), lambda b, h: (b, h))],
        out_shape=[jax.ShapeDtypeStruct((bsz * n, w), BF16),
                   jax.ShapeDtypeStruct((bsz * n_ctx, w), BF16)],
        scratch_shapes=[pltpu.VMEM((n, HEAD_W), F32), pltpu.VMEM((n_ctx, HEAD_W), F32),
                        pltpu.VMEM((n, HEAD_W), BF16), pltpu.VMEM((n, HEAD_W), BF16),
                        pltpu.VMEM((n_ctx, HEAD_W), BF16), pltpu.VMEM((n_ctx, HEAD_W), BF16),
                        pltpu.VMEM((nc, HEAD_W, HEAD_W), F32), pltpu.VMEM((nc, HEAD_W, HEAD_W), F32),
                        pltpu.VMEM((nc, 1, HEAD_W), F32), pltpu.VMEM((nc, 1, HEAD_W), F32)],
        compiler_params=_cparams(("parallel", "parallel")),
        name="hgrn2",
    )(*([proj_l] * 5 + [proj_c] * 5), lb_f.reshape(1, w), lb_b.reshape(1, w), gain.reshape(1, HEAD_W))


def _half_mean_matrix():
    r = lax.broadcasted_iota(I32, (LANES, LANES), 0) // B_DH
    c = lax.broadcasted_iota(I32, (LANES, LANES), 1) // B_DH
    return (r == c).astype(BF16)


def _qk_prep(t, gain, bd, cos, sin):
    sq_hi, sq_lo = _split_bf16(t * t)
    ms = (_dot(sq_hi, bd) + _dot(sq_lo, bd)) * (1.0 / B_DH)
    y = t * lax.rsqrt(ms + EPS) * gain
    if cos is None:
        return y
    lane = lax.broadcasted_iota(I32, y.shape, 1)
    partner = jnp.where(lane % 2 == 0, pltpu.roll(y, LANES - 1, 1), pltpu.roll(y, 1, 1))
    return y * cos + partner * sin


def _attn_kernel(q_l, q_c, k_l, k_c, v_l, v_c, cosq, sinq, cosk, sink, gq_ref, gk_ref, go_ref, lam_ref,
                 o_l, o_c, kp, vp, *, out_scale):
    qb = pl.program_id(2)
    n_ctx = k_c.shape[0]
    bd = _half_mean_matrix()
    lane = lax.broadcasted_iota(I32, (1, LANES), 1)
    m1 = (lane < B_DH).astype(F32)
    m2 = 1.0 - m1
    lam = lam_ref[:, 0:1]

    @pl.when(qb == 0)
    def _():
        kp[0:n_ctx, :] = _qk_prep(k_c[...], gk_ref[...], bd, None, None).astype(BF16)
        kp[n_ctx:, :] = _qk_prep(k_l[...], gk_ref[...], bd, cosk[...], sink[...]).astype(BF16)
        ones_col = (lax.broadcasted_iota(I32, (1, LANES), 1) == 0).astype(BF16)
        vp[0:n_ctx, 0:HEAD_W] = v_c[...].astype(BF16)
        vp[n_ctx:, 0:HEAD_W] = v_l[...].astype(BF16)
        vp[:, HEAD_W:] = jnp.broadcast_to(ones_col, (vp.shape[0], LANES))

    def attend(q, k_ref, v_ref, n_keys):
        q = q * (B_DH ** -0.5 * LOG2_E)
        s = [_dot_nt((q * msk).astype(BF16), k_ref[0:n_keys, :]) for msk in (m1, m2)]
        e = [jnp.exp2(t - jnp.max(t, axis=-1, keepdims=True)).astype(BF16) for t in s]
        r = [_dot(t, v_ref[0:n_keys, :]) for t in e]
        o = (r[0][:, 0:HEAD_W] * (1.0 / r[0][:, HEAD_W:HEAD_W + 1])
             - r[1][:, 0:HEAD_W] * (lam / r[1][:, HEAD_W:HEAD_W + 1]))
        y = o * lax.rsqrt(jnp.mean(o * o, axis=-1, keepdims=True) + EPS) * go_ref[...]
        return (y * out_scale).astype(BF16)

    @pl.when(qb == 0)
    def _():
        q = _qk_prep(q_c[...], gq_ref[...], bd, None, None)
        o_c[...] = attend(q, kp, vp, n_ctx)

    @pl.when(qb > 0)
    def _():
        q = _qk_prep(q_l[...], gq_ref[...], bd, cosq[...], sinq[...])
        o_l[...] = attend(q, kp, vp, kp.shape[0])


def _rope_tables(n):
    n_rows = n // GRID_W
    rowp = jnp.repeat(jnp.arange(n_rows), GRID_W).astype(F32)
    colp = jnp.tile(jnp.arange(GRID_W), n_rows).astype(F32)
    pairs = B_DH // 4
    inv = ROPE_BASE ** (-jnp.arange(pairs, dtype=F32) / pairs)
    ang = jnp.concatenate([rowp[:, None] * inv, colp[:, None] * inv], axis=-1)
    cos = jnp.repeat(jnp.cos(ang), 2, axis=-1)
    sin = jnp.repeat(jnp.sin(ang), 2, axis=-1) * jnp.tile(jnp.array([-1.0, 1.0], F32), B_DH // 2)
    return jnp.tile(cos, (1, 2)), jnp.tile(sin, (1, 2))


def _attn(proj_l, proj_c, qk_gain, out_gain, lam, lam_init, bsz, tq=256):
    n = proj_l.shape[0] // bsz
    n_ctx = proj_c.shape[0] // bsz
    assert n_ctx == tq
    nqb = n // tq
    w = B_HEADS * HEAD_W
    cos, sin = _rope_tables(n)
    qcol, kcol, vcol = 5 * A_HEADS, 5 * A_HEADS + B_HEADS, 5 * A_HEADS + 2 * B_HEADS

    def lat_q(b, h, qb):
        return (b * nqb + jnp.maximum(qb - 1, 0), qcol + h)

    in_specs = [pl.BlockSpec((tq, HEAD_W), lat_q),
                pl.BlockSpec((n_ctx, HEAD_W), lambda b, h, qb: (b, qcol + h)),
                pl.BlockSpec((n, HEAD_W), lambda b, h, qb: (b, kcol + h)),
                pl.BlockSpec((n_ctx, HEAD_W), lambda b, h, qb: (b, kcol + h)),
                pl.BlockSpec((n, HEAD_W), lambda b, h, qb: (b, vcol + h)),
                pl.BlockSpec((n_ctx, HEAD_W), lambda b, h, qb: (b, vcol + h)),
                pl.BlockSpec((tq, LANES), lambda b, h, qb: (jnp.maximum(qb - 1, 0), 0)),
                pl.BlockSpec((tq, LANES), lambda b, h, qb: (jnp.maximum(qb - 1, 0), 0)),
                pl.BlockSpec((n, LANES), lambda b, h, qb: (0, 0)),
                pl.BlockSpec((n, LANES), lambda b, h, qb: (0, 0)),
                pl.BlockSpec((1, LANES), lambda b, h, qb: (0, 0)),
                pl.BlockSpec((1, LANES), lambda b, h, qb: (0, 0)),
                pl.BlockSpec((1, LANES), lambda b, h, qb: (0, 0)),
                pl.BlockSpec((1, LANES), lambda b, h, qb: (0, 0))]
    return pl.pallas_call(
        functools.partial(_attn_kernel, out_scale=1.0 - lam_init),
        grid=(bsz, B_HEADS, nqb + 1),
        in_specs=in_specs,
        out_specs=[pl.BlockSpec((tq, HEAD_W), lambda b, h, qb: (b * nqb + jnp.maximum(qb - 1, 0), h)),
                   pl.BlockSpec((n_ctx, HEAD_W), lambda b, h, qb: (b, h))],
        out_shape=[jax.ShapeDtypeStruct((bsz * n, w), BF16),
                   jax.ShapeDtypeStruct((bsz * n_ctx, w), BF16)],
        scratch_shapes=[pltpu.VMEM((n + n_ctx, HEAD_W), BF16), pltpu.VMEM((n + n_ctx, HEAD_W + LANES), BF16)],
        compiler_params=_cparams(("parallel", "parallel", "arbitrary")),
        name="diff_attn",
    )(proj_l, proj_c, proj_l, proj_c, proj_l, proj_c, cos, sin, cos, sin,
      jnp.tile(qk_gain[0], 2).reshape(1, LANES), jnp.tile(qk_gain[1], 2).reshape(1, LANES),
      out_gain.reshape(1, LANES), jnp.full((1, LANES), lam, F32))


def _scan_steps(a, b, reverse, axis):
    n = a.shape[axis]
    pos = lax.broadcasted_iota(I32, a.shape, axis)
    s = 1
    while s < n:
        keep = (pos < n - s) if reverse else (pos >= s)
        shift = n - s if reverse else s
        a_sh = jnp.where(keep, pltpu.roll(a, shift, axis), 1.0)
        b_sh = jnp.where(keep, pltpu.roll(b, shift, axis), 0.0)
        b = a * b_sh + b
        a = a * a_sh
        s *= 2
    return a, b


def _block_scan(a, b, reverse):
    return _scan_steps(a, b, reverse, 1)


def _rglru_kernel(y_l, u_l, u_c, cw_ref, cb_ref, gw_ref, gb_ref, lam_ref, o_ref,
                  upad, a_f, b_f, a_b, b_b, eb):
    n = u_l.shape[0]
    n_ctx = u_c.shape[0]
    tot = n + n_ctx
    blk = SUBLANES
    cw = cw_ref[...]
    cb = cb_ref[...]
    pad = SUBLANES

    def conv(u_ref, rows):
        upad[0:pad, :] = jnp.zeros((pad, LANES), F32)
        upad[pad:pad + rows, :] = u_ref[...]
        upad[pad + rows:pad + rows + pad, :] = jnp.zeros((pad, LANES), F32)
        acc = cb + jnp.zeros((rows, LANES), F32)
        for j in range(CONV_W):
            off = pad + j - CONV_W // 2
            acc = acc + cw[j:j + 1, :] * upad[off:off + rows, :]
        return acc

    def gates(uc, rows_f, rows_b):
        ub = uc.astype(BF16)
        for d, (a_ref, b_ref, rows) in enumerate(((a_f, b_f, rows_f), (a_b, b_b, rows_b))):
            lam = lam_ref[d:d + 1, :]
            neg_sp = -(jnp.maximum(-lam, 0.0) + jnp.log(1.0 + jnp.exp(-jnp.abs(lam))))
            r = _sigmoid(_dot(ub, gw_ref[d, 0].astype(BF16)) + gb_ref[d, 0:1, :])
            i = _sigmoid(_dot(ub, gw_ref[d, 1].astype(BF16)) + gb_ref[d, 1:2, :])
            a = jnp.exp((RG_C * neg_sp) * r)
            a_ref[rows, :] = a
            x = (1.0 - a) * (1.0 + a)
            b_ref[rows, :] = (x * lax.rsqrt(jnp.maximum(x, TINY))) * (i * uc)

    gates(conv(u_c, n_ctx), pl.ds(0, n_ctx), pl.ds(n, n_ctx))
    gates(conv(u_l, n), pl.ds(n_ctx, n), pl.ds(0, n))

    n_blk = tot // blk
    h_sum = None
    for a_ref, b_ref, reverse, lat0 in ((a_f, b_f, False, n_ctx), (a_b, b_b, True, 0)):
        chunk = 256
        for c0 in range(0, tot, chunk):
            a1, b1 = _block_scan(a_ref[c0:c0 + chunk, :].reshape(chunk // blk, blk, LANES),
                                 b_ref[c0:c0 + chunk, :].reshape(chunk // blk, blk, LANES), reverse)
            a_ref[c0:c0 + chunk, :] = a1.reshape(chunk, LANES)
            b_ref[c0:c0 + chunk, :] = b1.reshape(chunk, LANES)
        end = 0 if reverse else blk - 1
        ae = a_ref[pl.ds(end, n_blk, stride=blk), :]
        be = b_ref[pl.ds(end, n_blk, stride=blk), :]
        _, he = _scan_steps(ae, be, reverse, 0)
        zero = jnp.zeros((1, LANES), F32)
        if reverse:
            eb[0:n_blk - 1, :] = he[1:, :]
            eb[n_blk - 1:n_blk, :] = zero
        else:
            eb[0:1, :] = zero
            eb[1:n_blk, :] = he[:n_blk - 1, :]

        def apply(j, carry, a_ref=a_ref, b_ref=b_ref):
            rows = pl.ds(pl.multiple_of(j * blk, blk), blk)
            b_ref[rows, :] = b_ref[rows, :] + a_ref[rows, :] * eb[pl.ds(j, 1), :]
            return carry

        lax.fori_loop(0, n_blk, apply, 0, unroll=8)
        h = b_ref[lat0:lat0 + n, :]
        h_sum = h if h_sum is None else h_sum + h

    y = y_l[...]
    gelu = 0.5 * y * (1.0 + jnp.tanh(0.7978845608028654 * (y + 0.044715 * y * y * y)))
    o_ref[...] = (h_sum * gelu).astype(o_ref.dtype)


def _rglru(proj_l, proj_c, conv_w, conv_b, gate_w, gate_b, lam, bsz):
    n = proj_l.shape[0] // bsz
    n_ctx = proj_c.shape[0] // bsz
    w = RG_HEADS * HEAD_W
    tot = n + n_ctx
    return pl.pallas_call(
        _rglru_kernel,
        grid=(bsz, RG_HEADS),
        in_specs=[pl.BlockSpec((n, HEAD_W), lambda b, h: (b, h)),
                  pl.BlockSpec((n, HEAD_W), lambda b, h: (b, RG_HEADS + h)),
                  pl.BlockSpec((n_ctx, HEAD_W), lambda b, h: (b, RG_HEADS + h)),
                  pl.BlockSpec((CONV_W, HEAD_W), lambda b, h: (0, h)),
                  pl.BlockSpec((1, HEAD_W), lambda b, h: (0, h)),
                  pl.BlockSpec((2, 2, None, HEAD_W, HEAD_W), lambda b, h: (0, 0, h, 0, 0)),
                  pl.BlockSpec((2, 2, HEAD_W), lambda b, h: (0, 0, h)),
                  pl.BlockSpec((2, HEAD_W), lambda b, h: (0, h))],
        out_specs=pl.BlockSpec((n, HEAD_W), lambda b, h: (b, h)),
        out_shape=jax.ShapeDtypeStruct((bsz * n, w), BF16),
        scratch_shapes=[pltpu.VMEM((n + 2 * SUBLANES, LANES), F32)]
                       + [pltpu.VMEM((tot, LANES), F32)] * 4
                       + [pltpu.VMEM((tot // SUBLANES, LANES), F32)],
        compiler_params=_cparams(("parallel", "parallel")),
        name="rglru",
    )(proj_l, proj_l, proj_c, conv_w, conv_b.reshape(1, w), gate_w, gate_b, lam)


def _out_proj_kernel(*refs, n_mix, n_groups, epg):
    mix_refs = refs[:n_mix]
    (w_ref, x_ref, gate_ref, g_ref, sh_ref, sc_ref, wr_hi_ref, wr_lo_ref, rb_ref,
     xo_ref, f_ref, gid_ref) = refs[n_mix:]
    d = x_ref.shape[1]
    k0 = 0
    mix = None
    for m_ref in mix_refs:
        kk = m_ref.shape[1]
        part = _dot(m_ref[...], w_ref[k0:k0 + kk, :])
        mix = part if mix is None else mix + part
        k0 += kk
    x = x_ref[...] + gate_ref[0] * mix
    xo_ref[...] = x
    y = x * lax.rsqrt(jnp.mean(x * x, axis=-1, keepdims=True) + EPS) * g_ref[...]
    f = y * (1.0 + sc_ref[0]) + sh_ref[0]
    f_ref[:, 0:d] = f

    f_hi, f_lo = _split_bf16(f)
    lg = _dot(f_hi, wr_hi_ref[...]) + _dot(f_lo, wr_hi_ref[...]) + _dot(f_hi, wr_lo_ref[...]) + rb_ref[...]
    lane = lax.broadcasted_iota(I32, lg.shape, 1).astype(F32)
    neg = -jnp.inf
    big = float(ROUTE_W)
    gl = jnp.where(lane < n_groups, lg, neg)
    gmax = jnp.max(gl, axis=-1, keepdims=True)
    gidx = jnp.min(jnp.where(gl == gmax, lane, big), axis=-1, keepdims=True)
    gw = 1.0 / jnp.sum(jnp.exp(gl - gmax), axis=-1, keepdims=True)
    base = n_groups + gidx * epg
    el = jnp.where((lane >= base) & (lane < base + epg), lg, neg)
    v1 = jnp.max(el, axis=-1, keepdims=True)
    i1 = jnp.min(jnp.where(el == v1, lane, big), axis=-1, keepdims=True)
    el2 = jnp.where(lane == i1, neg, el)
    v2 = jnp.max(el2, axis=-1, keepdims=True)
    i2 = jnp.min(jnp.where(el2 == v2, lane, big), axis=-1, keepdims=True)
    t = jnp.exp(v2 - v1)
    w1 = gw / (1.0 + t)
    w2 = w1 * t
    s1 = i1 - base
    s2 = i2 - base
    lo = jnp.minimum(s1, s2)
    hi = jnp.maximum(s1, s2)
    pair = lo * (2 * epg - 1 - lo) * 0.5 + (hi - lo - 1.0)
    w_lo = jnp.where(s1 < s2, w1, w2)
    w_hi = jnp.where(s1 < s2, w2, w1)
    f_ref[:, d:] = jnp.where(lane == 0.0, w_lo, 0.0) + jnp.where(lane == 1.0, w_hi, 0.0)
    n_pairs = epg * (epg - 1) // 2
    cls = jnp.broadcast_to(gidx * n_pairs + pair, (lg.shape[0], LANES)).T
    gid_ref[...] = cls[0:SUBLANES, :].astype(I32)


def _out_proj(mixes, w_out, x2, gain, mods, w_router, b_router, rows_per_mod, mod_base, tm=256):
    r, d = x2.shape
    k = w_out.shape[0]
    tpm = rows_per_mod // tm
    wr_hi, wr_lo = _split_bf16(w_router)

    def mrow(i):
        return (mod_base + i // tpm) * N_MOD

    const2 = lambda i: (0, 0)
    in_specs = ([pl.BlockSpec((tm, m.shape[1]), lambda i: (i, 0)) for m in mixes]
                + [pl.BlockSpec((k, d), const2),
                   pl.BlockSpec((tm, d), lambda i: (i, 0)),
                   pl.BlockSpec((1, 1, d), lambda i: (mrow(i) + 2, 0, 0)),
                   pl.BlockSpec((1, d), const2),
                   pl.BlockSpec((1, 1, d), lambda i: (mrow(i) + 3, 0, 0)),
                   pl.BlockSpec((1, 1, d), lambda i: (mrow(i) + 4, 0, 0)),
                   pl.BlockSpec((d, ROUTE_W), const2),
                   pl.BlockSpec((d, ROUTE_W), const2),
                   pl.BlockSpec((1, ROUTE_W), const2)])
    x_new, fext, cls = pl.pallas_call(
        functools.partial(_out_proj_kernel, n_mix=len(mixes), n_groups=N_GROUPS, epg=EXPERTS_PER_GROUP),
        grid=(r // tm,),
        in_specs=in_specs,
        out_specs=[pl.BlockSpec((tm, d), lambda i: (i, 0)),
                   pl.BlockSpec((tm, d + ROUTE_W), lambda i: (i, 0)),
                   pl.BlockSpec((SUBLANES, tm), lambda i: (i, 0))],
        out_shape=[jax.ShapeDtypeStruct((r, d), F32),
                   jax.ShapeDtypeStruct((r, d + ROUTE_W), F32),
                   jax.ShapeDtypeStruct((r // tm * SUBLANES, tm), I32)],
        compiler_params=_cparams(("parallel",)),
        name="out_proj",
    )(*mixes, w_out, x2, mods, gain.reshape(1, d), mods, mods, wr_hi, wr_lo, b_router)
    return x_new, fext, cls.reshape(r // tm, SUBLANES, tm)[:, 0, :].reshape(r)


def _row_copy(src, s, dst, t, sem):
    return pltpu.make_async_copy(src.at[pl.ds(s, 1), :], dst.at[pl.ds(t, 1), :], sem)


def _dispatch_kernel(pos_ref, src_ref, dst_in, dst, sem, *, tm):
    del dst_in

    def body(j, carry):
        _row_copy(src_ref, j, dst, pos_ref[0, 0, j], sem).start()
        return carry
    lax.fori_loop(0, tm, body, 0, unroll=ROW_DMA_UNROLL)

    def wbody(j, carry):
        _row_copy(src_ref, 0, dst, 0, sem).wait()
        return carry
    lax.fori_loop(0, tm, wbody, 0, unroll=True)


def _dispatch(src, pos, sorted_in, tm=512):
    r, w = src.shape
    return pl.pallas_call(
        functools.partial(_dispatch_kernel, tm=tm),
        grid=(r // tm,),
        in_specs=[pl.BlockSpec((1, 1, tm), lambda i: (i, 0, 0), memory_space=pltpu.SMEM),
                  pl.BlockSpec((tm, w), lambda i: (i, 0)),
                  pl.BlockSpec(memory_space=pl.ANY)],
        out_specs=pl.BlockSpec(memory_space=pl.ANY),
        out_shape=jax.ShapeDtypeStruct(sorted_in.shape, sorted_in.dtype),
        scratch_shapes=[pltpu.SemaphoreType.DMA(())],
        input_output_aliases={2: 0},
        compiler_params=_cparams(("arbitrary",)),
        name="moe_dispatch",
    )(pos.reshape(r // tm, 1, tm), src, sorted_in)


def _moe_kernel(te_ref, tv_ref, x_ref, w1_ref, w3_ref, w2_ref, o_ref):
    t = pl.program_id(0)
    s = pl.program_id(1)
    d = o_ref.shape[1]
    valid = tv_ref[t] > 0

    @pl.when(valid)
    def _():
        x = x_ref[:, 0:d].astype(BF16)
        h1 = _dot(x, w1_ref[...])
        h3 = _dot(x, w3_ref[...])
        cw = x_ref[:, d:]
        lane = lax.broadcasted_iota(I32, cw.shape, 1)
        cws = jnp.sum(jnp.where(lane == s, cw, 0.0), axis=-1, keepdims=True)
        y = _dot((h1 * _sigmoid(h1) * h3 * cws).astype(BF16), w2_ref[...])

        @pl.when(s == 0)
        def _():
            o_ref[...] = y

        @pl.when(s > 0)
        def _():
            o_ref[...] += y

    @pl.when(jnp.logical_not(valid) & (s == 0))
    def _():
        o_ref[...] = jnp.zeros_like(o_ref)


def _moe_ffn(tile_expert, tile_valid, f_sorted, w1, w3, w2, tm):
    p, dw = f_sorted.shape
    d = dw - ROUTE_W
    fe = w1.shape[2]

    def wmap(t, s, te, tv):
        return (te[2 * t + s], 0, 0)

    grid_spec = pltpu.PrefetchScalarGridSpec(
        num_scalar_prefetch=2,
        grid=(p // tm, 2),
        in_specs=[pl.BlockSpec((tm, dw), lambda t, s, te, tv: (t, 0)),
                  pl.BlockSpec((None, d, fe), wmap),
                  pl.BlockSpec((None, d, fe), wmap),
                  pl.BlockSpec((None, fe, d), wmap)],
        out_specs=pl.BlockSpec((tm, d), lambda t, s, te, tv: (t, 0)),
    )
    return pl.pallas_call(
        _moe_kernel,
        grid_spec=grid_spec,
        out_shape=jax.ShapeDtypeStruct((p, d), F32),
        compiler_params=_cparams(("arbitrary", "arbitrary")),
        name="moe_ffn",
    )(tile_expert, tile_valid, f_sorted, w1, w3, w2)


def _combine_kernel(pos_cur, pos_nxt, y_hbm, x_ref, gate_ref, o_ref, ybuf, sem, *, tm):
    i = pl.program_id(0)
    n_steps = pl.num_programs(0)

    def issue(pos_ref, slot):
        def body(j, carry):
            _row_copy(y_hbm, pos_ref[0, 0, j], ybuf.at[slot], j, sem.at[slot]).start()
            return carry
        lax.fori_loop(0, tm, body, 0, unroll=ROW_DMA_UNROLL)

    @pl.when(i == 0)
    def _():
        issue(pos_cur, 0)

    @pl.when(i + 1 < n_steps)
    def _():
        issue(pos_nxt, (i + 1) % 2)

    slot = i % 2

    def wbody(j, carry):
        _row_copy(y_hbm, 0, ybuf.at[slot], 0, sem.at[slot]).wait()
        return carry
    lax.fori_loop(0, tm, wbody, 0, unroll=True)
    o_ref[...] = x_ref[...] + gate_ref[0] * ybuf[slot]


def _combine(y_sorted, pos, x2, mods, rows_per_mod, mod_base, tm=256):
    r, d = x2.shape
    n_steps = r // tm
    tpm = rows_per_mod // tm
    pos3 = pos.reshape(n_steps, 1, tm)
    return pl.pallas_call(
        functools.partial(_combine_kernel, tm=tm),
        grid=(n_steps,),
        in_specs=[pl.BlockSpec((1, 1, tm), lambda i: (i, 0, 0), memory_space=pltpu.SMEM),
                  pl.BlockSpec((1, 1, tm), lambda i: (jnp.minimum(i + 1, n_steps - 1), 0, 0),
                               memory_space=pltpu.SMEM),
                  pl.BlockSpec(memory_space=pl.ANY),
                  pl.BlockSpec((tm, d), lambda i: (i, 0)),
                  pl.BlockSpec((1, 1, d), lambda i: ((mod_base + i // tpm) * N_MOD + 5, 0, 0))],
        out_specs=pl.BlockSpec((tm, d), lambda i: (i, 0)),
        out_shape=jax.ShapeDtypeStruct((r, d), F32),
        scratch_shapes=[pltpu.VMEM((2, tm, d), F32), pltpu.SemaphoreType.DMA((2,))],
        compiler_params=_cparams(("arbitrary",)),
        name="moe_combine",
    )(pos3, pos3, y_sorted, x2, mods)


def _sorted_positions(gids, tm, p_rows):
    g = jnp.concatenate(gids)
    r = g.shape[0]
    epg = EXPERTS_PER_GROUP
    n_pairs = epg * (epg - 1) // 2
    n_cls = N_GROUPS * n_pairs
    onehot = (g[:, None] == jnp.arange(n_cls, dtype=I32)[None, :]).astype(I32)
    counts = jnp.sum(onehot, axis=0)
    rank = jnp.sum((jnp.cumsum(onehot, axis=0) - 1) * onehot, axis=1)
    padded = ((counts + tm - 1) // tm) * tm
    ends = jnp.cumsum(padded)
    pos = jnp.sum(onehot * (ends - padded)[None, :], axis=1) + rank
    p = r + n_cls * tm if p_rows is None else p_rows
    assert p >= r + n_cls * tm and p % tm == 0
    tile_start = jnp.arange(p // tm, dtype=I32) * tm
    tile_cls = jnp.minimum(jnp.sum((tile_start[:, None] >= ends[None, :]).astype(I32), axis=1), n_cls - 1)
    tile_valid = (tile_start < ends[-1]).astype(I32)
    pairs = [(i, j) for i in range(epg) for j in range(i + 1, epg)]
    pair_lo = jnp.array([a for a, _ in pairs], I32)
    pair_hi = jnp.array([b for _, b in pairs], I32)
    grp = tile_cls // n_pairs
    tile_expert = jnp.stack([grp * epg + pair_lo[tile_cls % n_pairs],
                             grp * epg + pair_hi[tile_cls % n_pairs]], axis=1).reshape(-1)
    return pos.astype(I32), tile_expert.astype(I32), tile_valid, p


def _moe(fexts, gids, w1, w3, w2, f_sorted=None, tm=512):
    pos, tile_expert, tile_valid, p = _sorted_positions(gids, tm, None if f_sorted is None else f_sorted.shape[0])
    sizes = [f.shape[0] for f in fexts]
    poss, off = [], 0
    for s in sizes:
        poss.append(lax.slice(pos, (off,), (off + s,)))
        off += s
    if f_sorted is None:
        f_sorted = jnp.zeros((p, fexts[0].shape[1]), F32)
    for fext, ps in zip(fexts, poss):
        f_sorted = _dispatch(fext, ps, f_sorted)
    y_sorted = _moe_ffn(tile_expert, tile_valid, f_sorted, w1, w3, w2, tm)
    return y_sorted, poss, f_sorted


def kernel(x, c, ctx, c_ctx, ada_w, ada_b, norm_mix, norm_ffn, even_w_in, even_w_out, hgrn_lb_logits, hgrn_out_norm, diff_qk_norm, diff_lambda, diff_out_norm, odd_w_in, odd_conv_w, odd_conv_b, rg_gate_w, rg_gate_b, rg_lambda, odd_w_out, moe_w_grp, moe_b_grp, moe_w_exp, moe_b_exp, moe_w1, moe_w3, moe_w2):
    bsz, n, d = x.shape
    n_ctx = ctx.shape[1]
    depth = ada_w.shape[0]
    assert depth == 2

    cvec = jnp.zeros((MOD_ROWS, d), F32).at[:bsz].set(c).at[bsz].set(c_ctx)
    mods_all = _ada_all(cvec, ada_w, ada_b).reshape(depth, MOD_ROWS * N_MOD, 1, d)
    x_l = x.reshape(bsz * n, d)
    x_c = ctx.reshape(bsz * n_ctx, d)
    lb_all = jnp.cumsum(jax.nn.softmax(hgrn_lb_logits.astype(F32), axis=1), axis=1)

    def router(l):
        w = jnp.concatenate([moe_w_grp[l], moe_w_exp[l]], axis=1)
        b = jnp.concatenate([moe_b_grp[l], moe_b_exp[l]])
        padw = ROUTE_W - w.shape[1]
        return jnp.pad(w, ((0, 0), (0, padw))), jnp.pad(b, (0, padw)).reshape(1, ROUTE_W)

    l = 0
    mods = mods_all[l]
    lam_init = 0.8 - 0.6 * math.exp(-0.3 * l)
    lv = diff_lambda[0].astype(F32)
    lam = jnp.exp(jnp.sum(lv[0] * lv[1])) - jnp.exp(jnp.sum(lv[2] * lv[3])) + lam_init
    w_in = even_w_in[0].astype(BF16)
    proj_l = _norm_mod_mm(x_l, norm_mix[l], mods, w_in, n, 0, tm=1024)
    proj_c = _norm_mod_mm(x_c, norm_mix[l], mods, w_in, bsz * n_ctx, bsz, tm=1024)
    a_l, a_c = _hgrn(proj_l, proj_c, lb_all[0, 0], lb_all[1, 0], hgrn_out_norm[0], bsz)
    b_l, b_c = _attn(proj_l, proj_c, diff_qk_norm[0], diff_out_norm[0], lam, lam_init, bsz)
    w_out = even_w_out[0].astype(BF16)
    wr, br = router(l)
    x_l, f_l, g_l = _out_proj([a_l, b_l], w_out, x_l, norm_ffn[l], mods, wr, br, n, 0)
    x_c, f_c, g_c = _out_proj([a_c, b_c], w_out, x_c, norm_ffn[l], mods, wr, br, bsz * n_ctx, bsz)
    y_sorted, (pos_l, pos_c), f_sorted = _moe([f_l, f_c], [g_l, g_c], moe_w1[l].astype(BF16),
                                              moe_w3[l].astype(BF16), moe_w2[l].astype(BF16))
    x_l = _combine(y_sorted, pos_l, x_l, mods, n, 0)
    x_c = _combine(y_sorted, pos_c, x_c, mods, bsz * n_ctx, bsz)

    l = 1
    mods = mods_all[l]
    w_in = odd_w_in[0].astype(BF16)
    proj_l = _norm_mod_mm(x_l, norm_mix[l], mods, w_in, n, 0, tm=1024)
    proj_c = _norm_mod_mm(x_c, norm_mix[l], mods, w_in, bsz * n_ctx, bsz, tm=1024)
    gated = _rglru(proj_l, proj_c, odd_conv_w[0], odd_conv_b[0], rg_gate_w[0], rg_gate_b[0], rg_lambda[0], bsz)
    wr, br = router(l)
    x_l, f_l, g_l = _out_proj([gated], odd_w_out[0].astype(BF16), x_l, norm_ffn[l], mods, wr, br, n, 0)
    y_sorted, (pos_l,), _ = _moe([f_l], [g_l], moe_w1[l].astype(BF16), moe_w3[l].astype(BF16),
                                 moe_w2[l].astype(BF16), f_sorted)
    return _combine(y_sorted, pos_l, x_l, mods, n, 0).reshape(bsz, n, d)
```

```python
import functools
import math

import jax
import jax.numpy as jnp
from jax import lax
from jax.experimental import pallas as pl
from jax.experimental.pallas import tpu as pltpu

F32 = jnp.float32
BF16 = jnp.bfloat16
I32 = jnp.int32

EPS = 1e-6
LOG2_E = 1.4426950408889634
TINY = 1e-30
LANES = 128
SUBLANES = 8
GRID_W = 64
A_HEADS = 8
HEAD_W = 128
HGRN_CHUNK = 32
HGRN_BLOCK = 256
B_HEADS = 8
B_DH = 64
ROPE_BASE = 10000.0
RG_HEADS = 16
CONV_W = 4
RG_C = 8.0
N_GROUPS = 4
EXPERTS_PER_GROUP = 4
N_EXPERTS = 16
N_MOD = 6
MOD_ROWS = 16
ROUTE_W = LANES
ROW_DMA_UNROLL = 8

VMEM_LIMIT = 56 * 1024 * 1024


def _cparams(sem):
    return pltpu.CompilerParams(dimension_semantics=sem, vmem_limit_bytes=VMEM_LIMIT)


def _sigmoid(x):
    return 0.5 * jnp.tanh(0.5 * x) + 0.5


def _dot(a, b):
    return jnp.dot(a, b, preferred_element_type=F32)


def _dot_nt(a, b):
    return lax.dot_general(a, b, (((1,), (1,)), ((), ())), preferred_element_type=F32)


def _dot_tn(a, b):
    return lax.dot_general(a, b, (((0,), (0,)), ((), ())), preferred_element_type=F32)


def _split_bf16(x):
    hi = x.astype(BF16)
    return hi, (x - hi.astype(F32)).astype(BF16)


def _ada_kernel(c_ref, w_ref, b_ref, o_ref):
    c = c_ref[...]
    a = (c * _sigmoid(c)).astype(BF16)
    o_ref[...] = _dot(a, w_ref[...].astype(BF16)) + b_ref[...]


def _ada_all(cvec, ada_w, ada_b, tn=1024):
    depth, d, n = ada_w.shape
    return pl.pallas_call(
        _ada_kernel,
        grid=(depth, n // tn),
        in_specs=[pl.BlockSpec((MOD_ROWS, d), lambda l, j: (0, 0)),
                  pl.BlockSpec((None, d, tn), lambda l, j: (l, 0, j)),
                  pl.BlockSpec((None, 1, tn), lambda l, j: (l, 0, j))],
        out_specs=pl.BlockSpec((None, MOD_ROWS, tn), lambda l, j: (l, 0, j)),
        out_shape=jax.ShapeDtypeStruct((depth, MOD_ROWS, n), F32),
        compiler_params=_cparams(("arbitrary", "arbitrary")),
        name="ada_mod",
    )(cvec, ada_w, ada_b.reshape(depth, 1, n))


def _norm_mod_mm_kernel(x_ref, g_ref, sh_ref, sc_ref, w_ref, o_ref, h_scr):
    @pl.when(pl.program_id(1) == 0)
    def _():
        x = x_ref[...]
        y = x * lax.rsqrt(jnp.mean(x * x, axis=-1, keepdims=True) + EPS) * g_ref[...]
        h_scr[...] = (y * (1.0 + sc_ref[0]) + sh_ref[0]).astype(BF16)

    o_ref[...] = _dot(h_scr[...], w_ref[...].astype(BF16)).astype(o_ref.dtype)


def _norm_mod_mm(x2, gain, mods, w, rows_per_mod, mod_base, tm, tn=512):
    r, d = x2.shape
    n = w.shape[2]
    tpm = rows_per_mod // tm

    def mrow(i):
        return (mod_base + i // tpm) * N_MOD

    return pl.pallas_call(
        _norm_mod_mm_kernel,
        grid=(r // tm, n // tn),
        in_specs=[pl.BlockSpec((tm, d), lambda i, j: (i, 0)),
                  pl.BlockSpec((1, d), lambda i, j: (0, 0)),
                  pl.BlockSpec((1, 1, d), lambda i, j: (mrow(i) + 0, 0, 0)),
                  pl.BlockSpec((1, 1, d), lambda i, j: (mrow(i) + 1, 0, 0)),
                  pl.BlockSpec((None, d, tn), lambda i, j: (0, 0, j))],
        out_specs=pl.BlockSpec((tm, tn), lambda i, j: (i, j)),
        out_shape=jax.ShapeDtypeStruct((r, n), F32),
        scratch_shapes=[pltpu.VMEM((tm, d), BF16)],
        compiler_params=_cparams(("parallel", "arbitrary")),
        name="norm_mod_mm",
    )(x2, gain.reshape(1, d), mods, mods, w)


def _hgrn_kernel(ql, ffl, fbl, vl, gl, qc, ffc, fbc, vc, gc, lbf_ref, lbb_ref, gain_ref,
                 ol_ref, oc_ref,
                 oi_l, oi_c, qtf_l, qtb_l, qtf_c, qtb_c, kv_f, kv_b, dec_f, dec_b):
    c_sz = HGRN_CHUNK
    blk = HGRN_BLOCK
    cpb = blk // c_sz
    nb_l = ql.shape[0] // blk
    nc_l = ql.shape[0] // c_sz
    nc_c = qc.shape[0] // c_sz
    nc = nc_l + nc_c
    assert qc.shape[0] == blk
    row = lax.broadcasted_iota(I32, (blk, blk), 0)
    col = lax.broadcasted_iota(I32, (blk, blk), 1)
    same = (row // c_sz) == (col // c_sz)
    tril = same & (row >= col)
    triu = same & (row <= col)
    lbf = lbf_ref[...]
    lbb = lbb_ref[...]

    exp_mask = (lax.broadcasted_iota(I32, (blk, cpb * HEAD_W), 0) // c_sz
                == lax.broadcasted_iota(I32, (blk, cpb * HEAD_W), 1) // HEAD_W)

    def local_terms(r0, q_ref, ff_ref, fb_ref, v_ref, oi_ref, qtf_ref, qtb_ref, gid_f, gid_b):
        rows = pl.ds(r0, blk)
        q = q_ref[rows, :]
        qs = q * _sigmoid(q)
        vb = v_ref[rows, :].astype(BF16)
        vt = v_ref[rows, :].T.astype(BF16)
        two = (0, 1)
        mask = (tril, triu)
        last = (c_sz - 1, 0)
        f = [lb + (1.0 - lb) * _sigmoid(ref[rows, :]) for ref, lb in ((ff_ref, lbf), (fb_ref, lbb))]
        k = [1.0 - f[d] for d in two]
        lf2 = [jnp.concatenate(_split_bf16(jnp.log(f[d])), axis=1) for d in two]
        cum2 = [_dot(jnp.where(mask[d], 1.0, 0.0).astype(BF16), lf2[d]) for d in two]
        cum = [cum2[d][:, :HEAD_W] + cum2[d][:, HEAD_W:] for d in two]
        tot = [jnp.concatenate(
            [jnp.broadcast_to(cum[d][c * c_sz + last[d]:c * c_sz + last[d] + 1, :], (c_sz, HEAD_W))
             for c in range(cpb)], axis=0) for d in two]
        e = [jnp.exp(cum[d]) for d in two]
        qt = [(qs * e[d]).astype(BF16) for d in two]
        kt = [(k[d] * jnp.exp(-cum[d])).astype(BF16) for d in two]
        att = [jnp.where(mask[d], _dot_nt(qt[d], kt[d]), 0.0).astype(BF16) for d in two]
        oi_ref[rows, :] = _dot(att[0], vb) + _dot(att[1], vb)
        k2 = [(k[d] * jnp.exp(tot[d] - cum[d])).astype(BF16) for d in two]
        k2x = [jnp.where(exp_mask, jnp.concatenate([k2[d]] * cpb, axis=1), jnp.zeros((), BF16)) for d in two]
        kvs = [_dot(vt, k2x[d]) for d in two]
        for d, (qt_ref, kv_ref, dec_ref, gid) in enumerate(((qtf_ref, kv_f, dec_f, gid_f),
                                                            (qtb_ref, kv_b, dec_b, gid_b))):
            for c in range(cpb):
                kv_ref[gid + c] = kvs[d][:, c * HEAD_W:(c + 1) * HEAD_W]
                dec_ref[gid + c] = e[d][c * c_sz + last[d]:c * c_sz + last[d] + 1, :]
            qt_ref[rows, :] = qt[d]

    local_terms(0, qc, ffc, fbc, vc, oi_c, qtf_c, qtb_c, 0, nc_l)

    def local_l(i, carry):
        local_terms(pl.multiple_of(i * blk, blk), ql, ffl, fbl, vl, oi_l, qtf_l, qtb_l,
                    nc_c + i * cpb, i * cpb)
        return carry

    lax.fori_loop(0, nb_l, local_l, 0, unroll=2)

    def rec_f(i, s):
        new = s * dec_f[i] + kv_f[i]
        kv_f[i] = s
        return new

    def rec_b(i, s):
        j = nc - 1 - i
        new = s * dec_b[j] + kv_b[j]
        kv_b[j] = s
        return new

    s0 = jnp.zeros((HEAD_W, HEAD_W), F32)
    lax.fori_loop(0, nc, rec_f, s0, unroll=2)
    lax.fori_loop(0, nc, rec_b, s0, unroll=2)

    gain = gain_ref[...]

    def finish(r0, g_ref, oi_ref, qtf_ref, qtb_ref, o_ref, gid_f, gid_b):
        parts = []
        for c in range(cpb):
            rows_c = pl.ds(r0 + c * c_sz, c_sz)
            parts.append(_dot_nt(qtf_ref[rows_c, :], kv_f[gid_f + c].astype(BF16))
                         + _dot_nt(qtb_ref[rows_c, :], kv_b[gid_b + c].astype(BF16)))
        rows = pl.ds(r0, blk)
        o = oi_ref[rows, :] + jnp.concatenate(parts, axis=0)
        y = o * lax.rsqrt(jnp.mean(o * o, axis=-1, keepdims=True) + EPS) * gain
        g = g_ref[rows, :]
        o_ref[rows, :] = (y * (g * _sigmoid(g))).astype(o_ref.dtype)

    finish(0, gc, oi_c, qtf_c, qtb_c, oc_ref, 0, nc_l)

    def fin_l(i, carry):
        finish(pl.multiple_of(i * blk, blk), gl, oi_l, qtf_l, qtb_l, ol_ref, nc_c + i * cpb, i * cpb)
        return carry

    lax.fori_loop(0, nb_l, fin_l, 0, unroll=True)


def _hgrn(proj_l, proj_c, lb_f, lb_b, gain, bsz):
    n = proj_l.shape[0] // bsz
    n_ctx = proj_c.shape[0] // bsz
    nc = (n + n_ctx) // HGRN_CHUNK
    w = A_HEADS * HEAD_W

    def col(k):
        return lambda b, h: (b, k * A_HEADS + h)

    in_specs = ([pl.BlockSpec((n, HEAD_W), col(k)) for k in range(5)]
                + [pl.BlockSpec((n_ctx, HEAD_W), col(k)) for k in range(5)]
                + [pl.BlockSpec((1, HEAD_W), lambda b, h: (0, h)),
                   pl.BlockSpec((1, HEAD_W), lambda b, h: (0, h)),
                   pl.BlockSpec((1, HEAD_W), lambda b, h: (0, 0))])
    return pl.pallas_call(
        _hgrn_kernel,
        grid=(bsz, A_HEADS),
        in_specs=in_specs,
        out_specs=[pl.BlockSpec((n, HEAD_W), lambda b, h: (b, h)),
                   pl.BlockSpec((n_ctx, HEAD_W), lambda b, h: (b, h))],
        out_shape=[jax.ShapeDtypeStruct((bsz * n, w), BF16),
                   jax.ShapeDtypeStruct((bsz * n_ctx, w), BF16)],
        scratch_shapes=[pltpu.VMEM((n, HEAD_W), F32), pltpu.VMEM((n_ctx, HEAD_W), F32),
                        pltpu.VMEM((n, HEAD_W), BF16), pltpu.VMEM((n, HEAD_W), BF16),
                        pltpu.VMEM((n_ctx, HEAD_W), BF16), pltpu.VMEM((n_ctx, HEAD_W), BF16),
                        pltpu.VMEM((nc, HEAD_W, HEAD_W), F32), pltpu.VMEM((nc, HEAD_W, HEAD_W), F32),
                        pltpu.VMEM((nc, 1, HEAD_W), F32), pltpu.VMEM((nc, 1, HEAD_W), F32)],
        compiler_params=_cparams(("parallel", "parallel")),
        name="hgrn2",
    )(*([proj_l] * 5 + [proj_c] * 5), lb_f.reshape(1, w), lb_b.reshape(1, w), gain.reshape(1, HEAD_W))


def _half_mean_matrix():
    r = lax.broadcasted_iota(I32, (LANES, LANES), 0) // B_DH
    c = lax.broadcasted_iota(I32, (LANES, LANES), 1) // B_DH
    return (r == c).astype(BF16)


def _qk_prep(t, gain, bd, cos, sin):
    sq_hi, sq_lo = _split_bf16(t * t)
    ms = (_dot(sq_hi, bd) + _dot(sq_lo, bd)) * (1.0 / B_DH)
    y = t * lax.rsqrt(ms + EPS) * gain
    if cos is None:
        return y
    lane = lax.broadcasted_iota(I32, y.shape, 1)
    partner = jnp.where(lane % 2 == 0, pltpu.roll(y, LANES - 1, 1), pltpu.roll(y, 1, 1))
    return y * cos + partner * sin


def _attn_kernel(q_l, q_c, k_l, k_c, v_l, v_c, cosq, sinq, cosk, sink, gq_ref, gk_ref, go_ref, lam_ref,
                 o_l, o_c, kp, vp, *, out_scale):
    qb = pl.program_id(2)
    n_ctx = k_c.shape[0]
    bd = _half_mean_matrix()
    lane = lax.broadcasted_iota(I32, (1, LANES), 1)
    m1 = (lane < B_DH).astype(F32)
    m2 = 1.0 - m1
    lam = lam_ref[:, 0:1]

    @pl.when(qb == 0)
    def _():
        kp[0:n_ctx, :] = _qk_prep(k_c[...], gk_ref[...], bd, None, None).astype(BF16)
        kp[n_ctx:, :] = _qk_prep(k_l[...], gk_ref[...], bd, cosk[...], sink[...]).astype(BF16)
        ones_col = (lax.broadcasted_iota(I32, (1, LANES), 1) == 0).astype(BF16)
        vp[0:n_ctx, 0:HEAD_W] = v_c[...].astype(BF16)
        vp[n_ctx:, 0:HEAD_W] = v_l[...].astype(BF16)
        vp[:, HEAD_W:] = jnp.broadcast_to(ones_col, (vp.shape[0], LANES))

    def attend(q, k_ref, v_ref, n_keys):
        q = q * (B_DH ** -0.5 * LOG2_E)
        s = [_dot_nt((q * msk).astype(BF16), k_ref[0:n_keys, :]) for msk in (m1, m2)]
        e = [jnp.exp2(t - jnp.max(t, axis=-1, keepdims=True)).astype(BF16) for t in s]
        r = [_dot(t, v_ref[0:n_keys, :]) for t in e]
        o = (r[0][:, 0:HEAD_W] * (1.0 / r[0][:, HEAD_W:HEAD_W + 1])
             - r[1][:, 0:HEAD_W] * (lam / r[1][:, HEAD_W:HEAD_W + 1]))
        y = o * lax.rsqrt(jnp.mean(o * o, axis=-1, keepdims=True) + EPS) * go_ref[...]
        return (y * out_scale).astype(BF16)

    @pl.when(qb == 0)
    def _():
        q = _qk_prep(q_c[...], gq_ref[...], bd, None, None)
        o_c[...] = attend(q, kp, vp, n_ctx)

    @pl.when(qb > 0)
    def _():
        q = _qk_prep(q_l[...], gq_ref[...], bd, cosq[...], sinq[...])
        o_l[...] = attend(q, kp, vp, kp.shape[0])


def _rope_tables(n):
    n_rows = n // GRID_W
    rowp = jnp.repeat(jnp.arange(n_rows), GRID_W).astype(F32)
    colp = jnp.tile(jnp.arange(GRID_W), n_rows).astype(F32)
    pairs = B_DH // 4
    inv = ROPE_BASE ** (-jnp.arange(pairs, dtype=F32) / pairs)
    ang = jnp.concatenate([rowp[:, None] * inv, colp[:, None] * inv], axis=-1)
    cos = jnp.repeat(jnp.cos(ang), 2, axis=-1)
    sin = jnp.repeat(jnp.sin(ang), 2, axis=-1) * jnp.tile(jnp.array([-1.0, 1.0], F32), B_DH // 2)
    return jnp.tile(cos, (1, 2)), jnp.tile(sin, (1, 2))


def _attn(proj_l, proj_c, qk_gain, out_gain, lam, lam_init, bsz, tq=512):
    n = proj_l.shape[0] // bsz
    n_ctx = proj_c.shape[0] // bsz
    nqb = n // tq
    w = B_HEADS * HEAD_W
    cos, sin = _rope_tables(n)
    qcol, kcol, vcol = 5 * A_HEADS, 5 * A_HEADS + B_HEADS, 5 * A_HEADS + 2 * B_HEADS

    def lat_q(b, h, qb):
        return (b * nqb + jnp.maximum(qb - 1, 0), qcol + h)

    in_specs = [pl.BlockSpec((tq, HEAD_W), lat_q),
                pl.BlockSpec((n_ctx, HEAD_W), lambda b, h, qb: (b, qcol + h)),
                pl.BlockSpec((n, HEAD_W), lambda b, h, qb: (b, kcol + h)),
                pl.BlockSpec((n_ctx, HEAD_W), lambda b, h, qb: (b, kcol + h)),
                pl.BlockSpec((n, HEAD_W), lambda b, h, qb: (b, vcol + h)),
                pl.BlockSpec((n_ctx, HEAD_W), lambda b, h, qb: (b, vcol + h)),
                pl.BlockSpec((tq, LANES), lambda b, h, qb: (jnp.maximum(qb - 1, 0), 0)),
                pl.BlockSpec((tq, LANES), lambda b, h, qb: (jnp.maximum(qb - 1, 0), 0)),
                pl.BlockSpec((n, LANES), lambda b, h, qb: (0, 0)),
                pl.BlockSpec((n, LANES), lambda b, h, qb: (0, 0)),
                pl.BlockSpec((1, LANES), lambda b, h, qb: (0, 0)),
                pl.BlockSpec((1, LANES), lambda b, h, qb: (0, 0)),
                pl.BlockSpec((1, LANES), lambda b, h, qb: (0, 0)),
                pl.BlockSpec((1, LANES), lambda b, h, qb: (0, 0))]
    return pl.pallas_call(
        functools.partial(_attn_kernel, out_scale=1.0 - lam_init),
        grid=(bsz, B_HEADS, nqb + 1),
        in_specs=in_specs,
        out_specs=[pl.BlockSpec((tq, HEAD_W), lambda b, h, qb: (b * nqb + jnp.maximum(qb - 1, 0), h)),
                   pl.BlockSpec((n_ctx, HEAD_W), lambda b, h, qb: (b, h))],
        out_shape=[jax.ShapeDtypeStruct((bsz * n, w), BF16),
                   jax.ShapeDtypeStruct((bsz * n_ctx, w), BF16)],
        scratch_shapes=[pltpu.VMEM((n + n_ctx, HEAD_W), BF16), pltpu.VMEM((n + n_ctx, HEAD_W + LANES), BF16)],
        compiler_params=_cparams(("parallel", "parallel", "arbitrary")),
        name="diff_attn",
    )(proj_l, proj_c, proj_l, proj_c, proj_l, proj_c, cos, sin, cos, sin,
      jnp.tile(qk_gain[0], 2).reshape(1, LANES), jnp.tile(qk_gain[1], 2).reshape(1, LANES),
      out_gain.reshape(1, LANES), jnp.full((1, LANES), lam, F32))


def _scan_steps(a, b, reverse, axis):
    n = a.shape[axis]
    pos = lax.broadcasted_iota(I32, a.shape, axis)
    s = 1
    while s < n:
        keep = (pos < n - s) if reverse else (pos >= s)
        shift = n - s if reverse else s
        a_sh = jnp.where(keep, pltpu.roll(a, shift, axis), 1.0)
        b_sh = jnp.where(keep, pltpu.roll(b, shift, axis), 0.0)
        b = a * b_sh + b
        a = a * a_sh
        s *= 2
    return a, b


def _block_scan(a, b, reverse):
    return _scan_steps(a, b, reverse, 1)


def _rglru_kernel(y_l, u_l, u_c, cw_ref, cb_ref, gw_ref, gb_ref, lam_ref, o_ref,
                  upad, a_f, b_f, a_b, b_b, eb):
    n = u_l.shape[0]
    n_ctx = u_c.shape[0]
    tot = n + n_ctx
    blk = SUBLANES
    cw = cw_ref[...]
    cb = cb_ref[...]
    pad = SUBLANES

    def conv(u_ref, rows):
        upad[0:pad, :] = jnp.zeros((pad, LANES), F32)
        upad[pad:pad + rows, :] = u_ref[...]
        upad[pad + rows:pad + rows + pad, :] = jnp.zeros((pad, LANES), F32)
        acc = cb + jnp.zeros((rows, LANES), F32)
        for j in range(CONV_W):
            off = pad + j - CONV_W // 2
            acc = acc + cw[j:j + 1, :] * upad[off:off + rows, :]
        return acc

    def gates(uc, rows_f, rows_b):
        ub = uc.astype(BF16)
        for d, (a_ref, b_ref, rows) in enumerate(((a_f, b_f, rows_f), (a_b, b_b, rows_b))):
            lam = lam_ref[d:d + 1, :]
            neg_sp = -(jnp.maximum(-lam, 0.0) + jnp.log(1.0 + jnp.exp(-jnp.abs(lam))))
            r = _sigmoid(_dot(ub, gw_ref[d, 0].astype(BF16)) + gb_ref[d, 0:1, :])
            i = _sigmoid(_dot(ub, gw_ref[d, 1].astype(BF16)) + gb_ref[d, 1:2, :])
            a = jnp.exp((RG_C * neg_sp) * r)
            a_ref[rows, :] = a
            x = (1.0 - a) * (1.0 + a)
            b_ref[rows, :] = (x * lax.rsqrt(jnp.maximum(x, TINY))) * (i * uc)

    gates(conv(u_c, n_ctx), pl.ds(0, n_ctx), pl.ds(n, n_ctx))
    gates(conv(u_l, n), pl.ds(n_ctx, n), pl.ds(0, n))

    n_blk = tot // blk
    h_sum = None
    for a_ref, b_ref, reverse, lat0 in ((a_f, b_f, False, n_ctx), (a_b, b_b, True, 0)):
        chunk = 256
        for c0 in range(0, tot, chunk):
            a1, b1 = _block_scan(a_ref[c0:c0 + chunk, :].reshape(chunk // blk, blk, LANES),
                                 b_ref[c0:c0 + chunk, :].reshape(chunk // blk, blk, LANES), reverse)
            a_ref[c0:c0 + chunk, :] = a1.reshape(chunk, LANES)
            b_ref[c0:c0 + chunk, :] = b1.reshape(chunk, LANES)
        end = 0 if reverse else blk - 1
        ae = a_ref[pl.ds(end, n_blk, stride=blk), :]
        be = b_ref[pl.ds(end, n_blk, stride=blk), :]
        _, he = _scan_steps(ae, be, reverse, 0)
        zero = jnp.zeros((1, LANES), F32)
        if reverse:
            eb[0:n_blk - 1, :] = he[1:, :]
            eb[n_blk - 1:n_blk, :] = zero
        else:
            eb[0:1, :] = zero
            eb[1:n_blk, :] = he[:n_blk - 1, :]

        def apply(j, carry, a_ref=a_ref, b_ref=b_ref):
            rows = pl.ds(pl.multiple_of(j * blk, blk), blk)
            b_ref[rows, :] = b_ref[rows, :] + a_ref[rows, :] * eb[pl.ds(j, 1), :]
            return carry

        lax.fori_loop(0, n_blk, apply, 0, unroll=8)
        h = b_ref[lat0:lat0 + n, :]
        h_sum = h if h_sum is None else h_sum + h

    y = y_l[...]
    gelu = 0.5 * y * (1.0 + jnp.tanh(0.7978845608028654 * (y + 0.044715 * y * y * y)))
    o_ref[...] = (h_sum * gelu).astype(o_ref.dtype)


def _rglru(proj_l, proj_c, conv_w, conv_b, gate_w, gate_b, lam, bsz):
    n = proj_l.shape[0] // bsz
    n_ctx = proj_c.shape[0] // bsz
    w = RG_HEADS * HEAD_W
    tot = n + n_ctx
    return pl.pallas_call(
        _rglru_kernel,
        grid=(bsz, RG_HEADS),
        in_specs=[pl.BlockSpec((n, HEAD_W), lambda b, h: (b, h)),
                  pl.BlockSpec((n, HEAD_W), lambda b, h: (b, RG_HEADS + h)),
                  pl.BlockSpec((n_ctx, HEAD_W), lambda b, h: (b, RG_HEADS + h)),
                  pl.BlockSpec((CONV_W, HEAD_W), lambda b, h: (0, h)),
                  pl.BlockSpec((1, HEAD_W), lambda b, h: (0, h)),
                  pl.BlockSpec((2, 2, None, HEAD_W, HEAD_W), lambda b, h: (0, 0, h, 0, 0)),
                  pl.BlockSpec((2, 2, HEAD_W), lambda b, h: (0, 0, h)),
                  pl.BlockSpec((2, HEAD_W), lambda b, h: (0, h))],
        out_specs=pl.BlockSpec((n, HEAD_W), lambda b, h: (b, h)),
        out_shape=jax.ShapeDtypeStruct((bsz * n, w), BF16),
        scratch_shapes=[pltpu.VMEM((n + 2 * SUBLANES, LANES), F32)]
                       + [pltpu.VMEM((tot, LANES), F32)] * 4
                       + [pltpu.VMEM((tot // SUBLANES, LANES), F32)],
        compiler_params=_cparams(("parallel", "parallel")),
        name="rglru",
    )(proj_l, proj_l, proj_c, conv_w, conv_b.reshape(1, w), gate_w, gate_b, lam)


def _out_proj_kernel(*refs, n_mix, n_groups, epg, row_chunk):
    mix_refs = refs[:n_mix]
    (w_ref, x_ref, gate_ref, g_ref, sh_ref, sc_ref, wr_ref, rb_ref,
     xo_ref, f_ref, gid_ref) = refs[n_mix:]
    d = x_ref.shape[1]
    n_pairs = epg * (epg - 1) // 2
    for r0 in range(0, x_ref.shape[0], row_chunk):
        rows = slice(r0, r0 + row_chunk)
        k0 = 0
        mix = None
        for m_ref in mix_refs:
            kk = m_ref.shape[1]
            part = _dot(m_ref[rows, :], w_ref[k0:k0 + kk, :])
            mix = part if mix is None else mix + part
            k0 += kk
        x = x_ref[rows, :] + gate_ref[0] * mix
        xo_ref[rows, :] = x
        y = x * lax.rsqrt(jnp.mean(x * x, axis=-1, keepdims=True) + EPS) * g_ref[...]
        f = y * (1.0 + sc_ref[0]) + sh_ref[0]
        f_ref[rows, 0:d] = f

        f_hi, f_lo = _split_bf16(f)
        hh = _dot(f_hi, wr_ref[...])
        lg = hh[:, 0:ROUTE_W] + hh[:, ROUTE_W:] + _dot(f_lo, wr_ref[:, 0:ROUTE_W]) + rb_ref[...]
        lane = lax.broadcasted_iota(I32, lg.shape, 1).astype(F32)
        neg = -jnp.inf
        big = float(ROUTE_W)
        gl = jnp.where(lane < n_groups, lg, neg)
        gmax = jnp.max(gl, axis=-1, keepdims=True)
        gidx = jnp.min(jnp.where(gl == gmax, lane, big), axis=-1, keepdims=True)
        gw = 1.0 / jnp.sum(jnp.exp(gl - gmax), axis=-1, keepdims=True)
        base = n_groups + gidx * epg
        el = jnp.where((lane >= base) & (lane < base + epg), lg, neg)
        v1 = jnp.max(el, axis=-1, keepdims=True)
        i1 = jnp.min(jnp.where(el == v1, lane, big), axis=-1, keepdims=True)
        el2 = jnp.where(lane == i1, neg, el)
        v2 = jnp.max(el2, axis=-1, keepdims=True)
        i2 = jnp.min(jnp.where(el2 == v2, lane, big), axis=-1, keepdims=True)
        t = jnp.exp(v2 - v1)
        w1 = gw / (1.0 + t)
        w2 = w1 * t
        s1 = i1 - base
        s2 = i2 - base
        lo = jnp.minimum(s1, s2)
        hi = jnp.maximum(s1, s2)
        pair = lo * (2 * epg - 1 - lo) * 0.5 + (hi - lo - 1.0)
        w_lo = jnp.where(s1 < s2, w1, w2)
        w_hi = jnp.where(s1 < s2, w2, w1)
        f_ref[rows, d:] = jnp.where(lane == 0.0, w_lo, 0.0) + jnp.where(lane == 1.0, w_hi, 0.0)
        cls = jnp.broadcast_to(gidx * n_pairs + pair, (row_chunk, LANES)).T
        gid_ref[:, rows] = cls[0:SUBLANES, :].astype(I32)


def _out_proj(mixes, w_out, x2, gain, mods, w_router, b_router, rows_per_mod, mod_base, tm=256):
    r, d = x2.shape
    k = w_out.shape[0]
    tpm = rows_per_mod // tm
    wr = jnp.concatenate(_split_bf16(w_router), axis=1)

    def mrow(i):
        return (mod_base + i // tpm) * N_MOD

    const2 = lambda i: (0, 0)
    in_specs = ([pl.BlockSpec((tm, m.shape[1]), lambda i: (i, 0)) for m in mixes]
                + [pl.BlockSpec((k, d), const2),
                   pl.BlockSpec((tm, d), lambda i: (i, 0)),
                   pl.BlockSpec((1, 1, d), lambda i: (mrow(i) + 2, 0, 0)),
                   pl.BlockSpec((1, d), const2),
                   pl.BlockSpec((1, 1, d), lambda i: (mrow(i) + 3, 0, 0)),
                   pl.BlockSpec((1, 1, d), lambda i: (mrow(i) + 4, 0, 0)),
                   pl.BlockSpec((d, 2 * ROUTE_W), const2),
                   pl.BlockSpec((1, ROUTE_W), const2)])
    x_new, fext, cls = pl.pallas_call(
        functools.partial(_out_proj_kernel, n_mix=len(mixes), n_groups=N_GROUPS, epg=EXPERTS_PER_GROUP,
                          row_chunk=tm),
        grid=(r // tm,),
        in_specs=in_specs,
        out_specs=[pl.BlockSpec((tm, d), lambda i: (i, 0)),
                   pl.BlockSpec((tm, d + ROUTE_W), lambda i: (i, 0)),
                   pl.BlockSpec((SUBLANES, tm), lambda i: (i, 0))],
        out_shape=[jax.ShapeDtypeStruct((r, d), F32),
                   jax.ShapeDtypeStruct((r, d + ROUTE_W), F32),
                   jax.ShapeDtypeStruct((r // tm * SUBLANES, tm), I32)],
        compiler_params=_cparams(("parallel",)),
        name="out_proj",
    )(*mixes, w_out, x2, mods, gain.reshape(1, d), mods, mods, wr, b_router)
    return x_new, fext, cls.reshape(r // tm, SUBLANES, tm)[:, 0, :].reshape(r)


def _row_copy(src, s, dst, t, sem):
    return pltpu.make_async_copy(src.at[pl.ds(s, 1), :], dst.at[pl.ds(t, 1), :], sem)


def _dispatch_kernel(pos_ref, src_ref, dst_in, dst, sem, *, tm):
    del dst_in

    def body(j, carry):
        _row_copy(src_ref, j, dst, pos_ref[0, 0, j], sem).start()
        return carry
    lax.fori_loop(0, tm, body, 0, unroll=ROW_DMA_UNROLL)

    def wbody(j, carry):
        _row_copy(src_ref, 0, dst, 0, sem).wait()
        return carry
    lax.fori_loop(0, tm, wbody, 0, unroll=True)


def _dispatch(src, pos, sorted_in, tm=512):
    r, w = src.shape
    return pl.pallas_call(
        functools.partial(_dispatch_kernel, tm=tm),
        grid=(r // tm,),
        in_specs=[pl.BlockSpec((1, 1, tm), lambda i: (i, 0, 0), memory_space=pltpu.SMEM),
                  pl.BlockSpec((tm, w), lambda i: (i, 0)),
                  pl.BlockSpec(memory_space=pl.ANY)],
        out_specs=pl.BlockSpec(memory_space=pl.ANY),
        out_shape=jax.ShapeDtypeStruct(sorted_in.shape, sorted_in.dtype),
        scratch_shapes=[pltpu.SemaphoreType.DMA(())],
        input_output_aliases={2: 0},
        compiler_params=_cparams(("arbitrary",)),
        name="moe_dispatch",
    )(pos.reshape(r // tm, 1, tm), src, sorted_in)


def _pair_slot(t, s):
    return jnp.bitwise_xor(s, jnp.bitwise_and(t, 1))


def _moe_kernel(te_ref, tv_ref, x_ref, w1_ref, w3_ref, w2_ref, o_ref):
    t = pl.program_id(0)
    s = pl.program_id(1)
    d = o_ref.shape[1]
    valid = tv_ref[t] > 0
    slot = _pair_slot(t, s)

    @pl.when(valid)
    def _():
        x = x_ref[:, 0:d].astype(BF16)
        h1 = _dot(x, w1_ref[...].astype(BF16))
        h3 = _dot(x, w3_ref[...].astype(BF16))
        cw = x_ref[:, d:]
        lane = lax.broadcasted_iota(I32, cw.shape, 1)
        cws = jnp.sum(jnp.where(lane == slot, cw, 0.0), axis=-1, keepdims=True)
        y = _dot((h1 * _sigmoid(h1) * h3 * cws).astype(BF16), w2_ref[...].astype(BF16))

        @pl.when(s == 0)
        def _():
            o_ref[...] = y

        @pl.when(s > 0)
        def _():
            o_ref[...] += y

    @pl.when(jnp.logical_not(valid) & (s == 0))
    def _():
        o_ref[...] = jnp.zeros_like(o_ref)


def _moe_ffn(tile_expert, tile_valid, f_sorted, w1, w3, w2, layer, tm):
    p, dw = f_sorted.shape
    d = dw - ROUTE_W
    fe = w1.shape[3]

    def wmap(t, s, te, tv):
        return (layer, te[2 * t + _pair_slot(t, s)], 0, 0)

    grid_spec = pltpu.PrefetchScalarGridSpec(
        num_scalar_prefetch=2,
        grid=(p // tm, 2),
        in_specs=[pl.BlockSpec((tm, dw), lambda t, s, te, tv: (jnp.minimum(t, tv[tv.shape[0] - 1]), 0)),
                  pl.BlockSpec((None, None, d, fe), wmap),
                  pl.BlockSpec((None, None, d, fe), wmap),
                  pl.BlockSpec((None, None, fe, d), wmap)],
        out_specs=pl.BlockSpec((tm, d), lambda t, s, te, tv: (t, 0)),
    )
    return pl.pallas_call(
        _moe_kernel,
        grid_spec=grid_spec,
        out_shape=jax.ShapeDtypeStruct((p, d), F32),
        compiler_params=_cparams(("arbitrary", "arbitrary")),
        name="moe_ffn",
    )(tile_expert, tile_valid, f_sorted, w1, w3, w2)


def _combine_kernel(pos_cur, pos_nxt, y_hbm, x_ref, gate_ref, o_ref, ybuf, sem, *, tm):
    i = pl.program_id(0)
    n_steps = pl.num_programs(0)

    def issue(pos_ref, slot):
        def body(j, carry):
            _row_copy(y_hbm, pos_ref[0, 0, j], ybuf.at[slot], j, sem.at[slot]).start()
            return carry
        lax.fori_loop(0, tm, body, 0, unroll=ROW_DMA_UNROLL)

    @pl.when(i == 0)
    def _():
        issue(pos_cur, 0)

    @pl.when(i + 1 < n_steps)
    def _():
        issue(pos_nxt, (i + 1) % 2)

    slot = i % 2

    def wbody(j, carry):
        _row_copy(y_hbm, 0, ybuf.at[slot], 0, sem.at[slot]).wait()
        return carry
    lax.fori_loop(0, tm, wbody, 0, unroll=True)
    o_ref[...] = x_ref[...] + gate_ref[0] * ybuf[slot]


def _combine(y_sorted, pos, x2, mods, rows_per_mod, mod_base, tm=256):
    r, d = x2.shape
    n_steps = r // tm
    tpm = rows_per_mod // tm
    pos3 = pos.reshape(n_steps, 1, tm)
    return pl.pallas_call(
        functools.partial(_combine_kernel, tm=tm),
        grid=(n_steps,),
        in_specs=[pl.BlockSpec((1, 1, tm), lambda i: (i, 0, 0), memory_space=pltpu.SMEM),
                  pl.BlockSpec((1, 1, tm), lambda i: (jnp.minimum(i + 1, n_steps - 1), 0, 0),
                               memory_space=pltpu.SMEM),
                  pl.BlockSpec(memory_space=pl.ANY),
                  pl.BlockSpec((tm, d), lambda i: (i, 0)),
                  pl.BlockSpec((1, 1, d), lambda i: ((mod_base + i // tpm) * N_MOD + 5, 0, 0))],
        out_specs=pl.BlockSpec((tm, d), lambda i: (i, 0)),
        out_shape=jax.ShapeDtypeStruct((r, d), F32),
        scratch_shapes=[pltpu.VMEM((2, tm, d), F32), pltpu.SemaphoreType.DMA((2,))],
        compiler_params=_cparams(("arbitrary",)),
        name="moe_combine",
    )(pos3, pos3, y_sorted, x2, mods)


def _sorted_positions(gids, tm, p_rows):
    g = jnp.concatenate(gids)
    r = g.shape[0]
    epg = EXPERTS_PER_GROUP
    n_pairs = epg * (epg - 1) // 2
    n_cls = N_GROUPS * n_pairs
    onehot = (g[:, None] == jnp.arange(n_cls, dtype=I32)[None, :]).astype(I32)
    counts = jnp.sum(onehot, axis=0)
    rank = jnp.sum((jnp.cumsum(onehot, axis=0) - 1) * onehot, axis=1)
    padded = ((counts + tm - 1) // tm) * tm
    ends = jnp.cumsum(padded)
    pos = jnp.sum(onehot * (ends - padded)[None, :], axis=1) + rank
    p = r + n_cls * tm if p_rows is None else p_rows
    assert p >= r + n_cls * tm and p % tm == 0
    tile_start = jnp.arange(p // tm, dtype=I32) * tm
    tile_cls = jnp.minimum(jnp.sum((tile_start[:, None] >= ends[None, :]).astype(I32), axis=1), n_cls - 1)
    tile_ok = tile_start < ends[-1]
    tile_valid = jnp.concatenate([tile_ok.astype(I32), (ends[-1:] // tm - 1).astype(I32)])
    pairs = [(i, j) for i in range(epg) for j in range(i + 1, epg)]
    pair_lo = jnp.array([a for a, _ in pairs], I32)
    pair_hi = jnp.array([b for _, b in pairs], I32)
    grp = tile_cls // n_pairs
    tile_expert = jnp.stack([grp * epg + pair_lo[tile_cls % n_pairs],
                             grp * epg + pair_hi[tile_cls % n_pairs]], axis=1)
    tile_expert = jnp.where(tile_ok[:, None], tile_expert, 0).reshape(-1)
    return pos.astype(I32), tile_expert.astype(I32), tile_valid, p


def _moe(fexts, gids, w1, w3, w2, layer, f_sorted=None, tm=512):
    pos, tile_expert, tile_valid, p = _sorted_positions(gids, tm, None if f_sorted is None else f_sorted.shape[0])
    sizes = [f.shape[0] for f in fexts]
    poss, off = [], 0
    for s in sizes:
        poss.append(lax.slice(pos, (off,), (off + s,)))
        off += s
    if f_sorted is None:
        f_sorted = jnp.zeros((p, fexts[0].shape[1]), F32)
    for fext, ps in zip(fexts, poss):
        f_sorted = _dispatch(fext, ps, f_sorted)
    y_sorted = _moe_ffn(tile_expert, tile_valid, f_sorted, w1, w3, w2, layer, tm)
    return y_sorted, poss, f_sorted


def kernel(x, c, ctx, c_ctx, ada_w, ada_b, norm_mix, norm_ffn, even_w_in, even_w_out, hgrn_lb_logits, hgrn_out_norm, diff_qk_norm, diff_lambda, diff_out_norm, odd_w_in, odd_conv_w, odd_conv_b, rg_gate_w, rg_gate_b, rg_lambda, odd_w_out, moe_w_grp, moe_b_grp, moe_w_exp, moe_b_exp, moe_w1, moe_w3, moe_w2):
    bsz, n, d = x.shape
    n_ctx = ctx.shape[1]
    depth = ada_w.shape[0]
    assert depth == 2

    cvec = jnp.zeros((MOD_ROWS, d), F32).at[:bsz].set(c).at[bsz].set(c_ctx)
    mods_all = _ada_all(cvec, ada_w, ada_b).reshape(depth, MOD_ROWS * N_MOD, 1, d)
    x_l = x.reshape(bsz * n, d)
    x_c = ctx.reshape(bsz * n_ctx, d)
    lb_all = jnp.cumsum(jax.nn.softmax(hgrn_lb_logits.astype(F32), axis=1), axis=1)

    def router(l):
        w = jnp.concatenate([moe_w_grp[l], moe_w_exp[l]], axis=1)
        b = jnp.concatenate([moe_b_grp[l], moe_b_exp[l]])
        padw = ROUTE_W - w.shape[1]
        return jnp.pad(w, ((0, 0), (0, padw))), jnp.pad(b, (0, padw)).reshape(1, ROUTE_W)

    l = 0
    mods = mods_all[l]
    lam_init = 0.8 - 0.6 * math.exp(-0.3 * l)
    lv = diff_lambda[0].astype(F32)
    lam = jnp.exp(jnp.sum(lv[0] * lv[1])) - jnp.exp(jnp.sum(lv[2] * lv[3])) + lam_init
    w_in = even_w_in
    proj_l = _norm_mod_mm(x_l, norm_mix[l], mods, w_in, n, 0, tm=1024)
    proj_c = _norm_mod_mm(x_c, norm_mix[l], mods, w_in, bsz * n_ctx, bsz, tm=1024)
    a_l, a_c = _hgrn(proj_l, proj_c, lb_all[0, 0], lb_all[1, 0], hgrn_out_norm[0], bsz)
    b_l, b_c = _attn(proj_l, proj_c, diff_qk_norm[0], diff_out_norm[0], lam, lam_init, bsz)
    w_out = even_w_out[0].astype(BF16)
    wr, br = router(l)
    x_l, f_l, g_l = _out_proj([a_l, b_l], w_out, x_l, norm_ffn[l], mods, wr, br, n, 0)
    x_c, f_c, g_c = _out_proj([a_c, b_c], w_out, x_c, norm_ffn[l], mods, wr, br, bsz * n_ctx, bsz)
    y_sorted, (pos_l, pos_c), f_sorted = _moe([f_l, f_c], [g_l, g_c], moe_w1, moe_w3, moe_w2, l)
    x_l = _combine(y_sorted, pos_l, x_l, mods, n, 0)
    x_c = _combine(y_sorted, pos_c, x_c, mods, bsz * n_ctx, bsz)

    l = 1
    mods = mods_all[l]
    w_in = odd_w_in
    proj_l = _norm_mod_mm(x_l, norm_mix[l], mods, w_in, n, 0, tm=1024)
    proj_c = _norm_mod_mm(x_c, norm_mix[l], mods, w_in, bsz * n_ctx, bsz, tm=1024)
    gated = _rglru(proj_l, proj_c, odd_conv_w[0], odd_conv_b[0], rg_gate_w[0], rg_gate_b[0], rg_lambda[0], bsz)
    wr, br = router(l)
    x_l, f_l, g_l = _out_proj([gated], odd_w_out[0].astype(BF16), x_l, norm_ffn[l], mods, wr, br, n, 0)
    y_sorted, (pos_l,), _ = _moe([f_l], [g_l], moe_w1, moe_w3, moe_w2, l, f_sorted)
    return _combine(y_sorted, pos_l, x_l, mods, n, 0).reshape(bsz, n, d)
```

```python
import functools
import math

import jax
import jax.numpy as jnp
from jax import lax
from jax.experimental import pallas as pl
from jax.experimental.pallas import tpu as pltpu

F32 = jnp.float32
BF16 = jnp.bfloat16
I32 = jnp.int32

EPS = 1e-6
LOG2_E = 1.4426950408889634
TINY = 1e-30
LANES = 128
SUBLANES = 8
BF16_SUBLANES = 16
GRID_W = 64
A_HEADS = 8
HEAD_W = 128
HGRN_CHUNK = 32
HGRN_BLOCK = 256
B_HEADS = 8
B_DH = 64
ROPE_BASE = 10000.0
ATTN_KEY_CHUNK = 256
ATTN_SKEW = 3
RG_HEADS = 16
CONV_W = 4
RG_C = 8.0
N_GROUPS = 4
EXPERTS_PER_GROUP = 4
N_EXPERTS = 16
N_MOD = 6
MOD_ROWS = 16
ROUTE_W = LANES
ROW_DMA_UNROLL = 8

VMEM_LIMIT = 56 * 1024 * 1024


def _cparams(sem):
    return pltpu.CompilerParams(dimension_semantics=sem, vmem_limit_bytes=VMEM_LIMIT)


def _sigmoid(x):
    return 0.5 * jnp.tanh(0.5 * x) + 0.5


def _dot(a, b):
    return jnp.dot(a, b, preferred_element_type=F32)


def _dot_nt(a, b):
    return lax.dot_general(a, b, (((1,), (1,)), ((), ())), preferred_element_type=F32)


def _dot_tn(a, b):
    return lax.dot_general(a, b, (((0,), (0,)), ((), ())), preferred_element_type=F32)


def _split_bf16(x):
    hi = x.astype(BF16)
    return hi, (x - hi.astype(F32)).astype(BF16)


def _ada_kernel(c_ref, w_ref, b_ref, o_ref):
    c = c_ref[...]
    a = (c * _sigmoid(c)).astype(BF16)
    o_ref[...] = _dot(a, w_ref[...].astype(BF16)) + b_ref[...]


def _ada_all(cvec, ada_w, ada_b, tn=1024):
    depth, d, n = ada_w.shape
    return pl.pallas_call(
        _ada_kernel,
        grid=(depth, n // tn),
        in_specs=[pl.BlockSpec((MOD_ROWS, d), lambda l, j: (0, 0)),
                  pl.BlockSpec((None, d, tn), lambda l, j: (l, 0, j)),
                  pl.BlockSpec((None, 1, tn), lambda l, j: (l, 0, j))],
        out_specs=pl.BlockSpec((None, MOD_ROWS, tn), lambda l, j: (l, 0, j)),
        out_shape=jax.ShapeDtypeStruct((depth, MOD_ROWS, n), F32),
        compiler_params=_cparams(("arbitrary", "arbitrary")),
        name="ada_mod",
    )(cvec, ada_w, ada_b.reshape(depth, 1, n))


def _norm_mod_mm_kernel(x_ref, g_ref, sh_ref, sc_ref, w_ref, o_ref, h_scr):
    @pl.when(pl.program_id(1) == 0)
    def _():
        x = x_ref[...]
        y = x * lax.rsqrt(jnp.mean(x * x, axis=-1, keepdims=True) + EPS) * g_ref[...]
        h_scr[...] = (y * (1.0 + sc_ref[0]) + sh_ref[0]).astype(BF16)

    o_ref[...] = _dot(h_scr[...], w_ref[...].astype(BF16)).astype(o_ref.dtype)


def _norm_mod_mm(x2, gain, mods, w, rows_per_mod, mod_base, tm, tn=512):
    r, d = x2.shape
    n = w.shape[2]
    tpm = rows_per_mod // tm

    def mrow(i):
        return (mod_base + i // tpm) * N_MOD

    return pl.pallas_call(
        _norm_mod_mm_kernel,
        grid=(r // tm, n // tn),
        in_specs=[pl.BlockSpec((tm, d), lambda i, j: (i, 0)),
                  pl.BlockSpec((1, d), lambda i, j: (0, 0)),
                  pl.BlockSpec((1, 1, d), lambda i, j: (mrow(i) + 0, 0, 0)),
                  pl.BlockSpec((1, 1, d), lambda i, j: (mrow(i) + 1, 0, 0)),
                  pl.BlockSpec((None, d, tn), lambda i, j: (0, 0, j))],
        out_specs=pl.BlockSpec((tm, tn), lambda i, j: (i, j)),
        out_shape=jax.ShapeDtypeStruct((r, n), F32),
        scratch_shapes=[pltpu.VMEM((tm, d), BF16)],
        compiler_params=_cparams(("parallel", "arbitrary")),
        name="norm_mod_mm",
    )(x2, gain.reshape(1, d), mods, mods, w)


def _hgrn_kernel(ql, ffl, fbl, vl, gl, qc, ffc, fbc, vc, gc, lbf_ref, lbb_ref, gain_ref,
                 ol_ref, oc_ref,
                 oi_l, oi_c, qtf_l, qtb_l, qtf_c, qtb_c, kv_f, kv_b, dec_f, dec_b):
    c_sz = HGRN_CHUNK
    blk = HGRN_BLOCK
    cpb = blk // c_sz
    nb_l = ql.shape[0] // blk
    nc_l = ql.shape[0] // c_sz
    nc_c = qc.shape[0] // c_sz
    nc = nc_l + nc_c
    assert qc.shape[0] == blk
    row = lax.broadcasted_iota(I32, (blk, blk), 0)
    col = lax.broadcasted_iota(I32, (blk, blk), 1)
    same = (row // c_sz) == (col // c_sz)
    tril = same & (row >= col)
    triu = same & (row <= col)
    lbf = lbf_ref[...]
    lbb = lbb_ref[...]

    exp_mask = (lax.broadcasted_iota(I32, (blk, cpb * HEAD_W), 0) // c_sz
                == lax.broadcasted_iota(I32, (blk, cpb * HEAD_W), 1) // HEAD_W)

    def local_terms(r0, q_ref, ff_ref, fb_ref, v_ref, oi_ref, qtf_ref, qtb_ref, gid_f, gid_b):
        rows = pl.ds(r0, blk)
        q = q_ref[rows, :]
        qs = q * _sigmoid(q)
        vb = v_ref[rows, :].astype(BF16)
        vt = v_ref[rows, :].T.astype(BF16)
        two = (0, 1)
        mask = (tril, triu)
        last = (c_sz - 1, 0)
        f = [lb + (1.0 - lb) * _sigmoid(ref[rows, :]) for ref, lb in ((ff_ref, lbf), (fb_ref, lbb))]
        k = [1.0 - f[d] for d in two]
        lf2 = [jnp.concatenate(_split_bf16(jnp.log(f[d])), axis=1) for d in two]
        cum2 = [_dot(jnp.where(mask[d], 1.0, 0.0).astype(BF16), lf2[d]) for d in two]
        cum = [cum2[d][:, :HEAD_W] + cum2[d][:, HEAD_W:] for d in two]
        tot = [jnp.concatenate(
            [jnp.broadcast_to(cum[d][c * c_sz + last[d]:c * c_sz + last[d] + 1, :], (c_sz, HEAD_W))
             for c in range(cpb)], axis=0) for d in two]
        e = [jnp.exp(cum[d]) for d in two]
        qt = [(qs * e[d]).astype(BF16) for d in two]
        kt = [(k[d] * jnp.exp(-cum[d])).astype(BF16) for d in two]
        att = [jnp.where(mask[d], _dot_nt(qt[d], kt[d]), 0.0).astype(BF16) for d in two]
        oi_ref[rows, :] = _dot(att[0], vb) + _dot(att[1], vb)
        k2 = [(k[d] * jnp.exp(tot[d] - cum[d])).astype(BF16) for d in two]
        k2x = [jnp.where(exp_mask, jnp.concatenate([k2[d]] * cpb, axis=1), jnp.zeros((), BF16)) for d in two]
        kvs = [_dot(vt, k2x[d]) for d in two]
        for d, (qt_ref, kv_ref, dec_ref, gid) in enumerate(((qtf_ref, kv_f, dec_f, gid_f),
                                                            (qtb_ref, kv_b, dec_b, gid_b))):
            for c in range(cpb):
                kv_ref[gid + c] = kvs[d][:, c * HEAD_W:(c + 1) * HEAD_W]
                dec_ref[gid + c] = e[d][c * c_sz + last[d]:c * c_sz + last[d] + 1, :]
            qt_ref[rows, :] = qt[d]

    local_terms(0, qc, ffc, fbc, vc, oi_c, qtf_c, qtb_c, 0, nc_l)

    def local_l(i, carry):
        local_terms(pl.multiple_of(i * blk, blk), ql, ffl, fbl, vl, oi_l, qtf_l, qtb_l,
                    nc_c + i * cpb, i * cpb)
        return carry

    lax.fori_loop(0, nb_l, local_l, 0, unroll=2)

    def rec_f(i, s):
        new = s * dec_f[i] + kv_f[i]
        kv_f[i] = s
        return new

    def rec_b(i, s):
        j = nc - 1 - i
        new = s * dec_b[j] + kv_b[j]
        kv_b[j] = s
        return new

    s0 = jnp.zeros((HEAD_W, HEAD_W), F32)
    lax.fori_loop(0, nc, rec_f, s0, unroll=2)
    lax.fori_loop(0, nc, rec_b, s0, unroll=2)

    gain = gain_ref[...]

    def finish(r0, g_ref, oi_ref, qtf_ref, qtb_ref, o_ref, gid_f, gid_b):
        parts = []
        for c in range(cpb):
            rows_c = pl.ds(r0 + c * c_sz, c_sz)
            parts.append(_dot_nt(qtf_ref[rows_c, :], kv_f[gid_f + c].astype(BF16))
                         + _dot_nt(qtb_ref[rows_c, :], kv_b[gid_b + c].astype(BF16)))
        rows = pl.ds(r0, blk)
        o = oi_ref[rows, :] + jnp.concatenate(parts, axis=0)
        y = o * lax.rsqrt(jnp.mean(o * o, axis=-1, keepdims=True) + EPS) * gain
        g = g_ref[rows, :]
        o_ref[rows, :] = (y * (g * _sigmoid(g))).astype(o_ref.dtype)

    finish(0, gc, oi_c, qtf_c, qtb_c, oc_ref, 0, nc_l)

    def fin_l(i, carry):
        finish(pl.multiple_of(i * blk, blk), gl, oi_l, qtf_l, qtb_l, ol_ref, nc_c + i * cpb, i * cpb)
        return carry

    lax.fori_loop(0, nb_l, fin_l, 0, unroll=True)


def _hgrn(proj_l, proj_c, lb_f, lb_b, gain, bsz):
    n = proj_l.shape[0] // bsz
    n_ctx = proj_c.shape[0] // bsz
    nc = (n + n_ctx) // HGRN_CHUNK
    w = A_HEADS * HEAD_W

    def col(k):
        return lambda b, h: (b, k * A_HEADS + h)

    in_specs = ([pl.BlockSpec((n, HEAD_W), col(k)) for k in range(5)]
                + [pl.BlockSpec((n_ctx, HEAD_W), col(k)) for k in range(5)]
                + [pl.BlockSpec((1, HEAD_W), lambda b, h: (0, h)),
                   pl.BlockSpec((1, HEAD_W), lambda b, h: (0, h)),
                   pl.BlockSpec((1, HEAD_W), lambda b, h: (0, 0))])
    return pl.pallas_call(
        _hgrn_kernel,
        grid=(bsz, A_HEADS),
        in_specs=in_specs,
        out_specs=[pl.BlockSpec((n, HEAD_W), lambda b, h: (b, h)),
                   pl.BlockSpec((n_ctx, HEAD_W), lambda b, h: (b, h))],
        out_shape=[jax.ShapeDtypeStruct((bsz * n, w), BF16),
                   jax.ShapeDtypeStruct((bsz * n_ctx, w), BF16)],
        scratch_shapes=[pltpu.VMEM((n, HEAD_W), F32), pltpu.VMEM((n_ctx, HEAD_W), F32),
                        pltpu.VMEM((n, HEAD_W), BF16), pltpu.VMEM((n, HEAD_W), BF16),
                        pltpu.VMEM((n_ctx, HEAD_W), BF16), pltpu.VMEM((n_ctx, HEAD_W), BF16),
                        pltpu.VMEM((nc, HEAD_W, HEAD_W), F32), pltpu.VMEM((nc, HEAD_W, HEAD_W), F32),
                        pltpu.VMEM((nc, 1, HEAD_W), F32), pltpu.VMEM((nc, 1, HEAD_W), F32)],
        compiler_params=_cparams(("parallel", "parallel")),
        name="hgrn2",
    )(*([proj_l] * 5 + [proj_c] * 5), lb_f.reshape(1, w), lb_b.reshape(1, w), gain.reshape(1, HEAD_W))


def _half_mean_matrix():
    r = lax.broadcasted_iota(I32, (LANES, LANES), 0) // B_DH
    c = lax.broadcasted_iota(I32, (LANES, LANES), 1) // B_DH
    return (r == c).astype(BF16)


def _qk_prep(t, gain, bd, cos, sin):
    sq_hi, sq_lo = _split_bf16(t * t)
    ms = (_dot(sq_hi, bd) + _dot(sq_lo, bd)) * (1.0 / B_DH)
    y = t * lax.rsqrt(ms + EPS) * gain
    if cos is None:
        return y
    lane = lax.broadcasted_iota(I32, y.shape, 1)
    partner = jnp.where(lane % 2 == 0, pltpu.roll(y, LANES - 1, 1), pltpu.roll(y, 1, 1))
    return y * cos + partner * sin


def _attn_kernel(q_l, q_c, k_l, k_c, v_l, v_c, cosq, sinq, cosk, sink, gq_ref, gk_ref, go_ref, lam_ref,
                 o_l, o_c, kp, vp, s_even, s_odd, m_even, m_odd, *, out_scale):
    step = pl.program_id(2)
    n_ctx = k_c.shape[0]
    s_bufs = (s_even, s_odd)
    m_bufs = (m_even, m_odd)
    bd = _half_mean_matrix()
    lane = lax.broadcasted_iota(I32, (1, LANES), 1)
    masks = ((lane < B_DH).astype(F32), (lane >= B_DH).astype(F32))
    lam = lam_ref[:, 0:1]
    q_scale = B_DH ** -0.5 * LOG2_E

    def masked_q(q):
        q = q * q_scale
        return [(q * msk).astype(BF16) for msk in masks]

    def epilogue(r):
        o = (r[0][0:HEAD_W, :] * (1.0 / r[0][HEAD_W:HEAD_W + 1, :])
             - r[1][0:HEAD_W, :] * (lam / r[1][HEAD_W:HEAD_W + 1, :]))
        gain = jnp.concatenate([go_ref[...]] * (o.shape[1] // LANES), axis=1)
        y = o * lax.rsqrt(jnp.mean(o * o, axis=0, keepdims=True) + EPS) * gain
        return (y * out_scale).T.astype(BF16)

    @pl.when((step == 0) & (pl.program_id(0) == 0) & (pl.program_id(1) == 0))
    def _():
        s_odd[...] = jnp.zeros_like(s_odd)
        m_odd[...] = jnp.zeros_like(m_odd)

    @pl.when(step == 0)
    def _():
        kp[0:n_ctx, :] = _qk_prep(k_c[...], gk_ref[...], bd, None, None).astype(BF16)
        kp[n_ctx:, :] = _qk_prep(k_l[...], gk_ref[...], bd, cosk[...], sink[...]).astype(BF16)
        vp[0:HEAD_W, 0:n_ctx] = v_c[...].T.astype(BF16)
        vp[0:HEAD_W, n_ctx:] = v_l[...].T.astype(BF16)
        pad_rows = vp.shape[0] - HEAD_W
        ones_row = (lax.broadcasted_iota(I32, (pad_rows, 1), 0) == 0).astype(BF16)
        vp[HEAD_W:, :] = jnp.broadcast_to(ones_row, (pad_rows, vp.shape[1]))
        qm = masked_q(_qk_prep(q_c[...], gq_ref[...], bd, None, None))
        s_ctx = [_dot_nt(kp[0:n_ctx, :], t) for t in qm]
        e_ctx = [jnp.exp2(t - jnp.max(t, axis=0, keepdims=True)).astype(BF16) for t in s_ctx]
        o_c[...] = epilogue([_dot(vp[:, 0:n_ctx], t) for t in e_ctx])

    kc = ATTN_KEY_CHUNK
    for parity in (0, 1):
        @pl.when((step > 0) & (step % 2 == parity))
        def _(old=parity, new=1 - parity):
            qm = masked_q(_qk_prep(q_l[...], gq_ref[...], bd, cosq[...], sinq[...]))
            mx_old = [m_bufs[old][i, 0:1, :] for i in range(2)]
            acc = [None, None]
            mx_new = [None, None]
            n_chunks = kp.shape[0] // kc
            for c in range(n_chunks + ATTN_SKEW):
                if c < n_chunks:
                    ks = slice(c * kc, (c + 1) * kc)
                    for i in range(2):
                        e = jnp.exp2(s_bufs[old][i, ks, :] - mx_old[i]).astype(BF16)
                        part = _dot(vp[:, ks], e)
                        acc[i] = part if acc[i] is None else acc[i] + part
                if c == n_chunks - 1:
                    o_l[...] = epilogue(acc)
                if c >= ATTN_SKEW:
                    ks = slice((c - ATTN_SKEW) * kc, (c - ATTN_SKEW + 1) * kc)
                    for i in range(2):
                        t = _dot_nt(kp[ks, :], qm[i])
                        s_bufs[new][i, ks, :] = t
                        tm = jnp.max(t, axis=0, keepdims=True)
                        mx_new[i] = tm if mx_new[i] is None else jnp.maximum(mx_new[i], tm)
            for i in range(2):
                m_bufs[new][i] = jnp.broadcast_to(mx_new[i], m_bufs[new].shape[1:])


def _rope_tables(n):
    n_rows = n // GRID_W
    rowp = jnp.repeat(jnp.arange(n_rows), GRID_W).astype(F32)
    colp = jnp.tile(jnp.arange(GRID_W), n_rows).astype(F32)
    pairs = B_DH // 4
    inv = ROPE_BASE ** (-jnp.arange(pairs, dtype=F32) / pairs)
    ang = jnp.concatenate([rowp[:, None] * inv, colp[:, None] * inv], axis=-1)
    cos = jnp.repeat(jnp.cos(ang), 2, axis=-1)
    sin = jnp.repeat(jnp.sin(ang), 2, axis=-1) * jnp.tile(jnp.array([-1.0, 1.0], F32), B_DH // 2)
    return jnp.tile(cos, (1, 2)), jnp.tile(sin, (1, 2))


def _attn(proj_l, proj_c, qk_gain, out_gain, lam, lam_init, bsz, tq=512):
    n = proj_l.shape[0] // bsz
    n_ctx = proj_c.shape[0] // bsz
    nqb = n // tq
    w = B_HEADS * HEAD_W
    cos, sin = _rope_tables(n)
    qcol, kcol, vcol = 5 * A_HEADS, 5 * A_HEADS + B_HEADS, 5 * A_HEADS + 2 * B_HEADS

    def q_blk(t):
        return jnp.clip(t - 1, 0, nqb - 1)

    def o_blk(t):
        return jnp.clip(t - 2, 0, nqb - 1)

    def lat_q(b, h, t):
        return (b * nqb + q_blk(t), qcol + h)

    in_specs = [pl.BlockSpec((tq, HEAD_W), lat_q),
                pl.BlockSpec((n_ctx, HEAD_W), lambda b, h, qb: (b, qcol + h)),
                pl.BlockSpec((n, HEAD_W), lambda b, h, qb: (b, kcol + h)),
                pl.BlockSpec((n_ctx, HEAD_W), lambda b, h, qb: (b, kcol + h)),
                pl.BlockSpec((n, HEAD_W), lambda b, h, qb: (b, vcol + h)),
                pl.BlockSpec((n_ctx, HEAD_W), lambda b, h, qb: (b, vcol + h)),
                pl.BlockSpec((tq, LANES), lambda b, h, t: (q_blk(t), 0)),
                pl.BlockSpec((tq, LANES), lambda b, h, t: (q_blk(t), 0)),
                pl.BlockSpec((n, LANES), lambda b, h, qb: (0, 0)),
                pl.BlockSpec((n, LANES), lambda b, h, qb: (0, 0)),
                pl.BlockSpec((1, LANES), lambda b, h, qb: (0, 0)),
                pl.BlockSpec((1, LANES), lambda b, h, qb: (0, 0)),
                pl.BlockSpec((HEAD_W, LANES), lambda b, h, qb: (0, 0)),
                pl.BlockSpec((1, LANES), lambda b, h, qb: (0, 0))]
    return pl.pallas_call(
        functools.partial(_attn_kernel, out_scale=1.0 - lam_init),
        grid=(bsz, B_HEADS, nqb + 2),
        in_specs=in_specs,
        out_specs=[pl.BlockSpec((tq, HEAD_W), lambda b, h, t: (b * nqb + o_blk(t), h)),
                   pl.BlockSpec((n_ctx, HEAD_W), lambda b, h, qb: (b, h))],
        out_shape=[jax.ShapeDtypeStruct((bsz * n, w), BF16),
                   jax.ShapeDtypeStruct((bsz * n_ctx, w), BF16)],
        scratch_shapes=[pltpu.VMEM((n + n_ctx, HEAD_W), BF16),
                        pltpu.VMEM((HEAD_W + BF16_SUBLANES, n + n_ctx), BF16),
                        pltpu.VMEM((2, n + n_ctx, tq), F32), pltpu.VMEM((2, n + n_ctx, tq), F32),
                        pltpu.VMEM((2, SUBLANES, tq), F32), pltpu.VMEM((2, SUBLANES, tq), F32)],
        compiler_params=_cparams(("arbitrary", "arbitrary", "arbitrary")),
        name="diff_attn",
    )(proj_l, proj_c, proj_l, proj_c, proj_l, proj_c, cos, sin, cos, sin,
      jnp.tile(qk_gain[0], 2).reshape(1, LANES), jnp.tile(qk_gain[1], 2).reshape(1, LANES),
      jnp.broadcast_to(out_gain[:, None], (HEAD_W, LANES)), jnp.full((1, LANES), lam, F32))


def _scan_steps(a, b, reverse, axis):
    n = a.shape[axis]
    pos = lax.broadcasted_iota(I32, a.shape, axis)
    s = 1
    while s < n:
        keep = (pos < n - s) if reverse else (pos >= s)
        shift = n - s if reverse else s
        a_sh = jnp.where(keep, pltpu.roll(a, shift, axis), 1.0)
        b_sh = jnp.where(keep, pltpu.roll(b, shift, axis), 0.0)
        b = a * b_sh + b
        a = a * a_sh
        s *= 2
    return a, b


def _block_scan(a, b, reverse):
    return _scan_steps(a, b, reverse, 1)


def _rglru_kernel(y_l, u_l, u_c, cw_ref, cb_ref, gw_ref, gb_ref, lam_ref, o_ref,
                  upad, a_f, b_f, a_b, b_b, eb):
    n = u_l.shape[0]
    n_ctx = u_c.shape[0]
    tot = n + n_ctx
    blk = SUBLANES
    cw = cw_ref[...]
    cb = cb_ref[...]
    pad = SUBLANES

    def conv(u_ref, rows):
        upad[0:pad, :] = jnp.zeros((pad, LANES), F32)
        upad[pad:pad + rows, :] = u_ref[...]
        upad[pad + rows:pad + rows + pad, :] = jnp.zeros((pad, LANES), F32)
        acc = cb + jnp.zeros((rows, LANES), F32)
        for j in range(CONV_W):
            off = pad + j - CONV_W // 2
            acc = acc + cw[j:j + 1, :] * upad[off:off + rows, :]
        return acc

    def gates(uc, rows_f, rows_b):
        ub = uc.astype(BF16)
        for d, (a_ref, b_ref, rows) in enumerate(((a_f, b_f, rows_f), (a_b, b_b, rows_b))):
            lam = lam_ref[d:d + 1, :]
            neg_sp = -(jnp.maximum(-lam, 0.0) + jnp.log(1.0 + jnp.exp(-jnp.abs(lam))))
            r = _sigmoid(_dot(ub, gw_ref[d, 0].astype(BF16)) + gb_ref[d, 0:1, :])
            i = _sigmoid(_dot(ub, gw_ref[d, 1].astype(BF16)) + gb_ref[d, 1:2, :])
            a = jnp.exp((RG_C * neg_sp) * r)
            a_ref[rows, :] = a
            x = (1.0 - a) * (1.0 + a)
            b_ref[rows, :] = (x * lax.rsqrt(jnp.maximum(x, TINY))) * (i * uc)

    gates(conv(u_c, n_ctx), pl.ds(0, n_ctx), pl.ds(n, n_ctx))
    gates(conv(u_l, n), pl.ds(n_ctx, n), pl.ds(0, n))

    n_blk = tot // blk
    h_sum = None
    for a_ref, b_ref, reverse, lat0 in ((a_f, b_f, False, n_ctx), (a_b, b_b, True, 0)):
        chunk = 256
        for c0 in range(0, tot, chunk):
            a1, b1 = _block_scan(a_ref[c0:c0 + chunk, :].reshape(chunk // blk, blk, LANES),
                                 b_ref[c0:c0 + chunk, :].reshape(chunk // blk, blk, LANES), reverse)
            a_ref[c0:c0 + chunk, :] = a1.reshape(chunk, LANES)
            b_ref[c0:c0 + chunk, :] = b1.reshape(chunk, LANES)
        end = 0 if reverse else blk - 1
        ae = a_ref[pl.ds(end, n_blk, stride=blk), :]
        be = b_ref[pl.ds(end, n_blk, stride=blk), :]
        _, he = _scan_steps(ae, be, reverse, 0)
        zero = jnp.zeros((1, LANES), F32)
        if reverse:
            eb[0:n_blk - 1, :] = he[1:, :]
            eb[n_blk - 1:n_blk, :] = zero
        else:
            eb[0:1, :] = zero
            eb[1:n_blk, :] = he[:n_blk - 1, :]

        def apply(j, carry, a_ref=a_ref, b_ref=b_ref):
            rows = pl.ds(pl.multiple_of(j * blk, blk), blk)
            b_ref[rows, :] = b_ref[rows, :] + a_ref[rows, :] * eb[pl.ds(j, 1), :]
            return carry

        lax.fori_loop(0, n_blk, apply, 0, unroll=8)
        h = b_ref[lat0:lat0 + n, :]
        h_sum = h if h_sum is None else h_sum + h

    y = y_l[...]
    gelu = 0.5 * y * (1.0 + jnp.tanh(0.7978845608028654 * (y + 0.044715 * y * y * y)))
    o_ref[...] = (h_sum * gelu).astype(o_ref.dtype)


def _rglru(proj_l, proj_c, conv_w, conv_b, gate_w, gate_b, lam, bsz):
    n = proj_l.shape[0] // bsz
    n_ctx = proj_c.shape[0] // bsz
    w = RG_HEADS * HEAD_W
    tot = n + n_ctx
    return pl.pallas_call(
        _rglru_kernel,
        grid=(bsz, RG_HEADS),
        in_specs=[pl.BlockSpec((n, HEAD_W), lambda b, h: (b, h)),
                  pl.BlockSpec((n, HEAD_W), lambda b, h: (b, RG_HEADS + h)),
                  pl.BlockSpec((n_ctx, HEAD_W), lambda b, h: (b, RG_HEADS + h)),
                  pl.BlockSpec((CONV_W, HEAD_W), lambda b, h: (0, h)),
                  pl.BlockSpec((1, HEAD_W), lambda b, h: (0, h)),
                  pl.BlockSpec((2, 2, None, HEAD_W, HEAD_W), lambda b, h: (0, 0, h, 0, 0)),
                  pl.BlockSpec((2, 2, HEAD_W), lambda b, h: (0, 0, h)),
                  pl.BlockSpec((2, HEAD_W), lambda b, h: (0, h))],
        out_specs=pl.BlockSpec((n, HEAD_W), lambda b, h: (b, h)),
        out_shape=jax.ShapeDtypeStruct((bsz * n, w), BF16),
        scratch_shapes=[pltpu.VMEM((n + 2 * SUBLANES, LANES), F32)]
                       + [pltpu.VMEM((tot, LANES), F32)] * 4
                       + [pltpu.VMEM((tot // SUBLANES, LANES), F32)],
        compiler_params=_cparams(("parallel", "parallel")),
        name="rglru",
    )(proj_l, proj_l, proj_c, conv_w, conv_b.reshape(1, w), gate_w, gate_b, lam)


def _out_proj_kernel(*refs, n_mix, n_groups, epg, row_chunk):
    mix_refs = refs[:n_mix]
    (w_ref, x_ref, gate_ref, g_ref, sh_ref, sc_ref, wr_ref, rb_ref,
     xo_ref, f_ref, gid_ref) = refs[n_mix:]
    d = x_ref.shape[1]
    n_pairs = epg * (epg - 1) // 2
    for r0 in range(0, x_ref.shape[0], row_chunk):
        rows = slice(r0, r0 + row_chunk)
        k0 = 0
        mix = None
        for m_ref in mix_refs:
            kk = m_ref.shape[1]
            part = _dot(m_ref[rows, :], w_ref[k0:k0 + kk, :])
            mix = part if mix is None else mix + part
            k0 += kk
        x = x_ref[rows, :] + gate_ref[0] * mix
        xo_ref[rows, :] = x
        y = x * lax.rsqrt(jnp.mean(x * x, axis=-1, keepdims=True) + EPS) * g_ref[...]
        f = y * (1.0 + sc_ref[0]) + sh_ref[0]
        f_ref[rows, 0:d] = f

        f_hi, f_lo = _split_bf16(f)
        hh = _dot(f_hi, wr_ref[...])
        lg = hh[:, 0:ROUTE_W] + hh[:, ROUTE_W:] + _dot(f_lo, wr_ref[:, 0:ROUTE_W]) + rb_ref[...]
        lane = lax.broadcasted_iota(I32, lg.shape, 1).astype(F32)
        neg = -jnp.inf
        big = float(ROUTE_W)
        gl = jnp.where(lane < n_groups, lg, neg)
        gmax = jnp.max(gl, axis=-1, keepdims=True)
        gidx = jnp.min(jnp.where(gl == gmax, lane, big), axis=-1, keepdims=True)
        gw = 1.0 / jnp.sum(jnp.exp(gl - gmax), axis=-1, keepdims=True)
        base = n_groups + gidx * epg
        el = jnp.where((lane >= base) & (lane < base + epg), lg, neg)
        v1 = jnp.max(el, axis=-1, keepdims=True)
        i1 = jnp.min(jnp.where(el == v1, lane, big), axis=-1, keepdims=True)
        el2 = jnp.where(lane == i1, neg, el)
        v2 = jnp.max(el2, axis=-1, keepdims=True)
        i2 = jnp.min(jnp.where(el2 == v2, lane, big), axis=-1, keepdims=True)
        t = jnp.exp(v2 - v1)
        w1 = gw / (1.0 + t)
        w2 = w1 * t
        s1 = i1 - base
        s2 = i2 - base
        lo = jnp.minimum(s1, s2)
        hi = jnp.maximum(s1, s2)
        pair = lo * (2 * epg - 1 - lo) * 0.5 + (hi - lo - 1.0)
        w_lo = jnp.where(s1 < s2, w1, w2)
        w_hi = jnp.where(s1 < s2, w2, w1)
        f_ref[rows, d:] = jnp.where(lane == 0.0, w_lo, 0.0) + jnp.where(lane == 1.0, w_hi, 0.0)
        cls = jnp.broadcast_to(gidx * n_pairs + pair, (row_chunk, LANES)).T
        gid_ref[:, rows] = cls[0:SUBLANES, :].astype(I32)


def _out_proj(mixes, w_out, x2, gain, mods, w_router, b_router, rows_per_mod, mod_base, tm=256):
    r, d = x2.shape
    k = w_out.shape[0]
    tpm = rows_per_mod // tm
    wr = jnp.concatenate(_split_bf16(w_router), axis=1)

    def mrow(i):
        return (mod_base + i // tpm) * N_MOD

    const2 = lambda i: (0, 0)
    in_specs = ([pl.BlockSpec((tm, m.shape[1]), lambda i: (i, 0)) for m in mixes]
                + [pl.BlockSpec((k, d), const2),
                   pl.BlockSpec((tm, d), lambda i: (i, 0)),
                   pl.BlockSpec((1, 1, d), lambda i: (mrow(i) + 2, 0, 0)),
                   pl.BlockSpec((1, d), const2),
                   pl.BlockSpec((1, 1, d), lambda i: (mrow(i) + 3, 0, 0)),
                   pl.BlockSpec((1, 1, d), lambda i: (mrow(i) + 4, 0, 0)),
                   pl.BlockSpec((d, 2 * ROUTE_W), const2),
                   pl.BlockSpec((1, ROUTE_W), const2)])
    x_new, fext, cls = pl.pallas_call(
        functools.partial(_out_proj_kernel, n_mix=len(mixes), n_groups=N_GROUPS, epg=EXPERTS_PER_GROUP,
                          row_chunk=tm),
        grid=(r // tm,),
        in_specs=in_specs,
        out_specs=[pl.BlockSpec((tm, d), lambda i: (i, 0)),
                   pl.BlockSpec((tm, d + ROUTE_W), lambda i: (i, 0)),
                   pl.BlockSpec((SUBLANES, tm), lambda i: (i, 0))],
        out_shape=[jax.ShapeDtypeStruct((r, d), F32),
                   jax.ShapeDtypeStruct((r, d + ROUTE_W), F32),
                   jax.ShapeDtypeStruct((r // tm * SUBLANES, tm), I32)],
        compiler_params=_cparams(("parallel",)),
        name="out_proj",
    )(*mixes, w_out, x2, mods, gain.reshape(1, d), mods, mods, wr, b_router)
    return x_new, fext, cls.reshape(r // tm, SUBLANES, tm)[:, 0, :].reshape(r)


def _row_copy(src, s, dst, t, sem):
    return pltpu.make_async_copy(src.at[pl.ds(s, 1), :], dst.at[pl.ds(t, 1), :], sem)


def _dispatch_kernel(pos_ref, src_ref, dst_in, dst, sem, *, tm):
    del dst_in

    def body(j, carry):
        _row_copy(src_ref, j, dst, pos_ref[0, 0, j], sem).start()
        return carry
    lax.fori_loop(0, tm, body, 0, unroll=ROW_DMA_UNROLL)

    def wbody(j, carry):
        _row_copy(src_ref, 0, dst, 0, sem).wait()
        return carry
    lax.fori_loop(0, tm, wbody, 0, unroll=True)


def _dispatch(src, pos, sorted_in, tm=512):
    r, w = src.shape
    return pl.pallas_call(
        functools.partial(_dispatch_kernel, tm=tm),
        grid=(r // tm,),
        in_specs=[pl.BlockSpec((1, 1, tm), lambda i: (i, 0, 0), memory_space=pltpu.SMEM),
                  pl.BlockSpec((tm, w), lambda i: (i, 0)),
                  pl.BlockSpec(memory_space=pl.ANY)],
        out_specs=pl.BlockSpec(memory_space=pl.ANY),
        out_shape=jax.ShapeDtypeStruct(sorted_in.shape, sorted_in.dtype),
        scratch_shapes=[pltpu.SemaphoreType.DMA(())],
        input_output_aliases={2: 0},
        compiler_params=_cparams(("arbitrary",)),
        name="moe_dispatch",
    )(pos.reshape(r // tm, 1, tm), src, sorted_in)


def _pair_slot(t, s):
    return jnp.bitwise_xor(s, jnp.bitwise_and(t, 1))


def _moe_kernel(te_ref, tv_ref, x_ref, w1_ref, w3_ref, w2_ref, o_ref):
    t = pl.program_id(0)
    s = pl.program_id(1)
    d = o_ref.shape[1]
    valid = tv_ref[t] > 0
    slot = _pair_slot(t, s)

    @pl.when(valid)
    def _():
        x = x_ref[:, 0:d].astype(BF16)
        h1 = _dot(x, w1_ref[...].astype(BF16))
        h3 = _dot(x, w3_ref[...].astype(BF16))
        cw = x_ref[:, d:]
        lane = lax.broadcasted_iota(I32, cw.shape, 1)
        cws = jnp.sum(jnp.where(lane == slot, cw, 0.0), axis=-1, keepdims=True)
        y = _dot((h1 * _sigmoid(h1) * h3 * cws).astype(BF16), w2_ref[...].astype(BF16))

        @pl.when(s == 0)
        def _():
            o_ref[...] = y

        @pl.when(s > 0)
        def _():
            o_ref[...] += y

    @pl.when(jnp.logical_not(valid) & (s == 0))
    def _():
        o_ref[...] = jnp.zeros_like(o_ref)


def _moe_ffn(tile_expert, tile_valid, f_sorted, w1, w3, w2, layer, tm):
    p, dw = f_sorted.shape
    d = dw - ROUTE_W
    fe = w1.shape[3]

    def wmap(t, s, te, tv):
        return (layer, te[2 * t + _pair_slot(t, s)], 0, 0)

    grid_spec = pltpu.PrefetchScalarGridSpec(
        num_scalar_prefetch=2,
        grid=(p // tm, 2),
        in_specs=[pl.BlockSpec((tm, dw), lambda t, s, te, tv: (jnp.minimum(t, tv[tv.shape[0] - 1]), 0)),
                  pl.BlockSpec((None, None, d, fe), wmap),
                  pl.BlockSpec((None, None, d, fe), wmap),
                  pl.BlockSpec((None, None, fe, d), wmap)],
        out_specs=pl.BlockSpec((tm, d), lambda t, s, te, tv: (t, 0)),
    )
    return pl.pallas_call(
        _moe_kernel,
        grid_spec=grid_spec,
        out_shape=jax.ShapeDtypeStruct((p, d), F32),
        compiler_params=_cparams(("arbitrary", "arbitrary")),
        name="moe_ffn",
    )(tile_expert, tile_valid, f_sorted, w1, w3, w2)


def _combine_kernel(pos_cur, pos_nxt, y_hbm, x_ref, gate_ref, o_ref, ybuf, sem, *, tm):
    i = pl.program_id(0)
    n_steps = pl.num_programs(0)

    def issue(pos_ref, slot):
        def body(j, carry):
            _row_copy(y_hbm, pos_ref[0, 0, j], ybuf.at[slot], j, sem.at[slot]).start()
            return carry
        lax.fori_loop(0, tm, body, 0, unroll=ROW_DMA_UNROLL)

    @pl.when(i == 0)
    def _():
        issue(pos_cur, 0)

    @pl.when(i + 1 < n_steps)
    def _():
        issue(pos_nxt, (i + 1) % 2)

    slot = i % 2

    def wbody(j, carry):
        _row_copy(y_hbm, 0, ybuf.at[slot], 0, sem.at[slot]).wait()
        return carry
    lax.fori_loop(0, tm, wbody, 0, unroll=True)
    o_ref[...] = x_ref[...] + gate_ref[0] * ybuf[slot]


def _combine(y_sorted, pos, x2, mods, rows_per_mod, mod_base, tm=256):
    r, d = x2.shape
    n_steps = r // tm
    tpm = rows_per_mod // tm
    pos3 = pos.reshape(n_steps, 1, tm)
    return pl.pallas_call(
        functools.partial(_combine_kernel, tm=tm),
        grid=(n_steps,),
        in_specs=[pl.BlockSpec((1, 1, tm), lambda i: (i, 0, 0), memory_space=pltpu.SMEM),
                  pl.BlockSpec((1, 1, tm), lambda i: (jnp.minimum(i + 1, n_steps - 1), 0, 0),
                               memory_space=pltpu.SMEM),
                  pl.BlockSpec(memory_space=pl.ANY),
                  pl.BlockSpec((tm, d), lambda i: (i, 0)),
                  pl.BlockSpec((1, 1, d), lambda i: ((mod_base + i // tpm) * N_MOD + 5, 0, 0))],
        out_specs=pl.BlockSpec((tm, d), lambda i: (i, 0)),
        out_shape=jax.ShapeDtypeStruct((r, d), F32),
        scratch_shapes=[pltpu.VMEM((2, tm, d), F32), pltpu.SemaphoreType.DMA((2,))],
        compiler_params=_cparams(("arbitrary",)),
        name="moe_combine",
    )(pos3, pos3, y_sorted, x2, mods)


def _sorted_positions(gids, tm, p_rows):
    g = jnp.concatenate(gids)
    r = g.shape[0]
    epg = EXPERTS_PER_GROUP
    n_pairs = epg * (epg - 1) // 2
    n_cls = N_GROUPS * n_pairs
    onehot = (g[:, None] == jnp.arange(n_cls, dtype=I32)[None, :]).astype(I32)
    counts = jnp.sum(onehot, axis=0)
    rank = jnp.sum((jnp.cumsum(onehot, axis=0) - 1) * onehot, axis=1)
    padded = ((counts + tm - 1) // tm) * tm
    ends = jnp.cumsum(padded)
    pos = jnp.sum(onehot * (ends - padded)[None, :], axis=1) + rank
    p = r + n_cls * tm if p_rows is None else p_rows
    assert p >= r + n_cls * tm and p % tm == 0
    tile_start = jnp.arange(p // tm, dtype=I32) * tm
    tile_cls = jnp.minimum(jnp.sum((tile_start[:, None] >= ends[None, :]).astype(I32), axis=1), n_cls - 1)
    tile_ok = tile_start < ends[-1]
    tile_valid = jnp.concatenate([tile_ok.astype(I32), (ends[-1:] // tm - 1).astype(I32)])
    pairs = [(i, j) for i in range(epg) for j in range(i + 1, epg)]
    pair_lo = jnp.array([a for a, _ in pairs], I32)
    pair_hi = jnp.array([b for _, b in pairs], I32)
    grp = tile_cls // n_pairs
    tile_expert = jnp.stack([grp * epg + pair_lo[tile_cls % n_pairs],
                             grp * epg + pair_hi[tile_cls % n_pairs]], axis=1)
    tile_expert = jnp.where(tile_ok[:, None], tile_expert, 0).reshape(-1)
    return pos.astype(I32), tile_expert.astype(I32), tile_valid, p


def _moe(fexts, gids, w1, w3, w2, layer, f_sorted=None, tm=512):
    pos, tile_expert, tile_valid, p = _sorted_positions(gids, tm, None if f_sorted is None else f_sorted.shape[0])
    sizes = [f.shape[0] for f in fexts]
    poss, off = [], 0
    for s in sizes:
        poss.append(lax.slice(pos, (off,), (off + s,)))
        off += s
    if f_sorted is None:
        f_sorted = jnp.zeros((p, fexts[0].shape[1]), F32)
    for fext, ps in zip(fexts, poss):
        f_sorted = _dispatch(fext, ps, f_sorted)
    y_sorted = _moe_ffn(tile_expert, tile_valid, f_sorted, w1, w3, w2, layer, tm)
    return y_sorted, poss, f_sorted


def kernel(x, c, ctx, c_ctx, ada_w, ada_b, norm_mix, norm_ffn, even_w_in, even_w_out, hgrn_lb_logits, hgrn_out_norm, diff_qk_norm, diff_lambda, diff_out_norm, odd_w_in, odd_conv_w, odd_conv_b, rg_gate_w, rg_gate_b, rg_lambda, odd_w_out, moe_w_grp, moe_b_grp, moe_w_exp, moe_b_exp, moe_w1, moe_w3, moe_w2):
    bsz, n, d = x.shape
    n_ctx = ctx.shape[1]
    depth = ada_w.shape[0]
    assert depth == 2

    cvec = jnp.zeros((MOD_ROWS, d), F32).at[:bsz].set(c).at[bsz].set(c_ctx)
    mods_all = _ada_all(cvec, ada_w, ada_b).reshape(depth, MOD_ROWS * N_MOD, 1, d)
    x_l = x.reshape(bsz * n, d)
    x_c = ctx.reshape(bsz * n_ctx, d)
    lb_all = jnp.cumsum(jax.nn.softmax(hgrn_lb_logits.astype(F32), axis=1), axis=1)

    def router(l):
        w = jnp.concatenate([moe_w_grp[l], moe_w_exp[l]], axis=1)
        b = jnp.concatenate([moe_b_grp[l], moe_b_exp[l]])
        padw = ROUTE_W - w.shape[1]
        return jnp.pad(w, ((0, 0), (0, padw))), jnp.pad(b, (0, padw)).reshape(1, ROUTE_W)

    l = 0
    mods = mods_all[l]
    lam_init = 0.8 - 0.6 * math.exp(-0.3 * l)
    lv = diff_lambda[0].astype(F32)
    lam = jnp.exp(jnp.sum(lv[0] * lv[1])) - jnp.exp(jnp.sum(lv[2] * lv[3])) + lam_init
    w_in = even_w_in
    proj_l = _norm_mod_mm(x_l, norm_mix[l], mods, w_in, n, 0, tm=1024)
    proj_c = _norm_mod_mm(x_c, norm_mix[l], mods, w_in, bsz * n_ctx, bsz, tm=1024)
    a_l, a_c = _hgrn(proj_l, proj_c, lb_all[0, 0], lb_all[1, 0], hgrn_out_norm[0], bsz)
    b_l, b_c = _attn(proj_l, proj_c, diff_qk_norm[0], diff_out_norm[0], lam, lam_init, bsz)
    w_out = even_w_out[0].astype(BF16)
    wr, br = router(l)
    x_l, f_l, g_l = _out_proj([a_l, b_l], w_out, x_l, norm_ffn[l], mods, wr, br, n, 0)
    x_c, f_c, g_c = _out_proj([a_c, b_c], w_out, x_c, norm_ffn[l], mods, wr, br, bsz * n_ctx, bsz)
    y_sorted, (pos_l, pos_c), f_sorted = _moe([f_l, f_c], [g_l, g_c], moe_w1, moe_w3, moe_w2, l)
    x_l = _combine(y_sorted, pos_l, x_l, mods, n, 0)
    x_c = _combine(y_sorted, pos_c, x_c, mods, bsz * n_ctx, bsz)

    l = 1
    mods = mods_all[l]
    w_in = odd_w_in
    proj_l = _norm_mod_mm(x_l, norm_mix[l], mods, w_in, n, 0, tm=1024)
    proj_c = _norm_mod_mm(x_c, norm_mix[l], mods, w_in, bsz * n_ctx, bsz, tm=1024)
    gated = _rglru(proj_l, proj_c, odd_conv_w[0], odd_conv_b[0], rg_gate_w[0], rg_gate_b[0], rg_lambda[0], bsz)
    wr, br = router(l)
    x_l, f_l, g_l = _out_proj([gated], odd_w_out[0].astype(BF16), x_l, norm_ffn[l], mods, wr, br, n, 0)
    y_sorted, (pos_l,), _ = _moe([f_l], [g_l], moe_w1, moe_w3, moe_w2, l, f_sorted)
    return _combine(y_sorted, pos_l, x_l, mods, n, 0).reshape(bsz, n, d)
```

```python
import functools
import math

import jax
import jax.numpy as jnp
from jax import lax
from jax.experimental import pallas as pl
from jax.experimental.pallas import tpu as pltpu

F32 = jnp.float32
BF16 = jnp.bfloat16
I32 = jnp.int32

EPS = 1e-6
LOG2_E = 1.4426950408889634
TINY = 1e-30
LANES = 128
SUBLANES = 8
BF16_SUBLANES = 16
GRID_W = 64
A_HEADS = 8
HEAD_W = 128
HGRN_CHUNK = 32
HGRN_BLOCK = 256
B_HEADS = 8
B_DH = 64
ROPE_BASE = 10000.0
ATTN_KEY_CHUNK = 256
ATTN_SKEW = 3
RG_HEADS = 16
CONV_W = 4
RG_C = 8.0
N_GROUPS = 4
EXPERTS_PER_GROUP = 4
N_EXPERTS = 16
N_MOD = 6
MOD_ROWS = 16
ROUTE_W = LANES
ROW_DMA_UNROLL = 8

VMEM_LIMIT = 56 * 1024 * 1024


def _cparams(sem):
    return pltpu.CompilerParams(dimension_semantics=sem, vmem_limit_bytes=VMEM_LIMIT)


def _sigmoid(x):
    return 0.5 * jnp.tanh(0.5 * x) + 0.5


def _dot(a, b):
    return jnp.dot(a, b, preferred_element_type=F32)


def _dot_nt(a, b):
    return lax.dot_general(a, b, (((1,), (1,)), ((), ())), preferred_element_type=F32)


def _dot_tn(a, b):
    return lax.dot_general(a, b, (((0,), (0,)), ((), ())), preferred_element_type=F32)


def _split_bf16(x):
    hi = x.astype(BF16)
    return hi, (x - hi.astype(F32)).astype(BF16)


def _ada_kernel(c_ref, w_ref, b_ref, o_ref):
    c = c_ref[...]
    a = (c * _sigmoid(c)).astype(BF16)
    o_ref[...] = _dot(a, w_ref[...].astype(BF16)) + b_ref[...]


def _ada_all(cvec, ada_w, ada_b, tn=1024):
    depth, d, n = ada_w.shape
    return pl.pallas_call(
        _ada_kernel,
        grid=(depth, n // tn),
        in_specs=[pl.BlockSpec((MOD_ROWS, d), lambda l, j: (0, 0)),
                  pl.BlockSpec((None, d, tn), lambda l, j: (l, 0, j)),
                  pl.BlockSpec((None, 1, tn), lambda l, j: (l, 0, j))],
        out_specs=pl.BlockSpec((None, MOD_ROWS, tn), lambda l, j: (l, 0, j)),
        out_shape=jax.ShapeDtypeStruct((depth, MOD_ROWS, n), F32),
        compiler_params=_cparams(("arbitrary", "arbitrary")),
        name="ada_mod",
    )(cvec, ada_w, ada_b.reshape(depth, 1, n))


def _norm_mod_mm_kernel(x_ref, g_ref, sh_ref, sc_ref, w_ref, o_ref, h_scr):
    @pl.when(pl.program_id(1) == 0)
    def _():
        x = x_ref[...]
        y = x * lax.rsqrt(jnp.mean(x * x, axis=-1, keepdims=True) + EPS) * g_ref[...]
        h_scr[...] = (y * (1.0 + sc_ref[0]) + sh_ref[0]).astype(BF16)

    o_ref[...] = _dot(h_scr[...], w_ref[...].astype(BF16)).astype(o_ref.dtype)


def _norm_mod_mm(x2, gain, mods, w, rows_per_mod, mod_base, tm, tn=1024, cols=None):
    r, d = x2.shape
    col0, n = (0, w.shape[2]) if cols is None else cols
    jb = col0 // tn
    tpm = rows_per_mod // tm

    def mrow(i):
        return (mod_base + i // tpm) * N_MOD

    return pl.pallas_call(
        _norm_mod_mm_kernel,
        grid=(r // tm, n // tn),
        in_specs=[pl.BlockSpec((tm, d), lambda i, j: (i, 0)),
                  pl.BlockSpec((1, d), lambda i, j: (0, 0)),
                  pl.BlockSpec((1, 1, d), lambda i, j: (mrow(i) + 0, 0, 0)),
                  pl.BlockSpec((1, 1, d), lambda i, j: (mrow(i) + 1, 0, 0)),
                  pl.BlockSpec((None, d, tn), lambda i, j: (0, 0, jb + j))],
        out_specs=pl.BlockSpec((tm, tn), lambda i, j: (i, j)),
        out_shape=jax.ShapeDtypeStruct((r, n), F32),
        scratch_shapes=[pltpu.VMEM((tm, d), BF16)],
        compiler_params=_cparams(("parallel", "arbitrary")),
        name="norm_mod_mm",
    )(x2, gain.reshape(1, d), mods, mods, w)


def _hgrn_kernel(ql, ffl, fbl, vl, gl, qc, ffc, fbc, vc, gc, lbf_ref, lbb_ref, gain_ref,
                 ol_ref, oc_ref,
                 oi_l, oi_c, qtf_l, qtb_l, qtf_c, qtb_c, kv_f, kv_b, dec_f, dec_b):
    c_sz = HGRN_CHUNK
    blk = HGRN_BLOCK
    cpb = blk // c_sz
    nb_l = ql.shape[0] // blk
    nc_l = ql.shape[0] // c_sz
    nc_c = qc.shape[0] // c_sz
    nc = nc_l + nc_c
    assert qc.shape[0] == blk
    row = lax.broadcasted_iota(I32, (blk, blk), 0)
    col = lax.broadcasted_iota(I32, (blk, blk), 1)
    same = (row // c_sz) == (col // c_sz)
    tril = same & (row >= col)
    triu = same & (row <= col)
    lbf = lbf_ref[...]
    lbb = lbb_ref[...]

    exp_mask = (lax.broadcasted_iota(I32, (blk, cpb * HEAD_W), 0) // c_sz
                == lax.broadcasted_iota(I32, (blk, cpb * HEAD_W), 1) // HEAD_W)

    def local_terms(r0, q_ref, ff_ref, fb_ref, v_ref, oi_ref, qtf_ref, qtb_ref, gid_f, gid_b):
        rows = pl.ds(r0, blk)
        q = q_ref[rows, :]
        qs = q * _sigmoid(q)
        vb = v_ref[rows, :].astype(BF16)
        vt = v_ref[rows, :].T.astype(BF16)
        two = (0, 1)
        mask = (tril, triu)
        last = (c_sz - 1, 0)
        f = [lb + (1.0 - lb) * _sigmoid(ref[rows, :]) for ref, lb in ((ff_ref, lbf), (fb_ref, lbb))]
        k = [1.0 - f[d] for d in two]
        lf2 = [jnp.concatenate(_split_bf16(jnp.log(f[d])), axis=1) for d in two]
        cum2 = [_dot(jnp.where(mask[d], 1.0, 0.0).astype(BF16), lf2[d]) for d in two]
        cum = [cum2[d][:, :HEAD_W] + cum2[d][:, HEAD_W:] for d in two]
        tot = [jnp.concatenate(
            [jnp.broadcast_to(cum[d][c * c_sz + last[d]:c * c_sz + last[d] + 1, :], (c_sz, HEAD_W))
             for c in range(cpb)], axis=0) for d in two]
        e = [jnp.exp(cum[d]) for d in two]
        qt = [(qs * e[d]).astype(BF16) for d in two]
        kt = [(k[d] * jnp.exp(-cum[d])).astype(BF16) for d in two]
        att = [jnp.where(mask[d], _dot_nt(qt[d], kt[d]), 0.0).astype(BF16) for d in two]
        oi_ref[rows, :] = _dot(att[0], vb) + _dot(att[1], vb)
        k2 = [(k[d] * jnp.exp(tot[d] - cum[d])).astype(BF16) for d in two]
        k2x = [jnp.where(exp_mask, jnp.concatenate([k2[d]] * cpb, axis=1), jnp.zeros((), BF16)) for d in two]
        kvs = [_dot(vt, k2x[d]) for d in two]
        for d, (qt_ref, kv_ref, dec_ref, gid) in enumerate(((qtf_ref, kv_f, dec_f, gid_f),
                                                            (qtb_ref, kv_b, dec_b, gid_b))):
            for c in range(cpb):
                kv_ref[gid + c] = kvs[d][:, c * HEAD_W:(c + 1) * HEAD_W]
                dec_ref[gid + c] = e[d][c * c_sz + last[d]:c * c_sz + last[d] + 1, :]
            qt_ref[rows, :] = qt[d]

    local_terms(0, qc, ffc, fbc, vc, oi_c, qtf_c, qtb_c, 0, nc_l)

    def local_l(i, carry):
        local_terms(pl.multiple_of(i * blk, blk), ql, ffl, fbl, vl, oi_l, qtf_l, qtb_l,
                    nc_c + i * cpb, i * cpb)
        return carry

    lax.fori_loop(0, nb_l, local_l, 0, unroll=4)

    def rec_f(i, s):
        new = s * dec_f[i] + kv_f[i]
        kv_f[i] = s
        return new

    def rec_b(i, s):
        j = nc - 1 - i
        new = s * dec_b[j] + kv_b[j]
        kv_b[j] = s
        return new

    s0 = jnp.zeros((HEAD_W, HEAD_W), F32)
    lax.fori_loop(0, nc, rec_f, s0, unroll=2)
    lax.fori_loop(0, nc, rec_b, s0, unroll=2)

    gain = gain_ref[...]

    def finish(r0, g_ref, oi_ref, qtf_ref, qtb_ref, o_ref, gid_f, gid_b):
        parts = []
        for c in range(cpb):
            rows_c = pl.ds(r0 + c * c_sz, c_sz)
            parts.append(_dot_nt(qtf_ref[rows_c, :], kv_f[gid_f + c].astype(BF16))
                         + _dot_nt(qtb_ref[rows_c, :], kv_b[gid_b + c].astype(BF16)))
        rows = pl.ds(r0, blk)
        o = oi_ref[rows, :] + jnp.concatenate(parts, axis=0)
        y = o * lax.rsqrt(jnp.mean(o * o, axis=-1, keepdims=True) + EPS) * gain
        g = g_ref[rows, :]
        o_ref[rows, :] = (y * (g * _sigmoid(g))).astype(o_ref.dtype)

    finish(0, gc, oi_c, qtf_c, qtb_c, oc_ref, 0, nc_l)

    def fin_l(i, carry):
        finish(pl.multiple_of(i * blk, blk), gl, oi_l, qtf_l, qtb_l, ol_ref, nc_c + i * cpb, i * cpb)
        return carry

    lax.fori_loop(0, nb_l, fin_l, 0, unroll=True)


def _hgrn(proj_l, proj_c, lb_f, lb_b, gain, bsz):
    n = proj_l.shape[0] // bsz
    n_ctx = proj_c.shape[0] // bsz
    nc = (n + n_ctx) // HGRN_CHUNK
    w = A_HEADS * HEAD_W

    def col(k):
        return lambda b, h: (b, k * A_HEADS + h)

    in_specs = ([pl.BlockSpec((n, HEAD_W), col(k)) for k in range(5)]
                + [pl.BlockSpec((n_ctx, HEAD_W), col(k)) for k in range(5)]
                + [pl.BlockSpec((1, HEAD_W), lambda b, h: (0, h)),
                   pl.BlockSpec((1, HEAD_W), lambda b, h: (0, h)),
                   pl.BlockSpec((1, HEAD_W), lambda b, h: (0, 0))])
    return pl.pallas_call(
        _hgrn_kernel,
        grid=(bsz, A_HEADS),
        in_specs=in_specs,
        out_specs=[pl.BlockSpec((n, HEAD_W), lambda b, h: (b, h)),
                   pl.BlockSpec((n_ctx, HEAD_W), lambda b, h: (b, h))],
        out_shape=[jax.ShapeDtypeStruct((bsz * n, w), BF16),
                   jax.ShapeDtypeStruct((bsz * n_ctx, w), BF16)],
        scratch_shapes=[pltpu.VMEM((n, HEAD_W), F32), pltpu.VMEM((n_ctx, HEAD_W), F32),
                        pltpu.VMEM((n, HEAD_W), BF16), pltpu.VMEM((n, HEAD_W), BF16),
                        pltpu.VMEM((n_ctx, HEAD_W), BF16), pltpu.VMEM((n_ctx, HEAD_W), BF16),
                        pltpu.VMEM((nc, HEAD_W, HEAD_W), F32), pltpu.VMEM((nc, HEAD_W, HEAD_W), F32),
                        pltpu.VMEM((nc, 1, HEAD_W), F32), pltpu.VMEM((nc, 1, HEAD_W), F32)],
        compiler_params=_cparams(("parallel", "parallel")),
        name="hgrn2",
    )(*([proj_l] * 5 + [proj_c] * 5), lb_f.reshape(1, w), lb_b.reshape(1, w), gain.reshape(1, HEAD_W))


def _half_mean_matrix():
    r = lax.broadcasted_iota(I32, (LANES, LANES), 0) // B_DH
    c = lax.broadcasted_iota(I32, (LANES, LANES), 1) // B_DH
    return (r == c).astype(BF16)


def _qk_prep(t, gain, bd, cos, sin):
    sq_hi, sq_lo = _split_bf16(t * t)
    ms = (_dot(sq_hi, bd) + _dot(sq_lo, bd)) * (1.0 / B_DH)
    y = t * lax.rsqrt(ms + EPS) * gain
    if cos is None:
        return y
    lane = lax.broadcasted_iota(I32, y.shape, 1)
    partner = jnp.where(lane % 2 == 0, pltpu.roll(y, LANES - 1, 1), pltpu.roll(y, 1, 1))
    return y * cos + partner * sin


def _attn_kernel(q_l, q_c, k_l, k_c, v_l, v_c, cosq, sinq, cosk, sink, gq_ref, gk_ref, go_ref, lam_ref,
                 o_l, o_c, kp, vp, s_even, s_odd, m_even, m_odd, *, out_scale):
    step = pl.program_id(2)
    n_ctx = k_c.shape[0]
    s_bufs = (s_even, s_odd)
    m_bufs = (m_even, m_odd)
    bd = _half_mean_matrix()
    lane = lax.broadcasted_iota(I32, (1, LANES), 1)
    masks = ((lane < B_DH).astype(F32), (lane >= B_DH).astype(F32))
    lam = lam_ref[:, 0:1]
    q_scale = B_DH ** -0.5 * LOG2_E

    def masked_q(q):
        q = q * q_scale
        return [(q * msk).astype(BF16) for msk in masks]

    def epilogue(r):
        o = (r[0][0:HEAD_W, :] * (1.0 / r[0][HEAD_W:HEAD_W + 1, :])
             - r[1][0:HEAD_W, :] * (lam / r[1][HEAD_W:HEAD_W + 1, :]))
        gain = jnp.concatenate([go_ref[...]] * (o.shape[1] // LANES), axis=1)
        y = o * lax.rsqrt(jnp.mean(o * o, axis=0, keepdims=True) + EPS) * gain
        return (y * out_scale).T.astype(BF16)

    @pl.when((step == 0) & (pl.program_id(0) == 0) & (pl.program_id(1) == 0))
    def _():
        s_odd[...] = jnp.zeros_like(s_odd)
        m_odd[...] = jnp.zeros_like(m_odd)

    @pl.when(step == 0)
    def _():
        kp[0:n_ctx, :] = _qk_prep(k_c[...], gk_ref[...], bd, None, None).astype(BF16)
        kp[n_ctx:, :] = _qk_prep(k_l[...], gk_ref[...], bd, cosk[...], sink[...]).astype(BF16)
        vp[0:HEAD_W, 0:n_ctx] = v_c[...].T.astype(BF16)
        vp[0:HEAD_W, n_ctx:] = v_l[...].T.astype(BF16)
        pad_rows = vp.shape[0] - HEAD_W
        ones_row = (lax.broadcasted_iota(I32, (pad_rows, 1), 0) == 0).astype(BF16)
        vp[HEAD_W:, :] = jnp.broadcast_to(ones_row, (pad_rows, vp.shape[1]))
        qm = masked_q(_qk_prep(q_c[...], gq_ref[...], bd, None, None))
        s_ctx = [_dot_nt(kp[0:n_ctx, :], t) for t in qm]
        e_ctx = [jnp.exp2(t - jnp.max(t, axis=0, keepdims=True)).astype(BF16) for t in s_ctx]
        o_c[...] = epilogue([_dot(vp[:, 0:n_ctx], t) for t in e_ctx])

    kc = ATTN_KEY_CHUNK
    for parity in (0, 1):
        @pl.when((step > 0) & (step % 2 == parity))
        def _(old=parity, new=1 - parity):
            qm = masked_q(_qk_prep(q_l[...], gq_ref[...], bd, cosq[...], sinq[...]))
            mx_old = [m_bufs[old][i, 0:1, :] for i in range(2)]
            acc = [None, None]
            mx_new = [None, None]
            n_chunks = kp.shape[0] // kc
            for c in range(n_chunks + ATTN_SKEW):
                if c < n_chunks:
                    ks = slice(c * kc, (c + 1) * kc)
                    for i in range(2):
                        e = jnp.exp2(s_bufs[old][i, ks, :] - mx_old[i]).astype(BF16)
                        part = _dot(vp[:, ks], e)
                        acc[i] = part if acc[i] is None else acc[i] + part
                if c == n_chunks - 1:
                    o_l[...] = epilogue(acc)
                if c >= ATTN_SKEW:
                    ks = slice((c - ATTN_SKEW) * kc, (c - ATTN_SKEW + 1) * kc)
                    for i in range(2):
                        t = _dot_nt(kp[ks, :], qm[i])
                        s_bufs[new][i, ks, :] = t
                        tm = jnp.max(t, axis=0, keepdims=True)
                        mx_new[i] = tm if mx_new[i] is None else jnp.maximum(mx_new[i], tm)
            for i in range(2):
                m_bufs[new][i] = jnp.broadcast_to(mx_new[i], m_bufs[new].shape[1:])


def _rope_tables(n):
    n_rows = n // GRID_W
    rowp = jnp.repeat(jnp.arange(n_rows), GRID_W).astype(F32)
    colp = jnp.tile(jnp.arange(GRID_W), n_rows).astype(F32)
    pairs = B_DH // 4
    inv = ROPE_BASE ** (-jnp.arange(pairs, dtype=F32) / pairs)
    ang = jnp.concatenate([rowp[:, None] * inv, colp[:, None] * inv], axis=-1)
    cos = jnp.repeat(jnp.cos(ang), 2, axis=-1)
    sin = jnp.repeat(jnp.sin(ang), 2, axis=-1) * jnp.tile(jnp.array([-1.0, 1.0], F32), B_DH // 2)
    return jnp.tile(cos, (1, 2)), jnp.tile(sin, (1, 2))


def _attn(proj_l, proj_c, qk_gain, out_gain, lam, lam_init, bsz, tq=512):
    n = proj_l.shape[0] // bsz
    n_ctx = proj_c.shape[0] // bsz
    nqb = n // tq
    w = B_HEADS * HEAD_W
    cos, sin = _rope_tables(n)
    qcol, kcol, vcol = 5 * A_HEADS, 5 * A_HEADS + B_HEADS, 5 * A_HEADS + 2 * B_HEADS

    def q_blk(t):
        return jnp.clip(t - 1, 0, nqb - 1)

    def o_blk(t):
        return jnp.clip(t - 2, 0, nqb - 1)

    def lat_q(b, h, t):
        return (b * nqb + q_blk(t), qcol + h)

    in_specs = [pl.BlockSpec((tq, HEAD_W), lat_q),
                pl.BlockSpec((n_ctx, HEAD_W), lambda b, h, qb: (b, qcol + h)),
                pl.BlockSpec((n, HEAD_W), lambda b, h, qb: (b, kcol + h)),
                pl.BlockSpec((n_ctx, HEAD_W), lambda b, h, qb: (b, kcol + h)),
                pl.BlockSpec((n, HEAD_W), lambda b, h, qb: (b, vcol + h)),
                pl.BlockSpec((n_ctx, HEAD_W), lambda b, h, qb: (b, vcol + h)),
                pl.BlockSpec((tq, LANES), lambda b, h, t: (q_blk(t), 0)),
                pl.BlockSpec((tq, LANES), lambda b, h, t: (q_blk(t), 0)),
                pl.BlockSpec((n, LANES), lambda b, h, qb: (0, 0)),
                pl.BlockSpec((n, LANES), lambda b, h, qb: (0, 0)),
                pl.BlockSpec((1, LANES), lambda b, h, qb: (0, 0)),
                pl.BlockSpec((1, LANES), lambda b, h, qb: (0, 0)),
                pl.BlockSpec((HEAD_W, LANES), lambda b, h, qb: (0, 0)),
                pl.BlockSpec((1, LANES), lambda b, h, qb: (0, 0))]
    return pl.pallas_call(
        functools.partial(_attn_kernel, out_scale=1.0 - lam_init),
        grid=(bsz, B_HEADS, nqb + 2),
        in_specs=in_specs,
        out_specs=[pl.BlockSpec((tq, HEAD_W), lambda b, h, t: (b * nqb + o_blk(t), h)),
                   pl.BlockSpec((n_ctx, HEAD_W), lambda b, h, qb: (b, h))],
        out_shape=[jax.ShapeDtypeStruct((bsz * n, w), BF16),
                   jax.ShapeDtypeStruct((bsz * n_ctx, w), BF16)],
        scratch_shapes=[pltpu.VMEM((n + n_ctx, HEAD_W), BF16),
                        pltpu.VMEM((HEAD_W + BF16_SUBLANES, n + n_ctx), BF16),
                        pltpu.VMEM((2, n + n_ctx, tq), F32), pltpu.VMEM((2, n + n_ctx, tq), F32),
                        pltpu.VMEM((2, SUBLANES, tq), F32), pltpu.VMEM((2, SUBLANES, tq), F32)],
        compiler_params=_cparams(("arbitrary", "arbitrary", "arbitrary")),
        name="diff_attn",
    )(proj_l, proj_c, proj_l, proj_c, proj_l, proj_c, cos, sin, cos, sin,
      jnp.tile(qk_gain[0], 2).reshape(1, LANES), jnp.tile(qk_gain[1], 2).reshape(1, LANES),
      jnp.broadcast_to(out_gain[:, None], (HEAD_W, LANES)), jnp.full((1, LANES), lam, F32))


def _scan_steps(a, b, reverse, axis):
    n = a.shape[axis]
    pos = lax.broadcasted_iota(I32, a.shape, axis)
    s = 1
    while s < n:
        keep = (pos < n - s) if reverse else (pos >= s)
        shift = n - s if reverse else s
        a_sh = jnp.where(keep, pltpu.roll(a, shift, axis), 1.0)
        b_sh = jnp.where(keep, pltpu.roll(b, shift, axis), 0.0)
        b = a * b_sh + b
        a = a * a_sh
        s *= 2
    return a, b


def _rglru_kernel(y_l, u_l, u_c, cw_ref, cb_ref, gw_ref, gb_ref, lam_ref, o_ref,
                  upad, a_f, b_f, a_b, b_b, h_f, p_f, h_b, p_b, *, seg_len, pitch):
    n = u_l.shape[0]
    n_ctx = u_c.shape[0]
    nseg = SUBLANES
    cw = cw_ref[...]
    cb = cb_ref[...]
    pad = SUBLANES

    def pieces(start, rows):
        out = []
        for r in range(nseg):
            lo, hi = max(start, r * seg_len), min(start + rows, (r + 1) * seg_len)
            if lo < hi:
                out.append((r, lo - r * seg_len, lo - start, hi - lo))
        return out

    def put(ref, start, val):
        for r, off, src, ln in pieces(start, val.shape[0]):
            ref[r * pitch + off:r * pitch + off + ln, :] = val[src:src + ln, :]

    def conv(u_ref, rows):
        upad[0:pad, :] = jnp.zeros((pad, LANES), F32)
        upad[pad:pad + rows, :] = u_ref[...]
        upad[pad + rows:pad + rows + pad, :] = jnp.zeros((pad, LANES), F32)
        acc = cb + jnp.zeros((rows, LANES), F32)
        for j in range(CONV_W):
            off = pad + j - CONV_W // 2
            acc = acc + cw[j:j + 1, :] * upad[off:off + rows, :]
        return acc

    def gates(uc, start_f, start_b):
        ub = uc.astype(BF16)
        for d, (a_ref, b_ref, start) in enumerate(((a_f, b_f, start_f), (a_b, b_b, start_b))):
            lam = lam_ref[d:d + 1, :]
            neg_sp = -(jnp.maximum(-lam, 0.0) + jnp.log(1.0 + jnp.exp(-jnp.abs(lam))))
            r = _sigmoid(_dot(ub, gw_ref[d, 0].astype(BF16)) + gb_ref[d, 0:1, :])
            i = _sigmoid(_dot(ub, gw_ref[d, 1].astype(BF16)) + gb_ref[d, 1:2, :])
            a = jnp.exp((RG_C * neg_sp) * r)
            put(a_ref, start, a)
            x = (1.0 - a) * (1.0 + a)
            put(b_ref, start, (x * lax.rsqrt(jnp.maximum(x, TINY))) * (i * uc))

    gates(conv(u_c, n_ctx), 0, n)
    gates(conv(u_l, n), n_ctx, 0)

    def step(i, carry):
        hf, pf, hb, pb = carry
        rows_f = pl.ds(i, nseg, stride=pitch)
        rows_b = pl.ds(seg_len - 1 - i, nseg, stride=pitch)
        af, ab = a_f[rows_f, :], a_b[rows_b, :]
        hf = af * hf + b_f[rows_f, :]
        hb = ab * hb + b_b[rows_b, :]
        pf = pf * af
        pb = pb * ab
        h_f[rows_f, :] = hf
        p_f[rows_f, :] = pf
        h_b[rows_b, :] = hb
        p_b[rows_b, :] = pb
        return hf, pf, hb, pb

    zero = jnp.zeros((nseg, LANES), F32)
    one = jnp.ones((nseg, LANES), F32)
    hf, pf, hb, pb = lax.fori_loop(0, seg_len, step, (zero, one, zero, one), unroll=8)

    seg = lax.broadcasted_iota(I32, (nseg, LANES), 0)
    _, ef = _scan_steps(pf, hf, False, 0)
    _, eb = _scan_steps(pb, hb, True, 0)
    carry_f = jnp.where(seg >= 1, pltpu.roll(ef, 1, 0), 0.0)
    carry_b = jnp.where(seg < nseg - 1, pltpu.roll(eb, nseg - 1, 0), 0.0)

    cuts = sorted({0, n} | {k * seg_len for k in range(nseg + 1) if 0 < k * seg_len < n}
                  | {k * seg_len - n_ctx for k in range(nseg + 1) if 0 < k * seg_len - n_ctx < n})
    for i0, i1 in zip(cuts[:-1], cuts[1:]):
        ln = i1 - i0
        (rf, of, _, _), = pieces(n_ctx + i0, ln)
        (rb, ob, _, _), = pieces(i0, ln)
        sf = slice(rf * pitch + of, rf * pitch + of + ln)
        sb = slice(rb * pitch + ob, rb * pitch + ob + ln)
        h = (h_f[sf, :] + p_f[sf, :] * carry_f[rf:rf + 1, :]) + (h_b[sb, :] + p_b[sb, :] * carry_b[rb:rb + 1, :])
        y = y_l[i0:i1, :]
        gelu = 0.5 * y * (1.0 + jnp.tanh(0.7978845608028654 * (y + 0.044715 * y * y * y)))
        o_ref[i0:i1, :] = (h * gelu).astype(o_ref.dtype)


def _rglru(proj_l, u_c, conv_w, conv_b, gate_w, gate_b, lam, bsz):
    n = proj_l.shape[0] // bsz
    n_ctx = u_c.shape[0] // bsz
    w = RG_HEADS * HEAD_W
    seg_len = (n + n_ctx) // SUBLANES
    assert seg_len * SUBLANES == n + n_ctx and seg_len % SUBLANES == 0
    assert n_ctx % SUBLANES == 0 and n_ctx <= seg_len
    pitch = seg_len + SUBLANES
    return pl.pallas_call(
        functools.partial(_rglru_kernel, seg_len=seg_len, pitch=pitch),
        grid=(bsz, RG_HEADS),
        in_specs=[pl.BlockSpec((n, HEAD_W), lambda b, h: (b, h)),
                  pl.BlockSpec((n, HEAD_W), lambda b, h: (b, RG_HEADS + h)),
                  pl.BlockSpec((n_ctx, HEAD_W), lambda b, h: (b, h)),
                  pl.BlockSpec((CONV_W, HEAD_W), lambda b, h: (0, h)),
                  pl.BlockSpec((1, HEAD_W), lambda b, h: (0, h)),
                  pl.BlockSpec((2, 2, None, HEAD_W, HEAD_W), lambda b, h: (0, 0, h, 0, 0)),
                  pl.BlockSpec((2, 2, HEAD_W), lambda b, h: (0, 0, h)),
                  pl.BlockSpec((2, HEAD_W), lambda b, h: (0, h))],
        out_specs=pl.BlockSpec((n, HEAD_W), lambda b, h: (b, h)),
        out_shape=jax.ShapeDtypeStruct((bsz * n, w), BF16),
        scratch_shapes=[pltpu.VMEM((n + 2 * SUBLANES, LANES), F32)]
                       + [pltpu.VMEM((SUBLANES * pitch, LANES), F32)] * 8,
        compiler_params=_cparams(("parallel", "parallel")),
        name="rglru",
    )(proj_l, proj_l, u_c, conv_w, conv_b.reshape(1, w), gate_w, gate_b, lam)


def _out_proj_kernel(*refs, n_mix, n_groups, epg, row_chunk):
    mix_refs = refs[:n_mix]
    (w_ref, x_ref, gate_ref, g_ref, sh_ref, sc_ref, wr_ref, rb_ref,
     xo_ref, f_ref, gid_ref) = refs[n_mix:]
    d = x_ref.shape[1]
    n_pairs = epg * (epg - 1) // 2
    for r0 in range(0, x_ref.shape[0], row_chunk):
        rows = slice(r0, r0 + row_chunk)
        k0 = 0
        mix = None
        for m_ref in mix_refs:
            kk = m_ref.shape[1]
            part = _dot(m_ref[rows, :], w_ref[k0:k0 + kk, :])
            mix = part if mix is None else mix + part
            k0 += kk
        x = x_ref[rows, :] + gate_ref[0] * mix
        xo_ref[rows, :] = x
        y = x * lax.rsqrt(jnp.mean(x * x, axis=-1, keepdims=True) + EPS) * g_ref[...]
        f = y * (1.0 + sc_ref[0]) + sh_ref[0]
        f_ref[rows, 0:d] = f

        f_hi, f_lo = _split_bf16(f)
        hh = _dot(f_hi, wr_ref[...])
        lg = hh[:, 0:ROUTE_W] + hh[:, ROUTE_W:] + _dot(f_lo, wr_ref[:, 0:ROUTE_W]) + rb_ref[...]
        lane = lax.broadcasted_iota(I32, lg.shape, 1).astype(F32)
        neg = -jnp.inf
        big = float(ROUTE_W)
        gl = jnp.where(lane < n_groups, lg, neg)
        gmax = jnp.max(gl, axis=-1, keepdims=True)
        gidx = jnp.min(jnp.where(gl == gmax, lane, big), axis=-1, keepdims=True)
        gw = 1.0 / jnp.sum(jnp.exp(gl - gmax), axis=-1, keepdims=True)
        base = n_groups + gidx * epg
        el = jnp.where((lane >= base) & (lane < base + epg), lg, neg)
        v1 = jnp.max(el, axis=-1, keepdims=True)
        i1 = jnp.min(jnp.where(el == v1, lane, big), axis=-1, keepdims=True)
        el2 = jnp.where(lane == i1, neg, el)
        v2 = jnp.max(el2, axis=-1, keepdims=True)
        i2 = jnp.min(jnp.where(el2 == v2, lane, big), axis=-1, keepdims=True)
        t = jnp.exp(v2 - v1)
        w1 = gw / (1.0 + t)
        w2 = w1 * t
        s1 = i1 - base
        s2 = i2 - base
        lo = jnp.minimum(s1, s2)
        hi = jnp.maximum(s1, s2)
        pair = lo * (2 * epg - 1 - lo) * 0.5 + (hi - lo - 1.0)
        w_lo = jnp.where(s1 < s2, w1, w2)
        w_hi = jnp.where(s1 < s2, w2, w1)
        f_ref[rows, d:] = jnp.where(lane == 0.0, w_lo, 0.0) + jnp.where(lane == 1.0, w_hi, 0.0)
        cls = jnp.broadcast_to(gidx * n_pairs + pair, (row_chunk, LANES)).T
        gid_ref[:, rows] = cls[0:SUBLANES, :].astype(I32)


def _out_proj(mixes, w_out, x2, gain, mods, w_router, b_router, rows_per_mod, mod_base, tm=256):
    r, d = x2.shape
    k = w_out.shape[0]
    tpm = rows_per_mod // tm
    wr = jnp.concatenate(_split_bf16(w_router), axis=1)

    def mrow(i):
        return (mod_base + i // tpm) * N_MOD

    const2 = lambda i: (0, 0)
    in_specs = ([pl.BlockSpec((tm, m.shape[1]), lambda i: (i, 0)) for m in mixes]
                + [pl.BlockSpec((k, d), const2),
                   pl.BlockSpec((tm, d), lambda i: (i, 0)),
                   pl.BlockSpec((1, 1, d), lambda i: (mrow(i) + 2, 0, 0)),
                   pl.BlockSpec((1, d), const2),
                   pl.BlockSpec((1, 1, d), lambda i: (mrow(i) + 3, 0, 0)),
                   pl.BlockSpec((1, 1, d), lambda i: (mrow(i) + 4, 0, 0)),
                   pl.BlockSpec((d, 2 * ROUTE_W), const2),
                   pl.BlockSpec((1, ROUTE_W), const2)])
    x_new, fext, cls = pl.pallas_call(
        functools.partial(_out_proj_kernel, n_mix=len(mixes), n_groups=N_GROUPS, epg=EXPERTS_PER_GROUP,
                          row_chunk=tm),
        grid=(r // tm,),
        in_specs=in_specs,
        out_specs=[pl.BlockSpec((tm, d), lambda i: (i, 0)),
                   pl.BlockSpec((tm, d + ROUTE_W), lambda i: (i, 0)),
                   pl.BlockSpec((SUBLANES, tm), lambda i: (i, 0))],
        out_shape=[jax.ShapeDtypeStruct((r, d), F32),
                   jax.ShapeDtypeStruct((r, d + ROUTE_W), F32),
                   jax.ShapeDtypeStruct((r // tm * SUBLANES, tm), I32)],
        compiler_params=_cparams(("parallel",)),
        name="out_proj",
    )(*mixes, w_out, x2, mods, gain.reshape(1, d), mods, mods, wr, b_router)
    return x_new, fext, cls.reshape(r // tm, SUBLANES, tm)[:, 0, :].reshape(r)


def _row_copy(src, s, dst, t, sem):
    return pltpu.make_async_copy(src.at[pl.ds(s, 1), :], dst.at[pl.ds(t, 1), :], sem)


def _dispatch_kernel(pos_ref, src_ref, dst_in, dst, sem, *, tm):
    del dst_in

    def body(j, carry):
        _row_copy(src_ref, j, dst, pos_ref[0, 0, j], sem).start()
        return carry
    lax.fori_loop(0, tm, body, 0, unroll=ROW_DMA_UNROLL)

    def wbody(j, carry):
        _row_copy(src_ref, 0, dst, 0, sem).wait()
        return carry
    lax.fori_loop(0, tm, wbody, 0, unroll=True)


def _dispatch(src, pos, sorted_in, tm=512):
    r, w = src.shape
    return pl.pallas_call(
        functools.partial(_dispatch_kernel, tm=tm),
        grid=(r // tm,),
        in_specs=[pl.BlockSpec((1, 1, tm), lambda i: (i, 0, 0), memory_space=pltpu.SMEM),
                  pl.BlockSpec((tm, w), lambda i: (i, 0)),
                  pl.BlockSpec(memory_space=pl.ANY)],
        out_specs=pl.BlockSpec(memory_space=pl.ANY),
        out_shape=jax.ShapeDtypeStruct(sorted_in.shape, sorted_in.dtype),
        scratch_shapes=[pltpu.SemaphoreType.DMA(())],
        input_output_aliases={2: 0},
        compiler_params=_cparams(("arbitrary",)),
        name="moe_dispatch",
    )(pos.reshape(r // tm, 1, tm), src, sorted_in)


def _pair_slot(t, s):
    return jnp.bitwise_xor(s, jnp.bitwise_and(t, 1))


def _moe_kernel(te_ref, tv_ref, x_ref, w1_ref, w3_ref, w2_ref, o_ref):
    t = pl.program_id(0)
    s = pl.program_id(1)
    d = o_ref.shape[1]
    valid = tv_ref[t] > 0
    slot = _pair_slot(t, s)

    @pl.when(valid)
    def _():
        x = x_ref[:, 0:d].astype(BF16)
        h1 = _dot(x, w1_ref[...].astype(BF16))
        h3 = _dot(x, w3_ref[...].astype(BF16))
        cw = x_ref[:, d:]
        lane = lax.broadcasted_iota(I32, cw.shape, 1)
        cws = jnp.sum(jnp.where(lane == slot, cw, 0.0), axis=-1, keepdims=True)
        y = _dot((h1 * _sigmoid(h1) * h3 * cws).astype(BF16), w2_ref[...].astype(BF16))

        @pl.when(s == 0)
        def _():
            o_ref[...] = y

        @pl.when(s > 0)
        def _():
            o_ref[...] += y

    @pl.when(jnp.logical_not(valid) & (s == 0))
    def _():
        o_ref[...] = jnp.zeros_like(o_ref)


def _moe_ffn(tile_expert, tile_valid, f_sorted, w1, w3, w2, layer, tm):
    p, dw = f_sorted.shape
    d = dw - ROUTE_W
    fe = w1.shape[3]

    def wmap(t, s, te, tv):
        return (layer, te[2 * t + _pair_slot(t, s)], 0, 0)

    grid_spec = pltpu.PrefetchScalarGridSpec(
        num_scalar_prefetch=2,
        grid=(p // tm, 2),
        in_specs=[pl.BlockSpec((tm, dw), lambda t, s, te, tv: (jnp.minimum(t, tv[tv.shape[0] - 1]), 0)),
                  pl.BlockSpec((None, None, d, fe), wmap),
                  pl.BlockSpec((None, None, d, fe), wmap),
                  pl.BlockSpec((None, None, fe, d), wmap)],
        out_specs=pl.BlockSpec((tm, d), lambda t, s, te, tv: (t, 0)),
    )
    return pl.pallas_call(
        _moe_kernel,
        grid_spec=grid_spec,
        out_shape=jax.ShapeDtypeStruct((p, d), F32),
        compiler_params=_cparams(("arbitrary", "arbitrary")),
        name="moe_ffn",
    )(tile_expert, tile_valid, f_sorted, w1, w3, w2)


def _combine_kernel(pos_cur, pos_nxt, y_hbm, x_ref, gate_ref, o_ref, ybuf, sem, *, tm):
    i = pl.program_id(0)
    n_steps = pl.num_programs(0)

    def issue(pos_ref, slot):
        def body(j, carry):
            _row_copy(y_hbm, pos_ref[0, 0, j], ybuf.at[slot], j, sem.at[slot]).start()
            return carry
        lax.fori_loop(0, tm, body, 0, unroll=ROW_DMA_UNROLL)

    @pl.when(i == 0)
    def _():
        issue(pos_cur, 0)

    @pl.when(i + 1 < n_steps)
    def _():
        issue(pos_nxt, (i + 1) % 2)

    slot = i % 2

    def wbody(j, carry):
        _row_copy(y_hbm, 0, ybuf.at[slot], 0, sem.at[slot]).wait()
        return carry
    lax.fori_loop(0, tm, wbody, 0, unroll=True)
    o_ref[...] = x_ref[...] + gate_ref[0] * ybuf[slot]


def _combine(y_sorted, pos, x2, mods, rows_per_mod, mod_base, tm=256):
    r, d = x2.shape
    n_steps = r // tm
    tpm = rows_per_mod // tm
    pos3 = pos.reshape(n_steps, 1, tm)
    return pl.pallas_call(
        functools.partial(_combine_kernel, tm=tm),
        grid=(n_steps,),
        in_specs=[pl.BlockSpec((1, 1, tm), lambda i: (i, 0, 0), memory_space=pltpu.SMEM),
                  pl.BlockSpec((1, 1, tm), lambda i: (jnp.minimum(i + 1, n_steps - 1), 0, 0),
                               memory_space=pltpu.SMEM),
                  pl.BlockSpec(memory_space=pl.ANY),
                  pl.BlockSpec((tm, d), lambda i: (i, 0)),
                  pl.BlockSpec((1, 1, d), lambda i: ((mod_base + i // tpm) * N_MOD + 5, 0, 0))],
        out_specs=pl.BlockSpec((tm, d), lambda i: (i, 0)),
        out_shape=jax.ShapeDtypeStruct((r, d), F32),
        scratch_shapes=[pltpu.VMEM((2, tm, d), F32), pltpu.SemaphoreType.DMA((2,))],
        compiler_params=_cparams(("arbitrary",)),
        name="moe_combine",
    )(pos3, pos3, y_sorted, x2, mods)


def _sorted_positions(gids, tm, p_rows):
    g = jnp.concatenate(gids)
    r = g.shape[0]
    epg = EXPERTS_PER_GROUP
    n_pairs = epg * (epg - 1) // 2
    n_cls = N_GROUPS * n_pairs
    onehot = (g[:, None] == jnp.arange(n_cls, dtype=I32)[None, :]).astype(I32)
    counts = jnp.sum(onehot, axis=0)
    rank = jnp.sum((jnp.cumsum(onehot, axis=0) - 1) * onehot, axis=1)
    padded = ((counts + tm - 1) // tm) * tm
    ends = jnp.cumsum(padded)
    pos = jnp.sum(onehot * (ends - padded)[None, :], axis=1) + rank
    p = r + n_cls * tm if p_rows is None else p_rows
    assert p >= r + n_cls * tm and p % tm == 0
    tile_start = jnp.arange(p // tm, dtype=I32) * tm
    tile_cls = jnp.minimum(jnp.sum((tile_start[:, None] >= ends[None, :]).astype(I32), axis=1), n_cls - 1)
    tile_ok = tile_start < ends[-1]
    tile_valid = jnp.concatenate([tile_ok.astype(I32), (ends[-1:] // tm - 1).astype(I32)])
    pairs = [(i, j) for i in range(epg) for j in range(i + 1, epg)]
    pair_lo = jnp.array([a for a, _ in pairs], I32)
    pair_hi = jnp.array([b for _, b in pairs], I32)
    grp = tile_cls // n_pairs
    tile_expert = jnp.stack([grp * epg + pair_lo[tile_cls % n_pairs],
                             grp * epg + pair_hi[tile_cls % n_pairs]], axis=1)
    tile_expert = jnp.where(tile_ok[:, None], tile_expert, 0).reshape(-1)
    return pos.astype(I32), tile_expert.astype(I32), tile_valid, p


def _moe(fexts, gids, w1, w3, w2, layer, f_sorted=None, tm=512):
    pos, tile_expert, tile_valid, p = _sorted_positions(gids, tm, None if f_sorted is None else f_sorted.shape[0])
    sizes = [f.shape[0] for f in fexts]
    poss, off = [], 0
    for s in sizes:
        poss.append(lax.slice(pos, (off,), (off + s,)))
        off += s
    if f_sorted is None:
        f_sorted = jnp.zeros((p, fexts[0].shape[1]), F32)
    for fext, ps in zip(fexts, poss):
        f_sorted = _dispatch(fext, ps, f_sorted)
    y_sorted = _moe_ffn(tile_expert, tile_valid, f_sorted, w1, w3, w2, layer, tm)
    return y_sorted, poss, f_sorted


def kernel(x, c, ctx, c_ctx, ada_w, ada_b, norm_mix, norm_ffn, even_w_in, even_w_out, hgrn_lb_logits, hgrn_out_norm, diff_qk_norm, diff_lambda, diff_out_norm, odd_w_in, odd_conv_w, odd_conv_b, rg_gate_w, rg_gate_b, rg_lambda, odd_w_out, moe_w_grp, moe_b_grp, moe_w_exp, moe_b_exp, moe_w1, moe_w3, moe_w2):
    bsz, n, d = x.shape
    n_ctx = ctx.shape[1]
    depth = ada_w.shape[0]
    assert depth == 2

    cvec = jnp.zeros((MOD_ROWS, d), F32).at[:bsz].set(c).at[bsz].set(c_ctx)
    mods_all = _ada_all(cvec, ada_w, ada_b).reshape(depth, MOD_ROWS * N_MOD, 1, d)
    x_l = x.reshape(bsz * n, d)
    x_c = ctx.reshape(bsz * n_ctx, d)
    lb_all = jnp.cumsum(jax.nn.softmax(hgrn_lb_logits.astype(F32), axis=1), axis=1)

    def router(l):
        w = jnp.concatenate([moe_w_grp[l], moe_w_exp[l]], axis=1)
        b = jnp.concatenate([moe_b_grp[l], moe_b_exp[l]])
        padw = ROUTE_W - w.shape[1]
        return jnp.pad(w, ((0, 0), (0, padw))), jnp.pad(b, (0, padw)).reshape(1, ROUTE_W)

    l = 0
    mods = mods_all[l]
    lam_init = 0.8 - 0.6 * math.exp(-0.3 * l)
    lv = diff_lambda[0].astype(F32)
    lam = jnp.exp(jnp.sum(lv[0] * lv[1])) - jnp.exp(jnp.sum(lv[2] * lv[3])) + lam_init
    w_in = even_w_in
    proj_l = _norm_mod_mm(x_l, norm_mix[l], mods, w_in, n, 0, tm=1024)
    proj_c = _norm_mod_mm(x_c, norm_mix[l], mods, w_in, bsz * n_ctx, bsz, tm=1024)
    a_l, a_c = _hgrn(proj_l, proj_c, lb_all[0, 0], lb_all[1, 0], hgrn_out_norm[0], bsz)
    b_l, b_c = _attn(proj_l, proj_c, diff_qk_norm[0], diff_out_norm[0], lam, lam_init, bsz)
    w_out = even_w_out[0].astype(BF16)
    wr, br = router(l)
    x_l, f_l, g_l = _out_proj([a_l, b_l], w_out, x_l, norm_ffn[l], mods, wr, br, n, 0)
    x_c, f_c, g_c = _out_proj([a_c, b_c], w_out, x_c, norm_ffn[l], mods, wr, br, bsz * n_ctx, bsz)
    y_sorted, (pos_l, pos_c), f_sorted = _moe([f_l, f_c], [g_l, g_c], moe_w1, moe_w3, moe_w2, l)
    x_l = _combine(y_sorted, pos_l, x_l, mods, n, 0)
    x_c = _combine(y_sorted, pos_c, x_c, mods, bsz * n_ctx, bsz)

    l = 1
    mods = mods_all[l]
    w_in = odd_w_in
    proj_l = _norm_mod_mm(x_l, norm_mix[l], mods, w_in, n, 0, tm=1024)
    rg_w = w_in.shape[2] // 2
    u_c = _norm_mod_mm(x_c, norm_mix[l], mods, w_in, bsz * n_ctx, bsz, tm=1024, cols=(rg_w, rg_w))
    gated = _rglru(proj_l, u_c,odd_conv_w[0], odd_conv_b[0], rg_gate_w[0], rg_gate_b[0], rg_lambda[0], bsz)
    wr, br = router(l)
    x_l, f_l, g_l = _out_proj([gated], odd_w_out[0].astype(BF16), x_l, norm_ffn[l], mods, wr, br, n, 0)
    y_sorted, (pos_l,), _ = _moe([f_l], [g_l], moe_w1, moe_w3, moe_w2, l, f_sorted)
    return _combine(y_sorted, pos_l, x_l, mods, n, 0).reshape(bsz, n, d)
```

```python
import functools
import math

import jax
import jax.numpy as jnp
from jax import lax
from jax.experimental import pallas as pl
from jax.experimental.pallas import tpu as pltpu

F32 = jnp.float32
BF16 = jnp.bfloat16
I32 = jnp.int32

EPS = 1e-6
LOG2_E = 1.4426950408889634
TINY = 1e-30
LANES = 128
SUBLANES = 8
BF16_SUBLANES = 16
GRID_W = 64
A_HEADS = 8
HEAD_W = 128
HGRN_CHUNK = 32
HGRN_BLOCK = 256
B_HEADS = 8
B_DH = 64
ROPE_BASE = 10000.0
ATTN_KEY_CHUNK = 256
ATTN_SKEW = 3
RG_HEADS = 16
CONV_W = 4
RG_C = 8.0
N_GROUPS = 4
EXPERTS_PER_GROUP = 4
N_EXPERTS = 16
N_MOD = 6
MOD_ROWS = 16
ROUTE_W = LANES
ROW_DMA_UNROLL = 8

VMEM_LIMIT = 56 * 1024 * 1024


def _cparams(sem):
    return pltpu.CompilerParams(dimension_semantics=sem, vmem_limit_bytes=VMEM_LIMIT)


def _sigmoid(x):
    return 0.5 * jnp.tanh(0.5 * x) + 0.5


def _dot(a, b):
    return jnp.dot(a, b, preferred_element_type=F32)


def _dot_nt(a, b):
    return lax.dot_general(a, b, (((1,), (1,)), ((), ())), preferred_element_type=F32)


def _dot_tn(a, b):
    return lax.dot_general(a, b, (((0,), (0,)), ((), ())), preferred_element_type=F32)


def _split_bf16(x):
    hi = x.astype(BF16)
    return hi, (x - hi.astype(F32)).astype(BF16)


def _ada_kernel(c_ref, w_ref, b_ref, o_ref):
    c = c_ref[...]
    a = (c * _sigmoid(c)).astype(BF16)
    o_ref[...] = _dot(a, w_ref[...].astype(BF16)) + b_ref[...]


def _ada_all(cvec, ada_w, ada_b, tn=1024):
    depth, d, n = ada_w.shape
    return pl.pallas_call(
        _ada_kernel,
        grid=(depth, n // tn),
        in_specs=[pl.BlockSpec((MOD_ROWS, d), lambda l, j: (0, 0)),
                  pl.BlockSpec((None, d, tn), lambda l, j: (l, 0, j)),
                  pl.BlockSpec((None, 1, tn), lambda l, j: (l, 0, j))],
        out_specs=pl.BlockSpec((None, MOD_ROWS, tn), lambda l, j: (l, 0, j)),
        out_shape=jax.ShapeDtypeStruct((depth, MOD_ROWS, n), F32),
        compiler_params=_cparams(("arbitrary", "arbitrary")),
        name="ada_mod",
    )(cvec, ada_w, ada_b.reshape(depth, 1, n))


def _norm_mod_mm_kernel(x_ref, g_ref, sh_ref, sc_ref, w_ref, o_ref, h_scr):
    @pl.when(pl.program_id(1) == 0)
    def _():
        x = x_ref[...]
        y = x * lax.rsqrt(jnp.mean(x * x, axis=-1, keepdims=True) + EPS) * g_ref[...]
        h_scr[...] = (y * (1.0 + sc_ref[0]) + sh_ref[0]).astype(BF16)

    o_ref[...] = _dot(h_scr[...], w_ref[...].astype(BF16)).astype(o_ref.dtype)


def _norm_mod_mm(x2, gain, mods, w, rows_per_mod, mod_base, tm, tn=1024, cols=None):
    r, d = x2.shape
    col0, n = (0, w.shape[2]) if cols is None else cols
    jb = col0 // tn
    tpm = rows_per_mod // tm

    def mrow(i):
        return (mod_base + i // tpm) * N_MOD

    return pl.pallas_call(
        _norm_mod_mm_kernel,
        grid=(r // tm, n // tn),
        in_specs=[pl.BlockSpec((tm, d), lambda i, j: (i, 0)),
                  pl.BlockSpec((1, d), lambda i, j: (0, 0)),
                  pl.BlockSpec((1, 1, d), lambda i, j: (mrow(i) + 0, 0, 0)),
                  pl.BlockSpec((1, 1, d), lambda i, j: (mrow(i) + 1, 0, 0)),
                  pl.BlockSpec((None, d, tn), lambda i, j: (0, 0, jb + j))],
        out_specs=pl.BlockSpec((tm, tn), lambda i, j: (i, j)),
        out_shape=jax.ShapeDtypeStruct((r, n), F32),
        scratch_shapes=[pltpu.VMEM((tm, d), BF16)],
        compiler_params=_cparams(("parallel", "arbitrary")),
        name="norm_mod_mm",
    )(x2, gain.reshape(1, d), mods, mods, w)


def _hgrn_kernel(ql, ffl, fbl, vl, gl, qc, ffc, fbc, vc, gc, lbf_ref, lbb_ref, gain_ref,
                 ol_ref, oc_ref,
                 oi_l, oi_c, qtf_l, qtb_l, qtf_c, qtb_c, kv_f, kv_b, dec_f, dec_b):
    c_sz = HGRN_CHUNK
    blk = HGRN_BLOCK
    cpb = blk // c_sz
    nb_l = ql.shape[0] // blk
    nc_l = ql.shape[0] // c_sz
    nc_c = qc.shape[0] // c_sz
    nc = nc_l + nc_c
    assert qc.shape[0] == blk
    row = lax.broadcasted_iota(I32, (blk, blk), 0)
    col = lax.broadcasted_iota(I32, (blk, blk), 1)
    same = (row // c_sz) == (col // c_sz)
    tril = same & (row >= col)
    triu = same & (row <= col)
    lbf = lbf_ref[...]
    lbb = lbb_ref[...]

    exp_mask = (lax.broadcasted_iota(I32, (blk, cpb * HEAD_W), 0) // c_sz
                == lax.broadcasted_iota(I32, (blk, cpb * HEAD_W), 1) // HEAD_W)

    def local_terms(r0, q_ref, ff_ref, fb_ref, v_ref, oi_ref, qtf_ref, qtb_ref, gid_f, gid_b):
        rows = pl.ds(r0, blk)
        q = q_ref[rows, :]
        qs = q * _sigmoid(q)
        vb = v_ref[rows, :].astype(BF16)
        vt = v_ref[rows, :].T.astype(BF16)
        two = (0, 1)
        mask = (tril, triu)
        last = (c_sz - 1, 0)
        f = [lb + (1.0 - lb) * _sigmoid(ref[rows, :]) for ref, lb in ((ff_ref, lbf), (fb_ref, lbb))]
        k = [1.0 - f[d] for d in two]
        lf2 = [jnp.concatenate(_split_bf16(jnp.log(f[d])), axis=1) for d in two]
        cum2 = [_dot(jnp.where(mask[d], 1.0, 0.0).astype(BF16), lf2[d]) for d in two]
        cum = [cum2[d][:, :HEAD_W] + cum2[d][:, HEAD_W:] for d in two]
        tot = [jnp.concatenate(
            [jnp.broadcast_to(cum[d][c * c_sz + last[d]:c * c_sz + last[d] + 1, :], (c_sz, HEAD_W))
             for c in range(cpb)], axis=0) for d in two]
        e = [jnp.exp(cum[d]) for d in two]
        qt = [(qs * e[d]).astype(BF16) for d in two]
        kt = [(k[d] * jnp.exp(-cum[d])).astype(BF16) for d in two]
        att = [jnp.where(mask[d], _dot_nt(qt[d], kt[d]), 0.0).astype(BF16) for d in two]
        oi_ref[rows, :] = _dot(att[0], vb) + _dot(att[1], vb)
        k2 = [(k[d] * jnp.exp(tot[d] - cum[d])).astype(BF16) for d in two]
        k2x = [jnp.where(exp_mask, jnp.concatenate([k2[d]] * cpb, axis=1), jnp.zeros((), BF16)) for d in two]
        kvs = [_dot(vt, k2x[d]) for d in two]
        for d, (qt_ref, kv_ref, dec_ref, gid) in enumerate(((qtf_ref, kv_f, dec_f, gid_f),
                                                            (qtb_ref, kv_b, dec_b, gid_b))):
            for c in range(cpb):
                kv_ref[gid + c] = kvs[d][:, c * HEAD_W:(c + 1) * HEAD_W]
                dec_ref[gid + c] = e[d][c * c_sz + last[d]:c * c_sz + last[d] + 1, :]
            qt_ref[rows, :] = qt[d]

    local_terms(0, qc, ffc, fbc, vc, oi_c, qtf_c, qtb_c, 0, nc_l)

    def local_l(i, carry):
        local_terms(pl.multiple_of(i * blk, blk), ql, ffl, fbl, vl, oi_l, qtf_l, qtb_l,
                    nc_c + i * cpb, i * cpb)
        return carry

    lax.fori_loop(0, nb_l, local_l, 0, unroll=True)

    def rec_f(i, s):
        new = s * dec_f[i] + kv_f[i]
        kv_f[i] = s
        return new

    def rec_b(i, s):
        j = nc - 1 - i
        new = s * dec_b[j] + kv_b[j]
        kv_b[j] = s
        return new

    s0 = jnp.zeros((HEAD_W, HEAD_W), F32)
    lax.fori_loop(0, nc, rec_f, s0, unroll=2)
    lax.fori_loop(0, nc, rec_b, s0, unroll=2)

    gain = gain_ref[...]

    def finish(r0, g_ref, oi_ref, qtf_ref, qtb_ref, o_ref, gid_f, gid_b):
        parts = []
        for c in range(cpb):
            rows_c = pl.ds(r0 + c * c_sz, c_sz)
            parts.append(_dot_nt(qtf_ref[rows_c, :], kv_f[gid_f + c].astype(BF16))
                         + _dot_nt(qtb_ref[rows_c, :], kv_b[gid_b + c].astype(BF16)))
        rows = pl.ds(r0, blk)
        o = oi_ref[rows, :] + jnp.concatenate(parts, axis=0)
        y = o * lax.rsqrt(jnp.mean(o * o, axis=-1, keepdims=True) + EPS) * gain
        g = g_ref[rows, :]
        o_ref[rows, :] = (y * (g * _sigmoid(g))).astype(o_ref.dtype)

    finish(0, gc, oi_c, qtf_c, qtb_c, oc_ref, 0, nc_l)

    def fin_l(i, carry):
        finish(pl.multiple_of(i * blk, blk), gl, oi_l, qtf_l, qtb_l, ol_ref, nc_c + i * cpb, i * cpb)
        return carry

    lax.fori_loop(0, nb_l, fin_l, 0, unroll=True)


def _hgrn(proj_l, proj_c, lb_f, lb_b, gain, bsz):
    n = proj_l.shape[0] // bsz
    n_ctx = proj_c.shape[0] // bsz
    nc = (n + n_ctx) // HGRN_CHUNK
    w = A_HEADS * HEAD_W

    def col(k):
        return lambda b, h: (b, k * A_HEADS + h)

    in_specs = ([pl.BlockSpec((n, HEAD_W), col(k)) for k in range(5)]
                + [pl.BlockSpec((n_ctx, HEAD_W), col(k)) for k in range(5)]
                + [pl.BlockSpec((1, HEAD_W), lambda b, h: (0, h)),
                   pl.BlockSpec((1, HEAD_W), lambda b, h: (0, h)),
                   pl.BlockSpec((1, HEAD_W), lambda b, h: (0, 0))])
    return pl.pallas_call(
        _hgrn_kernel,
        grid=(bsz, A_HEADS),
        in_specs=in_specs,
        out_specs=[pl.BlockSpec((n, HEAD_W), lambda b, h: (b, h)),
                   pl.BlockSpec((n_ctx, HEAD_W), lambda b, h: (b, h))],
        out_shape=[jax.ShapeDtypeStruct((bsz * n, w), BF16),
                   jax.ShapeDtypeStruct((bsz * n_ctx, w), BF16)],
        scratch_shapes=[pltpu.VMEM((n, HEAD_W), F32), pltpu.VMEM((n_ctx, HEAD_W), F32),
                        pltpu.VMEM((n, HEAD_W), BF16), pltpu.VMEM((n, HEAD_W), BF16),
                        pltpu.VMEM((n_ctx, HEAD_W), BF16), pltpu.VMEM((n_ctx, HEAD_W), BF16),
                        pltpu.VMEM((nc, HEAD_W, HEAD_W), F32), pltpu.VMEM((nc, HEAD_W, HEAD_W), F32),
                        pltpu.VMEM((nc, 1, HEAD_W), F32), pltpu.VMEM((nc, 1, HEAD_W), F32)],
        compiler_params=_cparams(("parallel", "parallel")),
        name="hgrn2",
    )(*([proj_l] * 5 + [proj_c] * 5), lb_f.reshape(1, w), lb_b.reshape(1, w), gain.reshape(1, HEAD_W))


def _half_mean_matrix():
    r = lax.broadcasted_iota(I32, (LANES, LANES), 0) // B_DH
    c = lax.broadcasted_iota(I32, (LANES, LANES), 1) // B_DH
    return (r == c).astype(BF16)


def _qk_prep(t, gain, bd, cos, sin):
    sq_hi, sq_lo = _split_bf16(t * t)
    ms = (_dot(sq_hi, bd) + _dot(sq_lo, bd)) * (1.0 / B_DH)
    y = t * lax.rsqrt(ms + EPS) * gain
    if cos is None:
        return y
    lane = lax.broadcasted_iota(I32, y.shape, 1)
    partner = jnp.where(lane % 2 == 0, pltpu.roll(y, LANES - 1, 1), pltpu.roll(y, 1, 1))
    return y * cos + partner * sin


def _attn_kernel(q_l, q_c, k_l, k_c, v_l, v_c, cosq, sinq, cosk, sink, gq_ref, gk_ref, go_ref, lam_ref,
                 o_l, o_c, kp, vp, s_even, s_odd, m_even, m_odd, *, out_scale):
    step = pl.program_id(2)
    n_ctx = k_c.shape[0]
    s_bufs = (s_even, s_odd)
    m_bufs = (m_even, m_odd)
    bd = _half_mean_matrix()
    lane = lax.broadcasted_iota(I32, (1, LANES), 1)
    masks = ((lane < B_DH).astype(F32), (lane >= B_DH).astype(F32))
    lam = lam_ref[:, 0:1]
    q_scale = B_DH ** -0.5 * LOG2_E

    def masked_q(q):
        q = q * q_scale
        return [(q * msk).astype(BF16) for msk in masks]

    def epilogue(r):
        o = (r[0][0:HEAD_W, :] * (1.0 / r[0][HEAD_W:HEAD_W + 1, :])
             - r[1][0:HEAD_W, :] * (lam / r[1][HEAD_W:HEAD_W + 1, :]))
        gain = jnp.concatenate([go_ref[...]] * (o.shape[1] // LANES), axis=1)
        y = o * lax.rsqrt(jnp.mean(o * o, axis=0, keepdims=True) + EPS) * gain
        return (y * out_scale).T.astype(BF16)

    @pl.when((step == 0) & (pl.program_id(0) == 0) & (pl.program_id(1) == 0))
    def _():
        s_odd[...] = jnp.zeros_like(s_odd)
        m_odd[...] = jnp.zeros_like(m_odd)

    @pl.when(step == 0)
    def _():
        kp[0:n_ctx, :] = _qk_prep(k_c[...], gk_ref[...], bd, None, None).astype(BF16)
        kp[n_ctx:, :] = _qk_prep(k_l[...], gk_ref[...], bd, cosk[...], sink[...]).astype(BF16)
        vp[0:HEAD_W, 0:n_ctx] = v_c[...].T.astype(BF16)
        vp[0:HEAD_W, n_ctx:] = v_l[...].T.astype(BF16)
        pad_rows = vp.shape[0] - HEAD_W
        ones_row = (lax.broadcasted_iota(I32, (pad_rows, 1), 0) == 0).astype(BF16)
        vp[HEAD_W:, :] = jnp.broadcast_to(ones_row, (pad_rows, vp.shape[1]))
        qm = masked_q(_qk_prep(q_c[...], gq_ref[...], bd, None, None))
        s_ctx = [_dot_nt(kp[0:n_ctx, :], t) for t in qm]
        e_ctx = [jnp.exp2(t - jnp.max(t, axis=0, keepdims=True)).astype(BF16) for t in s_ctx]
        o_c[...] = epilogue([_dot(vp[:, 0:n_ctx], t) for t in e_ctx])

    kc = ATTN_KEY_CHUNK
    for parity in (0, 1):
        @pl.when((step > 0) & (step % 2 == parity))
        def _(old=parity, new=1 - parity):
            qm = masked_q(_qk_prep(q_l[...], gq_ref[...], bd, cosq[...], sinq[...]))
            mx_old = [m_bufs[old][i, 0:1, :] for i in range(2)]
            acc = [None, None]
            mx_new = [None, None]
            n_chunks = kp.shape[0] // kc
            for c in range(n_chunks + ATTN_SKEW):
                if c < n_chunks:
                    ks = slice(c * kc, (c + 1) * kc)
                    for i in range(2):
                        e = jnp.exp2(s_bufs[old][i, ks, :] - mx_old[i]).astype(BF16)
                        part = _dot(vp[:, ks], e)
                        acc[i] = part if acc[i] is None else acc[i] + part
                if c == n_chunks - 1:
                    o_l[...] = epilogue(acc)
                if c >= ATTN_SKEW:
                    ks = slice((c - ATTN_SKEW) * kc, (c - ATTN_SKEW + 1) * kc)
                    for i in range(2):
                        t = _dot_nt(kp[ks, :], qm[i])
                        s_bufs[new][i, ks, :] = t
                        tm = jnp.max(t, axis=0, keepdims=True)
                        mx_new[i] = tm if mx_new[i] is None else jnp.maximum(mx_new[i], tm)
            for i in range(2):
                m_bufs[new][i] = jnp.broadcast_to(mx_new[i], m_bufs[new].shape[1:])


def _rope_tables(n):
    n_rows = n // GRID_W
    rowp = jnp.repeat(jnp.arange(n_rows), GRID_W).astype(F32)
    colp = jnp.tile(jnp.arange(GRID_W), n_rows).astype(F32)
    pairs = B_DH // 4
    inv = ROPE_BASE ** (-jnp.arange(pairs, dtype=F32) / pairs)
    ang = jnp.concatenate([rowp[:, None] * inv, colp[:, None] * inv], axis=-1)
    cos = jnp.repeat(jnp.cos(ang), 2, axis=-1)
    sin = jnp.repeat(jnp.sin(ang), 2, axis=-1) * jnp.tile(jnp.array([-1.0, 1.0], F32), B_DH // 2)
    return jnp.tile(cos, (1, 2)), jnp.tile(sin, (1, 2))


def _attn(proj_l, proj_c, qk_gain, out_gain, lam, lam_init, bsz, tq=512):
    n = proj_l.shape[0] // bsz
    n_ctx = proj_c.shape[0] // bsz
    nqb = n // tq
    w = B_HEADS * HEAD_W
    cos, sin = _rope_tables(n)
    qcol, kcol, vcol = 5 * A_HEADS, 5 * A_HEADS + B_HEADS, 5 * A_HEADS + 2 * B_HEADS

    def q_blk(t):
        return jnp.clip(t - 1, 0, nqb - 1)

    def o_blk(t):
        return jnp.clip(t - 2, 0, nqb - 1)

    def lat_q(b, h, t):
        return (b * nqb + q_blk(t), qcol + h)

    in_specs = [pl.BlockSpec((tq, HEAD_W), lat_q),
                pl.BlockSpec((n_ctx, HEAD_W), lambda b, h, qb: (b, qcol + h)),
                pl.BlockSpec((n, HEAD_W), lambda b, h, qb: (b, kcol + h)),
                pl.BlockSpec((n_ctx, HEAD_W), lambda b, h, qb: (b, kcol + h)),
                pl.BlockSpec((n, HEAD_W), lambda b, h, qb: (b, vcol + h)),
                pl.BlockSpec((n_ctx, HEAD_W), lambda b, h, qb: (b, vcol + h)),
                pl.BlockSpec((tq, LANES), lambda b, h, t: (q_blk(t), 0)),
                pl.BlockSpec((tq, LANES), lambda b, h, t: (q_blk(t), 0)),
                pl.BlockSpec((n, LANES), lambda b, h, qb: (0, 0)),
                pl.BlockSpec((n, LANES), lambda b, h, qb: (0, 0)),
                pl.BlockSpec((1, LANES), lambda b, h, qb: (0, 0)),
                pl.BlockSpec((1, LANES), lambda b, h, qb: (0, 0)),
                pl.BlockSpec((HEAD_W, LANES), lambda b, h, qb: (0, 0)),
                pl.BlockSpec((1, LANES), lambda b, h, qb: (0, 0))]
    return pl.pallas_call(
        functools.partial(_attn_kernel, out_scale=1.0 - lam_init),
        grid=(bsz, B_HEADS, nqb + 2),
        in_specs=in_specs,
        out_specs=[pl.BlockSpec((tq, HEAD_W), lambda b, h, t: (b * nqb + o_blk(t), h)),
                   pl.BlockSpec((n_ctx, HEAD_W), lambda b, h, qb: (b, h))],
        out_shape=[jax.ShapeDtypeStruct((bsz * n, w), BF16),
                   jax.ShapeDtypeStruct((bsz * n_ctx, w), BF16)],
        scratch_shapes=[pltpu.VMEM((n + n_ctx, HEAD_W), BF16),
                        pltpu.VMEM((HEAD_W + BF16_SUBLANES, n + n_ctx), BF16),
                        pltpu.VMEM((2, n + n_ctx, tq), F32), pltpu.VMEM((2, n + n_ctx, tq), F32),
                        pltpu.VMEM((2, SUBLANES, tq), F32), pltpu.VMEM((2, SUBLANES, tq), F32)],
        compiler_params=_cparams(("arbitrary", "arbitrary", "arbitrary")),
        name="diff_attn",
    )(proj_l, proj_c, proj_l, proj_c, proj_l, proj_c, cos, sin, cos, sin,
      jnp.tile(qk_gain[0], 2).reshape(1, LANES), jnp.tile(qk_gain[1], 2).reshape(1, LANES),
      jnp.broadcast_to(out_gain[:, None], (HEAD_W, LANES)), jnp.full((1, LANES), lam, F32))


def _scan_steps(a, b, reverse, axis):
    n = a.shape[axis]
    pos = lax.broadcasted_iota(I32, a.shape, axis)
    s = 1
    while s < n:
        keep = (pos < n - s) if reverse else (pos >= s)
        shift = n - s if reverse else s
        a_sh = jnp.where(keep, pltpu.roll(a, shift, axis), 1.0)
        b_sh = jnp.where(keep, pltpu.roll(b, shift, axis), 0.0)
        b = a * b_sh + b
        a = a * a_sh
        s *= 2
    return a, b


def _rglru_kernel(y_l, u_l, u_c, cw_ref, cb_ref, gw_ref, gb_ref, lam_ref, o_ref,
                  upad, a_f, b_f, a_b, b_b, h_f, p_f, h_b, p_b, *, seg_len, pitch):
    n = u_l.shape[0]
    n_ctx = u_c.shape[0]
    nseg = SUBLANES
    cw = cw_ref[...]
    cb = cb_ref[...]
    pad = SUBLANES

    def pieces(start, rows):
        out = []
        for r in range(nseg):
            lo, hi = max(start, r * seg_len), min(start + rows, (r + 1) * seg_len)
            if lo < hi:
                out.append((r, lo - r * seg_len, lo - start, hi - lo))
        return out

    def put(ref, start, val):
        for r, off, src, ln in pieces(start, val.shape[0]):
            ref[r * pitch + off:r * pitch + off + ln, :] = val[src:src + ln, :]

    def conv(u_ref, rows):
        upad[0:pad, :] = jnp.zeros((pad, LANES), F32)
        upad[pad:pad + rows, :] = u_ref[...]
        upad[pad + rows:pad + rows + pad, :] = jnp.zeros((pad, LANES), F32)
        acc = cb + jnp.zeros((rows, LANES), F32)
        for j in range(CONV_W):
            off = pad + j - CONV_W // 2
            acc = acc + cw[j:j + 1, :] * upad[off:off + rows, :]
        return acc

    def gates(uc, start_f, start_b):
        ub = uc.astype(BF16)
        for d, (a_ref, b_ref, start) in enumerate(((a_f, b_f, start_f), (a_b, b_b, start_b))):
            lam = lam_ref[d:d + 1, :]
            neg_sp = -(jnp.maximum(-lam, 0.0) + jnp.log(1.0 + jnp.exp(-jnp.abs(lam))))
            r = _sigmoid(_dot(ub, gw_ref[d, 0].astype(BF16)) + gb_ref[d, 0:1, :])
            i = _sigmoid(_dot(ub, gw_ref[d, 1].astype(BF16)) + gb_ref[d, 1:2, :])
            a = jnp.exp((RG_C * neg_sp) * r)
            put(a_ref, start, a)
            x = (1.0 - a) * (1.0 + a)
            put(b_ref, start, (x * lax.rsqrt(jnp.maximum(x, TINY))) * (i * uc))

    gates(conv(u_c, n_ctx), 0, n)
    gates(conv(u_l, n), n_ctx, 0)

    def step(i, carry):
        hf, pf, hb, pb = carry
        rows_f = pl.ds(i, nseg, stride=pitch)
        rows_b = pl.ds(seg_len - 1 - i, nseg, stride=pitch)
        af, ab = a_f[rows_f, :], a_b[rows_b, :]
        hf = af * hf + b_f[rows_f, :]
        hb = ab * hb + b_b[rows_b, :]
        pf = pf * af
        pb = pb * ab
        h_f[rows_f, :] = hf
        p_f[rows_f, :] = pf
        h_b[rows_b, :] = hb
        p_b[rows_b, :] = pb
        return hf, pf, hb, pb

    zero = jnp.zeros((nseg, LANES), F32)
    one = jnp.ones((nseg, LANES), F32)
    hf, pf, hb, pb = lax.fori_loop(0, seg_len, step, (zero, one, zero, one), unroll=8)

    seg = lax.broadcasted_iota(I32, (nseg, LANES), 0)
    _, ef = _scan_steps(pf, hf, False, 0)
    _, eb = _scan_steps(pb, hb, True, 0)
    carry_f = jnp.where(seg >= 1, pltpu.roll(ef, 1, 0), 0.0)
    carry_b = jnp.where(seg < nseg - 1, pltpu.roll(eb, nseg - 1, 0), 0.0)

    cuts = sorted({0, n} | {k * seg_len for k in range(nseg + 1) if 0 < k * seg_len < n}
                  | {k * seg_len - n_ctx for k in range(nseg + 1) if 0 < k * seg_len - n_ctx < n})
    for i0, i1 in zip(cuts[:-1], cuts[1:]):
        ln = i1 - i0
        (rf, of, _, _), = pieces(n_ctx + i0, ln)
        (rb, ob, _, _), = pieces(i0, ln)
        sf = slice(rf * pitch + of, rf * pitch + of + ln)
        sb = slice(rb * pitch + ob, rb * pitch + ob + ln)
        h = (h_f[sf, :] + p_f[sf, :] * carry_f[rf:rf + 1, :]) + (h_b[sb, :] + p_b[sb, :] * carry_b[rb:rb + 1, :])
        y = y_l[i0:i1, :]
        gelu = 0.5 * y * (1.0 + jnp.tanh(0.7978845608028654 * (y + 0.044715 * y * y * y)))
        o_ref[i0:i1, :] = (h * gelu).astype(o_ref.dtype)


def _rglru(proj_l, u_c, conv_w, conv_b, gate_w, gate_b, lam, bsz):
    n = proj_l.shape[0] // bsz
    n_ctx = u_c.shape[0] // bsz
    w = RG_HEADS * HEAD_W
    seg_len = (n + n_ctx) // SUBLANES
    assert seg_len * SUBLANES == n + n_ctx and seg_len % SUBLANES == 0
    assert n_ctx % SUBLANES == 0 and n_ctx <= seg_len
    pitch = seg_len + SUBLANES
    return pl.pallas_call(
        functools.partial(_rglru_kernel, seg_len=seg_len, pitch=pitch),
        grid=(bsz, RG_HEADS),
        in_specs=[pl.BlockSpec((n, HEAD_W), lambda b, h: (b, h)),
                  pl.BlockSpec((n, HEAD_W), lambda b, h: (b, RG_HEADS + h)),
                  pl.BlockSpec((n_ctx, HEAD_W), lambda b, h: (b, h)),
                  pl.BlockSpec((CONV_W, HEAD_W), lambda b, h: (0, h)),
                  pl.BlockSpec((1, HEAD_W), lambda b, h: (0, h)),
                  pl.BlockSpec((2, 2, None, HEAD_W, HEAD_W), lambda b, h: (0, 0, h, 0, 0)),
                  pl.BlockSpec((2, 2, HEAD_W), lambda b, h: (0, 0, h)),
                  pl.BlockSpec((2, HEAD_W), lambda b, h: (0, h))],
        out_specs=pl.BlockSpec((n, HEAD_W), lambda b, h: (b, h)),
        out_shape=jax.ShapeDtypeStruct((bsz * n, w), BF16),
        scratch_shapes=[pltpu.VMEM((n + 2 * SUBLANES, LANES), F32)]
                       + [pltpu.VMEM((SUBLANES * pitch, LANES), F32)] * 8,
        compiler_params=_cparams(("parallel", "parallel")),
        name="rglru",
    )(proj_l, proj_l, u_c, conv_w, conv_b.reshape(1, w), gate_w, gate_b, lam)


def _out_proj_kernel(*refs, n_mix, n_groups, epg, row_chunk):
    mix_refs = refs[:n_mix]
    (w_ref, x_ref, gate_ref, g_ref, sh_ref, sc_ref, wr_ref, rb_ref,
     xo_ref, f_ref, gid_ref) = refs[n_mix:]
    d = x_ref.shape[1]
    n_pairs = epg * (epg - 1) // 2
    for r0 in range(0, x_ref.shape[0], row_chunk):
        rows = slice(r0, r0 + row_chunk)
        k0 = 0
        mix = None
        for m_ref in mix_refs:
            kk = m_ref.shape[1]
            part = _dot(m_ref[rows, :], w_ref[k0:k0 + kk, :])
            mix = part if mix is None else mix + part
            k0 += kk
        x = x_ref[rows, :] + gate_ref[0] * mix
        xo_ref[rows, :] = x
        y = x * lax.rsqrt(jnp.mean(x * x, axis=-1, keepdims=True) + EPS) * g_ref[...]
        f = y * (1.0 + sc_ref[0]) + sh_ref[0]
        f_ref[rows, 0:d] = f

        f_hi, f_lo = _split_bf16(f)
        hh = _dot(f_hi, wr_ref[...])
        lg = hh[:, 0:ROUTE_W] + hh[:, ROUTE_W:] + _dot(f_lo, wr_ref[:, 0:ROUTE_W]) + rb_ref[...]
        lane = lax.broadcasted_iota(I32, lg.shape, 1).astype(F32)
        neg = -jnp.inf
        big = float(ROUTE_W)
        gl = jnp.where(lane < n_groups, lg, neg)
        gmax = jnp.max(gl, axis=-1, keepdims=True)
        gidx = jnp.min(jnp.where(gl == gmax, lane, big), axis=-1, keepdims=True)
        gw = 1.0 / jnp.sum(jnp.exp(gl - gmax), axis=-1, keepdims=True)
        base = n_groups + gidx * epg
        el = jnp.where((lane >= base) & (lane < base + epg), lg, neg)
        v1 = jnp.max(el, axis=-1, keepdims=True)
        i1 = jnp.min(jnp.where(el == v1, lane, big), axis=-1, keepdims=True)
        el2 = jnp.where(lane == i1, neg, el)
        v2 = jnp.max(el2, axis=-1, keepdims=True)
        i2 = jnp.min(jnp.where(el2 == v2, lane, big), axis=-1, keepdims=True)
        t = jnp.exp(v2 - v1)
        w1 = gw / (1.0 + t)
        w2 = w1 * t
        s1 = i1 - base
        s2 = i2 - base
        lo = jnp.minimum(s1, s2)
        hi = jnp.maximum(s1, s2)
        pair = lo * (2 * epg - 1 - lo) * 0.5 + (hi - lo - 1.0)
        w_lo = jnp.where(s1 < s2, w1, w2)
        w_hi = jnp.where(s1 < s2, w2, w1)
        f_ref[rows, d:] = jnp.where(lane == 0.0, w_lo, 0.0) + jnp.where(lane == 1.0, w_hi, 0.0)
        cls = jnp.broadcast_to(gidx * n_pairs + pair, (row_chunk, LANES)).T
        gid_ref[:, rows] = cls[0:SUBLANES, :].astype(I32)


def _out_proj(mixes, w_out, x2, gain, mods, w_router, b_router, rows_per_mod, mod_base, tm=512):
    r, d = x2.shape
    k = w_out.shape[0]
    tpm = rows_per_mod // tm
    wr = jnp.concatenate(_split_bf16(w_router), axis=1)

    def mrow(i):
        return (mod_base + i // tpm) * N_MOD

    const2 = lambda i: (0, 0)
    in_specs = ([pl.BlockSpec((tm, m.shape[1]), lambda i: (i, 0)) for m in mixes]
                + [pl.BlockSpec((k, d), const2),
                   pl.BlockSpec((tm, d), lambda i: (i, 0)),
                   pl.BlockSpec((1, 1, d), lambda i: (mrow(i) + 2, 0, 0)),
                   pl.BlockSpec((1, d), const2),
                   pl.BlockSpec((1, 1, d), lambda i: (mrow(i) + 3, 0, 0)),
                   pl.BlockSpec((1, 1, d), lambda i: (mrow(i) + 4, 0, 0)),
                   pl.BlockSpec((d, 2 * ROUTE_W), const2),
                   pl.BlockSpec((1, ROUTE_W), const2)])
    x_new, fext, cls = pl.pallas_call(
        functools.partial(_out_proj_kernel, n_mix=len(mixes), n_groups=N_GROUPS, epg=EXPERTS_PER_GROUP,
                          row_chunk=tm),
        grid=(r // tm,),
        in_specs=in_specs,
        out_specs=[pl.BlockSpec((tm, d), lambda i: (i, 0)),
                   pl.BlockSpec((tm, d + ROUTE_W), lambda i: (i, 0)),
                   pl.BlockSpec((SUBLANES, tm), lambda i: (i, 0))],
        out_shape=[jax.ShapeDtypeStruct((r, d), F32),
                   jax.ShapeDtypeStruct((r, d + ROUTE_W), F32),
                   jax.ShapeDtypeStruct((r // tm * SUBLANES, tm), I32)],
        compiler_params=_cparams(("parallel",)),
        name="out_proj",
    )(*mixes, w_out, x2, mods, gain.reshape(1, d), mods, mods, wr, b_router)
    return x_new, fext, cls.reshape(r // tm, SUBLANES, tm)[:, 0, :].reshape(r)


def _row_copy(src, s, dst, t, sem):
    return pltpu.make_async_copy(src.at[pl.ds(s, 1), :], dst.at[pl.ds(t, 1), :], sem)


def _dispatch_kernel(pos_ref, src_ref, dst_in, dst, sem, *, tm):
    del dst_in

    def body(g, carry):
        for k in range(ROW_DMA_UNROLL):
            j = g * ROW_DMA_UNROLL + k
            _row_copy(src_ref, j, dst, pos_ref[0, 0, j], sem).start(priority=k % 2)
        return carry
    lax.fori_loop(0, tm // ROW_DMA_UNROLL, body, 0)

    def wbody(j, carry):
        _row_copy(src_ref, 0, dst, 0, sem).wait()
        return carry
    lax.fori_loop(0, tm, wbody, 0, unroll=True)


def _dispatch(src, pos, sorted_in, tm=512):
    r, w = src.shape
    return pl.pallas_call(
        functools.partial(_dispatch_kernel, tm=tm),
        grid=(r // tm,),
        in_specs=[pl.BlockSpec((1, 1, tm), lambda i: (i, 0, 0), memory_space=pltpu.SMEM),
                  pl.BlockSpec((tm, w), lambda i: (i, 0)),
                  pl.BlockSpec(memory_space=pl.ANY)],
        out_specs=pl.BlockSpec(memory_space=pl.ANY),
        out_shape=jax.ShapeDtypeStruct(sorted_in.shape, sorted_in.dtype),
        scratch_shapes=[pltpu.SemaphoreType.DMA(())],
        input_output_aliases={2: 0},
        compiler_params=_cparams(("arbitrary",)),
        name="moe_dispatch",
    )(pos.reshape(r // tm, 1, tm), src, sorted_in)


def _pair_slot(t, s):
    return jnp.bitwise_xor(s, jnp.bitwise_and(t, 1))


def _moe_kernel(te_ref, tv_ref, x_ref, w1_ref, w3_ref, w2_ref, o_ref):
    t = pl.program_id(0)
    s = pl.program_id(1)
    d = o_ref.shape[1]
    valid = tv_ref[t] > 0
    slot = _pair_slot(t, s)

    @pl.when(valid)
    def _():
        x = x_ref[:, 0:d].astype(BF16)
        h1 = _dot(x, w1_ref[...].astype(BF16))
        h3 = _dot(x, w3_ref[...].astype(BF16))
        cw = x_ref[:, d:]
        lane = lax.broadcasted_iota(I32, cw.shape, 1)
        cws = jnp.sum(jnp.where(lane == slot, cw, 0.0), axis=-1, keepdims=True)
        y = _dot((h1 * _sigmoid(h1) * h3 * cws).astype(BF16), w2_ref[...].astype(BF16))

        @pl.when(s == 0)
        def _():
            o_ref[...] = y

        @pl.when(s > 0)
        def _():
            o_ref[...] += y

    @pl.when(jnp.logical_not(valid) & (s == 0))
    def _():
        o_ref[...] = jnp.zeros_like(o_ref)


def _moe_ffn(tile_expert, tile_valid, f_sorted, w1, w3, w2, layer, tm):
    p, dw = f_sorted.shape
    d = dw - ROUTE_W
    fe = w1.shape[3]

    def wmap(t, s, te, tv):
        return (layer, te[2 * t + _pair_slot(t, s)], 0, 0)

    grid_spec = pltpu.PrefetchScalarGridSpec(
        num_scalar_prefetch=2,
        grid=(p // tm, 2),
        in_specs=[pl.BlockSpec((tm, dw), lambda t, s, te, tv: (jnp.minimum(t, tv[tv.shape[0] - 1]), 0)),
                  pl.BlockSpec((None, None, d, fe), wmap),
                  pl.BlockSpec((None, None, d, fe), wmap),
                  pl.BlockSpec((None, None, fe, d), wmap)],
        out_specs=pl.BlockSpec((tm, d), lambda t, s, te, tv: (t, 0)),
    )
    return pl.pallas_call(
        _moe_kernel,
        grid_spec=grid_spec,
        out_shape=jax.ShapeDtypeStruct((p, d), F32),
        compiler_params=_cparams(("arbitrary", "arbitrary")),
        name="moe_ffn",
    )(tile_expert, tile_valid, f_sorted, w1, w3, w2)


def _combine_kernel(pos_cur, pos_nxt, y_hbm, x_ref, gate_ref, o_ref, ybuf, sem, *, tm):
    i = pl.program_id(0)
    n_steps = pl.num_programs(0)

    def issue(pos_ref, slot):
        def body(g, carry):
            for k in range(ROW_DMA_UNROLL):
                j = g * ROW_DMA_UNROLL + k
                _row_copy(y_hbm, pos_ref[0, 0, j], ybuf.at[slot], j, sem.at[slot]).start(priority=k % 2)
            return carry
        lax.fori_loop(0, tm // ROW_DMA_UNROLL, body, 0)

    @pl.when(i == 0)
    def _():
        issue(pos_cur, 0)

    @pl.when(i + 1 < n_steps)
    def _():
        issue(pos_nxt, (i + 1) % 2)

    slot = i % 2

    def wbody(j, carry):
        _row_copy(y_hbm, 0, ybuf.at[slot], 0, sem.at[slot]).wait()
        return carry
    lax.fori_loop(0, tm, wbody, 0, unroll=True)
    o_ref[...] = x_ref[...] + gate_ref[0] * ybuf[slot]


def _combine(y_sorted, pos, x2, mods, rows_per_mod, mod_base, tm=256):
    r, d = x2.shape
    n_steps = r // tm
    tpm = rows_per_mod // tm
    pos3 = pos.reshape(n_steps, 1, tm)
    return pl.pallas_call(
        functools.partial(_combine_kernel, tm=tm),
        grid=(n_steps,),
        in_specs=[pl.BlockSpec((1, 1, tm), lambda i: (i, 0, 0), memory_space=pltpu.SMEM),
                  pl.BlockSpec((1, 1, tm), lambda i: (jnp.minimum(i + 1, n_steps - 1), 0, 0),
                               memory_space=pltpu.SMEM),
                  pl.BlockSpec(memory_space=pl.ANY),
                  pl.BlockSpec((tm, d), lambda i: (i, 0)),
                  pl.BlockSpec((1, 1, d), lambda i: ((mod_base + i // tpm) * N_MOD + 5, 0, 0))],
        out_specs=pl.BlockSpec((tm, d), lambda i: (i, 0)),
        out_shape=jax.ShapeDtypeStruct((r, d), F32),
        scratch_shapes=[pltpu.VMEM((2, tm, d), F32), pltpu.SemaphoreType.DMA((2,))],
        compiler_params=_cparams(("arbitrary",)),
        name="moe_combine",
    )(pos3, pos3, y_sorted, x2, mods)


def _sorted_positions(gids, tm, p_rows):
    g = jnp.concatenate(gids)
    r = g.shape[0]
    epg = EXPERTS_PER_GROUP
    n_pairs = epg * (epg - 1) // 2
    n_cls = N_GROUPS * n_pairs
    onehot = (g[:, None] == jnp.arange(n_cls, dtype=I32)[None, :]).astype(I32)
    counts = jnp.sum(onehot, axis=0)
    rank = jnp.sum((jnp.cumsum(onehot, axis=0) - 1) * onehot, axis=1)
    padded = ((counts + tm - 1) // tm) * tm
    ends = jnp.cumsum(padded)
    pos = jnp.sum(onehot * (ends - padded)[None, :], axis=1) + rank
    p = r + n_cls * tm if p_rows is None else p_rows
    assert p >= r + n_cls * tm and p % tm == 0
    tile_start = jnp.arange(p // tm, dtype=I32) * tm
    tile_cls = jnp.minimum(jnp.sum((tile_start[:, None] >= ends[None, :]).astype(I32), axis=1), n_cls - 1)
    tile_ok = tile_start < ends[-1]
    tile_valid = jnp.concatenate([tile_ok.astype(I32), (ends[-1:] // tm - 1).astype(I32)])
    pairs = [(i, j) for i in range(epg) for j in range(i + 1, epg)]
    pair_lo = jnp.array([a for a, _ in pairs], I32)
    pair_hi = jnp.array([b for _, b in pairs], I32)
    grp = tile_cls // n_pairs
    tile_expert = jnp.stack([grp * epg + pair_lo[tile_cls % n_pairs],
                             grp * epg + pair_hi[tile_cls % n_pairs]], axis=1)
    tile_expert = jnp.where(tile_ok[:, None], tile_expert, 0).reshape(-1)
    return pos.astype(I32), tile_expert.astype(I32), tile_valid, p


def _moe(fexts, gids, w1, w3, w2, layer, f_sorted=None, tm=512):
    pos, tile_expert, tile_valid, p = _sorted_positions(gids, tm, None if f_sorted is None else f_sorted.shape[0])
    sizes = [f.shape[0] for f in fexts]
    poss, off = [], 0
    for s in sizes:
        poss.append(lax.slice(pos, (off,), (off + s,)))
        off += s
    if f_sorted is None:
        f_sorted = jnp.zeros((p, fexts[0].shape[1]), F32)
    for fext, ps in zip(fexts, poss):
        f_sorted = _dispatch(fext, ps, f_sorted)
    y_sorted = _moe_ffn(tile_expert, tile_valid, f_sorted, w1, w3, w2, layer, tm)
    return y_sorted, poss, f_sorted


def kernel(x, c, ctx, c_ctx, ada_w, ada_b, norm_mix, norm_ffn, even_w_in, even_w_out, hgrn_lb_logits, hgrn_out_norm, diff_qk_norm, diff_lambda, diff_out_norm, odd_w_in, odd_conv_w, odd_conv_b, rg_gate_w, rg_gate_b, rg_lambda, odd_w_out, moe_w_grp, moe_b_grp, moe_w_exp, moe_b_exp, moe_w1, moe_w3, moe_w2):
    bsz, n, d = x.shape
    n_ctx = ctx.shape[1]
    depth = ada_w.shape[0]
    assert depth == 2

    cvec = jnp.zeros((MOD_ROWS, d), F32).at[:bsz].set(c).at[bsz].set(c_ctx)
    mods_all = _ada_all(cvec, ada_w, ada_b).reshape(depth, MOD_ROWS * N_MOD, 1, d)
    x_l = x.reshape(bsz * n, d)
    x_c = ctx.reshape(bsz * n_ctx, d)
    lb_all = jnp.cumsum(jax.nn.softmax(hgrn_lb_logits.astype(F32), axis=1), axis=1)

    def router(l):
        w = jnp.concatenate([moe_w_grp[l], moe_w_exp[l]], axis=1)
        b = jnp.concatenate([moe_b_grp[l], moe_b_exp[l]])
        padw = ROUTE_W - w.shape[1]
        return jnp.pad(w, ((0, 0), (0, padw))), jnp.pad(b, (0, padw)).reshape(1, ROUTE_W)

    l = 0
    mods = mods_all[l]
    lam_init = 0.8 - 0.6 * math.exp(-0.3 * l)
    lv = diff_lambda[0].astype(F32)
    lam = jnp.exp(jnp.sum(lv[0] * lv[1])) - jnp.exp(jnp.sum(lv[2] * lv[3])) + lam_init
    w_in = even_w_in
    proj_l = _norm_mod_mm(x_l, norm_mix[l], mods, w_in, n, 0, tm=1024)
    proj_c = _norm_mod_mm(x_c, norm_mix[l], mods, w_in, bsz * n_ctx, bsz, tm=1024)
    a_l, a_c = _hgrn(proj_l, proj_c, lb_all[0, 0], lb_all[1, 0], hgrn_out_norm[0], bsz)
    b_l, b_c = _attn(proj_l, proj_c, diff_qk_norm[0], diff_out_norm[0], lam, lam_init, bsz)
    w_out = even_w_out[0].astype(BF16)
    wr, br = router(l)
    x_l, f_l, g_l = _out_proj([a_l, b_l], w_out, x_l, norm_ffn[l], mods, wr, br, n, 0)
    x_c, f_c, g_c = _out_proj([a_c, b_c], w_out, x_c, norm_ffn[l], mods, wr, br, bsz * n_ctx, bsz)
    y_sorted, (pos_l, pos_c), f_sorted = _moe([f_l, f_c], [g_l, g_c], moe_w1, moe_w3, moe_w2, l)
    x_l = _combine(y_sorted, pos_l, x_l, mods, n, 0)
    x_c = _combine(y_sorted, pos_c, x_c, mods, bsz * n_ctx, bsz)

    l = 1
    mods = mods_all[l]
    w_in = odd_w_in
    proj_l = _norm_mod_mm(x_l, norm_mix[l], mods, w_in, n, 0, tm=1024)
    rg_w = w_in.shape[2] // 2
    u_c = _norm_mod_mm(x_c, norm_mix[l], mods, w_in, bsz * n_ctx, bsz, tm=1024, cols=(rg_w, rg_w))
    gated = _rglru(proj_l, u_c,odd_conv_w[0], odd_conv_b[0], rg_gate_w[0], rg_gate_b[0], rg_lambda[0], bsz)
    wr, br = router(l)
    x_l, f_l, g_l = _out_proj([gated], odd_w_out[0].astype(BF16), x_l, norm_ffn[l], mods, wr, br, n, 0)
    y_sorted, (pos_l,), _ = _moe([f_l], [g_l], moe_w1, moe_w3, moe_w2, l, f_sorted)
    return _combine(y_sorted, pos_l, x_l, mods, n, 0).reshape(bsz, n, d)
```

```python
import functools
import math

import jax
import jax.numpy as jnp
from jax import lax
from jax.experimental import pallas as pl
from jax.experimental.pallas import tpu as pltpu

F32 = jnp.float32
BF16 = jnp.bfloat16
I32 = jnp.int32

EPS = 1e-6
LOG2_E = 1.4426950408889634
TINY = 1e-30
LANES = 128
SUBLANES = 8
BF16_SUBLANES = 16
GRID_W = 64
A_HEADS = 8
HEAD_W = 128
HGRN_CHUNK = 32
HGRN_BLOCK = 256
B_HEADS = 8
B_DH = 64
ROPE_BASE = 10000.0
ATTN_KEY_CHUNK = 256
ATTN_SKEW = 3
RG_HEADS = 16
CONV_W = 4
RG_C = 8.0
N_GROUPS = 4
EXPERTS_PER_GROUP = 4
N_MOD = 6
MOD_ROWS = 16
ROUTE_W = LANES
ROW_DMA_UNROLL = 8

V7X_VMEM_BYTES = 64 * 1024 * 1024
VMEM_LIMIT = V7X_VMEM_BYTES * 7 // 8


def _cparams(sem):
    return pltpu.CompilerParams(dimension_semantics=sem, vmem_limit_bytes=VMEM_LIMIT)


def _sigmoid(x):
    return 0.5 * jnp.tanh(0.5 * x) + 0.5


def _dot(a, b):
    return jnp.dot(a, b, preferred_element_type=F32)


def _dot_nt(a, b):
    return lax.dot_general(a, b, (((1,), (1,)), ((), ())), preferred_element_type=F32)


def _split_bf16(x):
    hi = x.astype(BF16)
    return hi, (x - hi.astype(F32)).astype(BF16)


def _pack_bf16_pairs(v):
    h = v.shape[1] // 2
    vb = v.astype(BF16)
    hi = pltpu.bitcast(vb[:, 0:h].astype(F32), jnp.uint32)
    lo = pltpu.bitcast(vb[:, h:].astype(F32), jnp.uint32)
    return pltpu.bitcast(hi | (lo >> 16), F32)


def _unpack_bf16_pairs(w):
    words = pltpu.bitcast(w, jnp.uint32)
    return jnp.concatenate([pltpu.bitcast(words & jnp.uint32(0xFFFF0000), F32),
                            pltpu.bitcast(words << 16, F32)], axis=1)


def _ada_kernel(c_ref, w_ref, b_ref, o_ref):
    c = c_ref[...]
    a = (c * _sigmoid(c)).astype(BF16)
    o_ref[...] = _dot(a, w_ref[...].astype(BF16)) + b_ref[...]


def _ada_all(cvec, ada_w, ada_b, tn=1024):
    depth, d, n = ada_w.shape
    return pl.pallas_call(
        _ada_kernel,
        grid=(depth, n // tn),
        in_specs=[pl.BlockSpec((MOD_ROWS, d), lambda l, j: (0, 0)),
                  pl.BlockSpec((None, d, tn), lambda l, j: (l, 0, j)),
                  pl.BlockSpec((None, 1, tn), lambda l, j: (l, 0, j))],
        out_specs=pl.BlockSpec((None, MOD_ROWS, tn), lambda l, j: (l, 0, j)),
        out_shape=jax.ShapeDtypeStruct((depth, MOD_ROWS, n), F32),
        compiler_params=_cparams(("arbitrary", "arbitrary")),
        name="ada_mod",
    )(cvec, ada_w, ada_b.reshape(depth, 1, n))


def _norm_mod_mm_kernel(x_ref, g_ref, sh_ref, sc_ref, w_ref, o_ref, h_scr):
    @pl.when(pl.program_id(1) == 0)
    def _():
        x = x_ref[...]
        y = x * lax.rsqrt(jnp.mean(x * x, axis=-1, keepdims=True) + EPS) * g_ref[...]
        h_scr[...] = (y * (1.0 + sc_ref[0]) + sh_ref[0]).astype(BF16)

    o_ref[...] = _dot(h_scr[...], w_ref[...].astype(BF16)).astype(o_ref.dtype)


def _norm_mod_mm(x2, gain, mods, w, rows_per_mod, mod_base, tm, tn=1024, cols=None):
    r, d = x2.shape
    col0, n = (0, w.shape[2]) if cols is None else cols
    jb = col0 // tn
    tpm = rows_per_mod // tm

    def mrow(i):
        return (mod_base + i // tpm) * N_MOD

    return pl.pallas_call(
        _norm_mod_mm_kernel,
        grid=(r // tm, n // tn),
        in_specs=[pl.BlockSpec((tm, d), lambda i, j: (i, 0)),
                  pl.BlockSpec((1, d), lambda i, j: (0, 0)),
                  pl.BlockSpec((1, 1, d), lambda i, j: (mrow(i) + 0, 0, 0)),
                  pl.BlockSpec((1, 1, d), lambda i, j: (mrow(i) + 1, 0, 0)),
                  pl.BlockSpec((None, d, tn), lambda i, j: (0, 0, jb + j))],
        out_specs=pl.BlockSpec((tm, tn), lambda i, j: (i, j)),
        out_shape=jax.ShapeDtypeStruct((r, n), F32),
        scratch_shapes=[pltpu.VMEM((tm, d), BF16)],
        compiler_params=_cparams(("parallel", "arbitrary")),
        name="norm_mod_mm",
    )(x2, gain.reshape(1, d), mods, mods, w)


def _hgrn_kernel(ql, ffl, fbl, vl, gl, qc, ffc, fbc, vc, gc, lbf_ref, lbb_ref, gain_ref,
                 ol_ref, oc_ref,
                 oi_l, oi_c, qtf_l, qtb_l, qtf_c, qtb_c, kv_f, kv_b, dec_f, dec_b):
    c_sz = HGRN_CHUNK
    blk = HGRN_BLOCK
    cpb = blk // c_sz
    nb_l = ql.shape[0] // blk
    nc_l = ql.shape[0] // c_sz
    nc_c = qc.shape[0] // c_sz
    nc = nc_l + nc_c
    assert qc.shape[0] == blk
    row = lax.broadcasted_iota(I32, (blk, blk), 0)
    col = lax.broadcasted_iota(I32, (blk, blk), 1)
    same = (row // c_sz) == (col // c_sz)
    tril = same & (row >= col)
    triu = same & (row <= col)
    lbf = lbf_ref[...]
    lbb = lbb_ref[...]

    exp_mask = (lax.broadcasted_iota(I32, (blk, cpb * HEAD_W), 0) // c_sz
                == lax.broadcasted_iota(I32, (blk, cpb * HEAD_W), 1) // HEAD_W)

    def local_terms(r0, q_ref, ff_ref, fb_ref, v_ref, oi_ref, qtf_ref, qtb_ref, gid_f, gid_b):
        rows = pl.ds(r0, blk)
        q = q_ref[rows, :]
        qs = q * _sigmoid(q)
        vb = v_ref[rows, :].astype(BF16)
        vt = v_ref[rows, :].T.astype(BF16)
        two = (0, 1)
        mask = (tril, triu)
        last = (c_sz - 1, 0)
        f = [lb + (1.0 - lb) * _sigmoid(ref[rows, :]) for ref, lb in ((ff_ref, lbf), (fb_ref, lbb))]
        k = [1.0 - f[d] for d in two]
        lf2 = [jnp.concatenate(_split_bf16(jnp.log(f[d])), axis=1) for d in two]
        cum2 = [_dot(jnp.where(mask[d], 1.0, 0.0).astype(BF16), lf2[d]) for d in two]
        cum = [cum2[d][:, :HEAD_W] + cum2[d][:, HEAD_W:] for d in two]
        tot = [jnp.concatenate(
            [jnp.broadcast_to(cum[d][c * c_sz + last[d]:c * c_sz + last[d] + 1, :], (c_sz, HEAD_W))
             for c in range(cpb)], axis=0) for d in two]
        e = [jnp.exp(cum[d]) for d in two]
        qt = [(qs * e[d]).astype(BF16) for d in two]
        kt = [(k[d] * jnp.exp(-cum[d])).astype(BF16) for d in two]
        att = [jnp.where(mask[d], _dot_nt(qt[d], kt[d]), 0.0).astype(BF16) for d in two]
        oi_ref[rows, :] = _dot(att[0], vb) + _dot(att[1], vb)
        k2 = [(k[d] * jnp.exp(tot[d] - cum[d])).astype(BF16) for d in two]
        k2x = [jnp.where(exp_mask, jnp.concatenate([k2[d]] * cpb, axis=1), jnp.zeros((), BF16)) for d in two]
        kvs = [_dot(vt, k2x[d]) for d in two]
        for d, (qt_ref, kv_ref, dec_ref, gid) in enumerate(((qtf_ref, kv_f, dec_f, gid_f),
                                                            (qtb_ref, kv_b, dec_b, gid_b))):
            for c in range(cpb):
                kv_ref[gid + c] = kvs[d][:, c * HEAD_W:(c + 1) * HEAD_W]
                dec_ref[gid + c] = e[d][c * c_sz + last[d]:c * c_sz + last[d] + 1, :]
            qt_ref[rows, :] = qt[d]

    local_terms(0, qc, ffc, fbc, vc, oi_c, qtf_c, qtb_c, 0, nc_l)

    def local_l(i, carry):
        local_terms(pl.multiple_of(i * blk, blk), ql, ffl, fbl, vl, oi_l, qtf_l, qtb_l,
                    nc_c + i * cpb, i * cpb)
        return carry

    lax.fori_loop(0, nb_l, local_l, 0, unroll=True)

    def rec_f(i, s):
        new = s * dec_f[i] + kv_f[i]
        kv_f[i] = s
        return new

    def rec_b(i, s):
        j = nc - 1 - i
        new = s * dec_b[j] + kv_b[j]
        kv_b[j] = s
        return new

    s0 = jnp.zeros((HEAD_W, HEAD_W), F32)
    lax.fori_loop(0, nc, rec_f, s0, unroll=2)
    lax.fori_loop(0, nc, rec_b, s0, unroll=2)

    gain = gain_ref[...]

    def finish(r0, g_ref, oi_ref, qtf_ref, qtb_ref, o_ref, gid_f, gid_b):
        parts = []
        for c in range(cpb):
            rows_c = pl.ds(r0 + c * c_sz, c_sz)
            parts.append(_dot_nt(qtf_ref[rows_c, :], kv_f[gid_f + c].astype(BF16))
                         + _dot_nt(qtb_ref[rows_c, :], kv_b[gid_b + c].astype(BF16)))
        rows = pl.ds(r0, blk)
        o = oi_ref[rows, :] + jnp.concatenate(parts, axis=0)
        y = o * lax.rsqrt(jnp.mean(o * o, axis=-1, keepdims=True) + EPS) * gain
        g = g_ref[rows, :]
        o_ref[rows, :] = (y * (g * _sigmoid(g))).astype(o_ref.dtype)

    finish(0, gc, oi_c, qtf_c, qtb_c, oc_ref, 0, nc_l)

    def fin_l(i, carry):
        finish(pl.multiple_of(i * blk, blk), gl, oi_l, qtf_l, qtb_l, ol_ref, nc_c + i * cpb, i * cpb)
        return carry

    lax.fori_loop(0, nb_l, fin_l, 0, unroll=True)


def _hgrn(proj_l, proj_c, lb_f, lb_b, gain, bsz):
    n = proj_l.shape[0] // bsz
    n_ctx = proj_c.shape[0] // bsz
    nc = (n + n_ctx) // HGRN_CHUNK
    w = A_HEADS * HEAD_W

    def col(k):
        return lambda b, h: (b, k * A_HEADS + h)

    in_specs = ([pl.BlockSpec((n, HEAD_W), col(k)) for k in range(5)]
                + [pl.BlockSpec((n_ctx, HEAD_W), col(k)) for k in range(5)]
                + [pl.BlockSpec((1, HEAD_W), lambda b, h: (0, h)),
                   pl.BlockSpec((1, HEAD_W), lambda b, h: (0, h)),
                   pl.BlockSpec((1, HEAD_W), lambda b, h: (0, 0))])
    return pl.pallas_call(
        _hgrn_kernel,
        grid=(bsz, A_HEADS),
        in_specs=in_specs,
        out_specs=[pl.BlockSpec((n, HEAD_W), lambda b, h: (b, h)),
                   pl.BlockSpec((n_ctx, HEAD_W), lambda b, h: (b, h))],
        out_shape=[jax.ShapeDtypeStruct((bsz * n, w), BF16),
                   jax.ShapeDtypeStruct((bsz * n_ctx, w), BF16)],
        scratch_shapes=[pltpu.VMEM((n, HEAD_W), F32), pltpu.VMEM((n_ctx, HEAD_W), F32),
                        pltpu.VMEM((n, HEAD_W), BF16), pltpu.VMEM((n, HEAD_W), BF16),
                        pltpu.VMEM((n_ctx, HEAD_W), BF16), pltpu.VMEM((n_ctx, HEAD_W), BF16),
                        pltpu.VMEM((nc, HEAD_W, HEAD_W), F32), pltpu.VMEM((nc, HEAD_W, HEAD_W), F32),
                        pltpu.VMEM((nc, 1, HEAD_W), F32), pltpu.VMEM((nc, 1, HEAD_W), F32)],
        compiler_params=_cparams(("parallel", "parallel")),
        name="hgrn2",
    )(*([proj_l] * 5 + [proj_c] * 5), lb_f.reshape(1, w), lb_b.reshape(1, w), gain.reshape(1, HEAD_W))


def _half_mean_matrix():
    r = lax.broadcasted_iota(I32, (LANES, LANES), 0) // B_DH
    c = lax.broadcasted_iota(I32, (LANES, LANES), 1) // B_DH
    return (r == c).astype(BF16)


def _qk_prep(t, gain, bd, cos, sin):
    sq_hi, sq_lo = _split_bf16(t * t)
    ms = (_dot(sq_hi, bd) + _dot(sq_lo, bd)) * (1.0 / B_DH)
    y = t * lax.rsqrt(ms + EPS) * gain
    if cos is None:
        return y
    lane = lax.broadcasted_iota(I32, y.shape, 1)
    partner = jnp.where(lane % 2 == 0, pltpu.roll(y, LANES - 1, 1), pltpu.roll(y, 1, 1))
    return y * cos + partner * sin


def _attn_kernel(q_l, q_c, k_l, k_c, v_l, v_c, cosq, sinq, cosk, sink, gq_ref, gk_ref, go_ref, lam_ref,
                 o_l, o_c, kp, vp, s_even, s_odd, m_even, m_odd, *, out_scale):
    step = pl.program_id(2)
    n_ctx = k_c.shape[0]
    s_bufs = (s_even, s_odd)
    m_bufs = (m_even, m_odd)
    bd = _half_mean_matrix()
    lane = lax.broadcasted_iota(I32, (1, LANES), 1)
    masks = ((lane < B_DH).astype(F32), (lane >= B_DH).astype(F32))
    lam = lam_ref[:, 0:1]
    q_scale = B_DH ** -0.5 * LOG2_E

    def masked_q(q):
        q = q * q_scale
        return [(q * msk).astype(BF16) for msk in masks]

    def epilogue(r):
        o = (r[0][0:HEAD_W, :] * (1.0 / r[0][HEAD_W:HEAD_W + 1, :])
             - r[1][0:HEAD_W, :] * (lam / r[1][HEAD_W:HEAD_W + 1, :]))
        gain = jnp.concatenate([go_ref[...]] * (o.shape[1] // LANES), axis=1)
        y = o * lax.rsqrt(jnp.mean(o * o, axis=0, keepdims=True) + EPS) * gain
        return (y * out_scale).T.astype(BF16)

    @pl.when((step == 0) & (pl.program_id(0) == 0) & (pl.program_id(1) == 0))
    def _():
        s_odd[...] = jnp.zeros_like(s_odd)
        m_odd[...] = jnp.zeros_like(m_odd)

    @pl.when(step == 0)
    def _():
        kp[0:n_ctx, :] = _qk_prep(k_c[...], gk_ref[...], bd, None, None).astype(BF16)
        kp[n_ctx:, :] = _qk_prep(k_l[...], gk_ref[...], bd, cosk[...], sink[...]).astype(BF16)
        vp[0:HEAD_W, 0:n_ctx] = v_c[...].T.astype(BF16)
        vp[0:HEAD_W, n_ctx:] = v_l[...].T.astype(BF16)
        pad_rows = vp.shape[0] - HEAD_W
        ones_row = (lax.broadcasted_iota(I32, (pad_rows, 1), 0) == 0).astype(BF16)
        vp[HEAD_W:, :] = jnp.broadcast_to(ones_row, (pad_rows, vp.shape[1]))
        qm = masked_q(_qk_prep(q_c[...], gq_ref[...], bd, None, None))
        s_ctx = [_dot_nt(kp[0:n_ctx, :], t) for t in qm]
        e_ctx = [jnp.exp2(t - jnp.max(t, axis=0, keepdims=True)).astype(BF16) for t in s_ctx]
        o_c[...] = epilogue([_dot(vp[:, 0:n_ctx], t) for t in e_ctx])

    kc = ATTN_KEY_CHUNK
    for parity in (0, 1):
        @pl.when((step > 0) & (step % 2 == parity))
        def _(old=parity, new=1 - parity):
            qm = masked_q(_qk_prep(q_l[...], gq_ref[...], bd, cosq[...], sinq[...]))
            mx_old = [m_bufs[old][i, 0:1, :] for i in range(2)]
            acc = [None, None]
            mx_new = [None, None]
            n_chunks = kp.shape[0] // kc
            for c in range(n_chunks + ATTN_SKEW):
                if c < n_chunks:
                    ks = slice(c * kc, (c + 1) * kc)
                    for i in range(2):
                        e = jnp.exp2(s_bufs[old][i, ks, :] - mx_old[i]).astype(BF16)
                        part = _dot(vp[:, ks], e)
                        acc[i] = part if acc[i] is None else acc[i] + part
                if c == n_chunks - 1:
                    o_l[...] = epilogue(acc)
                if c >= ATTN_SKEW:
                    ks = slice((c - ATTN_SKEW) * kc, (c - ATTN_SKEW + 1) * kc)
                    for i in range(2):
                        t = _dot_nt(kp[ks, :], qm[i])
                        s_bufs[new][i, ks, :] = t
                        tm = jnp.max(t, axis=0, keepdims=True)
                        mx_new[i] = tm if mx_new[i] is None else jnp.maximum(mx_new[i], tm)
            for i in range(2):
                m_bufs[new][i] = jnp.broadcast_to(mx_new[i], m_bufs[new].shape[1:])


def _rope_tables(n):
    n_rows = n // GRID_W
    rowp = jnp.repeat(jnp.arange(n_rows), GRID_W).astype(F32)
    colp = jnp.tile(jnp.arange(GRID_W), n_rows).astype(F32)
    pairs = B_DH // 4
    inv = ROPE_BASE ** (-jnp.arange(pairs, dtype=F32) / pairs)
    ang = jnp.concatenate([rowp[:, None] * inv, colp[:, None] * inv], axis=-1)
    cos = jnp.repeat(jnp.cos(ang), 2, axis=-1)
    sin = jnp.repeat(jnp.sin(ang), 2, axis=-1) * jnp.tile(jnp.array([-1.0, 1.0], F32), B_DH // 2)
    return jnp.tile(cos, (1, 2)), jnp.tile(sin, (1, 2))


def _attn(proj_l, proj_c, qk_gain, out_gain, lam, lam_init, bsz, tq=512):
    n = proj_l.shape[0] // bsz
    n_ctx = proj_c.shape[0] // bsz
    nqb = n // tq
    w = B_HEADS * HEAD_W
    cos, sin = _rope_tables(n)
    qcol, kcol, vcol = 5 * A_HEADS, 5 * A_HEADS + B_HEADS, 5 * A_HEADS + 2 * B_HEADS

    def q_blk(t):
        return jnp.clip(t - 1, 0, nqb - 1)

    def o_blk(t):
        return jnp.clip(t - 2, 0, nqb - 1)

    def lat_q(b, h, t):
        return (b * nqb + q_blk(t), qcol + h)

    in_specs = [pl.BlockSpec((tq, HEAD_W), lat_q),
                pl.BlockSpec((n_ctx, HEAD_W), lambda b, h, qb: (b, qcol + h)),
                pl.BlockSpec((n, HEAD_W), lambda b, h, qb: (b, kcol + h)),
                pl.BlockSpec((n_ctx, HEAD_W), lambda b, h, qb: (b, kcol + h)),
                pl.BlockSpec((n, HEAD_W), lambda b, h, qb: (b, vcol + h)),
                pl.BlockSpec((n_ctx, HEAD_W), lambda b, h, qb: (b, vcol + h)),
                pl.BlockSpec((tq, LANES), lambda b, h, t: (q_blk(t), 0)),
                pl.BlockSpec((tq, LANES), lambda b, h, t: (q_blk(t), 0)),
                pl.BlockSpec((n, LANES), lambda b, h, qb: (0, 0)),
                pl.BlockSpec((n, LANES), lambda b, h, qb: (0, 0)),
                pl.BlockSpec((1, LANES), lambda b, h, qb: (0, 0)),
                pl.BlockSpec((1, LANES), lambda b, h, qb: (0, 0)),
                pl.BlockSpec((HEAD_W, LANES), lambda b, h, qb: (0, 0)),
                pl.BlockSpec((1, LANES), lambda b, h, qb: (0, 0))]
    return pl.pallas_call(
        functools.partial(_attn_kernel, out_scale=1.0 - lam_init),
        grid=(bsz, B_HEADS, nqb + 2),
        in_specs=in_specs,
        out_specs=[pl.BlockSpec((tq, HEAD_W), lambda b, h, t: (b * nqb + o_blk(t), h)),
                   pl.BlockSpec((n_ctx, HEAD_W), lambda b, h, qb: (b, h))],
        out_shape=[jax.ShapeDtypeStruct((bsz * n, w), BF16),
                   jax.ShapeDtypeStruct((bsz * n_ctx, w), BF16)],
        scratch_shapes=[pltpu.VMEM((n + n_ctx, HEAD_W), BF16),
                        pltpu.VMEM((HEAD_W + BF16_SUBLANES, n + n_ctx), BF16),
                        pltpu.VMEM((2, n + n_ctx, tq), F32), pltpu.VMEM((2, n + n_ctx, tq), F32),
                        pltpu.VMEM((2, SUBLANES, tq), F32), pltpu.VMEM((2, SUBLANES, tq), F32)],
        compiler_params=_cparams(("arbitrary", "arbitrary", "arbitrary")),
        name="diff_attn",
    )(proj_l, proj_c, proj_l, proj_c, proj_l, proj_c, cos, sin, cos, sin,
      jnp.tile(qk_gain[0], 2).reshape(1, LANES), jnp.tile(qk_gain[1], 2).reshape(1, LANES),
      jnp.broadcast_to(out_gain[:, None], (HEAD_W, LANES)), jnp.full((1, LANES), lam, F32))


def _scan_steps(a, b, reverse, axis):
    n = a.shape[axis]
    pos = lax.broadcasted_iota(I32, a.shape, axis)
    s = 1
    while s < n:
        keep = (pos < n - s) if reverse else (pos >= s)
        shift = n - s if reverse else s
        a_sh = jnp.where(keep, pltpu.roll(a, shift, axis), 1.0)
        b_sh = jnp.where(keep, pltpu.roll(b, shift, axis), 0.0)
        b = a * b_sh + b
        a = a * a_sh
        s *= 2
    return a, b


def _rglru_kernel(y_l, u_l, u_c, cw_ref, cb_ref, gw_ref, gb_ref, lam_ref, o_ref,
                  upad, a_f, b_f, a_b, b_b, h_f, p_f, h_b, p_b, *, seg_len, pitch):
    n = u_l.shape[0]
    n_ctx = u_c.shape[0]
    nseg = SUBLANES
    cw = cw_ref[...]
    cb = cb_ref[...]
    pad = SUBLANES

    def pieces(start, rows):
        out = []
        for r in range(nseg):
            lo, hi = max(start, r * seg_len), min(start + rows, (r + 1) * seg_len)
            if lo < hi:
                out.append((r, lo - r * seg_len, lo - start, hi - lo))
        return out

    def put(ref, start, val):
        for r, off, src, ln in pieces(start, val.shape[0]):
            ref[r * pitch + off:r * pitch + off + ln, :] = val[src:src + ln, :]

    def conv(u_ref, rows):
        upad[0:pad, :] = jnp.zeros((pad, LANES), F32)
        upad[pad:pad + rows, :] = u_ref[...]
        upad[pad + rows:pad + rows + pad, :] = jnp.zeros((pad, LANES), F32)
        acc = cb + jnp.zeros((rows, LANES), F32)
        for j in range(CONV_W):
            off = pad + j - CONV_W // 2
            acc = acc + cw[j:j + 1, :] * upad[off:off + rows, :]
        return acc

    def gates(uc, start_f, start_b):
        ub = uc.astype(BF16)
        for d, (a_ref, b_ref, start) in enumerate(((a_f, b_f, start_f), (a_b, b_b, start_b))):
            lam = lam_ref[d:d + 1, :]
            neg_sp = -(jnp.maximum(-lam, 0.0) + jnp.log(1.0 + jnp.exp(-jnp.abs(lam))))
            r = _sigmoid(_dot(ub, gw_ref[d, 0].astype(BF16)) + gb_ref[d, 0:1, :])
            i = _sigmoid(_dot(ub, gw_ref[d, 1].astype(BF16)) + gb_ref[d, 1:2, :])
            a = jnp.exp((RG_C * neg_sp) * r)
            put(a_ref, start, a)
            x = (1.0 - a) * (1.0 + a)
            put(b_ref, start, (x * lax.rsqrt(jnp.maximum(x, TINY))) * (i * uc))

    gates(conv(u_c, n_ctx), 0, n)
    gates(conv(u_l, n), n_ctx, 0)

    def step(i, carry):
        hf, pf, hb, pb = carry
        rows_f = pl.ds(i, nseg, stride=pitch)
        rows_b = pl.ds(seg_len - 1 - i, nseg, stride=pitch)
        af, ab = a_f[rows_f, :], a_b[rows_b, :]
        hf = af * hf + b_f[rows_f, :]
        hb = ab * hb + b_b[rows_b, :]
        pf = pf * af
        pb = pb * ab
        h_f[rows_f, :] = hf
        p_f[rows_f, :] = pf
        h_b[rows_b, :] = hb
        p_b[rows_b, :] = pb
        return hf, pf, hb, pb

    zero = jnp.zeros((nseg, LANES), F32)
    one = jnp.ones((nseg, LANES), F32)
    hf, pf, hb, pb = lax.fori_loop(0, seg_len, step, (zero, one, zero, one), unroll=8)

    seg = lax.broadcasted_iota(I32, (nseg, LANES), 0)
    _, ef = _scan_steps(pf, hf, False, 0)
    _, eb = _scan_steps(pb, hb, True, 0)
    carry_f = jnp.where(seg >= 1, pltpu.roll(ef, 1, 0), 0.0)
    carry_b = jnp.where(seg < nseg - 1, pltpu.roll(eb, nseg - 1, 0), 0.0)

    cuts = sorted({0, n} | {k * seg_len for k in range(nseg + 1) if 0 < k * seg_len < n}
                  | {k * seg_len - n_ctx for k in range(nseg + 1) if 0 < k * seg_len - n_ctx < n})
    for i0, i1 in zip(cuts[:-1], cuts[1:]):
        ln = i1 - i0
        (rf, of, _, _), = pieces(n_ctx + i0, ln)
        (rb, ob, _, _), = pieces(i0, ln)
        sf = slice(rf * pitch + of, rf * pitch + of + ln)
        sb = slice(rb * pitch + ob, rb * pitch + ob + ln)
        h = (h_f[sf, :] + p_f[sf, :] * carry_f[rf:rf + 1, :]) + (h_b[sb, :] + p_b[sb, :] * carry_b[rb:rb + 1, :])
        y = y_l[i0:i1, :]
        gelu = 0.5 * y * (1.0 + jnp.tanh(0.7978845608028654 * (y + 0.044715 * y * y * y)))
        o_ref[i0:i1, :] = (h * gelu).astype(o_ref.dtype)


def _rglru(proj_l, u_c, conv_w, conv_b, gate_w, gate_b, lam, bsz):
    n = proj_l.shape[0] // bsz
    n_ctx = u_c.shape[0] // bsz
    w = RG_HEADS * HEAD_W
    seg_len = (n + n_ctx) // SUBLANES
    assert seg_len * SUBLANES == n + n_ctx and seg_len % SUBLANES == 0
    assert n_ctx % SUBLANES == 0 and n_ctx <= seg_len
    pitch = seg_len + SUBLANES
    return pl.pallas_call(
        functools.partial(_rglru_kernel, seg_len=seg_len, pitch=pitch),
        grid=(bsz, RG_HEADS),
        in_specs=[pl.BlockSpec((n, HEAD_W), lambda b, h: (b, h)),
                  pl.BlockSpec((n, HEAD_W), lambda b, h: (b, RG_HEADS + h)),
                  pl.BlockSpec((n_ctx, HEAD_W), lambda b, h: (b, h)),
                  pl.BlockSpec((CONV_W, HEAD_W), lambda b, h: (0, h)),
                  pl.BlockSpec((1, HEAD_W), lambda b, h: (0, h)),
                  pl.BlockSpec((2, 2, None, HEAD_W, HEAD_W), lambda b, h: (0, 0, h, 0, 0)),
                  pl.BlockSpec((2, 2, HEAD_W), lambda b, h: (0, 0, h)),
                  pl.BlockSpec((2, HEAD_W), lambda b, h: (0, h))],
        out_specs=pl.BlockSpec((n, HEAD_W), lambda b, h: (b, h)),
        out_shape=jax.ShapeDtypeStruct((bsz * n, w), BF16),
        scratch_shapes=[pltpu.VMEM((n + 2 * SUBLANES, LANES), F32)]
                       + [pltpu.VMEM((SUBLANES * pitch, LANES), F32)] * 8,
        compiler_params=_cparams(("parallel", "parallel")),
        name="rglru",
    )(proj_l, proj_l, u_c, conv_w, conv_b.reshape(1, w), gate_w, gate_b, lam)


def _out_proj_kernel(*refs, n_mix, n_groups, epg, row_chunk):
    mix_refs = refs[:n_mix]
    (w_ref, x_ref, gate_ref, g_ref, sh_ref, sc_ref, wr_ref, rb_ref,
     xo_ref, f_ref, gid_ref) = refs[n_mix:]
    d = x_ref.shape[1]
    n_pairs = epg * (epg - 1) // 2
    for r0 in range(0, x_ref.shape[0], row_chunk):
        rows = slice(r0, r0 + row_chunk)
        k0 = 0
        mix = None
        for m_ref in mix_refs:
            kk = m_ref.shape[1]
            part = _dot(m_ref[rows, :], w_ref[k0:k0 + kk, :])
            mix = part if mix is None else mix + part
            k0 += kk
        x = x_ref[rows, :] + gate_ref[0] * mix
        xo_ref[rows, :] = x
        y = x * lax.rsqrt(jnp.mean(x * x, axis=-1, keepdims=True) + EPS) * g_ref[...]
        f = y * (1.0 + sc_ref[0]) + sh_ref[0]
        f_hi, f_lo = _split_bf16(f)
        f_ref[rows, 0:d // 2] = _pack_bf16_pairs(f)

        hh = _dot(f_hi, wr_ref[...])
        lg = hh[:, 0:ROUTE_W] + hh[:, ROUTE_W:] + _dot(f_lo, wr_ref[:, 0:ROUTE_W]) + rb_ref[...]
        lane = lax.broadcasted_iota(I32, lg.shape, 1).astype(F32)
        neg = -jnp.inf
        big = float(ROUTE_W)
        gl = jnp.where(lane < n_groups, lg, neg)
        gmax = jnp.max(gl, axis=-1, keepdims=True)
        gidx = jnp.min(jnp.where(gl == gmax, lane, big), axis=-1, keepdims=True)
        gw = 1.0 / jnp.sum(jnp.exp(gl - gmax), axis=-1, keepdims=True)
        base = n_groups + gidx * epg
        el = jnp.where((lane >= base) & (lane < base + epg), lg, neg)
        v1 = jnp.max(el, axis=-1, keepdims=True)
        i1 = jnp.min(jnp.where(el == v1, lane, big), axis=-1, keepdims=True)
        el2 = jnp.where(lane == i1, neg, el)
        v2 = jnp.max(el2, axis=-1, keepdims=True)
        i2 = jnp.min(jnp.where(el2 == v2, lane, big), axis=-1, keepdims=True)
        t = jnp.exp(v2 - v1)
        w1 = gw / (1.0 + t)
        w2 = w1 * t
        s1 = i1 - base
        s2 = i2 - base
        lo = jnp.minimum(s1, s2)
        hi = jnp.maximum(s1, s2)
        pair = lo * (2 * epg - 1 - lo) * 0.5 + (hi - lo - 1.0)
        w_lo = jnp.where(s1 < s2, w1, w2)
        w_hi = jnp.where(s1 < s2, w2, w1)
        f_ref[rows, d // 2:] = jnp.where(lane == 0.0, w_lo, 0.0) + jnp.where(lane == 1.0, w_hi, 0.0)
        cls = jnp.broadcast_to(gidx * n_pairs + pair, (row_chunk, LANES)).T
        gid_ref[:, rows] = cls[0:SUBLANES, :].astype(I32)


def _out_proj(mixes, w_out, x2, gain, mods, w_router, b_router, rows_per_mod, mod_base, tm=512):
    r, d = x2.shape
    k = w_out.shape[0]
    tpm = rows_per_mod // tm
    wr = jnp.concatenate(_split_bf16(w_router), axis=1)

    def mrow(i):
        return (mod_base + i // tpm) * N_MOD

    const2 = lambda i: (0, 0)
    in_specs = ([pl.BlockSpec((tm, m.shape[1]), lambda i: (i, 0)) for m in mixes]
                + [pl.BlockSpec((k, d), const2),
                   pl.BlockSpec((tm, d), lambda i: (i, 0)),
                   pl.BlockSpec((1, 1, d), lambda i: (mrow(i) + 2, 0, 0)),
                   pl.BlockSpec((1, d), const2),
                   pl.BlockSpec((1, 1, d), lambda i: (mrow(i) + 3, 0, 0)),
                   pl.BlockSpec((1, 1, d), lambda i: (mrow(i) + 4, 0, 0)),
                   pl.BlockSpec((d, 2 * ROUTE_W), const2),
                   pl.BlockSpec((1, ROUTE_W), const2)])
    x_new, fext, cls = pl.pallas_call(
        functools.partial(_out_proj_kernel, n_mix=len(mixes), n_groups=N_GROUPS, epg=EXPERTS_PER_GROUP,
                          row_chunk=tm),
        grid=(r // tm,),
        in_specs=in_specs,
        out_specs=[pl.BlockSpec((tm, d), lambda i: (i, 0)),
                   pl.BlockSpec((tm, d // 2 + ROUTE_W), lambda i: (i, 0)),
                   pl.BlockSpec((SUBLANES, tm), lambda i: (i, 0))],
        out_shape=[jax.ShapeDtypeStruct((r, d), F32),
                   jax.ShapeDtypeStruct((r, d // 2 + ROUTE_W), F32),
                   jax.ShapeDtypeStruct((r // tm * SUBLANES, tm), I32)],
        compiler_params=_cparams(("parallel",)),
        name="out_proj",
    )(*mixes, w_out, x2, mods, gain.reshape(1, d), mods, mods, wr, b_router)
    return x_new, fext, cls.reshape(r // tm, SUBLANES, tm)[:, 0, :].reshape(r)


def _row_copy(src, s, dst, t, sem):
    return pltpu.make_async_copy(src.at[pl.ds(s, 1), :], dst.at[pl.ds(t, 1), :], sem)


def _dispatch_kernel(pos_ref, src_ref, dst_in, dst, sem, *, tm):
    del dst_in

    def body(j, carry):
        _row_copy(src_ref, j, dst, pos_ref[0, 0, j], sem).start()
        return carry
    lax.fori_loop(0, tm, body, 0, unroll=ROW_DMA_UNROLL)

    def wbody(j, carry):
        _row_copy(src_ref, 0, dst, 0, sem).wait()
        return carry
    lax.fori_loop(0, tm, wbody, 0, unroll=True)


def _dispatch(src, pos, sorted_in, tm=512):
    r, w = src.shape
    return pl.pallas_call(
        functools.partial(_dispatch_kernel, tm=tm),
        grid=(r // tm,),
        in_specs=[pl.BlockSpec((1, 1, tm), lambda i: (i, 0, 0), memory_space=pltpu.SMEM),
                  pl.BlockSpec((tm, w), lambda i: (i, 0)),
                  pl.BlockSpec(memory_space=pl.ANY)],
        out_specs=pl.BlockSpec(memory_space=pl.ANY),
        out_shape=jax.ShapeDtypeStruct(sorted_in.shape, sorted_in.dtype),
        scratch_shapes=[pltpu.SemaphoreType.DMA(())],
        input_output_aliases={2: 0},
        compiler_params=_cparams(("arbitrary",)),
        name="moe_dispatch",
    )(pos.reshape(r // tm, 1, tm), src, sorted_in)


def _pair_slot(t, s):
    return jnp.bitwise_xor(s, jnp.bitwise_and(t, 1))


def _moe_kernel(te_ref, tv_ref, x_ref, w1_ref, w3_ref, w2_ref, o_ref, acc_ref):
    t = pl.program_id(0)
    s = pl.program_id(1)
    d = acc_ref.shape[1]
    valid = tv_ref[t] > 0
    slot = _pair_slot(t, s)

    @pl.when(valid)
    def _():
        x = _unpack_bf16_pairs(x_ref[:, 0:d // 2]).astype(BF16)
        h1 = _dot(x, w1_ref[...].astype(BF16))
        h3 = _dot(x, w3_ref[...].astype(BF16))
        cw = x_ref[:, d // 2:]
        lane = lax.broadcasted_iota(I32, cw.shape, 1)
        cws = jnp.sum(jnp.where(lane == slot, cw, 0.0), axis=-1, keepdims=True)
        y = _dot((h1 * _sigmoid(h1) * h3 * cws).astype(BF16), w2_ref[...].astype(BF16))

        @pl.when(s == 0)
        def _():
            acc_ref[...] = y

        @pl.when(s > 0)
        def _():
            o_ref[...] = _pack_bf16_pairs(acc_ref[...] + y)

    @pl.when(jnp.logical_not(valid) & (s == 0))
    def _():
        o_ref[...] = jnp.zeros_like(o_ref)


def _moe_ffn(tile_expert, tile_valid, f_sorted, w1, w3, w2, layer, tm):
    p, dw = f_sorted.shape
    d = 2 * (dw - ROUTE_W)
    fe = w1.shape[3]

    def wmap(t, s, te, tv):
        return (layer, te[2 * t + _pair_slot(t, s)], 0, 0)

    grid_spec = pltpu.PrefetchScalarGridSpec(
        num_scalar_prefetch=2,
        grid=(p // tm, 2),
        in_specs=[pl.BlockSpec((tm, dw), lambda t, s, te, tv: (jnp.minimum(t, tv[tv.shape[0] - 1]), 0)),
                  pl.BlockSpec((None, None, d, fe), wmap),
                  pl.BlockSpec((None, None, d, fe), wmap),
                  pl.BlockSpec((None, None, fe, d), wmap)],
        out_specs=pl.BlockSpec((tm, d // 2), lambda t, s, te, tv: (t, 0)),
        scratch_shapes=[pltpu.VMEM((tm, d), F32)],
    )
    return pl.pallas_call(
        _moe_kernel,
        grid_spec=grid_spec,
        out_shape=jax.ShapeDtypeStruct((p, d // 2), F32),
        compiler_params=_cparams(("arbitrary", "arbitrary")),
        name="moe_ffn",
    )(tile_expert, tile_valid, f_sorted, w1, w3, w2)


def _combine_kernel(pos_cur, pos_nxt, y_hbm, x_ref, gate_ref, o_ref, ybuf, sem, *, tm):
    i = pl.program_id(0)
    n_steps = pl.num_programs(0)

    def issue(pos_ref, slot):
        def body(j, carry):
            _row_copy(y_hbm, pos_ref[0, 0, j], ybuf.at[slot], j, sem.at[slot]).start()
            return carry
        lax.fori_loop(0, tm, body, 0, unroll=ROW_DMA_UNROLL)

    @pl.when(i == 0)
    def _():
        issue(pos_cur, 0)

    @pl.when(i + 1 < n_steps)
    def _():
        issue(pos_nxt, (i + 1) % 2)

    slot = i % 2

    def wbody(j, carry):
        _row_copy(y_hbm, 0, ybuf.at[slot], 0, sem.at[slot]).wait()
        return carry
    lax.fori_loop(0, tm, wbody, 0, unroll=True)
    o_ref[...] = x_ref[...] + gate_ref[0] * _unpack_bf16_pairs(ybuf[slot])


def _combine(y_sorted, pos, x2, mods, rows_per_mod, mod_base, tm=256):
    r, d = x2.shape
    n_steps = r // tm
    tpm = rows_per_mod // tm
    pos3 = pos.reshape(n_steps, 1, tm)
    return pl.pallas_call(
        functools.partial(_combine_kernel, tm=tm),
        grid=(n_steps,),
        in_specs=[pl.BlockSpec((1, 1, tm), lambda i: (i, 0, 0), memory_space=pltpu.SMEM),
                  pl.BlockSpec((1, 1, tm), lambda i: (jnp.minimum(i + 1, n_steps - 1), 0, 0),
                               memory_space=pltpu.SMEM),
                  pl.BlockSpec(memory_space=pl.ANY),
                  pl.BlockSpec((tm, d), lambda i: (i, 0)),
                  pl.BlockSpec((1, 1, d), lambda i: ((mod_base + i // tpm) * N_MOD + 5, 0, 0))],
        out_specs=pl.BlockSpec((tm, d), lambda i: (i, 0)),
        out_shape=jax.ShapeDtypeStruct((r, d), F32),
        scratch_shapes=[pltpu.VMEM((2, tm, d // 2), F32), pltpu.SemaphoreType.DMA((2,))],
        compiler_params=_cparams(("arbitrary",)),
        name="moe_combine",
    )(pos3, pos3, y_sorted, x2, mods)


def _sorted_positions(gids, tm, p_rows):
    g = jnp.concatenate(gids)
    r = g.shape[0]
    epg = EXPERTS_PER_GROUP
    n_pairs = epg * (epg - 1) // 2
    n_cls = N_GROUPS * n_pairs
    onehot = (g[:, None] == jnp.arange(n_cls, dtype=I32)[None, :]).astype(I32)
    counts = jnp.sum(onehot, axis=0)
    rank = jnp.sum((jnp.cumsum(onehot, axis=0) - 1) * onehot, axis=1)
    padded = ((counts + tm - 1) // tm) * tm
    ends = jnp.cumsum(padded)
    pos = jnp.sum(onehot * (ends - padded)[None, :], axis=1) + rank
    p = r + n_cls * tm if p_rows is None else p_rows
    assert p >= r + n_cls * tm and p % tm == 0
    tile_start = jnp.arange(p // tm, dtype=I32) * tm
    tile_cls = jnp.minimum(jnp.sum((tile_start[:, None] >= ends[None, :]).astype(I32), axis=1), n_cls - 1)
    tile_ok = tile_start < ends[-1]
    tile_valid = jnp.concatenate([tile_ok.astype(I32), (ends[-1:] // tm - 1).astype(I32)])
    pairs = [(i, j) for i in range(epg) for j in range(i + 1, epg)]
    pair_lo = jnp.array([a for a, _ in pairs], I32)
    pair_hi = jnp.array([b for _, b in pairs], I32)
    grp = tile_cls // n_pairs
    tile_expert = jnp.stack([grp * epg + pair_lo[tile_cls % n_pairs],
                             grp * epg + pair_hi[tile_cls % n_pairs]], axis=1)
    tile_expert = jnp.where(tile_ok[:, None], tile_expert, 0).reshape(-1)
    return pos.astype(I32), tile_expert.astype(I32), tile_valid, p


def _moe(fexts, gids, w1, w3, w2, layer, f_sorted=None, tm=512):
    pos, tile_expert, tile_valid, p = _sorted_positions(gids, tm, None if f_sorted is None else f_sorted.shape[0])
    sizes = [f.shape[0] for f in fexts]
    poss, off = [], 0
    for s in sizes:
        poss.append(lax.slice(pos, (off,), (off + s,)))
        off += s
    if f_sorted is None:
        f_sorted = jnp.zeros((p, fexts[0].shape[1]), F32)
    for fext, ps in zip(fexts, poss):
        f_sorted = _dispatch(fext, ps, f_sorted)
    y_sorted = _moe_ffn(tile_expert, tile_valid, f_sorted, w1, w3, w2, layer, tm)
    return y_sorted, poss, f_sorted


def kernel(x, c, ctx, c_ctx, ada_w, ada_b, norm_mix, norm_ffn, even_w_in, even_w_out, hgrn_lb_logits, hgrn_out_norm, diff_qk_norm, diff_lambda, diff_out_norm, odd_w_in, odd_conv_w, odd_conv_b, rg_gate_w, rg_gate_b, rg_lambda, odd_w_out, moe_w_grp, moe_b_grp, moe_w_exp, moe_b_exp, moe_w1, moe_w3, moe_w2):
    bsz, n, d = x.shape
    n_ctx = ctx.shape[1]
    depth = ada_w.shape[0]
    assert depth == 2

    cvec = jnp.zeros((MOD_ROWS, d), F32).at[:bsz].set(c).at[bsz].set(c_ctx)
    mods_all = _ada_all(cvec, ada_w, ada_b).reshape(depth, MOD_ROWS * N_MOD, 1, d)
    x_l = x.reshape(bsz * n, d)
    x_c = ctx.reshape(bsz * n_ctx, d)
    lb_all = jnp.cumsum(jax.nn.softmax(hgrn_lb_logits.astype(F32), axis=1), axis=1)

    def router(l):
        w = jnp.concatenate([moe_w_grp[l], moe_w_exp[l]], axis=1)
        b = jnp.concatenate([moe_b_grp[l], moe_b_exp[l]])
        padw = ROUTE_W - w.shape[1]
        return jnp.pad(w, ((0, 0), (0, padw))), jnp.pad(b, (0, padw)).reshape(1, ROUTE_W)

    l = 0
    mods = mods_all[l]
    lam_init = 0.8 - 0.6 * math.exp(-0.3 * l)
    lv = diff_lambda[0].astype(F32)
    lam = jnp.exp(jnp.sum(lv[0] * lv[1])) - jnp.exp(jnp.sum(lv[2] * lv[3])) + lam_init
    w_in = even_w_in
    proj_l = _norm_mod_mm(x_l, norm_mix[l], mods, w_in, n, 0, tm=1024)
    proj_c = _norm_mod_mm(x_c, norm_mix[l], mods, w_in, bsz * n_ctx, bsz, tm=1024)
    a_l, a_c = _hgrn(proj_l, proj_c, lb_all[0, 0], lb_all[1, 0], hgrn_out_norm[0], bsz)
    b_l, b_c = _attn(proj_l, proj_c, diff_qk_norm[0], diff_out_norm[0], lam, lam_init, bsz)
    w_out = even_w_out[0].astype(BF16)
    wr, br = router(l)
    x_l, f_l, g_l = _out_proj([a_l, b_l], w_out, x_l, norm_ffn[l], mods, wr, br, n, 0)
    x_c, f_c, g_c = _out_proj([a_c, b_c], w_out, x_c, norm_ffn[l], mods, wr, br, bsz * n_ctx, bsz)
    y_sorted, (pos_l, pos_c), f_sorted = _moe([f_l, f_c], [g_l, g_c], moe_w1, moe_w3, moe_w2, l)
    x_l = _combine(y_sorted, pos_l, x_l, mods, n, 0)
    x_c = _combine(y_sorted, pos_c, x_c, mods, bsz * n_ctx, bsz)

    l = 1
    mods = mods_all[l]
    w_in = odd_w_in
    proj_l = _norm_mod_mm(x_l, norm_mix[l], mods, w_in, n, 0, tm=1024)
    rg_w = w_in.shape[2] // 2
    u_c = _norm_mod_mm(x_c, norm_mix[l], mods, w_in, bsz * n_ctx, bsz, tm=1024, cols=(rg_w, rg_w))
    gated = _rglru(proj_l, u_c,odd_conv_w[0], odd_conv_b[0], rg_gate_w[0], rg_gate_b[0], rg_lambda[0], bsz)
    wr, br = router(l)
    x_l, f_l, g_l = _out_proj([gated], odd_w_out[0].astype(BF16), x_l, norm_ffn[l], mods, wr, br, n, 0)
    y_sorted, (pos_l,), _ = _moe([f_l], [g_l], moe_w1, moe_w3, moe_w2, l, f_sorted)
    return _combine(y_sorted, pos_l, x_l, mods, n, 0).reshape(bsz, n, d)
```

```python
import functools
import math

import jax
import jax.numpy as jnp
from jax import lax
from jax.experimental import pallas as pl
from jax.experimental.pallas import tpu as pltpu

F32 = jnp.float32
BF16 = jnp.bfloat16
I32 = jnp.int32

EPS = 1e-6
LOG2_E = 1.4426950408889634
TINY = 1e-30
GELU_TANH_SCALE = math.sqrt(2.0 / math.pi)
GELU_TANH_CUBIC = 0.044715
LANES = 128
SUBLANES = 8
BF16_SUBLANES = 16
GRID_W = 64
A_HEADS = 8
HEAD_W = 128
HGRN_CHUNK = 32
HGRN_BLOCK = 256
HGRN_GROUP = 3
B_HEADS = 8
B_DH = 64
ROPE_BASE = 10000.0
ATTN_KEY_CHUNK = 256
ATTN_SKEW = 3
RG_HEADS = 16
CONV_W = 4
RG_C = 8.0
N_GROUPS = 4
EXPERTS_PER_GROUP = 4
N_MOD = 6
MOD_ROWS = 16
ROUTE_W = LANES
ROW_DMA_UNROLL = 8

V7X_VMEM_BYTES = 64 * 1024 * 1024
VMEM_LIMIT = V7X_VMEM_BYTES * 7 // 8


def _cparams(sem):
    return pltpu.CompilerParams(dimension_semantics=sem, vmem_limit_bytes=VMEM_LIMIT)


def _sigmoid(x):
    return 0.5 * jnp.tanh(0.5 * x) + 0.5


def _dot(a, b):
    return jnp.dot(a, b, preferred_element_type=F32)


def _dot_nt(a, b):
    return lax.dot_general(a, b, (((1,), (1,)), ((), ())), preferred_element_type=F32)


def _split_bf16(x):
    hi = x.astype(BF16)
    return hi, (x - hi.astype(F32)).astype(BF16)


def _ada_kernel(c_ref, w_ref, b_ref, o_ref):
    c = c_ref[...]
    a = (c * _sigmoid(c)).astype(BF16)
    o_ref[...] = _dot(a, w_ref[...].astype(BF16)) + b_ref[...]


def _ada_all(cvec, ada_w, ada_b, tn=1024):
    depth, d, n = ada_w.shape
    return pl.pallas_call(
        _ada_kernel,
        grid=(depth, n // tn),
        in_specs=[pl.BlockSpec((MOD_ROWS, d), lambda l, j: (0, 0)),
                  pl.BlockSpec((None, d, tn), lambda l, j: (l, 0, j)),
                  pl.BlockSpec((None, 1, tn), lambda l, j: (l, 0, j))],
        out_specs=pl.BlockSpec((None, MOD_ROWS, tn), lambda l, j: (l, 0, j)),
        out_shape=jax.ShapeDtypeStruct((depth, MOD_ROWS, n), F32),
        compiler_params=_cparams(("arbitrary", "arbitrary")),
        name="ada_mod",
    )(cvec, ada_w, ada_b.reshape(depth, 1, n))


def _norm_mod_mm_kernel(x_ref, g_ref, sh_ref, sc_ref, w_ref, o_ref, h_scr):
    @pl.when(pl.program_id(1) == 0)
    def _():
        x = x_ref[...]
        y = x * lax.rsqrt(jnp.mean(x * x, axis=-1, keepdims=True) + EPS) * g_ref[...]
        h_scr[...] = (y * (1.0 + sc_ref[0]) + sh_ref[0]).astype(BF16)

    o_ref[...] = _dot(h_scr[...], w_ref[...].astype(BF16)).astype(o_ref.dtype)


def _norm_mod_mm(x2, gain, mods, w, rows_per_mod, mod_base, tm, tn=1024, cols=None):
    r, d = x2.shape
    col0, n = (0, w.shape[2]) if cols is None else cols
    jb = col0 // tn
    tpm = rows_per_mod // tm

    def mrow(i):
        return (mod_base + i // tpm) * N_MOD

    return pl.pallas_call(
        _norm_mod_mm_kernel,
        grid=(r // tm, n // tn),
        in_specs=[pl.BlockSpec((tm, d), lambda i, j: (i, 0)),
                  pl.BlockSpec((1, d), lambda i, j: (0, 0)),
                  pl.BlockSpec((1, 1, d), lambda i, j: (mrow(i) + 0, 0, 0)),
                  pl.BlockSpec((1, 1, d), lambda i, j: (mrow(i) + 1, 0, 0)),
                  pl.BlockSpec((None, d, tn), lambda i, j: (0, 0, jb + j))],
        out_specs=pl.BlockSpec((tm, tn), lambda i, j: (i, j)),
        out_shape=jax.ShapeDtypeStruct((r, n), F32),
        scratch_shapes=[pltpu.VMEM((tm, d), BF16)],
        compiler_params=_cparams(("parallel", "arbitrary")),
        name="norm_mod_mm",
    )(x2, gain.reshape(1, d), mods, mods, w)


def _hgrn_kernel(ql, ffl, fbl, vl, gl, qc, ffc, fbc, vc, gc, lbf_ref, lbb_ref, gain_ref,
                 ol_ref, oc_ref,
                 oi_l, oi_c, qtf_l, qtb_l, qtf_c, qtb_c, kv_f, kv_b, dec_f, dec_b):
    c_sz = HGRN_CHUNK
    blk = HGRN_BLOCK
    cpb = blk // c_sz
    nb_l = ql.shape[0] // blk
    nc_l = ql.shape[0] // c_sz
    nc_c = qc.shape[0] // c_sz
    nc = nc_l + nc_c
    assert qc.shape[0] == blk
    row = lax.broadcasted_iota(I32, (blk, blk), 0)
    col = lax.broadcasted_iota(I32, (blk, blk), 1)
    same = (row // c_sz) == (col // c_sz)
    tril = same & (row >= col)
    triu = same & (row <= col)
    lbf = lbf_ref[...]
    lbb = lbb_ref[...]

    exp_mask = (lax.broadcasted_iota(I32, (blk, cpb * HEAD_W), 0) // c_sz
                == lax.broadcasted_iota(I32, (blk, cpb * HEAD_W), 1) // HEAD_W)

    def local_terms(jobs):
        mask = (tril, triu)
        last = (c_sz - 1, 0)
        lbs = (lbf, lbb)
        chains = [(j, d) for j in range(len(jobs)) for d in (0, 1)]
        rows = [pl.ds(job[0], blk) for job in jobs]
        qs, vb, vt = [], [], []
        for job, rw in zip(jobs, rows):
            q = job[1][rw, :]
            qs.append(q * _sigmoid(q))
            vb.append(job[4][rw, :].astype(BF16))
            vt.append(job[4][rw, :].T.astype(BF16))
        f = [lbs[d] + (1.0 - lbs[d]) * _sigmoid(jobs[j][2 + d][rows[j], :]) for j, d in chains]
        k = [1.0 - t for t in f]
        lf2 = [jnp.concatenate(_split_bf16(jnp.log(t)), axis=1) for t in f]
        cum2 = [_dot(jnp.where(mask[d], 1.0, 0.0).astype(BF16), lf2[i]) for i, (j, d) in enumerate(chains)]
        cum = [t[:, :HEAD_W] + t[:, HEAD_W:] for t in cum2]
        tot = [jnp.concatenate(
            [jnp.broadcast_to(cum[i][c * c_sz + last[d]:c * c_sz + last[d] + 1, :], (c_sz, HEAD_W))
             for c in range(cpb)], axis=0) for i, (j, d) in enumerate(chains)]
        e = [jnp.exp(t) for t in cum]
        qt = [(qs[j] * e[i]).astype(BF16) for i, (j, d) in enumerate(chains)]
        kt = [(k[i] * jnp.exp(-cum[i])).astype(BF16) for i in range(len(chains))]
        att = [jnp.where(mask[d], _dot_nt(qt[i], kt[i]), 0.0).astype(BF16) for i, (j, d) in enumerate(chains)]
        for j, job in enumerate(jobs):
            job[5][rows[j], :] = _dot(att[2 * j], vb[j]) + _dot(att[2 * j + 1], vb[j])
        k2 = [(k[i] * jnp.exp(tot[i] - cum[i])).astype(BF16) for i in range(len(chains))]
        k2x = [jnp.where(exp_mask, jnp.concatenate([t] * cpb, axis=1), jnp.zeros((), BF16)) for t in k2]
        kvs = [_dot(vt[j], k2x[i]) for i, (j, d) in enumerate(chains)]
        for i, (j, d) in enumerate(chains):
            kv_ref, dec_ref = ((kv_f, dec_f), (kv_b, dec_b))[d]
            gid = jobs[j][8 + d]
            for c in range(cpb):
                kv_ref[gid + c] = kvs[i][:, c * HEAD_W:(c + 1) * HEAD_W]
                dec_ref[gid + c] = e[i][c * c_sz + last[d]:c * c_sz + last[d] + 1, :]
            jobs[j][6 + d][rows[j], :] = qt[i]

    ctx_job = (0, qc, ffc, fbc, vc, oi_c, qtf_c, qtb_c, 0, nc_l)
    lat_jobs = [(i * blk, ql, ffl, fbl, vl, oi_l, qtf_l, qtb_l, nc_c + i * cpb, i * cpb) for i in range(nb_l)]
    all_jobs = [ctx_job] + lat_jobs
    for g in range(0, len(all_jobs), HGRN_GROUP):
        local_terms(all_jobs[g:g + HGRN_GROUP])

    def rec_f(i, s):
        new = s * dec_f[i] + kv_f[i]
        kv_f[i] = s
        return new

    def rec_b(i, s):
        j = nc - 1 - i
        new = s * dec_b[j] + kv_b[j]
        kv_b[j] = s
        return new

    s0 = jnp.zeros((HEAD_W, HEAD_W), F32)
    lax.fori_loop(0, nc, rec_f, s0, unroll=2)
    lax.fori_loop(0, nc, rec_b, s0, unroll=2)

    gain = gain_ref[...]

    def finish(jobs):
        parts = [[_dot_nt(qtf_ref[pl.ds(r0 + c * c_sz, c_sz), :], kv_f[gid_f + c].astype(BF16))
                  + _dot_nt(qtb_ref[pl.ds(r0 + c * c_sz, c_sz), :], kv_b[gid_b + c].astype(BF16))
                  for c in range(cpb)]
                 for r0, _, _, qtf_ref, qtb_ref, _, gid_f, gid_b in jobs]
        for (r0, g_ref, oi_ref, _, _, o_ref, _, _), part in zip(jobs, parts):
            rows = pl.ds(r0, blk)
            o = oi_ref[rows, :] + jnp.concatenate(part, axis=0)
            y = o * lax.rsqrt(jnp.mean(o * o, axis=-1, keepdims=True) + EPS) * gain
            g = g_ref[rows, :]
            o_ref[rows, :] = (y * (g * _sigmoid(g))).astype(o_ref.dtype)

    fin_jobs = ([(0, gc, oi_c, qtf_c, qtb_c, oc_ref, 0, nc_l)]
                + [(i * blk, gl, oi_l, qtf_l, qtb_l, ol_ref, nc_c + i * cpb, i * cpb) for i in range(nb_l)])
    for g0 in range(0, len(fin_jobs), HGRN_GROUP):
        finish(fin_jobs[g0:g0 + HGRN_GROUP])


def _hgrn(proj_l, proj_c, lb_f, lb_b, gain, bsz):
    n = proj_l.shape[0] // bsz
    n_ctx = proj_c.shape[0] // bsz
    nc = (n + n_ctx) // HGRN_CHUNK
    w = A_HEADS * HEAD_W

    def col(k):
        return lambda b, h: (b, k * A_HEADS + h)

    in_specs = ([pl.BlockSpec((n, HEAD_W), col(k)) for k in range(5)]
                + [pl.BlockSpec((n_ctx, HEAD_W), col(k)) for k in range(5)]
                + [pl.BlockSpec((1, HEAD_W), lambda b, h: (0, h)),
                   pl.BlockSpec((1, HEAD_W), lambda b, h: (0, h)),
                   pl.BlockSpec((1, HEAD_W), lambda b, h: (0, 0))])
    return pl.pallas_call(
        _hgrn_kernel,
        grid=(bsz, A_HEADS),
        in_specs=in_specs,
        out_specs=[pl.BlockSpec((n, HEAD_W), lambda b, h: (b, h)),
                   pl.BlockSpec((n_ctx, HEAD_W), lambda b, h: (b, h))],
        out_shape=[jax.ShapeDtypeStruct((bsz * n, w), BF16),
                   jax.ShapeDtypeStruct((bsz * n_ctx, w), BF16)],
        scratch_shapes=[pltpu.VMEM((n, HEAD_W), F32), pltpu.VMEM((n_ctx, HEAD_W), F32),
                        pltpu.VMEM((n, HEAD_W), BF16), pltpu.VMEM((n, HEAD_W), BF16),
                        pltpu.VMEM((n_ctx, HEAD_W), BF16), pltpu.VMEM((n_ctx, HEAD_W), BF16),
                        pltpu.VMEM((nc, HEAD_W, HEAD_W), F32), pltpu.VMEM((nc, HEAD_W, HEAD_W), F32),
                        pltpu.VMEM((nc, 1, HEAD_W), F32), pltpu.VMEM((nc, 1, HEAD_W), F32)],
        compiler_params=_cparams(("parallel", "parallel")),
        name="hgrn2",
    )(*([proj_l] * 5 + [proj_c] * 5), lb_f.reshape(1, w), lb_b.reshape(1, w), gain.reshape(1, HEAD_W))


def _half_mean_matrix():
    r = lax.broadcasted_iota(I32, (LANES, LANES), 0) // B_DH
    c = lax.broadcasted_iota(I32, (LANES, LANES), 1) // B_DH
    return (r == c).astype(BF16)


def _qk_prep(t, gain, bd, cos, sin):
    sq_hi, sq_lo = _split_bf16(t * t)
    ms = (_dot(sq_hi, bd) + _dot(sq_lo, bd)) * (1.0 / B_DH)
    y = t * lax.rsqrt(ms + EPS) * gain
    if cos is None:
        return y
    lane = lax.broadcasted_iota(I32, y.shape, 1)
    partner = jnp.where(lane % 2 == 0, pltpu.roll(y, LANES - 1, 1), pltpu.roll(y, 1, 1))
    return y * cos + partner * sin


def _attn_kernel(q_l, q_c, k_l, k_c, v_l, v_c, cosq, sinq, cosk, sink, gq_ref, gk_ref, go_ref, lam_ref,
                 o_l, o_c, kp, vp, s_even, s_odd, m_even, m_odd, *, out_scale):
    step = pl.program_id(2)
    n_ctx = k_c.shape[0]
    s_bufs = (s_even, s_odd)
    m_bufs = (m_even, m_odd)
    bd = _half_mean_matrix()
    lane = lax.broadcasted_iota(I32, (1, LANES), 1)
    masks = ((lane < B_DH).astype(F32), (lane >= B_DH).astype(F32))
    lam = lam_ref[:, 0:1]
    q_scale = B_DH ** -0.5 * LOG2_E

    def masked_q(q):
        q = q * q_scale
        return [(q * msk).astype(BF16) for msk in masks]

    def epilogue(r):
        o = (r[0][0:HEAD_W, :] * (1.0 / r[0][HEAD_W:HEAD_W + 1, :])
             - r[1][0:HEAD_W, :] * (lam / r[1][HEAD_W:HEAD_W + 1, :]))
        gain = jnp.concatenate([go_ref[...]] * (o.shape[1] // LANES), axis=1)
        y = o * lax.rsqrt(jnp.mean(o * o, axis=0, keepdims=True) + EPS) * gain
        return (y * out_scale).T.astype(BF16)

    @pl.when((step == 0) & (pl.program_id(0) == 0) & (pl.program_id(1) == 0))
    def _():
        s_odd[...] = jnp.zeros_like(s_odd)
        m_odd[...] = jnp.zeros_like(m_odd)

    @pl.when(step == 0)
    def _():
        kp[0:n_ctx, :] = _qk_prep(k_c[...], gk_ref[...], bd, None, None).astype(BF16)
        kp[n_ctx:, :] = _qk_prep(k_l[...], gk_ref[...], bd, cosk[...], sink[...]).astype(BF16)
        vp[0:HEAD_W, 0:n_ctx] = v_c[...].T.astype(BF16)
        vp[0:HEAD_W, n_ctx:] = v_l[...].T.astype(BF16)
        pad_rows = vp.shape[0] - HEAD_W
        ones_row = (lax.broadcasted_iota(I32, (pad_rows, 1), 0) == 0).astype(BF16)
        vp[HEAD_W:, :] = jnp.broadcast_to(ones_row, (pad_rows, vp.shape[1]))
        qm = masked_q(_qk_prep(q_c[...], gq_ref[...], bd, None, None))
        s_ctx = [_dot_nt(kp[0:n_ctx, :], t) for t in qm]
        e_ctx = [jnp.exp2(t - jnp.max(t, axis=0, keepdims=True)).astype(BF16) for t in s_ctx]
        o_c[...] = epilogue([_dot(vp[:, 0:n_ctx], t) for t in e_ctx])

    kc = ATTN_KEY_CHUNK
    for parity in (0, 1):
        @pl.when((step > 0) & (step % 2 == parity))
        def _(old=parity, new=1 - parity):
            qm = masked_q(_qk_prep(q_l[...], gq_ref[...], bd, cosq[...], sinq[...]))
            mx_old = [m_bufs[old][i, 0:1, :] for i in range(2)]
            acc = [None, None]
            mx_new = [None, None]
            n_chunks = kp.shape[0] // kc
            for c in range(n_chunks + ATTN_SKEW):
                if c < n_chunks:
                    ks = slice(c * kc, (c + 1) * kc)
                    for i in range(2):
                        e = jnp.exp2(s_bufs[old][i, ks, :] - mx_old[i]).astype(BF16)
                        part = _dot(vp[:, ks], e)
                        acc[i] = part if acc[i] is None else acc[i] + part
                if c == n_chunks - 1:
                    o_l[...] = epilogue(acc)
                if c >= ATTN_SKEW:
                    ks = slice((c - ATTN_SKEW) * kc, (c - ATTN_SKEW + 1) * kc)
                    for i in range(2):
                        t = _dot_nt(kp[ks, :], qm[i])
                        s_bufs[new][i, ks, :] = t
                        tm = jnp.max(t, axis=0, keepdims=True)
                        mx_new[i] = tm if mx_new[i] is None else jnp.maximum(mx_new[i], tm)
            for i in range(2):
                m_bufs[new][i] = jnp.broadcast_to(mx_new[i], m_bufs[new].shape[1:])


def _rope_tables(n):
    n_rows = n // GRID_W
    rowp = jnp.repeat(jnp.arange(n_rows), GRID_W).astype(F32)
    colp = jnp.tile(jnp.arange(GRID_W), n_rows).astype(F32)
    pairs = B_DH // 4
    inv = ROPE_BASE ** (-jnp.arange(pairs, dtype=F32) / pairs)
    ang = jnp.concatenate([rowp[:, None] * inv, colp[:, None] * inv], axis=-1)
    cos = jnp.repeat(jnp.cos(ang), 2, axis=-1)
    sin = jnp.repeat(jnp.sin(ang), 2, axis=-1) * jnp.tile(jnp.array([-1.0, 1.0], F32), B_DH // 2)
    return jnp.tile(cos, (1, 2)), jnp.tile(sin, (1, 2))


def _attn(proj_l, proj_c, qk_gain, out_gain, lam, lam_init, bsz, tq=512):
    n = proj_l.shape[0] // bsz
    n_ctx = proj_c.shape[0] // bsz
    nqb = n // tq
    w = B_HEADS * HEAD_W
    cos, sin = _rope_tables(n)
    qcol, kcol, vcol = 5 * A_HEADS, 5 * A_HEADS + B_HEADS, 5 * A_HEADS + 2 * B_HEADS

    def q_blk(t):
        return jnp.clip(t - 1, 0, nqb - 1)

    def o_blk(t):
        return jnp.clip(t - 2, 0, nqb - 1)

    def lat_q(b, h, t):
        return (b * nqb + q_blk(t), qcol + h)

    in_specs = [pl.BlockSpec((tq, HEAD_W), lat_q),
                pl.BlockSpec((n_ctx, HEAD_W), lambda b, h, qb: (b, qcol + h)),
                pl.BlockSpec((n, HEAD_W), lambda b, h, qb: (b, kcol + h)),
                pl.BlockSpec((n_ctx, HEAD_W), lambda b, h, qb: (b, kcol + h)),
                pl.BlockSpec((n, HEAD_W), lambda b, h, qb: (b, vcol + h)),
                pl.BlockSpec((n_ctx, HEAD_W), lambda b, h, qb: (b, vcol + h)),
                pl.BlockSpec((tq, LANES), lambda b, h, t: (q_blk(t), 0)),
                pl.BlockSpec((tq, LANES), lambda b, h, t: (q_blk(t), 0)),
                pl.BlockSpec((n, LANES), lambda b, h, qb: (0, 0)),
                pl.BlockSpec((n, LANES), lambda b, h, qb: (0, 0)),
                pl.BlockSpec((1, LANES), lambda b, h, qb: (0, 0)),
                pl.BlockSpec((1, LANES), lambda b, h, qb: (0, 0)),
                pl.BlockSpec((HEAD_W, LANES), lambda b, h, qb: (0, 0)),
                pl.BlockSpec((1, LANES), lambda b, h, qb: (0, 0))]
    return pl.pallas_call(
        functools.partial(_attn_kernel, out_scale=1.0 - lam_init),
        grid=(bsz, B_HEADS, nqb + 2),
        in_specs=in_specs,
        out_specs=[pl.BlockSpec((tq, HEAD_W), lambda b, h, t: (b * nqb + o_blk(t), h)),
                   pl.BlockSpec((n_ctx, HEAD_W), lambda b, h, qb: (b, h))],
        out_shape=[jax.ShapeDtypeStruct((bsz * n, w), BF16),
                   jax.ShapeDtypeStruct((bsz * n_ctx, w), BF16)],
        scratch_shapes=[pltpu.VMEM((n + n_ctx, HEAD_W), BF16),
                        pltpu.VMEM((HEAD_W + BF16_SUBLANES, n + n_ctx), BF16),
                        pltpu.VMEM((2, n + n_ctx, tq), F32), pltpu.VMEM((2, n + n_ctx, tq), F32),
                        pltpu.VMEM((2, SUBLANES, tq), F32), pltpu.VMEM((2, SUBLANES, tq), F32)],
        compiler_params=_cparams(("arbitrary", "arbitrary", "arbitrary")),
        name="diff_attn",
    )(proj_l, proj_c, proj_l, proj_c, proj_l, proj_c, cos, sin, cos, sin,
      jnp.tile(qk_gain[0], 2).reshape(1, LANES), jnp.tile(qk_gain[1], 2).reshape(1, LANES),
      jnp.broadcast_to(out_gain[:, None], (HEAD_W, LANES)), jnp.full((1, LANES), lam, F32))


def _scan_steps(a, b, reverse, axis):
    n = a.shape[axis]
    pos = lax.broadcasted_iota(I32, a.shape, axis)
    s = 1
    while s < n:
        keep = (pos < n - s) if reverse else (pos >= s)
        shift = n - s if reverse else s
        a_sh = jnp.where(keep, pltpu.roll(a, shift, axis), 1.0)
        b_sh = jnp.where(keep, pltpu.roll(b, shift, axis), 0.0)
        b = a * b_sh + b
        a = a * a_sh
        s *= 2
    return a, b


def _rglru_kernel(y_l, u_l, u_c, cw_ref, cb_ref, gw_ref, gb_ref, lam_ref, o_ref,
                  upad, a_f, b_f, a_b, b_b, h_f, p_f, h_b, p_b, *, seg_len, pitch):
    n = u_l.shape[0]
    n_ctx = u_c.shape[0]
    nseg = SUBLANES
    cw = cw_ref[...]
    cb = cb_ref[...]
    pad = SUBLANES

    def pieces(start, rows):
        out = []
        for r in range(nseg):
            lo, hi = max(start, r * seg_len), min(start + rows, (r + 1) * seg_len)
            if lo < hi:
                out.append((r, lo - r * seg_len, lo - start, hi - lo))
        return out

    def put(ref, start, val):
        for r, off, src, ln in pieces(start, val.shape[0]):
            ref[r * pitch + off:r * pitch + off + ln, :] = val[src:src + ln, :]

    def conv(u_ref, rows):
        upad[0:pad, :] = jnp.zeros((pad, LANES), F32)
        upad[pad:pad + rows, :] = u_ref[...]
        upad[pad + rows:pad + rows + pad, :] = jnp.zeros((pad, LANES), F32)
        acc = cb + jnp.zeros((rows, LANES), F32)
        for j in range(CONV_W):
            off = pad + j - CONV_W // 2
            acc = acc + cw[j:j + 1, :] * upad[off:off + rows, :]
        return acc

    def gates(uc, start_f, start_b):
        ub = uc.astype(BF16)
        for d, (a_ref, b_ref, start) in enumerate(((a_f, b_f, start_f), (a_b, b_b, start_b))):
            lam = lam_ref[d:d + 1, :]
            neg_sp = -(jnp.maximum(-lam, 0.0) + jnp.log(1.0 + jnp.exp(-jnp.abs(lam))))
            r = _sigmoid(_dot(ub, gw_ref[d, 0].astype(BF16)) + gb_ref[d, 0:1, :])
            i = _sigmoid(_dot(ub, gw_ref[d, 1].astype(BF16)) + gb_ref[d, 1:2, :])
            a = jnp.exp((RG_C * neg_sp) * r)
            put(a_ref, start, a)
            x = (1.0 - a) * (1.0 + a)
            put(b_ref, start, (x * lax.rsqrt(jnp.maximum(x, TINY))) * (i * uc))

    gates(conv(u_c, n_ctx), 0, n)
    gates(conv(u_l, n), n_ctx, 0)

    def step(i, carry):
        hf, pf, hb, pb = carry
        rows_f = pl.ds(i, nseg, stride=pitch)
        rows_b = pl.ds(seg_len - 1 - i, nseg, stride=pitch)
        af, ab = a_f[rows_f, :], a_b[rows_b, :]
        hf = af * hf + b_f[rows_f, :]
        hb = ab * hb + b_b[rows_b, :]
        pf = pf * af
        pb = pb * ab
        h_f[rows_f, :] = hf
        p_f[rows_f, :] = pf
        h_b[rows_b, :] = hb
        p_b[rows_b, :] = pb
        return hf, pf, hb, pb

    zero = jnp.zeros((nseg, LANES), F32)
    one = jnp.ones((nseg, LANES), F32)
    hf, pf, hb, pb = lax.fori_loop(0, seg_len, step, (zero, one, zero, one), unroll=8)

    seg = lax.broadcasted_iota(I32, (nseg, LANES), 0)
    _, ef = _scan_steps(pf, hf, False, 0)
    _, eb = _scan_steps(pb, hb, True, 0)
    carry_f = jnp.where(seg >= 1, pltpu.roll(ef, 1, 0), 0.0)
    carry_b = jnp.where(seg < nseg - 1, pltpu.roll(eb, nseg - 1, 0), 0.0)

    cuts = sorted({0, n} | {k * seg_len for k in range(nseg + 1) if 0 < k * seg_len < n}
                  | {k * seg_len - n_ctx for k in range(nseg + 1) if 0 < k * seg_len - n_ctx < n})
    for i0, i1 in zip(cuts[:-1], cuts[1:]):
        ln = i1 - i0
        (rf, of, _, _), = pieces(n_ctx + i0, ln)
        (rb, ob, _, _), = pieces(i0, ln)
        sf = slice(rf * pitch + of, rf * pitch + of + ln)
        sb = slice(rb * pitch + ob, rb * pitch + ob + ln)
        h = (h_f[sf, :] + p_f[sf, :] * carry_f[rf:rf + 1, :]) + (h_b[sb, :] + p_b[sb, :] * carry_b[rb:rb + 1, :])
        y = y_l[i0:i1, :]
        gelu = 0.5 * y * (1.0 + jnp.tanh(GELU_TANH_SCALE * (y + GELU_TANH_CUBIC * y * y * y)))
        o_ref[i0:i1, :] = (h * gelu).astype(o_ref.dtype)


def _rglru(proj_l, u_c, conv_w, conv_b, gate_w, gate_b, lam, bsz):
    n = proj_l.shape[0] // bsz
    n_ctx = u_c.shape[0] // bsz
    w = RG_HEADS * HEAD_W
    seg_len = (n + n_ctx) // SUBLANES
    assert seg_len * SUBLANES == n + n_ctx and seg_len % SUBLANES == 0
    assert n_ctx % SUBLANES == 0 and n_ctx <= seg_len
    pitch = seg_len + SUBLANES
    return pl.pallas_call(
        functools.partial(_rglru_kernel, seg_len=seg_len, pitch=pitch),
        grid=(bsz, RG_HEADS),
        in_specs=[pl.BlockSpec((n, HEAD_W), lambda b, h: (b, h)),
                  pl.BlockSpec((n, HEAD_W), lambda b, h: (b, RG_HEADS + h)),
                  pl.BlockSpec((n_ctx, HEAD_W), lambda b, h: (b, h)),
                  pl.BlockSpec((CONV_W, HEAD_W), lambda b, h: (0, h)),
                  pl.BlockSpec((1, HEAD_W), lambda b, h: (0, h)),
                  pl.BlockSpec((2, 2, None, HEAD_W, HEAD_W), lambda b, h: (0, 0, h, 0, 0)),
                  pl.BlockSpec((2, 2, HEAD_W), lambda b, h: (0, 0, h)),
                  pl.BlockSpec((2, HEAD_W), lambda b, h: (0, h))],
        out_specs=pl.BlockSpec((n, HEAD_W), lambda b, h: (b, h)),
        out_shape=jax.ShapeDtypeStruct((bsz * n, w), BF16),
        scratch_shapes=[pltpu.VMEM((n + 2 * SUBLANES, LANES), F32)]
                       + [pltpu.VMEM((SUBLANES * pitch, LANES), F32)] * 8,
        compiler_params=_cparams(("parallel", "parallel")),
        name="rglru",
    )(proj_l, proj_l, u_c, conv_w, conv_b.reshape(1, w), gate_w, gate_b, lam)


def _out_proj_kernel(*refs, n_mix, n_groups, epg, row_chunk):
    mix_refs = refs[:n_mix]
    (w_ref, x_ref, gate_ref, g_ref, sh_ref, sc_ref, wr_ref, rb_ref,
     xo_ref, f_ref, gid_ref) = refs[n_mix:]
    d = x_ref.shape[1]
    n_pairs = epg * (epg - 1) // 2
    for r0 in range(0, x_ref.shape[0], row_chunk):
        rows = slice(r0, r0 + row_chunk)
        k0 = 0
        mix = None
        for m_ref in mix_refs:
            kk = m_ref.shape[1]
            part = _dot(m_ref[rows, :], w_ref[k0:k0 + kk, :])
            mix = part if mix is None else mix + part
            k0 += kk
        x = x_ref[rows, :] + gate_ref[0] * mix
        xo_ref[rows, :] = x
        y = x * lax.rsqrt(jnp.mean(x * x, axis=-1, keepdims=True) + EPS) * g_ref[...]
        f = y * (1.0 + sc_ref[0]) + sh_ref[0]
        f_ref[rows, 0:d] = f
        f_hi, f_lo = _split_bf16(f)

        hh = _dot(f_hi, wr_ref[...])
        lg = hh[:, 0:ROUTE_W] + hh[:, ROUTE_W:] + _dot(f_lo, wr_ref[:, 0:ROUTE_W]) + rb_ref[...]
        lane = lax.broadcasted_iota(I32, lg.shape, 1).astype(F32)
        neg = -jnp.inf
        big = float(ROUTE_W)
        gl = jnp.where(lane < n_groups, lg, neg)
        gmax = jnp.max(gl, axis=-1, keepdims=True)
        gidx = jnp.min(jnp.where(gl == gmax, lane, big), axis=-1, keepdims=True)
        gw = 1.0 / jnp.sum(jnp.exp(gl - gmax), axis=-1, keepdims=True)
        base = n_groups + gidx * epg
        el = jnp.where((lane >= base) & (lane < base + epg), lg, neg)
        v1 = jnp.max(el, axis=-1, keepdims=True)
        i1 = jnp.min(jnp.where(el == v1, lane, big), axis=-1, keepdims=True)
        el2 = jnp.where(lane == i1, neg, el)
        v2 = jnp.max(el2, axis=-1, keepdims=True)
        i2 = jnp.min(jnp.where(el2 == v2, lane, big), axis=-1, keepdims=True)
        t = jnp.exp(v2 - v1)
        w1 = gw / (1.0 + t)
        w2 = w1 * t
        s1 = i1 - base
        s2 = i2 - base
        lo = jnp.minimum(s1, s2)
        hi = jnp.maximum(s1, s2)
        pair = lo * (2 * epg - 1 - lo) * 0.5 + (hi - lo - 1.0)
        w_lo = jnp.where(s1 < s2, w1, w2)
        w_hi = jnp.where(s1 < s2, w2, w1)
        f_ref[rows, d:] = jnp.where(lane == 0.0, w_lo, 0.0) + jnp.where(lane == 1.0, w_hi, 0.0)
        cls = jnp.broadcast_to(gidx * n_pairs + pair, (row_chunk, LANES)).T
        gid_ref[:, rows] = cls[0:SUBLANES, :].astype(I32)


def _out_proj(mixes, w_out, x2, gain, mods, w_router, b_router, rows_per_mod, mod_base, tm=512):
    r, d = x2.shape
    k = w_out.shape[0]
    tpm = rows_per_mod // tm
    wr = jnp.concatenate(_split_bf16(w_router), axis=1)

    def mrow(i):
        return (mod_base + i // tpm) * N_MOD

    const2 = lambda i: (0, 0)
    in_specs = ([pl.BlockSpec((tm, m.shape[1]), lambda i: (i, 0)) for m in mixes]
                + [pl.BlockSpec((k, d), const2),
                   pl.BlockSpec((tm, d), lambda i: (i, 0)),
                   pl.BlockSpec((1, 1, d), lambda i: (mrow(i) + 2, 0, 0)),
                   pl.BlockSpec((1, d), const2),
                   pl.BlockSpec((1, 1, d), lambda i: (mrow(i) + 3, 0, 0)),
                   pl.BlockSpec((1, 1, d), lambda i: (mrow(i) + 4, 0, 0)),
                   pl.BlockSpec((d, 2 * ROUTE_W), const2),
                   pl.BlockSpec((1, ROUTE_W), const2)])
    x_new, fext, cls = pl.pallas_call(
        functools.partial(_out_proj_kernel, n_mix=len(mixes), n_groups=N_GROUPS, epg=EXPERTS_PER_GROUP,
                          row_chunk=tm),
        grid=(r // tm,),
        in_specs=in_specs,
        out_specs=[pl.BlockSpec((tm, d), lambda i: (i, 0)),
                   pl.BlockSpec((tm, d + ROUTE_W), lambda i: (i, 0)),
                   pl.BlockSpec((SUBLANES, tm), lambda i: (i, 0))],
        out_shape=[jax.ShapeDtypeStruct((r, d), F32),
                   jax.ShapeDtypeStruct((r, d + ROUTE_W), F32),
                   jax.ShapeDtypeStruct((r // tm * SUBLANES, tm), I32)],
        compiler_params=_cparams(("parallel",)),
        name="out_proj",
    )(*mixes, w_out, x2, mods, gain.reshape(1, d), mods, mods, wr, b_router)
    return x_new, fext, cls.reshape(r // tm, SUBLANES, tm)[:, 0, :].reshape(r)


def _row_copy(src, s, dst, t, sem):
    return pltpu.make_async_copy(src.at[pl.ds(s, 1), :], dst.at[pl.ds(t, 1), :], sem)


def _dispatch_kernel(pos_ref, src_ref, dst_in, dst, sem, *, tm):
    del dst_in

    def body(j, carry):
        _row_copy(src_ref, j, dst, pos_ref[0, 0, j], sem).start()
        return carry
    lax.fori_loop(0, tm, body, 0, unroll=ROW_DMA_UNROLL)

    def wbody(j, carry):
        _row_copy(src_ref, 0, dst, 0, sem).wait()
        return carry
    lax.fori_loop(0, tm, wbody, 0, unroll=True)


def _dispatch(src, pos, sorted_in, tm=512):
    r, w = src.shape
    return pl.pallas_call(
        functools.partial(_dispatch_kernel, tm=tm),
        grid=(r // tm,),
        in_specs=[pl.BlockSpec((1, 1, tm), lambda i: (i, 0, 0), memory_space=pltpu.SMEM),
                  pl.BlockSpec((tm, w), lambda i: (i, 0)),
                  pl.BlockSpec(memory_space=pl.ANY)],
        out_specs=pl.BlockSpec(memory_space=pl.ANY),
        out_shape=jax.ShapeDtypeStruct(sorted_in.shape, sorted_in.dtype),
        scratch_shapes=[pltpu.SemaphoreType.DMA(())],
        input_output_aliases={2: 0},
        compiler_params=_cparams(("arbitrary",)),
        name="moe_dispatch",
    )(pos.reshape(r // tm, 1, tm), src, sorted_in)


def _pair_slot(t, s):
    return jnp.bitwise_xor(s, jnp.bitwise_and(t, 1))


def _moe_kernel(te_ref, tv_ref, x_ref, w1_ref, w3_ref, w2_ref, o_ref):
    t = pl.program_id(0)
    s = pl.program_id(1)
    d = o_ref.shape[1]
    valid = tv_ref[t] > 0
    slot = _pair_slot(t, s)

    @pl.when(valid)
    def _():
        x = x_ref[:, 0:d].astype(BF16)
        h1 = _dot(x, w1_ref[...].astype(BF16))
        h3 = _dot(x, w3_ref[...].astype(BF16))
        cw = x_ref[:, d:]
        lane = lax.broadcasted_iota(I32, cw.shape, 1)
        cws = jnp.sum(jnp.where(lane == slot, cw, 0.0), axis=-1, keepdims=True)
        y = _dot((h1 * _sigmoid(h1) * h3 * cws).astype(BF16), w2_ref[...].astype(BF16))

        @pl.when(s == 0)
        def _():
            o_ref[...] = y

        @pl.when(s > 0)
        def _():
            o_ref[...] += y

    @pl.when(jnp.logical_not(valid) & (s == 0))
    def _():
        o_ref[...] = jnp.zeros_like(o_ref)


def _moe_ffn(tile_expert, tile_valid, f_sorted, w1, w3, w2, layer, tm):
    p, dw = f_sorted.shape
    d = dw - ROUTE_W
    fe = w1.shape[3]

    def wmap(t, s, te, tv):
        return (layer, te[2 * t + _pair_slot(t, s)], 0, 0)

    grid_spec = pltpu.PrefetchScalarGridSpec(
        num_scalar_prefetch=2,
        grid=(p // tm, 2),
        in_specs=[pl.BlockSpec((tm, dw), lambda t, s, te, tv: (jnp.minimum(t, tv[tv.shape[0] - 1]), 0)),
                  pl.BlockSpec((None, None, d, fe), wmap),
                  pl.BlockSpec((None, None, d, fe), wmap),
                  pl.BlockSpec((None, None, fe, d), wmap)],
        out_specs=pl.BlockSpec((tm, d), lambda t, s, te, tv: (t, 0)),
    )
    return pl.pallas_call(
        _moe_kernel,
        grid_spec=grid_spec,
        out_shape=jax.ShapeDtypeStruct((p, d), F32),
        compiler_params=_cparams(("arbitrary", "arbitrary")),
        name="moe_ffn",
    )(tile_expert, tile_valid, f_sorted, w1, w3, w2)


def _combine_kernel(pos_cur, pos_nxt, y_hbm, x_ref, gate_ref, o_ref, ybuf, sem, *, tm):
    i = pl.program_id(0)
    n_steps = pl.num_programs(0)

    def issue(pos_ref, slot):
        def body(j, carry):
            _row_copy(y_hbm, pos_ref[0, 0, j], ybuf.at[slot], j, sem.at[slot]).start()
            return carry
        lax.fori_loop(0, tm, body, 0, unroll=ROW_DMA_UNROLL)

    @pl.when(i == 0)
    def _():
        issue(pos_cur, 0)

    @pl.when(i + 1 < n_steps)
    def _():
        issue(pos_nxt, (i + 1) % 2)

    slot = i % 2

    def wbody(j, carry):
        _row_copy(y_hbm, 0, ybuf.at[slot], 0, sem.at[slot]).wait()
        return carry
    lax.fori_loop(0, tm, wbody, 0, unroll=True)
    o_ref[...] = x_ref[...] + gate_ref[0] * ybuf[slot]


def _combine(y_sorted, pos, x2, mods, rows_per_mod, mod_base, tm=256):
    r, d = x2.shape
    n_steps = r // tm
    tpm = rows_per_mod // tm
    pos3 = pos.reshape(n_steps, 1, tm)
    return pl.pallas_call(
        functools.partial(_combine_kernel, tm=tm),
        grid=(n_steps,),
        in_specs=[pl.BlockSpec((1, 1, tm), lambda i: (i, 0, 0), memory_space=pltpu.SMEM),
                  pl.BlockSpec((1, 1, tm), lambda i: (jnp.minimum(i + 1, n_steps - 1), 0, 0),
                               memory_space=pltpu.SMEM),
                  pl.BlockSpec(memory_space=pl.ANY),
                  pl.BlockSpec((tm, d), lambda i: (i, 0)),
                  pl.BlockSpec((1, 1, d), lambda i: ((mod_base + i // tpm) * N_MOD + 5, 0, 0))],
        out_specs=pl.BlockSpec((tm, d), lambda i: (i, 0)),
        out_shape=jax.ShapeDtypeStruct((r, d), F32),
        scratch_shapes=[pltpu.VMEM((2, tm, d), F32), pltpu.SemaphoreType.DMA((2,))],
        compiler_params=_cparams(("arbitrary",)),
        name="moe_combine",
    )(pos3, pos3, y_sorted, x2, mods)


def _sorted_positions(gids, tm, p_rows):
    g = jnp.concatenate(gids)
    r = g.shape[0]
    epg = EXPERTS_PER_GROUP
    n_pairs = epg * (epg - 1) // 2
    n_cls = N_GROUPS * n_pairs
    onehot = (g[:, None] == jnp.arange(n_cls, dtype=I32)[None, :]).astype(I32)
    counts = jnp.sum(onehot, axis=0)
    rank = jnp.sum((jnp.cumsum(onehot, axis=0) - 1) * onehot, axis=1)
    padded = ((counts + tm - 1) // tm) * tm
    ends = jnp.cumsum(padded)
    pos = jnp.sum(onehot * (ends - padded)[None, :], axis=1) + rank
    p = r + n_cls * tm if p_rows is None else p_rows
    assert p >= r + n_cls * tm and p % tm == 0
    tile_start = jnp.arange(p // tm, dtype=I32) * tm
    tile_cls = jnp.minimum(jnp.sum((tile_start[:, None] >= ends[None, :]).astype(I32), axis=1), n_cls - 1)
    tile_ok = tile_start < ends[-1]
    tile_valid = jnp.concatenate([tile_ok.astype(I32), (ends[-1:] // tm - 1).astype(I32)])
    pairs = [(i, j) for i in range(epg) for j in range(i + 1, epg)]
    pair_lo = jnp.array([a for a, _ in pairs], I32)
    pair_hi = jnp.array([b for _, b in pairs], I32)
    grp = tile_cls // n_pairs
    tile_expert = jnp.stack([grp * epg + pair_lo[tile_cls % n_pairs],
                             grp * epg + pair_hi[tile_cls % n_pairs]], axis=1)
    tile_expert = jnp.where(tile_ok[:, None], tile_expert, 0).reshape(-1)
    return pos.astype(I32), tile_expert.astype(I32), tile_valid, p


def _moe(fexts, gids, w1, w3, w2, layer, f_sorted=None, tm=512):
    pos, tile_expert, tile_valid, p = _sorted_positions(gids, tm, None if f_sorted is None else f_sorted.shape[0])
    sizes = [f.shape[0] for f in fexts]
    poss, off = [], 0
    for s in sizes:
        poss.append(lax.slice(pos, (off,), (off + s,)))
        off += s
    if f_sorted is None:
        f_sorted = jnp.zeros((p, fexts[0].shape[1]), F32)
    for fext, ps in zip(fexts, poss):
        f_sorted = _dispatch(fext, ps, f_sorted)
    y_sorted = _moe_ffn(tile_expert, tile_valid, f_sorted, w1, w3, w2, layer, tm)
    return y_sorted, poss, f_sorted


def kernel(x, c, ctx, c_ctx, ada_w, ada_b, norm_mix, norm_ffn, even_w_in, even_w_out, hgrn_lb_logits, hgrn_out_norm, diff_qk_norm, diff_lambda, diff_out_norm, odd_w_in, odd_conv_w, odd_conv_b, rg_gate_w, rg_gate_b, rg_lambda, odd_w_out, moe_w_grp, moe_b_grp, moe_w_exp, moe_b_exp, moe_w1, moe_w3, moe_w2):
    bsz, n, d = x.shape
    n_ctx = ctx.shape[1]
    depth = ada_w.shape[0]
    assert depth == 2

    cvec = jnp.zeros((MOD_ROWS, d), F32).at[:bsz].set(c).at[bsz].set(c_ctx)
    mods_all = _ada_all(cvec, ada_w, ada_b).reshape(depth, MOD_ROWS * N_MOD, 1, d)
    x_l = x.reshape(bsz * n, d)
    x_c = ctx.reshape(bsz * n_ctx, d)
    lb_all = jnp.cumsum(jax.nn.softmax(hgrn_lb_logits.astype(F32), axis=1), axis=1)

    def router(l):
        w = jnp.concatenate([moe_w_grp[l], moe_w_exp[l]], axis=1)
        b = jnp.concatenate([moe_b_grp[l], moe_b_exp[l]])
        padw = ROUTE_W - w.shape[1]
        return jnp.pad(w, ((0, 0), (0, padw))), jnp.pad(b, (0, padw)).reshape(1, ROUTE_W)

    l = 0
    mods = mods_all[l]
    lam_init = 0.8 - 0.6 * math.exp(-0.3 * l)
    lv = diff_lambda[0].astype(F32)
    lam = jnp.exp(jnp.sum(lv[0] * lv[1])) - jnp.exp(jnp.sum(lv[2] * lv[3])) + lam_init
    w_in = even_w_in
    proj_l = _norm_mod_mm(x_l, norm_mix[l], mods, w_in, n, 0, tm=1024)
    proj_c = _norm_mod_mm(x_c, norm_mix[l], mods, w_in, bsz * n_ctx, bsz, tm=1024)
    a_l, a_c = _hgrn(proj_l, proj_c, lb_all[0, 0], lb_all[1, 0], hgrn_out_norm[0], bsz)
    b_l, b_c = _attn(proj_l, proj_c, diff_qk_norm[0], diff_out_norm[0], lam, lam_init, bsz)
    w_out = even_w_out[0].astype(BF16)
    wr, br = router(l)
    x_l, f_l, g_l = _out_proj([a_l, b_l], w_out, x_l, norm_ffn[l], mods, wr, br, n, 0)
    x_c, f_c, g_c = _out_proj([a_c, b_c], w_out, x_c, norm_ffn[l], mods, wr, br, bsz * n_ctx, bsz)
    y_sorted, (pos_l, pos_c), f_sorted = _moe([f_l, f_c], [g_l, g_c], moe_w1, moe_w3, moe_w2, l)
    x_l = _combine(y_sorted, pos_l, x_l, mods, n, 0)
    x_c = _combine(y_sorted, pos_c, x_c, mods, bsz * n_ctx, bsz)

    l = 1
    mods = mods_all[l]
    w_in = odd_w_in
    proj_l = _norm_mod_mm(x_l, norm_mix[l], mods, w_in, n, 0, tm=1024)
    rg_w = w_in.shape[2] // 2
    u_c = _norm_mod_mm(x_c, norm_mix[l], mods, w_in, bsz * n_ctx, bsz, tm=1024, cols=(rg_w, rg_w))
    gated = _rglru(proj_l, u_c,odd_conv_w[0], odd_conv_b[0], rg_gate_w[0], rg_gate_b[0], rg_lambda[0], bsz)
    wr, br = router(l)
    x_l, f_l, g_l = _out_proj([gated], odd_w_out[0].astype(BF16), x_l, norm_ffn[l], mods, wr, br, n, 0)
    y_sorted, (pos_l,), _ = _moe([f_l], [g_l], moe_w1, moe_w3, moe_w2, l, f_sorted)
    return _combine(y_sorted, pos_l, x_l, mods, n, 0).reshape(bsz, n, d)
```

```python
import functools
import math

import jax
import jax.numpy as jnp
from jax import lax
from jax.experimental import pallas as pl
from jax.experimental.pallas import tpu as pltpu

F32 = jnp.float32
BF16 = jnp.bfloat16
I32 = jnp.int32

EPS = 1e-6
LOG2_E = 1.4426950408889634
TINY = 1e-30
GELU_TANH_SCALE = math.sqrt(2.0 / math.pi)
GELU_TANH_CUBIC = 0.044715
LANES = 128
SUBLANES = 8
BF16_SUBLANES = 16
GRID_W = 64
A_HEADS = 8
HEAD_W = 128
HGRN_CHUNK = 32
HGRN_BLOCK = 256
HGRN_GROUP = 3
B_HEADS = 8
B_DH = 64
ROPE_BASE = 10000.0
ATTN_KEY_CHUNK = 256
ATTN_SKEW = 3
RG_HEADS = 16
CONV_W = 4
RG_C = 8.0
N_GROUPS = 4
EXPERTS_PER_GROUP = 4
N_MOD = 6
MOD_ROWS = 16
ROUTE_W = LANES

V7X_VMEM_BYTES = 64 * 1024 * 1024
VMEM_LIMIT = V7X_VMEM_BYTES * 7 // 8


def _cparams(sem):
    return pltpu.CompilerParams(dimension_semantics=sem, vmem_limit_bytes=VMEM_LIMIT)


def _sigmoid(x):
    return 0.5 * jnp.tanh(0.5 * x) + 0.5


def _dot(a, b):
    return jnp.dot(a, b, preferred_element_type=F32)


def _dot_nt(a, b):
    return lax.dot_general(a, b, (((1,), (1,)), ((), ())), preferred_element_type=F32)


def _split_bf16(x):
    hi = x.astype(BF16)
    return hi, (x - hi.astype(F32)).astype(BF16)


def _ada_kernel(c_ref, w_ref, b_ref, o_ref):
    c = c_ref[...]
    a = (c * _sigmoid(c)).astype(BF16)
    o_ref[...] = _dot(a, w_ref[...].astype(BF16)) + b_ref[...]


def _ada_all(cvec, ada_w, ada_b, tn=1024):
    depth, d, n = ada_w.shape
    return pl.pallas_call(
        _ada_kernel,
        grid=(depth, n // tn),
        in_specs=[pl.BlockSpec((MOD_ROWS, d), lambda l, j: (0, 0)),
                  pl.BlockSpec((None, d, tn), lambda l, j: (l, 0, j)),
                  pl.BlockSpec((None, 1, tn), lambda l, j: (l, 0, j))],
        out_specs=pl.BlockSpec((None, MOD_ROWS, tn), lambda l, j: (l, 0, j)),
        out_shape=jax.ShapeDtypeStruct((depth, MOD_ROWS, n), F32),
        compiler_params=_cparams(("arbitrary", "arbitrary")),
        name="ada_mod",
    )(cvec, ada_w, ada_b.reshape(depth, 1, n))


def _norm_mod_mm_kernel(x_ref, g_ref, sh_ref, sc_ref, w_ref, o_ref, h_scr):
    @pl.when(pl.program_id(1) == 0)
    def _():
        x = x_ref[...]
        y = x * lax.rsqrt(jnp.mean(x * x, axis=-1, keepdims=True) + EPS) * g_ref[...]
        h_scr[...] = (y * (1.0 + sc_ref[0]) + sh_ref[0]).astype(BF16)

    o_ref[...] = _dot(h_scr[...], w_ref[...].astype(BF16)).astype(o_ref.dtype)


def _norm_mod_mm(x2, gain, mods, w, rows_per_mod, mod_base, tm, tn=1024, cols=None):
    r, d = x2.shape
    col0, n = (0, w.shape[2]) if cols is None else cols
    jb = col0 // tn
    tpm = rows_per_mod // tm

    def mrow(i):
        return (mod_base + i // tpm) * N_MOD

    return pl.pallas_call(
        _norm_mod_mm_kernel,
        grid=(r // tm, n // tn),
        in_specs=[pl.BlockSpec((tm, d), lambda i, j: (i, 0)),
                  pl.BlockSpec((1, d), lambda i, j: (0, 0)),
                  pl.BlockSpec((1, 1, d), lambda i, j: (mrow(i) + 0, 0, 0)),
                  pl.BlockSpec((1, 1, d), lambda i, j: (mrow(i) + 1, 0, 0)),
                  pl.BlockSpec((None, d, tn), lambda i, j: (0, 0, jb + j))],
        out_specs=pl.BlockSpec((tm, tn), lambda i, j: (i, j)),
        out_shape=jax.ShapeDtypeStruct((r, n), F32),
        scratch_shapes=[pltpu.VMEM((tm, d), BF16)],
        compiler_params=_cparams(("parallel", "arbitrary")),
        name="norm_mod_mm",
    )(x2, gain.reshape(1, d), mods, mods, w)


def _hgrn_kernel(ql, ffl, fbl, vl, gl, qc, ffc, fbc, vc, gc, lbf_ref, lbb_ref, gain_ref,
                 ol_ref, oc_ref,
                 oi_l, oi_c, qtf_l, qtb_l, qtf_c, qtb_c, kv_f, kv_b, dec_f, dec_b):
    c_sz = HGRN_CHUNK
    blk = HGRN_BLOCK
    cpb = blk // c_sz
    nb_l = ql.shape[0] // blk
    nc_l = ql.shape[0] // c_sz
    nc_c = qc.shape[0] // c_sz
    nc = nc_l + nc_c
    assert qc.shape[0] == blk
    row = lax.broadcasted_iota(I32, (blk, blk), 0)
    col = lax.broadcasted_iota(I32, (blk, blk), 1)
    same = (row // c_sz) == (col // c_sz)
    tril = same & (row >= col)
    triu = same & (row <= col)
    lbf = lbf_ref[...]
    lbb = lbb_ref[...]

    exp_mask = (lax.broadcasted_iota(I32, (blk, cpb * HEAD_W), 0) // c_sz
                == lax.broadcasted_iota(I32, (blk, cpb * HEAD_W), 1) // HEAD_W)

    def local_terms(jobs):
        mask = (tril, triu)
        last = (c_sz - 1, 0)
        lbs = (lbf, lbb)
        chains = [(j, d) for j in range(len(jobs)) for d in (0, 1)]
        rows = [pl.ds(job[0], blk) for job in jobs]
        qs, vb, vt = [], [], []
        for job, rw in zip(jobs, rows):
            q = job[1][rw, :]
            qs.append(q * _sigmoid(q))
            vb.append(job[4][rw, :].astype(BF16))
            vt.append(job[4][rw, :].T.astype(BF16))
        f = [lbs[d] + (1.0 - lbs[d]) * _sigmoid(jobs[j][2 + d][rows[j], :]) for j, d in chains]
        k = [1.0 - t for t in f]
        lf2 = [jnp.concatenate(_split_bf16(jnp.log(t)), axis=1) for t in f]
        cum2 = [_dot(jnp.where(mask[d], 1.0, 0.0).astype(BF16), lf2[i]) for i, (j, d) in enumerate(chains)]
        cum = [t[:, :HEAD_W] + t[:, HEAD_W:] for t in cum2]
        tot = [jnp.concatenate(
            [jnp.broadcast_to(cum[i][c * c_sz + last[d]:c * c_sz + last[d] + 1, :], (c_sz, HEAD_W))
             for c in range(cpb)], axis=0) for i, (j, d) in enumerate(chains)]
        e = [jnp.exp(t) for t in cum]
        qt = [(qs[j] * e[i]).astype(BF16) for i, (j, d) in enumerate(chains)]
        kt = [(k[i] * jnp.exp(-cum[i])).astype(BF16) for i in range(len(chains))]
        att = [jnp.where(mask[d], _dot_nt(qt[i], kt[i]), 0.0).astype(BF16) for i, (j, d) in enumerate(chains)]
        for j, job in enumerate(jobs):
            job[5][rows[j], :] = _dot(att[2 * j], vb[j]) + _dot(att[2 * j + 1], vb[j])
        k2 = [(k[i] * jnp.exp(tot[i] - cum[i])).astype(BF16) for i in range(len(chains))]
        k2x = [jnp.where(exp_mask, jnp.concatenate([t] * cpb, axis=1), jnp.zeros((), BF16)) for t in k2]
        kvs = [_dot(vt[j], k2x[i]) for i, (j, d) in enumerate(chains)]
        for i, (j, d) in enumerate(chains):
            kv_ref, dec_ref = ((kv_f, dec_f), (kv_b, dec_b))[d]
            gid = jobs[j][8 + d]
            for c in range(cpb):
                kv_ref[gid + c] = kvs[i][:, c * HEAD_W:(c + 1) * HEAD_W]
                dec_ref[gid + c] = e[i][c * c_sz + last[d]:c * c_sz + last[d] + 1, :]
            jobs[j][6 + d][rows[j], :] = qt[i]

    ctx_job = (0, qc, ffc, fbc, vc, oi_c, qtf_c, qtb_c, 0, nc_l)
    lat_jobs = [(i * blk, ql, ffl, fbl, vl, oi_l, qtf_l, qtb_l, nc_c + i * cpb, i * cpb) for i in range(nb_l)]
    all_jobs = [ctx_job] + lat_jobs
    for g in range(0, len(all_jobs), HGRN_GROUP):
        local_terms(all_jobs[g:g + HGRN_GROUP])

    def rec_f(i, s):
        new = s * dec_f[i] + kv_f[i]
        kv_f[i] = s
        return new

    def rec_b(i, s):
        j = nc - 1 - i
        new = s * dec_b[j] + kv_b[j]
        kv_b[j] = s
        return new

    s0 = jnp.zeros((HEAD_W, HEAD_W), F32)
    lax.fori_loop(0, nc, rec_f, s0, unroll=2)
    lax.fori_loop(0, nc, rec_b, s0, unroll=2)

    gain = gain_ref[...]

    def finish(jobs):
        parts = [[_dot_nt(qtf_ref[pl.ds(r0 + c * c_sz, c_sz), :], kv_f[gid_f + c].astype(BF16))
                  + _dot_nt(qtb_ref[pl.ds(r0 + c * c_sz, c_sz), :], kv_b[gid_b + c].astype(BF16))
                  for c in range(cpb)]
                 for r0, _, _, qtf_ref, qtb_ref, _, gid_f, gid_b in jobs]
        for (r0, g_ref, oi_ref, _, _, o_ref, _, _), part in zip(jobs, parts):
            rows = pl.ds(r0, blk)
            o = oi_ref[rows, :] + jnp.concatenate(part, axis=0)
            y = o * lax.rsqrt(jnp.mean(o * o, axis=-1, keepdims=True) + EPS) * gain
            g = g_ref[rows, :]
            o_ref[rows, :] = (y * (g * _sigmoid(g))).astype(o_ref.dtype)

    fin_jobs = ([(0, gc, oi_c, qtf_c, qtb_c, oc_ref, 0, nc_l)]
                + [(i * blk, gl, oi_l, qtf_l, qtb_l, ol_ref, nc_c + i * cpb, i * cpb) for i in range(nb_l)])
    for g0 in range(0, len(fin_jobs), HGRN_GROUP):
        finish(fin_jobs[g0:g0 + HGRN_GROUP])


def _hgrn(proj_l, proj_c, lb_f, lb_b, gain, bsz):
    n = proj_l.shape[0] // bsz
    n_ctx = proj_c.shape[0] // bsz
    nc = (n + n_ctx) // HGRN_CHUNK
    w = A_HEADS * HEAD_W

    def col(k):
        return lambda b, h: (b, k * A_HEADS + h)

    in_specs = ([pl.BlockSpec((n, HEAD_W), col(k)) for k in range(5)]
                + [pl.BlockSpec((n_ctx, HEAD_W), col(k)) for k in range(5)]
                + [pl.BlockSpec((1, HEAD_W), lambda b, h: (0, h)),
                   pl.BlockSpec((1, HEAD_W), lambda b, h: (0, h)),
                   pl.BlockSpec((1, HEAD_W), lambda b, h: (0, 0))])
    return pl.pallas_call(
        _hgrn_kernel,
        grid=(bsz, A_HEADS),
        in_specs=in_specs,
        out_specs=[pl.BlockSpec((n, HEAD_W), lambda b, h: (b, h)),
                   pl.BlockSpec((n_ctx, HEAD_W), lambda b, h: (b, h))],
        out_shape=[jax.ShapeDtypeStruct((bsz * n, w), BF16),
                   jax.ShapeDtypeStruct((bsz * n_ctx, w), BF16)],
        scratch_shapes=[pltpu.VMEM((n, HEAD_W), F32), pltpu.VMEM((n_ctx, HEAD_W), F32),
                        pltpu.VMEM((n, HEAD_W), BF16), pltpu.VMEM((n, HEAD_W), BF16),
                        pltpu.VMEM((n_ctx, HEAD_W), BF16), pltpu.VMEM((n_ctx, HEAD_W), BF16),
                        pltpu.VMEM((nc, HEAD_W, HEAD_W), F32), pltpu.VMEM((nc, HEAD_W, HEAD_W), F32),
                        pltpu.VMEM((nc, 1, HEAD_W), F32), pltpu.VMEM((nc, 1, HEAD_W), F32)],
        compiler_params=_cparams(("parallel", "parallel")),
        name="hgrn2",
    )(*([proj_l] * 5 + [proj_c] * 5), lb_f.reshape(1, w), lb_b.reshape(1, w), gain.reshape(1, HEAD_W))


def _half_mean_matrix():
    r = lax.broadcasted_iota(I32, (LANES, LANES), 0) // B_DH
    c = lax.broadcasted_iota(I32, (LANES, LANES), 1) // B_DH
    return (r == c).astype(BF16)


def _qk_prep(t, gain, bd, cos, sin):
    sq_hi, sq_lo = _split_bf16(t * t)
    ms = (_dot(sq_hi, bd) + _dot(sq_lo, bd)) * (1.0 / B_DH)
    y = t * lax.rsqrt(ms + EPS) * gain
    if cos is None:
        return y
    lane = lax.broadcasted_iota(I32, y.shape, 1)
    partner = jnp.where(lane % 2 == 0, pltpu.roll(y, LANES - 1, 1), pltpu.roll(y, 1, 1))
    return y * cos + partner * sin


def _attn_kernel(q_l, q_c, k_l, k_c, v_l, v_c, cosq, sinq, cosk, sink, gq_ref, gk_ref, go_ref, lam_ref,
                 o_l, o_c, kp, vp, s_even, s_odd, m_even, m_odd, *, out_scale):
    step = pl.program_id(2)
    n_ctx = k_c.shape[0]
    s_bufs = (s_even, s_odd)
    m_bufs = (m_even, m_odd)
    bd = _half_mean_matrix()
    lane = lax.broadcasted_iota(I32, (1, LANES), 1)
    masks = ((lane < B_DH).astype(F32), (lane >= B_DH).astype(F32))
    lam = lam_ref[:, 0:1]
    q_scale = B_DH ** -0.5 * LOG2_E

    def masked_q(q):
        q = q * q_scale
        return [(q * msk).astype(BF16) for msk in masks]

    def epilogue(r):
        o = (r[0][0:HEAD_W, :] * (1.0 / r[0][HEAD_W:HEAD_W + 1, :])
             - r[1][0:HEAD_W, :] * (lam / r[1][HEAD_W:HEAD_W + 1, :]))
        gain = jnp.concatenate([go_ref[...]] * (o.shape[1] // LANES), axis=1)
        y = o * lax.rsqrt(jnp.mean(o * o, axis=0, keepdims=True) + EPS) * gain
        return (y * out_scale).T.astype(BF16)

    @pl.when((step == 0) & (pl.program_id(0) == 0) & (pl.program_id(1) == 0))
    def _():
        s_odd[...] = jnp.zeros_like(s_odd)
        m_odd[...] = jnp.zeros_like(m_odd)

    @pl.when(step == 0)
    def _():
        kp[0:n_ctx, :] = _qk_prep(k_c[...], gk_ref[...], bd, None, None).astype(BF16)
        kp[n_ctx:, :] = _qk_prep(k_l[...], gk_ref[...], bd, cosk[...], sink[...]).astype(BF16)
        vp[0:HEAD_W, 0:n_ctx] = v_c[...].T.astype(BF16)
        vp[0:HEAD_W, n_ctx:] = v_l[...].T.astype(BF16)
        pad_rows = vp.shape[0] - HEAD_W
        ones_row = (lax.broadcasted_iota(I32, (pad_rows, 1), 0) == 0).astype(BF16)
        vp[HEAD_W:, :] = jnp.broadcast_to(ones_row, (pad_rows, vp.shape[1]))
        qm = masked_q(_qk_prep(q_c[...], gq_ref[...], bd, None, None))
        s_ctx = [_dot_nt(kp[0:n_ctx, :], t) for t in qm]
        e_ctx = [jnp.exp2(t - jnp.max(t, axis=0, keepdims=True)).astype(BF16) for t in s_ctx]
        o_c[...] = epilogue([_dot(vp[:, 0:n_ctx], t) for t in e_ctx])

    kc = ATTN_KEY_CHUNK
    for parity in (0, 1):
        @pl.when((step > 0) & (step % 2 == parity))
        def _(old=parity, new=1 - parity):
            qm = masked_q(_qk_prep(q_l[...], gq_ref[...], bd, cosq[...], sinq[...]))
            mx_old = [m_bufs[old][i, 0:1, :] for i in range(2)]
            acc = [None, None]
            mx_new = [None, None]
            n_chunks = kp.shape[0] // kc
            for c in range(n_chunks + ATTN_SKEW):
                if c < n_chunks:
                    ks = slice(c * kc, (c + 1) * kc)
                    for i in range(2):
                        e = jnp.exp2(s_bufs[old][i, ks, :] - mx_old[i]).astype(BF16)
                        part = _dot(vp[:, ks], e)
                        acc[i] = part if acc[i] is None else acc[i] + part
                if c == n_chunks - 1:
                    o_l[...] = epilogue(acc)
                if c >= ATTN_SKEW:
                    ks = slice((c - ATTN_SKEW) * kc, (c - ATTN_SKEW + 1) * kc)
                    for i in range(2):
                        t = _dot_nt(kp[ks, :], qm[i])
                        s_bufs[new][i, ks, :] = t
                        tm = jnp.max(t, axis=0, keepdims=True)
                        mx_new[i] = tm if mx_new[i] is None else jnp.maximum(mx_new[i], tm)
            for i in range(2):
                m_bufs[new][i] = jnp.broadcast_to(mx_new[i], m_bufs[new].shape[1:])


def _rope_tables(n):
    n_rows = n // GRID_W
    rowp = jnp.repeat(jnp.arange(n_rows), GRID_W).astype(F32)
    colp = jnp.tile(jnp.arange(GRID_W), n_rows).astype(F32)
    pairs = B_DH // 4
    inv = ROPE_BASE ** (-jnp.arange(pairs, dtype=F32) / pairs)
    ang = jnp.concatenate([rowp[:, None] * inv, colp[:, None] * inv], axis=-1)
    cos = jnp.repeat(jnp.cos(ang), 2, axis=-1)
    sin = jnp.repeat(jnp.sin(ang), 2, axis=-1) * jnp.tile(jnp.array([-1.0, 1.0], F32), B_DH // 2)
    return jnp.tile(cos, (1, 2)), jnp.tile(sin, (1, 2))


def _attn(proj_l, proj_c, qk_gain, out_gain, lam, lam_init, bsz, tq=512):
    n = proj_l.shape[0] // bsz
    n_ctx = proj_c.shape[0] // bsz
    nqb = n // tq
    w = B_HEADS * HEAD_W
    cos, sin = _rope_tables(n)
    qcol, kcol, vcol = 5 * A_HEADS, 5 * A_HEADS + B_HEADS, 5 * A_HEADS + 2 * B_HEADS

    def q_blk(t):
        return jnp.clip(t - 1, 0, nqb - 1)

    def o_blk(t):
        return jnp.clip(t - 2, 0, nqb - 1)

    def lat_q(b, h, t):
        return (b * nqb + q_blk(t), qcol + h)

    in_specs = [pl.BlockSpec((tq, HEAD_W), lat_q),
                pl.BlockSpec((n_ctx, HEAD_W), lambda b, h, qb: (b, qcol + h)),
                pl.BlockSpec((n, HEAD_W), lambda b, h, qb: (b, kcol + h)),
                pl.BlockSpec((n_ctx, HEAD_W), lambda b, h, qb: (b, kcol + h)),
                pl.BlockSpec((n, HEAD_W), lambda b, h, qb: (b, vcol + h)),
                pl.BlockSpec((n_ctx, HEAD_W), lambda b, h, qb: (b, vcol + h)),
                pl.BlockSpec((tq, LANES), lambda b, h, t: (q_blk(t), 0)),
                pl.BlockSpec((tq, LANES), lambda b, h, t: (q_blk(t), 0)),
                pl.BlockSpec((n, LANES), lambda b, h, qb: (0, 0)),
                pl.BlockSpec((n, LANES), lambda b, h, qb: (0, 0)),
                pl.BlockSpec((1, LANES), lambda b, h, qb: (0, 0)),
                pl.BlockSpec((1, LANES), lambda b, h, qb: (0, 0)),
                pl.BlockSpec((HEAD_W, LANES), lambda b, h, qb: (0, 0)),
                pl.BlockSpec((1, LANES), lambda b, h, qb: (0, 0))]
    return pl.pallas_call(
        functools.partial(_attn_kernel, out_scale=1.0 - lam_init),
        grid=(bsz, B_HEADS, nqb + 2),
        in_specs=in_specs,
        out_specs=[pl.BlockSpec((tq, HEAD_W), lambda b, h, t: (b * nqb + o_blk(t), h)),
                   pl.BlockSpec((n_ctx, HEAD_W), lambda b, h, qb: (b, h))],
        out_shape=[jax.ShapeDtypeStruct((bsz * n, w), BF16),
                   jax.ShapeDtypeStruct((bsz * n_ctx, w), BF16)],
        scratch_shapes=[pltpu.VMEM((n + n_ctx, HEAD_W), BF16),
                        pltpu.VMEM((HEAD_W + BF16_SUBLANES, n + n_ctx), BF16),
                        pltpu.VMEM((2, n + n_ctx, tq), F32), pltpu.VMEM((2, n + n_ctx, tq), F32),
                        pltpu.VMEM((2, SUBLANES, tq), F32), pltpu.VMEM((2, SUBLANES, tq), F32)],
        compiler_params=_cparams(("arbitrary", "arbitrary", "arbitrary")),
        name="diff_attn",
    )(proj_l, proj_c, proj_l, proj_c, proj_l, proj_c, cos, sin, cos, sin,
      jnp.tile(qk_gain[0], 2).reshape(1, LANES), jnp.tile(qk_gain[1], 2).reshape(1, LANES),
      jnp.broadcast_to(out_gain[:, None], (HEAD_W, LANES)), jnp.full((1, LANES), lam, F32))


def _scan_steps(a, b, reverse, axis):
    n = a.shape[axis]
    pos = lax.broadcasted_iota(I32, a.shape, axis)
    s = 1
    while s < n:
        keep = (pos < n - s) if reverse else (pos >= s)
        shift = n - s if reverse else s
        a_sh = jnp.where(keep, pltpu.roll(a, shift, axis), 1.0)
        b_sh = jnp.where(keep, pltpu.roll(b, shift, axis), 0.0)
        b = a * b_sh + b
        a = a * a_sh
        s *= 2
    return a, b


def _rglru_kernel(y_l, u_l, u_c, cw_ref, cb_ref, gw_ref, gb_ref, lam_ref, o_ref,
                  upad, a_f, b_f, a_b, b_b, h_f, p_f, h_b, p_b, *, seg_len, pitch):
    n = u_l.shape[0]
    n_ctx = u_c.shape[0]
    nseg = SUBLANES
    cw = cw_ref[...]
    cb = cb_ref[...]
    pad = SUBLANES

    def pieces(start, rows):
        out = []
        for r in range(nseg):
            lo, hi = max(start, r * seg_len), min(start + rows, (r + 1) * seg_len)
            if lo < hi:
                out.append((r, lo - r * seg_len, lo - start, hi - lo))
        return out

    def put(ref, start, val):
        for r, off, src, ln in pieces(start, val.shape[0]):
            ref[r * pitch + off:r * pitch + off + ln, :] = val[src:src + ln, :]

    def conv(u_ref, rows):
        upad[0:pad, :] = jnp.zeros((pad, LANES), F32)
        upad[pad:pad + rows, :] = u_ref[...]
        upad[pad + rows:pad + rows + pad, :] = jnp.zeros((pad, LANES), F32)
        acc = cb + jnp.zeros((rows, LANES), F32)
        for j in range(CONV_W):
            off = pad + j - CONV_W // 2
            acc = acc + cw[j:j + 1, :] * upad[off:off + rows, :]
        return acc

    def gates(uc, start_f, start_b):
        ub = uc.astype(BF16)
        for d, (a_ref, b_ref, start) in enumerate(((a_f, b_f, start_f), (a_b, b_b, start_b))):
            lam = lam_ref[d:d + 1, :]
            neg_sp = -(jnp.maximum(-lam, 0.0) + jnp.log(1.0 + jnp.exp(-jnp.abs(lam))))
            r = _sigmoid(_dot(ub, gw_ref[d, 0].astype(BF16)) + gb_ref[d, 0:1, :])
            i = _sigmoid(_dot(ub, gw_ref[d, 1].astype(BF16)) + gb_ref[d, 1:2, :])
            a = jnp.exp((RG_C * neg_sp) * r)
            put(a_ref, start, a)
            x = (1.0 - a) * (1.0 + a)
            put(b_ref, start, (x * lax.rsqrt(jnp.maximum(x, TINY))) * (i * uc))

    gates(conv(u_c, n_ctx), 0, n)
    gates(conv(u_l, n), n_ctx, 0)

    def step(i, carry):
        hf, pf, hb, pb = carry
        rows_f = pl.ds(i, nseg, stride=pitch)
        rows_b = pl.ds(seg_len - 1 - i, nseg, stride=pitch)
        af, ab = a_f[rows_f, :], a_b[rows_b, :]
        hf = af * hf + b_f[rows_f, :]
        hb = ab * hb + b_b[rows_b, :]
        pf = pf * af
        pb = pb * ab
        h_f[rows_f, :] = hf
        p_f[rows_f, :] = pf
        h_b[rows_b, :] = hb
        p_b[rows_b, :] = pb
        return hf, pf, hb, pb

    zero = jnp.zeros((nseg, LANES), F32)
    one = jnp.ones((nseg, LANES), F32)
    hf, pf, hb, pb = lax.fori_loop(0, seg_len, step, (zero, one, zero, one), unroll=8)

    seg = lax.broadcasted_iota(I32, (nseg, LANES), 0)
    _, ef = _scan_steps(pf, hf, False, 0)
    _, eb = _scan_steps(pb, hb, True, 0)
    carry_f = jnp.where(seg >= 1, pltpu.roll(ef, 1, 0), 0.0)
    carry_b = jnp.where(seg < nseg - 1, pltpu.roll(eb, nseg - 1, 0), 0.0)

    cuts = sorted({0, n} | {k * seg_len for k in range(nseg + 1) if 0 < k * seg_len < n}
                  | {k * seg_len - n_ctx for k in range(nseg + 1) if 0 < k * seg_len - n_ctx < n})
    for i0, i1 in zip(cuts[:-1], cuts[1:]):
        ln = i1 - i0
        (rf, of, _, _), = pieces(n_ctx + i0, ln)
        (rb, ob, _, _), = pieces(i0, ln)
        sf = slice(rf * pitch + of, rf * pitch + of + ln)
        sb = slice(rb * pitch + ob, rb * pitch + ob + ln)
        h = (h_f[sf, :] + p_f[sf, :] * carry_f[rf:rf + 1, :]) + (h_b[sb, :] + p_b[sb, :] * carry_b[rb:rb + 1, :])
        y = y_l[i0:i1, :]
        gelu = 0.5 * y * (1.0 + jnp.tanh(GELU_TANH_SCALE * (y + GELU_TANH_CUBIC * y * y * y)))
        o_ref[i0:i1, :] = (h * gelu).astype(o_ref.dtype)


def _rglru(proj_l, u_c, conv_w, conv_b, gate_w, gate_b, lam, bsz):
    n = proj_l.shape[0] // bsz
    n_ctx = u_c.shape[0] // bsz
    w = RG_HEADS * HEAD_W
    seg_len = (n + n_ctx) // SUBLANES
    assert seg_len * SUBLANES == n + n_ctx and seg_len % SUBLANES == 0
    assert n_ctx % SUBLANES == 0 and n_ctx <= seg_len
    pitch = seg_len + SUBLANES
    return pl.pallas_call(
        functools.partial(_rglru_kernel, seg_len=seg_len, pitch=pitch),
        grid=(bsz, RG_HEADS),
        in_specs=[pl.BlockSpec((n, HEAD_W), lambda b, h: (b, h)),
                  pl.BlockSpec((n, HEAD_W), lambda b, h: (b, RG_HEADS + h)),
                  pl.BlockSpec((n_ctx, HEAD_W), lambda b, h: (b, h)),
                  pl.BlockSpec((CONV_W, HEAD_W), lambda b, h: (0, h)),
                  pl.BlockSpec((1, HEAD_W), lambda b, h: (0, h)),
                  pl.BlockSpec((2, 2, None, HEAD_W, HEAD_W), lambda b, h: (0, 0, h, 0, 0)),
                  pl.BlockSpec((2, 2, HEAD_W), lambda b, h: (0, 0, h)),
                  pl.BlockSpec((2, HEAD_W), lambda b, h: (0, h))],
        out_specs=pl.BlockSpec((n, HEAD_W), lambda b, h: (b, h)),
        out_shape=jax.ShapeDtypeStruct((bsz * n, w), BF16),
        scratch_shapes=[pltpu.VMEM((n + 2 * SUBLANES, LANES), F32)]
                       + [pltpu.VMEM((SUBLANES * pitch, LANES), F32)] * 8,
        compiler_params=_cparams(("parallel", "parallel")),
        name="rglru",
    )(proj_l, proj_l, u_c, conv_w, conv_b.reshape(1, w), gate_w, gate_b, lam)


def _out_proj_kernel(*refs, n_mix, n_groups, epg, row_chunk):
    mix_refs = refs[:n_mix]
    (w_ref, x_ref, gate_ref, g_ref, sh_ref, sc_ref, wr_ref, rb_ref,
     xo_ref, f_ref, gid_ref) = refs[n_mix:]
    d = x_ref.shape[1]
    n_pairs = epg * (epg - 1) // 2
    for r0 in range(0, x_ref.shape[0], row_chunk):
        rows = slice(r0, r0 + row_chunk)
        k0 = 0
        mix = None
        for m_ref in mix_refs:
            kk = m_ref.shape[1]
            part = _dot(m_ref[rows, :], w_ref[k0:k0 + kk, :])
            mix = part if mix is None else mix + part
            k0 += kk
        x = x_ref[rows, :] + gate_ref[0] * mix
        xo_ref[rows, :] = x
        y = x * lax.rsqrt(jnp.mean(x * x, axis=-1, keepdims=True) + EPS) * g_ref[...]
        f = y * (1.0 + sc_ref[0]) + sh_ref[0]
        f_ref[rows, 0:d] = f
        f_hi, f_lo = _split_bf16(f)

        hh = _dot(f_hi, wr_ref[...])
        lg = hh[:, 0:ROUTE_W] + hh[:, ROUTE_W:] + _dot(f_lo, wr_ref[:, 0:ROUTE_W]) + rb_ref[...]
        lane = lax.broadcasted_iota(I32, lg.shape, 1).astype(F32)
        neg = -jnp.inf
        big = float(ROUTE_W)
        gl = jnp.where(lane < n_groups, lg, neg)
        gmax = jnp.max(gl, axis=-1, keepdims=True)
        gidx = jnp.min(jnp.where(gl == gmax, lane, big), axis=-1, keepdims=True)
        gw = 1.0 / jnp.sum(jnp.exp(gl - gmax), axis=-1, keepdims=True)
        base = n_groups + gidx * epg
        el = jnp.where((lane >= base) & (lane < base + epg), lg, neg)
        v1 = jnp.max(el, axis=-1, keepdims=True)
        i1 = jnp.min(jnp.where(el == v1, lane, big), axis=-1, keepdims=True)
        el2 = jnp.where(lane == i1, neg, el)
        v2 = jnp.max(el2, axis=-1, keepdims=True)
        i2 = jnp.min(jnp.where(el2 == v2, lane, big), axis=-1, keepdims=True)
        t = jnp.exp(v2 - v1)
        w1 = gw / (1.0 + t)
        w2 = w1 * t
        s1 = i1 - base
        s2 = i2 - base
        lo = jnp.minimum(s1, s2)
        hi = jnp.maximum(s1, s2)
        pair = lo * (2 * epg - 1 - lo) * 0.5 + (hi - lo - 1.0)
        w_lo = jnp.where(s1 < s2, w1, w2)
        w_hi = jnp.where(s1 < s2, w2, w1)
        f_ref[rows, d:] = jnp.where(lane == 0.0, w_lo, 0.0) + jnp.where(lane == 1.0, w_hi, 0.0)
        cls = jnp.broadcast_to(gidx * n_pairs + pair, (row_chunk, LANES)).T
        gid_ref[:, rows] = cls[0:SUBLANES, :].astype(I32)


def _out_proj(mixes, w_out, x2, gain, mods, w_router, b_router, rows_per_mod, mod_base, tm=512):
    r, d = x2.shape
    k = w_out.shape[0]
    tpm = rows_per_mod // tm
    wr = jnp.concatenate(_split_bf16(w_router), axis=1)

    def mrow(i):
        return (mod_base + i // tpm) * N_MOD

    const2 = lambda i: (0, 0)
    in_specs = ([pl.BlockSpec((tm, m.shape[1]), lambda i: (i, 0)) for m in mixes]
                + [pl.BlockSpec((k, d), const2),
                   pl.BlockSpec((tm, d), lambda i: (i, 0)),
                   pl.BlockSpec((1, 1, d), lambda i: (mrow(i) + 2, 0, 0)),
                   pl.BlockSpec((1, d), const2),
                   pl.BlockSpec((1, 1, d), lambda i: (mrow(i) + 3, 0, 0)),
                   pl.BlockSpec((1, 1, d), lambda i: (mrow(i) + 4, 0, 0)),
                   pl.BlockSpec((d, 2 * ROUTE_W), const2),
                   pl.BlockSpec((1, ROUTE_W), const2)])
    x_new, fext, cls = pl.pallas_call(
        functools.partial(_out_proj_kernel, n_mix=len(mixes), n_groups=N_GROUPS, epg=EXPERTS_PER_GROUP,
                          row_chunk=tm),
        grid=(r // tm,),
        in_specs=in_specs,
        out_specs=[pl.BlockSpec((tm, d), lambda i: (i, 0)),
                   pl.BlockSpec((tm, d + ROUTE_W), lambda i: (i, 0)),
                   pl.BlockSpec((SUBLANES, tm), lambda i: (i, 0))],
        out_shape=[jax.ShapeDtypeStruct((r, d), F32),
                   jax.ShapeDtypeStruct((r, d + ROUTE_W), F32),
                   jax.ShapeDtypeStruct((r // tm * SUBLANES, tm), I32)],
        compiler_params=_cparams(("parallel",)),
        name="out_proj",
    )(*mixes, w_out, x2, mods, gain.reshape(1, d), mods, mods, wr, b_router)
    return x_new, fext, cls.reshape(r // tm, SUBLANES, tm)[:, 0, :].reshape(r)


def _row_copy(src, s, dst, t, sem):
    return pltpu.make_async_copy(src.at[pl.ds(s, 1), :], dst.at[pl.ds(t, 1), :], sem)


def _dispatch_kernel(pos_ref, src_ref, dst_in, dst, sem, *, tm):
    del dst_in

    for j in range(tm):
        _row_copy(src_ref, j, dst, pos_ref[0, 0, j], sem).start()

    def wbody(j, carry):
        _row_copy(src_ref, 0, dst, 0, sem).wait()
        return carry
    lax.fori_loop(0, tm, wbody, 0, unroll=True)


def _dispatch(src, pos, sorted_in, tm=512):
    r, w = src.shape
    return pl.pallas_call(
        functools.partial(_dispatch_kernel, tm=tm),
        grid=(r // tm,),
        in_specs=[pl.BlockSpec((1, 1, tm), lambda i: (i, 0, 0), memory_space=pltpu.SMEM),
                  pl.BlockSpec((tm, w), lambda i: (i, 0)),
                  pl.BlockSpec(memory_space=pl.ANY)],
        out_specs=pl.BlockSpec(memory_space=pl.ANY),
        out_shape=jax.ShapeDtypeStruct(sorted_in.shape, sorted_in.dtype),
        scratch_shapes=[pltpu.SemaphoreType.DMA(())],
        input_output_aliases={2: 0},
        compiler_params=_cparams(("arbitrary",)),
        name="moe_dispatch",
    )(pos.reshape(r // tm, 1, tm), src, sorted_in)


def _pair_slot(t, s):
    return jnp.bitwise_xor(s, jnp.bitwise_and(t, 1))


def _moe_kernel(te_ref, tv_ref, x_ref, w1_ref, w3_ref, w2_ref, o_ref):
    t = pl.program_id(0)
    s = pl.program_id(1)
    d = o_ref.shape[1]
    valid = tv_ref[t] > 0
    slot = _pair_slot(t, s)

    @pl.when(valid)
    def _():
        x = x_ref[:, 0:d].astype(BF16)
        h1 = _dot(x, w1_ref[...].astype(BF16))
        h3 = _dot(x, w3_ref[...].astype(BF16))
        cw = x_ref[:, d:]
        lane = lax.broadcasted_iota(I32, cw.shape, 1)
        cws = jnp.sum(jnp.where(lane == slot, cw, 0.0), axis=-1, keepdims=True)
        y = _dot((h1 * _sigmoid(h1) * h3 * cws).astype(BF16), w2_ref[...].astype(BF16))

        @pl.when(s == 0)
        def _():
            o_ref[...] = y

        @pl.when(s > 0)
        def _():
            o_ref[...] += y

    @pl.when(jnp.logical_not(valid) & (s == 0))
    def _():
        o_ref[...] = jnp.zeros_like(o_ref)


def _moe_ffn(tile_expert, tile_valid, f_sorted, w1, w3, w2, layer, tm):
    p, dw = f_sorted.shape
    d = dw - ROUTE_W
    fe = w1.shape[3]

    def wmap(t, s, te, tv):
        return (layer, te[2 * t + _pair_slot(t, s)], 0, 0)

    grid_spec = pltpu.PrefetchScalarGridSpec(
        num_scalar_prefetch=2,
        grid=(p // tm, 2),
        in_specs=[pl.BlockSpec((tm, dw), lambda t, s, te, tv: (jnp.minimum(t, tv[tv.shape[0] - 1]), 0)),
                  pl.BlockSpec((None, None, d, fe), wmap),
                  pl.BlockSpec((None, None, d, fe), wmap),
                  pl.BlockSpec((None, None, fe, d), wmap)],
        out_specs=pl.BlockSpec((tm, d), lambda t, s, te, tv: (t, 0)),
    )
    return pl.pallas_call(
        _moe_kernel,
        grid_spec=grid_spec,
        out_shape=jax.ShapeDtypeStruct((p, d), F32),
        compiler_params=_cparams(("arbitrary", "arbitrary")),
        name="moe_ffn",
    )(tile_expert, tile_valid, f_sorted, w1, w3, w2)


def _combine_kernel(pos_cur, pos_nxt, y_hbm, x_ref, gate_ref, o_ref, ybuf, sem, *, tm):
    i = pl.program_id(0)
    n_steps = pl.num_programs(0)

    def issue(pos_ref, slot):
        for j in range(tm):
            _row_copy(y_hbm, pos_ref[0, 0, j], ybuf.at[slot], j, sem.at[slot]).start()

    @pl.when(i == 0)
    def _():
        issue(pos_cur, 0)

    @pl.when(i + 1 < n_steps)
    def _():
        issue(pos_nxt, (i + 1) % 2)

    slot = i % 2

    def wbody(j, carry):
        _row_copy(y_hbm, 0, ybuf.at[slot], 0, sem.at[slot]).wait()
        return carry
    lax.fori_loop(0, tm, wbody, 0, unroll=True)
    o_ref[...] = x_ref[...] + gate_ref[0] * ybuf[slot]


def _combine(y_sorted, pos, x2, mods, rows_per_mod, mod_base, tm=256):
    r, d = x2.shape
    n_steps = r // tm
    tpm = rows_per_mod // tm
    pos3 = pos.reshape(n_steps, 1, tm)
    return pl.pallas_call(
        functools.partial(_combine_kernel, tm=tm),
        grid=(n_steps,),
        in_specs=[pl.BlockSpec((1, 1, tm), lambda i: (i, 0, 0), memory_space=pltpu.SMEM),
                  pl.BlockSpec((1, 1, tm), lambda i: (jnp.minimum(i + 1, n_steps - 1), 0, 0),
                               memory_space=pltpu.SMEM),
                  pl.BlockSpec(memory_space=pl.ANY),
                  pl.BlockSpec((tm, d), lambda i: (i, 0)),
                  pl.BlockSpec((1, 1, d), lambda i: ((mod_base + i // tpm) * N_MOD + 5, 0, 0))],
        out_specs=pl.BlockSpec((tm, d), lambda i: (i, 0)),
        out_shape=jax.ShapeDtypeStruct((r, d), F32),
        scratch_shapes=[pltpu.VMEM((2, tm, d), F32), pltpu.SemaphoreType.DMA((2,))],
        compiler_params=_cparams(("arbitrary",)),
        name="moe_combine",
    )(pos3, pos3, y_sorted, x2, mods)


def _sorted_positions(gids, tm, p_rows):
    g = jnp.concatenate(gids)
    r = g.shape[0]
    epg = EXPERTS_PER_GROUP
    n_pairs = epg * (epg - 1) // 2
    n_cls = N_GROUPS * n_pairs
    onehot = (g[:, None] == jnp.arange(n_cls, dtype=I32)[None, :]).astype(I32)
    counts = jnp.sum(onehot, axis=0)
    rank = jnp.sum((jnp.cumsum(onehot, axis=0) - 1) * onehot, axis=1)
    padded = ((counts + tm - 1) // tm) * tm
    ends = jnp.cumsum(padded)
    pos = jnp.sum(onehot * (ends - padded)[None, :], axis=1) + rank
    p = r + n_cls * tm if p_rows is None else p_rows
    assert p >= r + n_cls * tm and p % tm == 0
    tile_start = jnp.arange(p // tm, dtype=I32) * tm
    tile_cls = jnp.minimum(jnp.sum((tile_start[:, None] >= ends[None, :]).astype(I32), axis=1), n_cls - 1)
    tile_ok = tile_start < ends[-1]
    tile_valid = jnp.concatenate([tile_ok.astype(I32), (ends[-1:] // tm - 1).astype(I32)])
    pairs = [(i, j) for i in range(epg) for j in range(i + 1, epg)]
    pair_lo = jnp.array([a for a, _ in pairs], I32)
    pair_hi = jnp.array([b for _, b in pairs], I32)
    grp = tile_cls // n_pairs
    tile_expert = jnp.stack([grp * epg + pair_lo[tile_cls % n_pairs],
                             grp * epg + pair_hi[tile_cls % n_pairs]], axis=1)
    tile_expert = jnp.where(tile_ok[:, None], tile_expert, 0).reshape(-1)
    return pos.astype(I32), tile_expert.astype(I32), tile_valid, p


def _moe(fexts, gids, w1, w3, w2, layer, f_sorted=None, tm=512):
    pos, tile_expert, tile_valid, p = _sorted_positions(gids, tm, None if f_sorted is None else f_sorted.shape[0])
    sizes = [f.shape[0] for f in fexts]
    poss, off = [], 0
    for s in sizes:
        poss.append(lax.slice(pos, (off,), (off + s,)))
        off += s
    if f_sorted is None:
        f_sorted = jnp.zeros((p, fexts[0].shape[1]), F32)
    for fext, ps in zip(fexts, poss):
        f_sorted = _dispatch(fext, ps, f_sorted)
    y_sorted = _moe_ffn(tile_expert, tile_valid, f_sorted, w1, w3, w2, layer, tm)
    return y_sorted, poss, f_sorted


def kernel(x, c, ctx, c_ctx, ada_w, ada_b, norm_mix, norm_ffn, even_w_in, even_w_out, hgrn_lb_logits, hgrn_out_norm, diff_qk_norm, diff_lambda, diff_out_norm, odd_w_in, odd_conv_w, odd_conv_b, rg_gate_w, rg_gate_b, rg_lambda, odd_w_out, moe_w_grp, moe_b_grp, moe_w_exp, moe_b_exp, moe_w1, moe_w3, moe_w2):
    bsz, n, d = x.shape
    n_ctx = ctx.shape[1]
    depth = ada_w.shape[0]
    assert depth == 2

    cvec = jnp.zeros((MOD_ROWS, d), F32).at[:bsz].set(c).at[bsz].set(c_ctx)
    mods_all = _ada_all(cvec, ada_w, ada_b).reshape(depth, MOD_ROWS * N_MOD, 1, d)
    x_l = x.reshape(bsz * n, d)
    x_c = ctx.reshape(bsz * n_ctx, d)
    lb_all = jnp.cumsum(jax.nn.softmax(hgrn_lb_logits.astype(F32), axis=1), axis=1)

    def router(l):
        w = jnp.concatenate([moe_w_grp[l], moe_w_exp[l]], axis=1)
        b = jnp.concatenate([moe_b_grp[l], moe_b_exp[l]])
        padw = ROUTE_W - w.shape[1]
        return jnp.pad(w, ((0, 0), (0, padw))), jnp.pad(b, (0, padw)).reshape(1, ROUTE_W)

    l = 0
    mods = mods_all[l]
    lam_init = 0.8 - 0.6 * math.exp(-0.3 * l)
    lv = diff_lambda[0].astype(F32)
    lam = jnp.exp(jnp.sum(lv[0] * lv[1])) - jnp.exp(jnp.sum(lv[2] * lv[3])) + lam_init
    w_in = even_w_in
    proj_l = _norm_mod_mm(x_l, norm_mix[l], mods, w_in, n, 0, tm=1024)
    proj_c = _norm_mod_mm(x_c, norm_mix[l], mods, w_in, bsz * n_ctx, bsz, tm=1024)
    a_l, a_c = _hgrn(proj_l, proj_c, lb_all[0, 0], lb_all[1, 0], hgrn_out_norm[0], bsz)
    b_l, b_c = _attn(proj_l, proj_c, diff_qk_norm[0], diff_out_norm[0], lam, lam_init, bsz)
    w_out = even_w_out[0].astype(BF16)
    wr, br = router(l)
    x_l, f_l, g_l = _out_proj([a_l, b_l], w_out, x_l, norm_ffn[l], mods, wr, br, n, 0)
    x_c, f_c, g_c = _out_proj([a_c, b_c], w_out, x_c, norm_ffn[l], mods, wr, br, bsz * n_ctx, bsz)
    y_sorted, (pos_l, pos_c), f_sorted = _moe([f_l, f_c], [g_l, g_c], moe_w1, moe_w3, moe_w2, l)
    x_l = _combine(y_sorted, pos_l, x_l, mods, n, 0)
    x_c = _combine(y_sorted, pos_c, x_c, mods, bsz * n_ctx, bsz)

    l = 1
    mods = mods_all[l]
    w_in = odd_w_in
    proj_l = _norm_mod_mm(x_l, norm_mix[l], mods, w_in, n, 0, tm=1024)
    rg_w = w_in.shape[2] // 2
    u_c = _norm_mod_mm(x_c, norm_mix[l], mods, w_in, bsz * n_ctx, bsz, tm=1024, cols=(rg_w, rg_w))
    gated = _rglru(proj_l, u_c,odd_conv_w[0], odd_conv_b[0], rg_gate_w[0], rg_gate_b[0], rg_lambda[0], bsz)
    wr, br = router(l)
    x_l, f_l, g_l = _out_proj([gated], odd_w_out[0].astype(BF16), x_l, norm_ffn[l], mods, wr, br, n, 0)
    y_sorted, (pos_l,), _ = _moe([f_l], [g_l], moe_w1, moe_w3, moe_w2, l, f_sorted)
    return _combine(y_sorted, pos_l, x_l, mods, n, 0).reshape(bsz, n, d)
```

```python
import functools
import math

import jax
import jax.numpy as jnp
from jax import lax
from jax.experimental import pallas as pl
from jax.experimental.pallas import tpu as pltpu

F32 = jnp.float32
BF16 = jnp.bfloat16
I32 = jnp.int32

EPS = 1e-6
LOG2_E = 1.4426950408889634
TINY = 1e-30
GELU_TANH_SCALE = math.sqrt(2.0 / math.pi)
GELU_TANH_CUBIC = 0.044715
LANES = 128
SUBLANES = 8
BF16_SUBLANES = 16
GRID_W = 64
A_HEADS = 8
HEAD_W = 128
HGRN_CHUNK = 32
HGRN_BLOCK = 256
HGRN_GROUP = 3
B_HEADS = 8
B_DH = 64
ROPE_BASE = 10000.0
ATTN_KEY_CHUNK = 256
ATTN_SKEW = 3
RG_HEADS = 16
CONV_W = 4
RG_C = 8.0
N_GROUPS = 4
EXPERTS_PER_GROUP = 4
N_MOD = 6
MOD_ROWS = 16
ROUTE_W = LANES
N_DMA_QUEUES = 2

V7X_VMEM_BYTES = 64 * 1024 * 1024
VMEM_LIMIT = V7X_VMEM_BYTES * 7 // 8


def _cparams(sem):
    return pltpu.CompilerParams(dimension_semantics=sem, vmem_limit_bytes=VMEM_LIMIT)


def _sigmoid(x):
    return 0.5 * jnp.tanh(0.5 * x) + 0.5


def _dot(a, b):
    return jnp.dot(a, b, preferred_element_type=F32)


def _dot_nt(a, b):
    return lax.dot_general(a, b, (((1,), (1,)), ((), ())), preferred_element_type=F32)


def _split_bf16(x):
    hi = x.astype(BF16)
    return hi, (x - hi.astype(F32)).astype(BF16)


def _ada_kernel(c_ref, w_ref, b_ref, o_ref):
    c = c_ref[...]
    a = (c * _sigmoid(c)).astype(BF16)
    o_ref[...] = _dot(a, w_ref[...].astype(BF16)) + b_ref[...]


def _ada_all(cvec, ada_w, ada_b, tn=1024):
    depth, d, n = ada_w.shape
    return pl.pallas_call(
        _ada_kernel,
        grid=(depth, n // tn),
        in_specs=[pl.BlockSpec((MOD_ROWS, d), lambda l, j: (0, 0)),
                  pl.BlockSpec((None, d, tn), lambda l, j: (l, 0, j)),
                  pl.BlockSpec((None, 1, tn), lambda l, j: (l, 0, j))],
        out_specs=pl.BlockSpec((None, MOD_ROWS, tn), lambda l, j: (l, 0, j)),
        out_shape=jax.ShapeDtypeStruct((depth, MOD_ROWS, n), F32),
        compiler_params=_cparams(("arbitrary", "arbitrary")),
        name="ada_mod",
    )(cvec, ada_w, ada_b.reshape(depth, 1, n))


def _norm_mod_mm_kernel(x_ref, g_ref, sh_ref, sc_ref, w_ref, o_ref, h_scr):
    @pl.when(pl.program_id(1) == 0)
    def _():
        x = x_ref[...]
        y = x * lax.rsqrt(jnp.mean(x * x, axis=-1, keepdims=True) + EPS) * g_ref[...]
        h_scr[...] = (y * (1.0 + sc_ref[0]) + sh_ref[0]).astype(BF16)

    o_ref[...] = _dot(h_scr[...], w_ref[...].astype(BF16)).astype(o_ref.dtype)


def _norm_mod_mm(x2, gain, mods, w, rows_per_mod, mod_base, tm, tn=1024, cols=None):
    r, d = x2.shape
    col0, n = (0, w.shape[2]) if cols is None else cols
    jb = col0 // tn
    tpm = rows_per_mod // tm

    def mrow(i):
        return (mod_base + i // tpm) * N_MOD

    return pl.pallas_call(
        _norm_mod_mm_kernel,
        grid=(r // tm, n // tn),
        in_specs=[pl.BlockSpec((tm, d), lambda i, j: (i, 0)),
                  pl.BlockSpec((1, d), lambda i, j: (0, 0)),
                  pl.BlockSpec((1, 1, d), lambda i, j: (mrow(i) + 0, 0, 0)),
                  pl.BlockSpec((1, 1, d), lambda i, j: (mrow(i) + 1, 0, 0)),
                  pl.BlockSpec((None, d, tn), lambda i, j: (0, 0, jb + j))],
        out_specs=pl.BlockSpec((tm, tn), lambda i, j: (i, j)),
        out_shape=jax.ShapeDtypeStruct((r, n), F32),
        scratch_shapes=[pltpu.VMEM((tm, d), BF16)],
        compiler_params=_cparams(("parallel", "arbitrary")),
        name="norm_mod_mm",
    )(x2, gain.reshape(1, d), mods, mods, w)


def _hgrn_kernel(ql, ffl, fbl, vl, gl, qc, ffc, fbc, vc, gc, lbf_ref, lbb_ref, gain_ref,
                 ol_ref, oc_ref,
                 oi_l, oi_c, qtf_l, qtb_l, qtf_c, qtb_c, kv_f, kv_b, dec_f, dec_b):
    c_sz = HGRN_CHUNK
    blk = HGRN_BLOCK
    cpb = blk // c_sz
    nb_l = ql.shape[0] // blk
    nc_l = ql.shape[0] // c_sz
    nc_c = qc.shape[0] // c_sz
    nc = nc_l + nc_c
    assert qc.shape[0] == blk
    row = lax.broadcasted_iota(I32, (blk, blk), 0)
    col = lax.broadcasted_iota(I32, (blk, blk), 1)
    same = (row // c_sz) == (col // c_sz)
    tril = same & (row >= col)
    triu = same & (row <= col)
    lbf = lbf_ref[...]
    lbb = lbb_ref[...]

    exp_mask = (lax.broadcasted_iota(I32, (blk, cpb * HEAD_W), 0) // c_sz
                == lax.broadcasted_iota(I32, (blk, cpb * HEAD_W), 1) // HEAD_W)

    def local_terms(jobs):
        mask = (tril, triu)
        last = (c_sz - 1, 0)
        lbs = (lbf, lbb)
        chains = [(j, d) for j in range(len(jobs)) for d in (0, 1)]
        rows = [pl.ds(job[0], blk) for job in jobs]
        qs, vb, vt = [], [], []
        for job, rw in zip(jobs, rows):
            q = job[1][rw, :]
            qs.append(q * _sigmoid(q))
            vb.append(job[4][rw, :].astype(BF16))
            vt.append(job[4][rw, :].T.astype(BF16))
        f = [lbs[d] + (1.0 - lbs[d]) * _sigmoid(jobs[j][2 + d][rows[j], :]) for j, d in chains]
        k = [1.0 - t for t in f]
        lf2 = [jnp.concatenate(_split_bf16(jnp.log(t)), axis=1) for t in f]
        cum2 = [_dot(jnp.where(mask[d], 1.0, 0.0).astype(BF16), lf2[i]) for i, (j, d) in enumerate(chains)]
        cum = [t[:, :HEAD_W] + t[:, HEAD_W:] for t in cum2]
        tot = [jnp.concatenate(
            [jnp.broadcast_to(cum[i][c * c_sz + last[d]:c * c_sz + last[d] + 1, :], (c_sz, HEAD_W))
             for c in range(cpb)], axis=0) for i, (j, d) in enumerate(chains)]
        e = [jnp.exp(t) for t in cum]
        qt = [(qs[j] * e[i]).astype(BF16) for i, (j, d) in enumerate(chains)]
        kt = [(k[i] * jnp.exp(-cum[i])).astype(BF16) for i in range(len(chains))]
        att = [jnp.where(mask[d], _dot_nt(qt[i], kt[i]), 0.0).astype(BF16) for i, (j, d) in enumerate(chains)]
        for j, job in enumerate(jobs):
            job[5][rows[j], :] = _dot(att[2 * j], vb[j]) + _dot(att[2 * j + 1], vb[j])
        k2 = [(k[i] * jnp.exp(tot[i] - cum[i])).astype(BF16) for i in range(len(chains))]
        k2x = [jnp.where(exp_mask, jnp.concatenate([t] * cpb, axis=1), jnp.zeros((), BF16)) for t in k2]
        kvs = [_dot(vt[j], k2x[i]) for i, (j, d) in enumerate(chains)]
        for i, (j, d) in enumerate(chains):
            kv_ref, dec_ref = ((kv_f, dec_f), (kv_b, dec_b))[d]
            gid = jobs[j][8 + d]
            for c in range(cpb):
                kv_ref[gid + c] = kvs[i][:, c * HEAD_W:(c + 1) * HEAD_W]
                dec_ref[gid + c] = e[i][c * c_sz + last[d]:c * c_sz + last[d] + 1, :]
            jobs[j][6 + d][rows[j], :] = qt[i]

    ctx_job = (0, qc, ffc, fbc, vc, oi_c, qtf_c, qtb_c, 0, nc_l)
    lat_jobs = [(i * blk, ql, ffl, fbl, vl, oi_l, qtf_l, qtb_l, nc_c + i * cpb, i * cpb) for i in range(nb_l)]
    all_jobs = [ctx_job] + lat_jobs
    for g in range(0, len(all_jobs), HGRN_GROUP):
        local_terms(all_jobs[g:g + HGRN_GROUP])

    def rec_f(i, s):
        new = s * dec_f[i] + kv_f[i]
        kv_f[i] = s
        return new

    def rec_b(i, s):
        j = nc - 1 - i
        new = s * dec_b[j] + kv_b[j]
        kv_b[j] = s
        return new

    s0 = jnp.zeros((HEAD_W, HEAD_W), F32)
    lax.fori_loop(0, nc, rec_f, s0, unroll=2)
    lax.fori_loop(0, nc, rec_b, s0, unroll=2)

    gain = gain_ref[...]

    def finish(jobs):
        parts = [[_dot_nt(qtf_ref[pl.ds(r0 + c * c_sz, c_sz), :], kv_f[gid_f + c].astype(BF16))
                  + _dot_nt(qtb_ref[pl.ds(r0 + c * c_sz, c_sz), :], kv_b[gid_b + c].astype(BF16))
                  for c in range(cpb)]
                 for r0, _, _, qtf_ref, qtb_ref, _, gid_f, gid_b in jobs]
        for (r0, g_ref, oi_ref, _, _, o_ref, _, _), part in zip(jobs, parts):
            rows = pl.ds(r0, blk)
            o = oi_ref[rows, :] + jnp.concatenate(part, axis=0)
            y = o * lax.rsqrt(jnp.mean(o * o, axis=-1, keepdims=True) + EPS) * gain
            g = g_ref[rows, :]
            o_ref[rows, :] = (y * (g * _sigmoid(g))).astype(o_ref.dtype)

    fin_jobs = ([(0, gc, oi_c, qtf_c, qtb_c, oc_ref, 0, nc_l)]
                + [(i * blk, gl, oi_l, qtf_l, qtb_l, ol_ref, nc_c + i * cpb, i * cpb) for i in range(nb_l)])
    for g0 in range(0, len(fin_jobs), HGRN_GROUP):
        finish(fin_jobs[g0:g0 + HGRN_GROUP])


def _hgrn(proj_l, proj_c, lb_f, lb_b, gain, bsz):
    n = proj_l.shape[0] // bsz
    n_ctx = proj_c.shape[0] // bsz
    nc = (n + n_ctx) // HGRN_CHUNK
    w = A_HEADS * HEAD_W

    def col(k):
        return lambda b, h: (b, k * A_HEADS + h)

    in_specs = ([pl.BlockSpec((n, HEAD_W), col(k)) for k in range(5)]
                + [pl.BlockSpec((n_ctx, HEAD_W), col(k)) for k in range(5)]
                + [pl.BlockSpec((1, HEAD_W), lambda b, h: (0, h)),
                   pl.BlockSpec((1, HEAD_W), lambda b, h: (0, h)),
                   pl.BlockSpec((1, HEAD_W), lambda b, h: (0, 0))])
    return pl.pallas_call(
        _hgrn_kernel,
        grid=(bsz, A_HEADS),
        in_specs=in_specs,
        out_specs=[pl.BlockSpec((n, HEAD_W), lambda b, h: (b, h)),
                   pl.BlockSpec((n_ctx, HEAD_W), lambda b, h: (b, h))],
        out_shape=[jax.ShapeDtypeStruct((bsz * n, w), BF16),
                   jax.ShapeDtypeStruct((bsz * n_ctx, w), BF16)],
        scratch_shapes=[pltpu.VMEM((n, HEAD_W), F32), pltpu.VMEM((n_ctx, HEAD_W), F32),
                        pltpu.VMEM((n, HEAD_W), BF16), pltpu.VMEM((n, HEAD_W), BF16),
                        pltpu.VMEM((n_ctx, HEAD_W), BF16), pltpu.VMEM((n_ctx, HEAD_W), BF16),
                        pltpu.VMEM((nc, HEAD_W, HEAD_W), F32), pltpu.VMEM((nc, HEAD_W, HEAD_W), F32),
                        pltpu.VMEM((nc, 1, HEAD_W), F32), pltpu.VMEM((nc, 1, HEAD_W), F32)],
        compiler_params=_cparams(("parallel", "parallel")),
        name="hgrn2",
    )(*([proj_l] * 5 + [proj_c] * 5), lb_f.reshape(1, w), lb_b.reshape(1, w), gain.reshape(1, HEAD_W))


def _half_mean_matrix():
    r = lax.broadcasted_iota(I32, (LANES, LANES), 0) // B_DH
    c = lax.broadcasted_iota(I32, (LANES, LANES), 1) // B_DH
    return (r == c).astype(BF16)


def _qk_prep(t, gain, bd, cos, sin):
    sq_hi, sq_lo = _split_bf16(t * t)
    ms = (_dot(sq_hi, bd) + _dot(sq_lo, bd)) * (1.0 / B_DH)
    y = t * lax.rsqrt(ms + EPS) * gain
    if cos is None:
        return y
    lane = lax.broadcasted_iota(I32, y.shape, 1)
    partner = jnp.where(lane % 2 == 0, pltpu.roll(y, LANES - 1, 1), pltpu.roll(y, 1, 1))
    return y * cos + partner * sin


def _attn_kernel(q_l, q_c, k_l, k_c, v_l, v_c, cosq, sinq, cosk, sink, gq_ref, gk_ref, go_ref, lam_ref,
                 o_l, o_c, kp, vp, s_even, s_odd, m_even, m_odd, *, out_scale):
    step = pl.program_id(2)
    n_ctx = k_c.shape[0]
    s_bufs = (s_even, s_odd)
    m_bufs = (m_even, m_odd)
    bd = _half_mean_matrix()
    lane = lax.broadcasted_iota(I32, (1, LANES), 1)
    masks = ((lane < B_DH).astype(F32), (lane >= B_DH).astype(F32))
    lam = lam_ref[:, 0:1]
    q_scale = B_DH ** -0.5 * LOG2_E

    def masked_q(q):
        q = q * q_scale
        return [(q * msk).astype(BF16) for msk in masks]

    def epilogue(r):
        o = (r[0][0:HEAD_W, :] * (1.0 / r[0][HEAD_W:HEAD_W + 1, :])
             - r[1][0:HEAD_W, :] * (lam / r[1][HEAD_W:HEAD_W + 1, :]))
        gain = jnp.concatenate([go_ref[...]] * (o.shape[1] // LANES), axis=1)
        y = o * lax.rsqrt(jnp.mean(o * o, axis=0, keepdims=True) + EPS) * gain
        return (y * out_scale).T.astype(BF16)

    @pl.when((step == 0) & (pl.program_id(0) == 0) & (pl.program_id(1) == 0))
    def _():
        s_odd[...] = jnp.zeros_like(s_odd)
        m_odd[...] = jnp.zeros_like(m_odd)

    @pl.when(step == 0)
    def _():
        kp[0:n_ctx, :] = _qk_prep(k_c[...], gk_ref[...], bd, None, None).astype(BF16)
        kp[n_ctx:, :] = _qk_prep(k_l[...], gk_ref[...], bd, cosk[...], sink[...]).astype(BF16)
        vp[0:HEAD_W, 0:n_ctx] = v_c[...].T.astype(BF16)
        vp[0:HEAD_W, n_ctx:] = v_l[...].T.astype(BF16)
        pad_rows = vp.shape[0] - HEAD_W
        ones_row = (lax.broadcasted_iota(I32, (pad_rows, 1), 0) == 0).astype(BF16)
        vp[HEAD_W:, :] = jnp.broadcast_to(ones_row, (pad_rows, vp.shape[1]))
        qm = masked_q(_qk_prep(q_c[...], gq_ref[...], bd, None, None))
        s_ctx = [_dot_nt(kp[0:n_ctx, :], t) for t in qm]
        e_ctx = [jnp.exp2(t - jnp.max(t, axis=0, keepdims=True)).astype(BF16) for t in s_ctx]
        o_c[...] = epilogue([_dot(vp[:, 0:n_ctx], t) for t in e_ctx])

    kc = ATTN_KEY_CHUNK
    for parity in (0, 1):
        @pl.when((step > 0) & (step % 2 == parity))
        def _(old=parity, new=1 - parity):
            qm = masked_q(_qk_prep(q_l[...], gq_ref[...], bd, cosq[...], sinq[...]))
            mx_old = [m_bufs[old][i, 0:1, :] for i in range(2)]
            acc = [None, None]
            mx_new = [None, None]
            n_chunks = kp.shape[0] // kc
            for c in range(n_chunks + ATTN_SKEW):
                if c < n_chunks:
                    ks = slice(c * kc, (c + 1) * kc)
                    for i in range(2):
                        e = jnp.exp2(s_bufs[old][i, ks, :] - mx_old[i]).astype(BF16)
                        part = _dot(vp[:, ks], e)
                        acc[i] = part if acc[i] is None else acc[i] + part
                if c == n_chunks - 1:
                    o_l[...] = epilogue(acc)
                if c >= ATTN_SKEW:
                    ks = slice((c - ATTN_SKEW) * kc, (c - ATTN_SKEW + 1) * kc)
                    for i in range(2):
                        t = _dot_nt(kp[ks, :], qm[i])
                        s_bufs[new][i, ks, :] = t
                        tm = jnp.max(t, axis=0, keepdims=True)
                        mx_new[i] = tm if mx_new[i] is None else jnp.maximum(mx_new[i], tm)
            for i in range(2):
                m_bufs[new][i] = jnp.broadcast_to(mx_new[i], m_bufs[new].shape[1:])


def _rope_tables(n):
    n_rows = n // GRID_W
    rowp = jnp.repeat(jnp.arange(n_rows), GRID_W).astype(F32)
    colp = jnp.tile(jnp.arange(GRID_W), n_rows).astype(F32)
    pairs = B_DH // 4
    inv = ROPE_BASE ** (-jnp.arange(pairs, dtype=F32) / pairs)
    ang = jnp.concatenate([rowp[:, None] * inv, colp[:, None] * inv], axis=-1)
    cos = jnp.repeat(jnp.cos(ang), 2, axis=-1)
    sin = jnp.repeat(jnp.sin(ang), 2, axis=-1) * jnp.tile(jnp.array([-1.0, 1.0], F32), B_DH // 2)
    return jnp.tile(cos, (1, 2)), jnp.tile(sin, (1, 2))


def _attn(proj_l, proj_c, qk_gain, out_gain, lam, lam_init, bsz, tq=512):
    n = proj_l.shape[0] // bsz
    n_ctx = proj_c.shape[0] // bsz
    nqb = n // tq
    w = B_HEADS * HEAD_W
    cos, sin = _rope_tables(n)
    qcol, kcol, vcol = 5 * A_HEADS, 5 * A_HEADS + B_HEADS, 5 * A_HEADS + 2 * B_HEADS

    def q_blk(t):
        return jnp.clip(t - 1, 0, nqb - 1)

    def o_blk(t):
        return jnp.clip(t - 2, 0, nqb - 1)

    def lat_q(b, h, t):
        return (b * nqb + q_blk(t), qcol + h)

    in_specs = [pl.BlockSpec((tq, HEAD_W), lat_q),
                pl.BlockSpec((n_ctx, HEAD_W), lambda b, h, qb: (b, qcol + h)),
                pl.BlockSpec((n, HEAD_W), lambda b, h, qb: (b, kcol + h)),
                pl.BlockSpec((n_ctx, HEAD_W), lambda b, h, qb: (b, kcol + h)),
                pl.BlockSpec((n, HEAD_W), lambda b, h, qb: (b, vcol + h)),
                pl.BlockSpec((n_ctx, HEAD_W), lambda b, h, qb: (b, vcol + h)),
                pl.BlockSpec((tq, LANES), lambda b, h, t: (q_blk(t), 0)),
                pl.BlockSpec((tq, LANES), lambda b, h, t: (q_blk(t), 0)),
                pl.BlockSpec((n, LANES), lambda b, h, qb: (0, 0)),
                pl.BlockSpec((n, LANES), lambda b, h, qb: (0, 0)),
                pl.BlockSpec((1, LANES), lambda b, h, qb: (0, 0)),
                pl.BlockSpec((1, LANES), lambda b, h, qb: (0, 0)),
                pl.BlockSpec((HEAD_W, LANES), lambda b, h, qb: (0, 0)),
                pl.BlockSpec((1, LANES), lambda b, h, qb: (0, 0))]
    return pl.pallas_call(
        functools.partial(_attn_kernel, out_scale=1.0 - lam_init),
        grid=(bsz, B_HEADS, nqb + 2),
        in_specs=in_specs,
        out_specs=[pl.BlockSpec((tq, HEAD_W), lambda b, h, t: (b * nqb + o_blk(t), h)),
                   pl.BlockSpec((n_ctx, HEAD_W), lambda b, h, qb: (b, h))],
        out_shape=[jax.ShapeDtypeStruct((bsz * n, w), BF16),
                   jax.ShapeDtypeStruct((bsz * n_ctx, w), BF16)],
        scratch_shapes=[pltpu.VMEM((n + n_ctx, HEAD_W), BF16),
                        pltpu.VMEM((HEAD_W + BF16_SUBLANES, n + n_ctx), BF16),
                        pltpu.VMEM((2, n + n_ctx, tq), F32), pltpu.VMEM((2, n + n_ctx, tq), F32),
                        pltpu.VMEM((2, SUBLANES, tq), F32), pltpu.VMEM((2, SUBLANES, tq), F32)],
        compiler_params=_cparams(("arbitrary", "arbitrary", "arbitrary")),
        name="diff_attn",
    )(proj_l, proj_c, proj_l, proj_c, proj_l, proj_c, cos, sin, cos, sin,
      jnp.tile(qk_gain[0], 2).reshape(1, LANES), jnp.tile(qk_gain[1], 2).reshape(1, LANES),
      jnp.broadcast_to(out_gain[:, None], (HEAD_W, LANES)), jnp.full((1, LANES), lam, F32))


def _scan_steps(a, b, reverse, axis):
    n = a.shape[axis]
    pos = lax.broadcasted_iota(I32, a.shape, axis)
    s = 1
    while s < n:
        keep = (pos < n - s) if reverse else (pos >= s)
        shift = n - s if reverse else s
        a_sh = jnp.where(keep, pltpu.roll(a, shift, axis), 1.0)
        b_sh = jnp.where(keep, pltpu.roll(b, shift, axis), 0.0)
        b = a * b_sh + b
        a = a * a_sh
        s *= 2
    return a, b


def _rglru_kernel(y_l, u_l, u_c, cw_ref, cb_ref, gw_ref, gb_ref, lam_ref, o_ref,
                  upad, a_f, b_f, a_b, b_b, h_f, p_f, h_b, p_b, *, seg_len, pitch):
    n = u_l.shape[0]
    n_ctx = u_c.shape[0]
    nseg = SUBLANES
    cw = cw_ref[...]
    cb = cb_ref[...]
    pad = SUBLANES

    def pieces(start, rows):
        out = []
        for r in range(nseg):
            lo, hi = max(start, r * seg_len), min(start + rows, (r + 1) * seg_len)
            if lo < hi:
                out.append((r, lo - r * seg_len, lo - start, hi - lo))
        return out

    def put(ref, start, val):
        for r, off, src, ln in pieces(start, val.shape[0]):
            ref[r * pitch + off:r * pitch + off + ln, :] = val[src:src + ln, :]

    def conv(u_ref, rows):
        upad[0:pad, :] = jnp.zeros((pad, LANES), F32)
        upad[pad:pad + rows, :] = u_ref[...]
        upad[pad + rows:pad + rows + pad, :] = jnp.zeros((pad, LANES), F32)
        acc = cb + jnp.zeros((rows, LANES), F32)
        for j in range(CONV_W):
            off = pad + j - CONV_W // 2
            acc = acc + cw[j:j + 1, :] * upad[off:off + rows, :]
        return acc

    def gates(uc, start_f, start_b):
        ub = uc.astype(BF16)
        for d, (a_ref, b_ref, start) in enumerate(((a_f, b_f, start_f), (a_b, b_b, start_b))):
            lam = lam_ref[d:d + 1, :]
            neg_sp = -(jnp.maximum(-lam, 0.0) + jnp.log(1.0 + jnp.exp(-jnp.abs(lam))))
            r = _sigmoid(_dot(ub, gw_ref[d, 0].astype(BF16)) + gb_ref[d, 0:1, :])
            i = _sigmoid(_dot(ub, gw_ref[d, 1].astype(BF16)) + gb_ref[d, 1:2, :])
            a = jnp.exp((RG_C * neg_sp) * r)
            put(a_ref, start, a)
            x = (1.0 - a) * (1.0 + a)
            put(b_ref, start, (x * lax.rsqrt(jnp.maximum(x, TINY))) * (i * uc))

    gates(conv(u_c, n_ctx), 0, n)
    gates(conv(u_l, n), n_ctx, 0)

    def step(i, carry):
        hf, pf, hb, pb = carry
        rows_f = pl.ds(i, nseg, stride=pitch)
        rows_b = pl.ds(seg_len - 1 - i, nseg, stride=pitch)
        af, ab = a_f[rows_f, :], a_b[rows_b, :]
        hf = af * hf + b_f[rows_f, :]
        hb = ab * hb + b_b[rows_b, :]
        pf = pf * af
        pb = pb * ab
        h_f[rows_f, :] = hf
        p_f[rows_f, :] = pf
        h_b[rows_b, :] = hb
        p_b[rows_b, :] = pb
        return hf, pf, hb, pb

    zero = jnp.zeros((nseg, LANES), F32)
    one = jnp.ones((nseg, LANES), F32)
    hf, pf, hb, pb = lax.fori_loop(0, seg_len, step, (zero, one, zero, one), unroll=8)

    seg = lax.broadcasted_iota(I32, (nseg, LANES), 0)
    _, ef = _scan_steps(pf, hf, False, 0)
    _, eb = _scan_steps(pb, hb, True, 0)
    carry_f = jnp.where(seg >= 1, pltpu.roll(ef, 1, 0), 0.0)
    carry_b = jnp.where(seg < nseg - 1, pltpu.roll(eb, nseg - 1, 0), 0.0)

    cuts = sorted({0, n} | {k * seg_len for k in range(nseg + 1) if 0 < k * seg_len < n}
                  | {k * seg_len - n_ctx for k in range(nseg + 1) if 0 < k * seg_len - n_ctx < n})
    for i0, i1 in zip(cuts[:-1], cuts[1:]):
        ln = i1 - i0
        (rf, of, _, _), = pieces(n_ctx + i0, ln)
        (rb, ob, _, _), = pieces(i0, ln)
        sf = slice(rf * pitch + of, rf * pitch + of + ln)
        sb = slice(rb * pitch + ob, rb * pitch + ob + ln)
        h = (h_f[sf, :] + p_f[sf, :] * carry_f[rf:rf + 1, :]) + (h_b[sb, :] + p_b[sb, :] * carry_b[rb:rb + 1, :])
        y = y_l[i0:i1, :]
        gelu = 0.5 * y * (1.0 + jnp.tanh(GELU_TANH_SCALE * (y + GELU_TANH_CUBIC * y * y * y)))
        o_ref[i0:i1, :] = (h * gelu).astype(o_ref.dtype)


def _rglru(proj_l, u_c, conv_w, conv_b, gate_w, gate_b, lam, bsz):
    n = proj_l.shape[0] // bsz
    n_ctx = u_c.shape[0] // bsz
    w = RG_HEADS * HEAD_W
    seg_len = (n + n_ctx) // SUBLANES
    assert seg_len * SUBLANES == n + n_ctx and seg_len % SUBLANES == 0
    assert n_ctx % SUBLANES == 0 and n_ctx <= seg_len
    pitch = seg_len + SUBLANES
    return pl.pallas_call(
        functools.partial(_rglru_kernel, seg_len=seg_len, pitch=pitch),
        grid=(bsz, RG_HEADS),
        in_specs=[pl.BlockSpec((n, HEAD_W), lambda b, h: (b, h)),
                  pl.BlockSpec((n, HEAD_W), lambda b, h: (b, RG_HEADS + h)),
                  pl.BlockSpec((n_ctx, HEAD_W), lambda b, h: (b, h)),
                  pl.BlockSpec((CONV_W, HEAD_W), lambda b, h: (0, h)),
                  pl.BlockSpec((1, HEAD_W), lambda b, h: (0, h)),
                  pl.BlockSpec((2, 2, None, HEAD_W, HEAD_W), lambda b, h: (0, 0, h, 0, 0)),
                  pl.BlockSpec((2, 2, HEAD_W), lambda b, h: (0, 0, h)),
                  pl.BlockSpec((2, HEAD_W), lambda b, h: (0, h))],
        out_specs=pl.BlockSpec((n, HEAD_W), lambda b, h: (b, h)),
        out_shape=jax.ShapeDtypeStruct((bsz * n, w), BF16),
        scratch_shapes=[pltpu.VMEM((n + 2 * SUBLANES, LANES), F32)]
                       + [pltpu.VMEM((SUBLANES * pitch, LANES), F32)] * 8,
        compiler_params=_cparams(("parallel", "parallel")),
        name="rglru",
    )(proj_l, proj_l, u_c, conv_w, conv_b.reshape(1, w), gate_w, gate_b, lam)


def _out_proj_kernel(*refs, n_mix, n_groups, epg, row_chunk):
    mix_refs = refs[:n_mix]
    (w_ref, x_ref, gate_ref, g_ref, sh_ref, sc_ref, wr_ref, rb_ref,
     xo_ref, f_ref, gid_ref) = refs[n_mix:]
    d = x_ref.shape[1]
    n_pairs = epg * (epg - 1) // 2
    for r0 in range(0, x_ref.shape[0], row_chunk):
        rows = slice(r0, r0 + row_chunk)
        k0 = 0
        mix = None
        for m_ref in mix_refs:
            kk = m_ref.shape[1]
            part = _dot(m_ref[rows, :], w_ref[k0:k0 + kk, :])
            mix = part if mix is None else mix + part
            k0 += kk
        x = x_ref[rows, :] + gate_ref[0] * mix
        xo_ref[rows, :] = x
        y = x * lax.rsqrt(jnp.mean(x * x, axis=-1, keepdims=True) + EPS) * g_ref[...]
        f = y * (1.0 + sc_ref[0]) + sh_ref[0]
        f_ref[rows, 0:d] = f
        f_hi, f_lo = _split_bf16(f)

        hh = _dot(f_hi, wr_ref[...])
        lg = hh[:, 0:ROUTE_W] + hh[:, ROUTE_W:] + _dot(f_lo, wr_ref[:, 0:ROUTE_W]) + rb_ref[...]
        lane = lax.broadcasted_iota(I32, lg.shape, 1).astype(F32)
        neg = -jnp.inf
        big = float(ROUTE_W)
        gl = jnp.where(lane < n_groups, lg, neg)
        gmax = jnp.max(gl, axis=-1, keepdims=True)
        gidx = jnp.min(jnp.where(gl == gmax, lane, big), axis=-1, keepdims=True)
        gw = 1.0 / jnp.sum(jnp.exp(gl - gmax), axis=-1, keepdims=True)
        base = n_groups + gidx * epg
        el = jnp.where((lane >= base) & (lane < base + epg), lg, neg)
        v1 = jnp.max(el, axis=-1, keepdims=True)
        i1 = jnp.min(jnp.where(el == v1, lane, big), axis=-1, keepdims=True)
        el2 = jnp.where(lane == i1, neg, el)
        v2 = jnp.max(el2, axis=-1, keepdims=True)
        i2 = jnp.min(jnp.where(el2 == v2, lane, big), axis=-1, keepdims=True)
        t = jnp.exp(v2 - v1)
        w1 = gw / (1.0 + t)
        w2 = w1 * t
        s1 = i1 - base
        s2 = i2 - base
        lo = jnp.minimum(s1, s2)
        hi = jnp.maximum(s1, s2)
        pair = lo * (2 * epg - 1 - lo) * 0.5 + (hi - lo - 1.0)
        w_lo = jnp.where(s1 < s2, w1, w2)
        w_hi = jnp.where(s1 < s2, w2, w1)
        f_ref[rows, d:] = jnp.where(lane == 0.0, w_lo, 0.0) + jnp.where(lane == 1.0, w_hi, 0.0)
        cls = jnp.broadcast_to(gidx * n_pairs + pair, (row_chunk, LANES)).T
        gid_ref[:, rows] = cls[0:SUBLANES, :].astype(I32)


def _out_proj(mixes, w_out, x2, gain, mods, w_router, b_router, rows_per_mod, mod_base, tm=512):
    r, d = x2.shape
    k = w_out.shape[0]
    tpm = rows_per_mod // tm
    wr = jnp.concatenate(_split_bf16(w_router), axis=1)

    def mrow(i):
        return (mod_base + i // tpm) * N_MOD

    const2 = lambda i: (0, 0)
    in_specs = ([pl.BlockSpec((tm, m.shape[1]), lambda i: (i, 0)) for m in mixes]
                + [pl.BlockSpec((k, d), const2),
                   pl.BlockSpec((tm, d), lambda i: (i, 0)),
                   pl.BlockSpec((1, 1, d), lambda i: (mrow(i) + 2, 0, 0)),
                   pl.BlockSpec((1, d), const2),
                   pl.BlockSpec((1, 1, d), lambda i: (mrow(i) + 3, 0, 0)),
                   pl.BlockSpec((1, 1, d), lambda i: (mrow(i) + 4, 0, 0)),
                   pl.BlockSpec((d, 2 * ROUTE_W), const2),
                   pl.BlockSpec((1, ROUTE_W), const2)])
    x_new, fext, cls = pl.pallas_call(
        functools.partial(_out_proj_kernel, n_mix=len(mixes), n_groups=N_GROUPS, epg=EXPERTS_PER_GROUP,
                          row_chunk=tm),
        grid=(r // tm,),
        in_specs=in_specs,
        out_specs=[pl.BlockSpec((tm, d), lambda i: (i, 0)),
                   pl.BlockSpec((tm, d + ROUTE_W), lambda i: (i, 0)),
                   pl.BlockSpec((SUBLANES, tm), lambda i: (i, 0))],
        out_shape=[jax.ShapeDtypeStruct((r, d), F32),
                   jax.ShapeDtypeStruct((r, d + ROUTE_W), F32),
                   jax.ShapeDtypeStruct((r // tm * SUBLANES, tm), I32)],
        compiler_params=_cparams(("parallel",)),
        name="out_proj",
    )(*mixes, w_out, x2, mods, gain.reshape(1, d), mods, mods, wr, b_router)
    return x_new, fext, cls.reshape(r // tm, SUBLANES, tm)[:, 0, :].reshape(r)


def _row_copy(src, s, dst, t, sem):
    return pltpu.make_async_copy(src.at[pl.ds(s, 1), :], dst.at[pl.ds(t, 1), :], sem)


def _dispatch_kernel(pos_ref, src_ref, dst_in, dst, sem, *, tm):
    del dst_in

    for j in range(tm):
        _row_copy(src_ref, j, dst, pos_ref[0, 0, j], sem).start(priority=j % N_DMA_QUEUES)

    def wbody(j, carry):
        _row_copy(src_ref, 0, dst, 0, sem).wait()
        return carry
    lax.fori_loop(0, tm, wbody, 0, unroll=True)


def _dispatch(src, pos, sorted_in, tm=512):
    r, w = src.shape
    return pl.pallas_call(
        functools.partial(_dispatch_kernel, tm=tm),
        grid=(r // tm,),
        in_specs=[pl.BlockSpec((1, 1, tm), lambda i: (i, 0, 0), memory_space=pltpu.SMEM),
                  pl.BlockSpec((tm, w), lambda i: (i, 0)),
                  pl.BlockSpec(memory_space=pl.ANY)],
        out_specs=pl.BlockSpec(memory_space=pl.ANY),
        out_shape=jax.ShapeDtypeStruct(sorted_in.shape, sorted_in.dtype),
        scratch_shapes=[pltpu.SemaphoreType.DMA(())],
        input_output_aliases={2: 0},
        compiler_params=_cparams(("arbitrary",)),
        name="moe_dispatch",
    )(pos.reshape(r // tm, 1, tm), src, sorted_in)


def _pair_slot(t, s):
    return jnp.bitwise_xor(s, jnp.bitwise_and(t, 1))


def _moe_kernel(te_ref, tv_ref, x_ref, w1_ref, w3_ref, w2_ref, o_ref):
    t = pl.program_id(0)
    s = pl.program_id(1)
    d = o_ref.shape[1]
    valid = tv_ref[t] > 0
    slot = _pair_slot(t, s)

    @pl.when(valid)
    def _():
        x = x_ref[:, 0:d].astype(BF16)
        h1 = _dot(x, w1_ref[...].astype(BF16))
        h3 = _dot(x, w3_ref[...].astype(BF16))
        cw = x_ref[:, d:]
        lane = lax.broadcasted_iota(I32, cw.shape, 1)
        cws = jnp.sum(jnp.where(lane == slot, cw, 0.0), axis=-1, keepdims=True)
        y = _dot((h1 * _sigmoid(h1) * h3 * cws).astype(BF16), w2_ref[...].astype(BF16))

        @pl.when(s == 0)
        def _():
            o_ref[...] = y

        @pl.when(s > 0)
        def _():
            o_ref[...] += y

    @pl.when(jnp.logical_not(valid) & (s == 0))
    def _():
        o_ref[...] = jnp.zeros_like(o_ref)


def _moe_ffn(tile_expert, tile_valid, f_sorted, w1, w3, w2, layer, tm):
    p, dw = f_sorted.shape
    d = dw - ROUTE_W
    fe = w1.shape[3]

    def wmap(t, s, te, tv):
        return (layer, te[2 * t + _pair_slot(t, s)], 0, 0)

    grid_spec = pltpu.PrefetchScalarGridSpec(
        num_scalar_prefetch=2,
        grid=(p // tm, 2),
        in_specs=[pl.BlockSpec((tm, dw), lambda t, s, te, tv: (jnp.minimum(t, tv[tv.shape[0] - 1]), 0)),
                  pl.BlockSpec((None, None, d, fe), wmap),
                  pl.BlockSpec((None, None, d, fe), wmap),
                  pl.BlockSpec((None, None, fe, d), wmap)],
        out_specs=pl.BlockSpec((tm, d), lambda t, s, te, tv: (t, 0)),
    )
    return pl.pallas_call(
        _moe_kernel,
        grid_spec=grid_spec,
        out_shape=jax.ShapeDtypeStruct((p, d), F32),
        compiler_params=_cparams(("arbitrary", "arbitrary")),
        name="moe_ffn",
    )(tile_expert, tile_valid, f_sorted, w1, w3, w2)


def _combine_kernel(pos_cur, pos_nxt, y_hbm, x_ref, gate_ref, o_ref, ybuf, sem, *, tm):
    i = pl.program_id(0)
    n_steps = pl.num_programs(0)

    def issue(pos_ref, slot):
        for j in range(tm):
            _row_copy(y_hbm, pos_ref[0, 0, j], ybuf.at[slot], j, sem.at[slot]).start(priority=j % N_DMA_QUEUES)

    @pl.when(i == 0)
    def _():
        issue(pos_cur, 0)

    @pl.when(i + 1 < n_steps)
    def _():
        issue(pos_nxt, (i + 1) % 2)

    slot = i % 2

    def wbody(j, carry):
        _row_copy(y_hbm, 0, ybuf.at[slot], 0, sem.at[slot]).wait()
        return carry
    lax.fori_loop(0, tm, wbody, 0, unroll=True)
    o_ref[...] = x_ref[...] + gate_ref[0] * ybuf[slot]


def _combine(y_sorted, pos, x2, mods, rows_per_mod, mod_base, tm=256):
    r, d = x2.shape
    n_steps = r // tm
    tpm = rows_per_mod // tm
    pos3 = pos.reshape(n_steps, 1, tm)
    return pl.pallas_call(
        functools.partial(_combine_kernel, tm=tm),
        grid=(n_steps,),
        in_specs=[pl.BlockSpec((1, 1, tm), lambda i: (i, 0, 0), memory_space=pltpu.SMEM),
                  pl.BlockSpec((1, 1, tm), lambda i: (jnp.minimum(i + 1, n_steps - 1), 0, 0),
                               memory_space=pltpu.SMEM),
                  pl.BlockSpec(memory_space=pl.ANY),
                  pl.BlockSpec((tm, d), lambda i: (i, 0)),
                  pl.BlockSpec((1, 1, d), lambda i: ((mod_base + i // tpm) * N_MOD + 5, 0, 0))],
        out_specs=pl.BlockSpec((tm, d), lambda i: (i, 0)),
        out_shape=jax.ShapeDtypeStruct((r, d), F32),
        scratch_shapes=[pltpu.VMEM((2, tm, d), F32), pltpu.SemaphoreType.DMA((2,))],
        compiler_params=_cparams(("arbitrary",)),
        name="moe_combine",
    )(pos3, pos3, y_sorted, x2, mods)


def _sorted_positions(gids, tm, p_rows):
    g = jnp.concatenate(gids)
    r = g.shape[0]
    epg = EXPERTS_PER_GROUP
    n_pairs = epg * (epg - 1) // 2
    n_cls = N_GROUPS * n_pairs
    onehot = (g[:, None] == jnp.arange(n_cls, dtype=I32)[None, :]).astype(I32)
    counts = jnp.sum(onehot, axis=0)
    rank = jnp.sum((jnp.cumsum(onehot, axis=0) - 1) * onehot, axis=1)
    padded = ((counts + tm - 1) // tm) * tm
    ends = jnp.cumsum(padded)
    pos = jnp.sum(onehot * (ends - padded)[None, :], axis=1) + rank
    p = r + n_cls * tm if p_rows is None else p_rows
    assert p >= r + n_cls * tm and p % tm == 0
    tile_start = jnp.arange(p // tm, dtype=I32) * tm
    tile_cls = jnp.minimum(jnp.sum((tile_start[:, None] >= ends[None, :]).astype(I32), axis=1), n_cls - 1)
    tile_ok = tile_start < ends[-1]
    tile_valid = jnp.concatenate([tile_ok.astype(I32), (ends[-1:] // tm - 1).astype(I32)])
    pairs = [(i, j) for i in range(epg) for j in range(i + 1, epg)]
    pair_lo = jnp.array([a for a, _ in pairs], I32)
    pair_hi = jnp.array([b for _, b in pairs], I32)
    grp = tile_cls // n_pairs
    tile_expert = jnp.stack([grp * epg + pair_lo[tile_cls % n_pairs],
                             grp * epg + pair_hi[tile_cls % n_pairs]], axis=1)
    tile_expert = jnp.where(tile_ok[:, None], tile_expert, 0).reshape(-1)
    return pos.astype(I32), tile_expert.astype(I32), tile_valid, p


def _moe(fexts, gids, w1, w3, w2, layer, f_sorted=None, tm=512):
    pos, tile_expert, tile_valid, p = _sorted_positions(gids, tm, None if f_sorted is None else f_sorted.shape[0])
    sizes = [f.shape[0] for f in fexts]
    poss, off = [], 0
    for s in sizes:
        poss.append(lax.slice(pos, (off,), (off + s,)))
        off += s
    if f_sorted is None:
        f_sorted = jnp.zeros((p, fexts[0].shape[1]), F32)
    for fext, ps in zip(fexts, poss):
        f_sorted = _dispatch(fext, ps, f_sorted)
    y_sorted = _moe_ffn(tile_expert, tile_valid, f_sorted, w1, w3, w2, layer, tm)
    return y_sorted, poss, f_sorted


def kernel(x, c, ctx, c_ctx, ada_w, ada_b, norm_mix, norm_ffn, even_w_in, even_w_out, hgrn_lb_logits, hgrn_out_norm, diff_qk_norm, diff_lambda, diff_out_norm, odd_w_in, odd_conv_w, odd_conv_b, rg_gate_w, rg_gate_b, rg_lambda, odd_w_out, moe_w_grp, moe_b_grp, moe_w_exp, moe_b_exp, moe_w1, moe_w3, moe_w2):
    bsz, n, d = x.shape
    n_ctx = ctx.shape[1]
    depth = ada_w.shape[0]
    assert depth == 2

    cvec = jnp.zeros((MOD_ROWS, d), F32).at[:bsz].set(c).at[bsz].set(c_ctx)
    mods_all = _ada_all(cvec, ada_w, ada_b).reshape(depth, MOD_ROWS * N_MOD, 1, d)
    x_l = x.reshape(bsz * n, d)
    x_c = ctx.reshape(bsz * n_ctx, d)
    lb_all = jnp.cumsum(jax.nn.softmax(hgrn_lb_logits.astype(F32), axis=1), axis=1)

    def router(l):
        w = jnp.concatenate([moe_w_grp[l], moe_w_exp[l]], axis=1)
        b = jnp.concatenate([moe_b_grp[l], moe_b_exp[l]])
        padw = ROUTE_W - w.shape[1]
        return jnp.pad(w, ((0, 0), (0, padw))), jnp.pad(b, (0, padw)).reshape(1, ROUTE_W)

    l = 0
    mods = mods_all[l]
    lam_init = 0.8 - 0.6 * math.exp(-0.3 * l)
    lv = diff_lambda[0].astype(F32)
    lam = jnp.exp(jnp.sum(lv[0] * lv[1])) - jnp.exp(jnp.sum(lv[2] * lv[3])) + lam_init
    w_in = even_w_in
    proj_l = _norm_mod_mm(x_l, norm_mix[l], mods, w_in, n, 0, tm=1024)
    proj_c = _norm_mod_mm(x_c, norm_mix[l], mods, w_in, bsz * n_ctx, bsz, tm=1024)
    a_l, a_c = _hgrn(proj_l, proj_c, lb_all[0, 0], lb_all[1, 0], hgrn_out_norm[0], bsz)
    b_l, b_c = _attn(proj_l, proj_c, diff_qk_norm[0], diff_out_norm[0], lam, lam_init, bsz)
    w_out = even_w_out[0].astype(BF16)
    wr, br = router(l)
    x_l, f_l, g_l = _out_proj([a_l, b_l], w_out, x_l, norm_ffn[l], mods, wr, br, n, 0)
    x_c, f_c, g_c = _out_proj([a_c, b_c], w_out, x_c, norm_ffn[l], mods, wr, br, bsz * n_ctx, bsz)
    y_sorted, (pos_l, pos_c), f_sorted = _moe([f_l, f_c], [g_l, g_c], moe_w1, moe_w3, moe_w2, l)
    x_l = _combine(y_sorted, pos_l, x_l, mods, n, 0)
    x_c = _combine(y_sorted, pos_c, x_c, mods, bsz * n_ctx, bsz)

    l = 1
    mods = mods_all[l]
    w_in = odd_w_in
    proj_l = _norm_mod_mm(x_l, norm_mix[l], mods, w_in, n, 0, tm=1024)
    rg_w = w_in.shape[2] // 2
    u_c = _norm_mod_mm(x_c, norm_mix[l], mods, w_in, bsz * n_ctx, bsz, tm=1024, cols=(rg_w, rg_w))
    gated = _rglru(proj_l, u_c,odd_conv_w[0], odd_conv_b[0], rg_gate_w[0], rg_gate_b[0], rg_lambda[0], bsz)
    wr, br = router(l)
    x_l, f_l, g_l = _out_proj([gated], odd_w_out[0].astype(BF16), x_l, norm_ffn[l], mods, wr, br, n, 0)
    y_sorted, (pos_l,), _ = _moe([f_l], [g_l], moe_w1, moe_w3, moe_w2, l, f_sorted)
    return _combine(y_sorted, pos_l, x_l, mods, n, 0).reshape(bsz, n, d)
```

```python
import functools
import math

import jax
import jax.numpy as jnp
from jax import lax
from jax.experimental import pallas as pl
from jax.experimental.pallas import tpu as pltpu

F32 = jnp.float32
BF16 = jnp.bfloat16
I32 = jnp.int32

EPS = 1e-6
LOG2_E = 1.4426950408889634
TINY = 1e-30
GELU_TANH_SCALE = math.sqrt(2.0 / math.pi)
GELU_TANH_CUBIC = 0.044715
LANES = 128
SUBLANES = 8
BF16_SUBLANES = 16
GRID_W = 64
A_HEADS = 8
HEAD_W = 128
HGRN_CHUNK = 32
HGRN_BLOCK = 256
HGRN_GROUP = 3
B_HEADS = 8
B_DH = 64
ROPE_BASE = 10000.0
ATTN_KEY_CHUNK = 256
ATTN_SKEW = 3
RG_HEADS = 16
CONV_W = 4
RG_C = 8.0
N_GROUPS = 4
EXPERTS_PER_GROUP = 4
N_MOD = 6
MOD_ROWS = 16
ROUTE_W = LANES

V7X_VMEM_BYTES = 64 * 1024 * 1024
VMEM_LIMIT = V7X_VMEM_BYTES * 7 // 8


def _cparams(sem):
    return pltpu.CompilerParams(dimension_semantics=sem, vmem_limit_bytes=VMEM_LIMIT)


def _sigmoid(x):
    return 0.5 * jnp.tanh(0.5 * x) + 0.5


def _dot(a, b):
    return jnp.dot(a, b, preferred_element_type=F32)


def _dot_nt(a, b):
    return lax.dot_general(a, b, (((1,), (1,)), ((), ())), preferred_element_type=F32)


def _split_bf16(x):
    hi = x.astype(BF16)
    return hi, (x - hi.astype(F32)).astype(BF16)


def _ada_kernel(c_ref, w_ref, b_ref, o_ref):
    c = c_ref[...]
    a = (c * _sigmoid(c)).astype(BF16)
    o_ref[...] = _dot(a, w_ref[...].astype(BF16)) + b_ref[...]


def _ada_all(cvec, ada_w, ada_b, tn=1024):
    depth, d, n = ada_w.shape
    return pl.pallas_call(
        _ada_kernel,
        grid=(depth, n // tn),
        in_specs=[pl.BlockSpec((MOD_ROWS, d), lambda l, j: (0, 0)),
                  pl.BlockSpec((None, d, tn), lambda l, j: (l, 0, j)),
                  pl.BlockSpec((None, 1, tn), lambda l, j: (l, 0, j))],
        out_specs=pl.BlockSpec((None, MOD_ROWS, tn), lambda l, j: (l, 0, j)),
        out_shape=jax.ShapeDtypeStruct((depth, MOD_ROWS, n), F32),
        compiler_params=_cparams(("arbitrary", "arbitrary")),
        name="ada_mod",
    )(cvec, ada_w, ada_b.reshape(depth, 1, n))


def _norm_mod_mm_kernel(x_ref, g_ref, sh_ref, sc_ref, w_ref, o_ref, h_scr):
    @pl.when(pl.program_id(1) == 0)
    def _():
        x = x_ref[...]
        y = x * lax.rsqrt(jnp.mean(x * x, axis=-1, keepdims=True) + EPS) * g_ref[...]
        h_scr[...] = (y * (1.0 + sc_ref[0]) + sh_ref[0]).astype(BF16)

    o_ref[...] = _dot(h_scr[...], w_ref[...].astype(BF16)).astype(o_ref.dtype)


def _norm_mod_mm(x2, gain, mods, w, rows_per_mod, mod_base, tm, tn=1024, cols=None):
    r, d = x2.shape
    col0, n = (0, w.shape[2]) if cols is None else cols
    jb = col0 // tn
    tpm = rows_per_mod // tm

    def mrow(i):
        return (mod_base + i // tpm) * N_MOD

    return pl.pallas_call(
        _norm_mod_mm_kernel,
        grid=(r // tm, n // tn),
        in_specs=[pl.BlockSpec((tm, d), lambda i, j: (i, 0)),
                  pl.BlockSpec((1, d), lambda i, j: (0, 0)),
                  pl.BlockSpec((1, 1, d), lambda i, j: (mrow(i) + 0, 0, 0)),
                  pl.BlockSpec((1, 1, d), lambda i, j: (mrow(i) + 1, 0, 0)),
                  pl.BlockSpec((None, d, tn), lambda i, j: (0, 0, jb + j))],
        out_specs=pl.BlockSpec((tm, tn), lambda i, j: (i, j)),
        out_shape=jax.ShapeDtypeStruct((r, n), F32),
        scratch_shapes=[pltpu.VMEM((tm, d), BF16)],
        compiler_params=_cparams(("parallel", "arbitrary")),
        name="norm_mod_mm",
    )(x2, gain.reshape(1, d), mods, mods, w)


def _hgrn_kernel(ql, ffl, fbl, vl, gl, qc, ffc, fbc, vc, gc, lbf_ref, lbb_ref, gain_ref,
                 ol_ref, oc_ref,
                 oi_l, oi_c, qtf_l, qtb_l, qtf_c, qtb_c, kv_f, kv_b, dec_f, dec_b):
    c_sz = HGRN_CHUNK
    blk = HGRN_BLOCK
    cpb = blk // c_sz
    nb_l = ql.shape[0] // blk
    nc_l = ql.shape[0] // c_sz
    nc_c = qc.shape[0] // c_sz
    nc = nc_l + nc_c
    assert qc.shape[0] == blk
    row = lax.broadcasted_iota(I32, (blk, blk), 0)
    col = lax.broadcasted_iota(I32, (blk, blk), 1)
    same = (row // c_sz) == (col // c_sz)
    tril = same & (row >= col)
    triu = same & (row <= col)
    lbf = lbf_ref[...]
    lbb = lbb_ref[...]

    exp_mask = (lax.broadcasted_iota(I32, (blk, cpb * HEAD_W), 0) // c_sz
                == lax.broadcasted_iota(I32, (blk, cpb * HEAD_W), 1) // HEAD_W)

    def local_terms(jobs):
        mask = (tril, triu)
        last = (c_sz - 1, 0)
        lbs = (lbf, lbb)
        chains = [(j, d) for j in range(len(jobs)) for d in (0, 1)]
        rows = [pl.ds(job[0], blk) for job in jobs]
        qs, vb, vt = [], [], []
        for job, rw in zip(jobs, rows):
            q = job[1][rw, :]
            qs.append(q * _sigmoid(q))
            vb.append(job[4][rw, :].astype(BF16))
            vt.append(job[4][rw, :].T.astype(BF16))
        f = [lbs[d] + (1.0 - lbs[d]) * _sigmoid(jobs[j][2 + d][rows[j], :]) for j, d in chains]
        k = [1.0 - t for t in f]
        lf2 = [jnp.concatenate(_split_bf16(jnp.log(t)), axis=1) for t in f]
        cum2 = [_dot(jnp.where(mask[d], 1.0, 0.0).astype(BF16), lf2[i]) for i, (j, d) in enumerate(chains)]
        cum = [t[:, :HEAD_W] + t[:, HEAD_W:] for t in cum2]
        tot = [jnp.concatenate(
            [jnp.broadcast_to(cum[i][c * c_sz + last[d]:c * c_sz + last[d] + 1, :], (c_sz, HEAD_W))
             for c in range(cpb)], axis=0) for i, (j, d) in enumerate(chains)]
        e = [jnp.exp(t) for t in cum]
        qt = [(qs[j] * e[i]).astype(BF16) for i, (j, d) in enumerate(chains)]
        kt = [(k[i] * jnp.exp(-cum[i])).astype(BF16) for i in range(len(chains))]
        att = [jnp.where(mask[d], _dot_nt(qt[i], kt[i]), 0.0).astype(BF16) for i, (j, d) in enumerate(chains)]
        for j, job in enumerate(jobs):
            job[5][rows[j], :] = _dot(att[2 * j], vb[j]) + _dot(att[2 * j + 1], vb[j])
        k2 = [(k[i] * jnp.exp(tot[i] - cum[i])).astype(BF16) for i in range(len(chains))]
        k2x = [jnp.where(exp_mask, jnp.concatenate([t] * cpb, axis=1), jnp.zeros((), BF16)) for t in k2]
        kvs = [_dot(vt[j], k2x[i]) for i, (j, d) in enumerate(chains)]
        for i, (j, d) in enumerate(chains):
            kv_ref, dec_ref = ((kv_f, dec_f), (kv_b, dec_b))[d]
            gid = jobs[j][8 + d]
            for c in range(cpb):
                kv_ref[gid + c] = kvs[i][:, c * HEAD_W:(c + 1) * HEAD_W]
                dec_ref[gid + c] = e[i][c * c_sz + last[d]:c * c_sz + last[d] + 1, :]
            jobs[j][6 + d][rows[j], :] = qt[i]

    ctx_job = (0, qc, ffc, fbc, vc, oi_c, qtf_c, qtb_c, 0, nc_l)
    lat_jobs = [(i * blk, ql, ffl, fbl, vl, oi_l, qtf_l, qtb_l, nc_c + i * cpb, i * cpb) for i in range(nb_l)]
    all_jobs = [ctx_job] + lat_jobs
    for g in range(0, len(all_jobs), HGRN_GROUP):
        local_terms(all_jobs[g:g + HGRN_GROUP])

    def rec_f(i, s):
        new = s * dec_f[i] + kv_f[i]
        kv_f[i] = s
        return new

    def rec_b(i, s):
        j = nc - 1 - i
        new = s * dec_b[j] + kv_b[j]
        kv_b[j] = s
        return new

    s0 = jnp.zeros((HEAD_W, HEAD_W), F32)
    lax.fori_loop(0, nc, rec_f, s0, unroll=2)
    lax.fori_loop(0, nc, rec_b, s0, unroll=2)

    gain = gain_ref[...]

    def finish(jobs):
        parts = [[_dot_nt(qtf_ref[pl.ds(r0 + c * c_sz, c_sz), :], kv_f[gid_f + c].astype(BF16))
                  + _dot_nt(qtb_ref[pl.ds(r0 + c * c_sz, c_sz), :], kv_b[gid_b + c].astype(BF16))
                  for c in range(cpb)]
                 for r0, _, _, qtf_ref, qtb_ref, _, gid_f, gid_b in jobs]
        for (r0, g_ref, oi_ref, _, _, o_ref, _, _), part in zip(jobs, parts):
            rows = pl.ds(r0, blk)
            o = oi_ref[rows, :] + jnp.concatenate(part, axis=0)
            y = o * lax.rsqrt(jnp.mean(o * o, axis=-1, keepdims=True) + EPS) * gain
            g = g_ref[rows, :]
            o_ref[rows, :] = (y * (g * _sigmoid(g))).astype(o_ref.dtype)

    fin_jobs = ([(0, gc, oi_c, qtf_c, qtb_c, oc_ref, 0, nc_l)]
                + [(i * blk, gl, oi_l, qtf_l, qtb_l, ol_ref, nc_c + i * cpb, i * cpb) for i in range(nb_l)])
    for g0 in range(0, len(fin_jobs), HGRN_GROUP):
        finish(fin_jobs[g0:g0 + HGRN_GROUP])


def _hgrn(proj_l, proj_c, lb_f, lb_b, gain, bsz):
    n = proj_l.shape[0] // bsz
    n_ctx = proj_c.shape[0] // bsz
    nc = (n + n_ctx) // HGRN_CHUNK
    w = A_HEADS * HEAD_W

    def col(k):
        return lambda b, h: (b, k * A_HEADS + h)

    in_specs = ([pl.BlockSpec((n, HEAD_W), col(k)) for k in range(5)]
                + [pl.BlockSpec((n_ctx, HEAD_W), col(k)) for k in range(5)]
                + [pl.BlockSpec((1, HEAD_W), lambda b, h: (0, h)),
                   pl.BlockSpec((1, HEAD_W), lambda b, h: (0, h)),
                   pl.BlockSpec((1, HEAD_W), lambda b, h: (0, 0))])
    return pl.pallas_call(
        _hgrn_kernel,
        grid=(bsz, A_HEADS),
        in_specs=in_specs,
        out_specs=[pl.BlockSpec((n, HEAD_W), lambda b, h: (b, h)),
                   pl.BlockSpec((n_ctx, HEAD_W), lambda b, h: (b, h))],
        out_shape=[jax.ShapeDtypeStruct((bsz * n, w), BF16),
                   jax.ShapeDtypeStruct((bsz * n_ctx, w), BF16)],
        scratch_shapes=[pltpu.VMEM((n, HEAD_W), F32), pltpu.VMEM((n_ctx, HEAD_W), F32),
                        pltpu.VMEM((n, HEAD_W), BF16), pltpu.VMEM((n, HEAD_W), BF16),
                        pltpu.VMEM((n_ctx, HEAD_W), BF16), pltpu.VMEM((n_ctx, HEAD_W), BF16),
                        pltpu.VMEM((nc, HEAD_W, HEAD_W), F32), pltpu.VMEM((nc, HEAD_W, HEAD_W), F32),
                        pltpu.VMEM((nc, 1, HEAD_W), F32), pltpu.VMEM((nc, 1, HEAD_W), F32)],
        compiler_params=_cparams(("parallel", "parallel")),
        name="hgrn2",
    )(*([proj_l] * 5 + [proj_c] * 5), lb_f.reshape(1, w), lb_b.reshape(1, w), gain.reshape(1, HEAD_W))


def _half_mean_matrix():
    r = lax.broadcasted_iota(I32, (LANES, LANES), 0) // B_DH
    c = lax.broadcasted_iota(I32, (LANES, LANES), 1) // B_DH
    return (r == c).astype(BF16)


def _qk_prep(t, gain, bd, cos, sin):
    sq_hi, sq_lo = _split_bf16(t * t)
    ms = (_dot(sq_hi, bd) + _dot(sq_lo, bd)) * (1.0 / B_DH)
    y = t * lax.rsqrt(ms + EPS) * gain
    if cos is None:
        return y
    lane = lax.broadcasted_iota(I32, y.shape, 1)
    partner = jnp.where(lane % 2 == 0, pltpu.roll(y, LANES - 1, 1), pltpu.roll(y, 1, 1))
    return y * cos + partner * sin


def _attn_kernel(q_l, q_c, k_l, k_c, v_l, v_c, cosq, sinq, cosk, sink, gq_ref, gk_ref, go_ref, lam_ref,
                 o_l, o_c, kp, vp, s_even, s_odd, m_even, m_odd, *, out_scale):
    step = pl.program_id(2)
    n_ctx = k_c.shape[0]
    s_bufs = (s_even, s_odd)
    m_bufs = (m_even, m_odd)
    bd = _half_mean_matrix()
    lane = lax.broadcasted_iota(I32, (1, LANES), 1)
    masks = ((lane < B_DH).astype(F32), (lane >= B_DH).astype(F32))
    lam = lam_ref[:, 0:1]
    q_scale = B_DH ** -0.5 * LOG2_E

    def masked_q(q):
        q = q * q_scale
        return [(q * msk).astype(BF16) for msk in masks]

    def epilogue(r):
        o = (r[0][0:HEAD_W, :] * (1.0 / r[0][HEAD_W:HEAD_W + 1, :])
             - r[1][0:HEAD_W, :] * (lam / r[1][HEAD_W:HEAD_W + 1, :]))
        gain = jnp.concatenate([go_ref[...]] * (o.shape[1] // LANES), axis=1)
        y = o * lax.rsqrt(jnp.mean(o * o, axis=0, keepdims=True) + EPS) * gain
        return (y * out_scale).T.astype(BF16)

    kc = ATTN_KEY_CHUNK
    n_chunks = kp.shape[0] // kc
    last_step = pl.num_programs(2) - 1

    def pipeline_step(old, new, do_finish, do_scores):
        if do_scores:
            qm = masked_q(_qk_prep(q_l[...], gq_ref[...], bd, cosq[...], sinq[...]))
        if do_finish:
            mx_old = [m_bufs[old][i, 0:1, :] for i in range(2)]
        acc = [None, None]
        mx_new = [None, None]
        for c in range(n_chunks + ATTN_SKEW):
            if do_finish and c < n_chunks:
                ks = slice(c * kc, (c + 1) * kc)
                for i in range(2):
                    e = jnp.exp2(s_bufs[old][i, ks, :] - mx_old[i]).astype(BF16)
                    part = _dot(vp[:, ks], e)
                    acc[i] = part if acc[i] is None else acc[i] + part
            if do_finish and c == n_chunks - 1:
                o_l[...] = epilogue(acc)
            if do_scores and c >= ATTN_SKEW:
                ks = slice((c - ATTN_SKEW) * kc, (c - ATTN_SKEW + 1) * kc)
                for i in range(2):
                    t = _dot_nt(kp[ks, :], qm[i])
                    s_bufs[new][i, ks, :] = t
                    tm = jnp.max(t, axis=0, keepdims=True)
                    mx_new[i] = tm if mx_new[i] is None else jnp.maximum(mx_new[i], tm)
        if do_scores:
            for i in range(2):
                m_bufs[new][i] = jnp.broadcast_to(mx_new[i], m_bufs[new].shape[1:])

    @pl.when(step == 0)
    def _():
        kp[0:n_ctx, :] = _qk_prep(k_c[...], gk_ref[...], bd, None, None).astype(BF16)
        kp[n_ctx:, :] = _qk_prep(k_l[...], gk_ref[...], bd, cosk[...], sink[...]).astype(BF16)
        vp[0:HEAD_W, 0:n_ctx] = v_c[...].T.astype(BF16)
        vp[0:HEAD_W, n_ctx:] = v_l[...].T.astype(BF16)
        pad_rows = vp.shape[0] - HEAD_W
        ones_row = (lax.broadcasted_iota(I32, (pad_rows, 1), 0) == 0).astype(BF16)
        vp[HEAD_W:, :] = jnp.broadcast_to(ones_row, (pad_rows, vp.shape[1]))
        qm = masked_q(_qk_prep(q_c[...], gq_ref[...], bd, None, None))
        s_ctx = [_dot_nt(kp[0:n_ctx, :], t) for t in qm]
        e_ctx = [jnp.exp2(t - jnp.max(t, axis=0, keepdims=True)).astype(BF16) for t in s_ctx]
        o_c[...] = epilogue([_dot(vp[:, 0:n_ctx], t) for t in e_ctx])
        pipeline_step(None, 0, False, True)

    for parity in (0, 1):
        @pl.when((step > 0) & (step < last_step) & (step % 2 == parity))
        def _(old=1 - parity, new=parity):
            pipeline_step(old, new, True, True)

        @pl.when((step == last_step) & (step % 2 == parity))
        def _(old=1 - parity):
            pipeline_step(old, None, True, False)


def _rope_tables(n):
    n_rows = n // GRID_W
    rowp = jnp.repeat(jnp.arange(n_rows), GRID_W).astype(F32)
    colp = jnp.tile(jnp.arange(GRID_W), n_rows).astype(F32)
    pairs = B_DH // 4
    inv = ROPE_BASE ** (-jnp.arange(pairs, dtype=F32) / pairs)
    ang = jnp.concatenate([rowp[:, None] * inv, colp[:, None] * inv], axis=-1)
    cos = jnp.repeat(jnp.cos(ang), 2, axis=-1)
    sin = jnp.repeat(jnp.sin(ang), 2, axis=-1) * jnp.tile(jnp.array([-1.0, 1.0], F32), B_DH // 2)
    return jnp.tile(cos, (1, 2)), jnp.tile(sin, (1, 2))


def _attn(proj_l, proj_c, qk_gain, out_gain, lam, lam_init, bsz, tq=512):
    n = proj_l.shape[0] // bsz
    n_ctx = proj_c.shape[0] // bsz
    nqb = n // tq
    w = B_HEADS * HEAD_W
    cos, sin = _rope_tables(n)
    qcol, kcol, vcol = 5 * A_HEADS, 5 * A_HEADS + B_HEADS, 5 * A_HEADS + 2 * B_HEADS

    def q_blk(t):
        return jnp.clip(t, 0, nqb - 1)

    def o_blk(t):
        return jnp.clip(t - 1, 0, nqb - 1)

    def lat_q(b, h, t):
        return (b * nqb + q_blk(t), qcol + h)

    in_specs = [pl.BlockSpec((tq, HEAD_W), lat_q),
                pl.BlockSpec((n_ctx, HEAD_W), lambda b, h, qb: (b, qcol + h)),
                pl.BlockSpec((n, HEAD_W), lambda b, h, qb: (b, kcol + h)),
                pl.BlockSpec((n_ctx, HEAD_W), lambda b, h, qb: (b, kcol + h)),
                pl.BlockSpec((n, HEAD_W), lambda b, h, qb: (b, vcol + h)),
                pl.BlockSpec((n_ctx, HEAD_W), lambda b, h, qb: (b, vcol + h)),
                pl.BlockSpec((tq, LANES), lambda b, h, t: (q_blk(t), 0)),
                pl.BlockSpec((tq, LANES), lambda b, h, t: (q_blk(t), 0)),
                pl.BlockSpec((n, LANES), lambda b, h, qb: (0, 0)),
                pl.BlockSpec((n, LANES), lambda b, h, qb: (0, 0)),
                pl.BlockSpec((1, LANES), lambda b, h, qb: (0, 0)),
                pl.BlockSpec((1, LANES), lambda b, h, qb: (0, 0)),
                pl.BlockSpec((HEAD_W, LANES), lambda b, h, qb: (0, 0)),
                pl.BlockSpec((1, LANES), lambda b, h, qb: (0, 0))]
    return pl.pallas_call(
        functools.partial(_attn_kernel, out_scale=1.0 - lam_init),
        grid=(bsz, B_HEADS, nqb + 1),
        in_specs=in_specs,
        out_specs=[pl.BlockSpec((tq, HEAD_W), lambda b, h, t: (b * nqb + o_blk(t), h)),
                   pl.BlockSpec((n_ctx, HEAD_W), lambda b, h, qb: (b, h))],
        out_shape=[jax.ShapeDtypeStruct((bsz * n, w), BF16),
                   jax.ShapeDtypeStruct((bsz * n_ctx, w), BF16)],
        scratch_shapes=[pltpu.VMEM((n + n_ctx, HEAD_W), BF16),
                        pltpu.VMEM((HEAD_W + BF16_SUBLANES, n + n_ctx), BF16),
                        pltpu.VMEM((2, n + n_ctx, tq), F32), pltpu.VMEM((2, n + n_ctx, tq), F32),
                        pltpu.VMEM((2, SUBLANES, tq), F32), pltpu.VMEM((2, SUBLANES, tq), F32)],
        compiler_params=_cparams(("arbitrary", "arbitrary", "arbitrary")),
        name="diff_attn",
    )(proj_l, proj_c, proj_l, proj_c, proj_l, proj_c, cos, sin, cos, sin,
      jnp.tile(qk_gain[0], 2).reshape(1, LANES), jnp.tile(qk_gain[1], 2).reshape(1, LANES),
      jnp.broadcast_to(out_gain[:, None], (HEAD_W, LANES)), jnp.full((1, LANES), lam, F32))


def _scan_steps(a, b, reverse, axis):
    n = a.shape[axis]
    pos = lax.broadcasted_iota(I32, a.shape, axis)
    s = 1
    while s < n:
        keep = (pos < n - s) if reverse else (pos >= s)
        shift = n - s if reverse else s
        a_sh = jnp.where(keep, pltpu.roll(a, shift, axis), 1.0)
        b_sh = jnp.where(keep, pltpu.roll(b, shift, axis), 0.0)
        b = a * b_sh + b
        a = a * a_sh
        s *= 2
    return a, b


def _rglru_kernel(y_l, u_l, u_c, cw_ref, cb_ref, gw_ref, gb_ref, lam_ref, o_ref,
                  upad, a_f, b_f, a_b, b_b, h_f, p_f, h_b, p_b, *, seg_len, pitch):
    n = u_l.shape[0]
    n_ctx = u_c.shape[0]
    nseg = SUBLANES
    cw = cw_ref[...]
    cb = cb_ref[...]
    pad = SUBLANES

    def pieces(start, rows):
        out = []
        for r in range(nseg):
            lo, hi = max(start, r * seg_len), min(start + rows, (r + 1) * seg_len)
            if lo < hi:
                out.append((r, lo - r * seg_len, lo - start, hi - lo))
        return out

    def put(ref, start, val):
        for r, off, src, ln in pieces(start, val.shape[0]):
            ref[r * pitch + off:r * pitch + off + ln, :] = val[src:src + ln, :]

    def conv(u_ref, rows):
        upad[0:pad, :] = jnp.zeros((pad, LANES), F32)
        upad[pad:pad + rows, :] = u_ref[...]
        upad[pad + rows:pad + rows + pad, :] = jnp.zeros((pad, LANES), F32)
        acc = cb + jnp.zeros((rows, LANES), F32)
        for j in range(CONV_W):
            off = pad + j - CONV_W // 2
            acc = acc + cw[j:j + 1, :] * upad[off:off + rows, :]
        return acc

    def gates(uc, start_f, start_b):
        ub = uc.astype(BF16)
        for d, (a_ref, b_ref, start) in enumerate(((a_f, b_f, start_f), (a_b, b_b, start_b))):
            lam = lam_ref[d:d + 1, :]
            neg_sp = -(jnp.maximum(-lam, 0.0) + jnp.log(1.0 + jnp.exp(-jnp.abs(lam))))
            r = _sigmoid(_dot(ub, gw_ref[d, 0].astype(BF16)) + gb_ref[d, 0:1, :])
            i = _sigmoid(_dot(ub, gw_ref[d, 1].astype(BF16)) + gb_ref[d, 1:2, :])
            a = jnp.exp((RG_C * neg_sp) * r)
            put(a_ref, start, a)
            x = (1.0 - a) * (1.0 + a)
            put(b_ref, start, (x * lax.rsqrt(jnp.maximum(x, TINY))) * (i * uc))

    gates(conv(u_c, n_ctx), 0, n)
    gates(conv(u_l, n), n_ctx, 0)

    def step(i, carry):
        hf, pf, hb, pb = carry
        rows_f = pl.ds(i, nseg, stride=pitch)
        rows_b = pl.ds(seg_len - 1 - i, nseg, stride=pitch)
        af, ab = a_f[rows_f, :], a_b[rows_b, :]
        hf = af * hf + b_f[rows_f, :]
        hb = ab * hb + b_b[rows_b, :]
        pf = pf * af
        pb = pb * ab
        h_f[rows_f, :] = hf
        p_f[rows_f, :] = pf
        h_b[rows_b, :] = hb
        p_b[rows_b, :] = pb
        return hf, pf, hb, pb

    zero = jnp.zeros((nseg, LANES), F32)
    one = jnp.ones((nseg, LANES), F32)
    hf, pf, hb, pb = lax.fori_loop(0, seg_len, step, (zero, one, zero, one), unroll=8)

    seg = lax.broadcasted_iota(I32, (nseg, LANES), 0)
    _, ef = _scan_steps(pf, hf, False, 0)
    _, eb = _scan_steps(pb, hb, True, 0)
    carry_f = jnp.where(seg >= 1, pltpu.roll(ef, 1, 0), 0.0)
    carry_b = jnp.where(seg < nseg - 1, pltpu.roll(eb, nseg - 1, 0), 0.0)

    cuts = sorted({0, n} | {k * seg_len for k in range(nseg + 1) if 0 < k * seg_len < n}
                  | {k * seg_len - n_ctx for k in range(nseg + 1) if 0 < k * seg_len - n_ctx < n})
    for i0, i1 in zip(cuts[:-1], cuts[1:]):
        ln = i1 - i0
        (rf, of, _, _), = pieces(n_ctx + i0, ln)
        (rb, ob, _, _), = pieces(i0, ln)
        sf = slice(rf * pitch + of, rf * pitch + of + ln)
        sb = slice(rb * pitch + ob, rb * pitch + ob + ln)
        h = (h_f[sf, :] + p_f[sf, :] * carry_f[rf:rf + 1, :]) + (h_b[sb, :] + p_b[sb, :] * carry_b[rb:rb + 1, :])
        y = y_l[i0:i1, :]
        gelu = 0.5 * y * (1.0 + jnp.tanh(GELU_TANH_SCALE * (y + GELU_TANH_CUBIC * y * y * y)))
        o_ref[i0:i1, :] = (h * gelu).astype(o_ref.dtype)


def _rglru(proj_l, u_c, conv_w, conv_b, gate_w, gate_b, lam, bsz):
    n = proj_l.shape[0] // bsz
    n_ctx = u_c.shape[0] // bsz
    w = RG_HEADS * HEAD_W
    seg_len = (n + n_ctx) // SUBLANES
    assert seg_len * SUBLANES == n + n_ctx and seg_len % SUBLANES == 0
    assert n_ctx % SUBLANES == 0 and n_ctx <= seg_len
    pitch = seg_len + SUBLANES
    return pl.pallas_call(
        functools.partial(_rglru_kernel, seg_len=seg_len, pitch=pitch),
        grid=(bsz, RG_HEADS),
        in_specs=[pl.BlockSpec((n, HEAD_W), lambda b, h: (b, h)),
                  pl.BlockSpec((n, HEAD_W), lambda b, h: (b, RG_HEADS + h)),
                  pl.BlockSpec((n_ctx, HEAD_W), lambda b, h: (b, h)),
                  pl.BlockSpec((CONV_W, HEAD_W), lambda b, h: (0, h)),
                  pl.BlockSpec((1, HEAD_W), lambda b, h: (0, h)),
                  pl.BlockSpec((2, 2, None, HEAD_W, HEAD_W), lambda b, h: (0, 0, h, 0, 0)),
                  pl.BlockSpec((2, 2, HEAD_W), lambda b, h: (0, 0, h)),
                  pl.BlockSpec((2, HEAD_W), lambda b, h: (0, h))],
        out_specs=pl.BlockSpec((n, HEAD_W), lambda b, h: (b, h)),
        out_shape=jax.ShapeDtypeStruct((bsz * n, w), BF16),
        scratch_shapes=[pltpu.VMEM((n + 2 * SUBLANES, LANES), F32)]
                       + [pltpu.VMEM((SUBLANES * pitch, LANES), F32)] * 8,
        compiler_params=_cparams(("parallel", "parallel")),
        name="rglru",
    )(proj_l, proj_l, u_c, conv_w, conv_b.reshape(1, w), gate_w, gate_b, lam)


def _out_proj_kernel(*refs, n_mix, n_groups, epg, row_chunk):
    mix_refs = refs[:n_mix]
    (w_ref, x_ref, gate_ref, g_ref, sh_ref, sc_ref, wr_ref, rb_ref,
     xo_ref, f_ref, gid_ref) = refs[n_mix:]
    d = x_ref.shape[1]
    n_pairs = epg * (epg - 1) // 2
    for r0 in range(0, x_ref.shape[0], row_chunk):
        rows = slice(r0, r0 + row_chunk)
        k0 = 0
        mix = None
        for m_ref in mix_refs:
            kk = m_ref.shape[1]
            part = _dot(m_ref[rows, :], w_ref[k0:k0 + kk, :])
            mix = part if mix is None else mix + part
            k0 += kk
        x = x_ref[rows, :] + gate_ref[0] * mix
        xo_ref[rows, :] = x
        y = x * lax.rsqrt(jnp.mean(x * x, axis=-1, keepdims=True) + EPS) * g_ref[...]
        f = y * (1.0 + sc_ref[0]) + sh_ref[0]
        f_ref[rows, 0:d] = f
        f_hi, f_lo = _split_bf16(f)

        hh = _dot(f_hi, wr_ref[...])
        lg = hh[:, 0:ROUTE_W] + hh[:, ROUTE_W:] + _dot(f_lo, wr_ref[:, 0:ROUTE_W]) + rb_ref[...]
        lane = lax.broadcasted_iota(I32, lg.shape, 1).astype(F32)
        neg = -jnp.inf
        big = float(ROUTE_W)
        gl = jnp.where(lane < n_groups, lg, neg)
        gmax = jnp.max(gl, axis=-1, keepdims=True)
        gidx = jnp.min(jnp.where(gl == gmax, lane, big), axis=-1, keepdims=True)
        gw = 1.0 / jnp.sum(jnp.exp(gl - gmax), axis=-1, keepdims=True)
        base = n_groups + gidx * epg
        el = jnp.where((lane >= base) & (lane < base + epg), lg, neg)
        v1 = jnp.max(el, axis=-1, keepdims=True)
        i1 = jnp.min(jnp.where(el == v1, lane, big), axis=-1, keepdims=True)
        el2 = jnp.where(lane == i1, neg, el)
        v2 = jnp.max(el2, axis=-1, keepdims=True)
        i2 = jnp.min(jnp.where(el2 == v2, lane, big), axis=-1, keepdims=True)
        t = jnp.exp(v2 - v1)
        w1 = gw / (1.0 + t)
        w2 = w1 * t
        s1 = i1 - base
        s2 = i2 - base
        lo = jnp.minimum(s1, s2)
        hi = jnp.maximum(s1, s2)
        pair = lo * (2 * epg - 1 - lo) * 0.5 + (hi - lo - 1.0)
        w_lo = jnp.where(s1 < s2, w1, w2)
        w_hi = jnp.where(s1 < s2, w2, w1)
        f_ref[rows, d:] = jnp.where(lane == 0.0, w_lo, 0.0) + jnp.where(lane == 1.0, w_hi, 0.0)
        cls = jnp.broadcast_to(gidx * n_pairs + pair, (row_chunk, LANES)).T
        gid_ref[:, rows] = cls[0:SUBLANES, :].astype(I32)


def _out_proj(mixes, w_out, x2, gain, mods, w_router, b_router, rows_per_mod, mod_base, tm=512):
    r, d = x2.shape
    k = w_out.shape[0]
    tpm = rows_per_mod // tm
    wr = jnp.concatenate(_split_bf16(w_router), axis=1)

    def mrow(i):
        return (mod_base + i // tpm) * N_MOD

    const2 = lambda i: (0, 0)
    in_specs = ([pl.BlockSpec((tm, m.shape[1]), lambda i: (i, 0)) for m in mixes]
                + [pl.BlockSpec((k, d), const2),
                   pl.BlockSpec((tm, d), lambda i: (i, 0)),
                   pl.BlockSpec((1, 1, d), lambda i: (mrow(i) + 2, 0, 0)),
                   pl.BlockSpec((1, d), const2),
                   pl.BlockSpec((1, 1, d), lambda i: (mrow(i) + 3, 0, 0)),
                   pl.BlockSpec((1, 1, d), lambda i: (mrow(i) + 4, 0, 0)),
                   pl.BlockSpec((d, 2 * ROUTE_W), const2),
                   pl.BlockSpec((1, ROUTE_W), const2)])
    x_new, fext, cls = pl.pallas_call(
        functools.partial(_out_proj_kernel, n_mix=len(mixes), n_groups=N_GROUPS, epg=EXPERTS_PER_GROUP,
                          row_chunk=tm),
        grid=(r // tm,),
        in_specs=in_specs,
        out_specs=[pl.BlockSpec((tm, d), lambda i: (i, 0)),
                   pl.BlockSpec((tm, d + ROUTE_W), lambda i: (i, 0)),
                   pl.BlockSpec((SUBLANES, tm), lambda i: (i, 0))],
        out_shape=[jax.ShapeDtypeStruct((r, d), F32),
                   jax.ShapeDtypeStruct((r, d + ROUTE_W), F32),
                   jax.ShapeDtypeStruct((r // tm * SUBLANES, tm), I32)],
        compiler_params=_cparams(("parallel",)),
        name="out_proj",
    )(*mixes, w_out, x2, mods, gain.reshape(1, d), mods, mods, wr, b_router)
    return x_new, fext, cls.reshape(r // tm, SUBLANES, tm)[:, 0, :].reshape(r)


def _row_copy(src, s, dst, t, sem):
    return pltpu.make_async_copy(src.at[pl.ds(s, 1), :], dst.at[pl.ds(t, 1), :], sem)


def _dispatch_kernel(pos_ref, src_ref, dst_in, dst, sem, *, tm):
    del dst_in

    for j in range(tm):
        _row_copy(src_ref, j, dst, pos_ref[0, 0, j], sem).start()

    def wbody(j, carry):
        _row_copy(src_ref, 0, dst, 0, sem).wait()
        return carry
    lax.fori_loop(0, tm, wbody, 0, unroll=True)


def _dispatch(src, pos, sorted_in, tm=512):
    r, w = src.shape
    return pl.pallas_call(
        functools.partial(_dispatch_kernel, tm=tm),
        grid=(r // tm,),
        in_specs=[pl.BlockSpec((1, 1, tm), lambda i: (i, 0, 0), memory_space=pltpu.SMEM),
                  pl.BlockSpec((tm, w), lambda i: (i, 0)),
                  pl.BlockSpec(memory_space=pl.ANY)],
        out_specs=pl.BlockSpec(memory_space=pl.ANY),
        out_shape=jax.ShapeDtypeStruct(sorted_in.shape, sorted_in.dtype),
        scratch_shapes=[pltpu.SemaphoreType.DMA(())],
        input_output_aliases={2: 0},
        compiler_params=_cparams(("arbitrary",)),
        name="moe_dispatch",
    )(pos.reshape(r // tm, 1, tm), src, sorted_in)


def _pair_slot(t, s):
    return jnp.bitwise_xor(s, jnp.bitwise_and(t, 1))


def _moe_kernel(te_ref, tv_ref, x_ref, w1_ref, w3_ref, w2_ref, o_ref):
    t = pl.program_id(0)
    s = pl.program_id(1)
    d = o_ref.shape[1]
    valid = tv_ref[t] > 0
    slot = _pair_slot(t, s)

    @pl.when(valid)
    def _():
        x = x_ref[:, 0:d].astype(BF16)
        h1 = _dot(x, w1_ref[...].astype(BF16))
        h3 = _dot(x, w3_ref[...].astype(BF16))
        cw = x_ref[:, d:]
        lane = lax.broadcasted_iota(I32, cw.shape, 1)
        cws = jnp.sum(jnp.where(lane == slot, cw, 0.0), axis=-1, keepdims=True)
        y = _dot((h1 * _sigmoid(h1) * h3 * cws).astype(BF16), w2_ref[...].astype(BF16))

        @pl.when(s == 0)
        def _():
            o_ref[...] = y

        @pl.when(s > 0)
        def _():
            o_ref[...] += y

    @pl.when(jnp.logical_not(valid) & (s == 0))
    def _():
        o_ref[...] = jnp.zeros_like(o_ref)


def _moe_ffn(tile_expert, tile_valid, f_sorted, w1, w3, w2, layer, tm):
    p, dw = f_sorted.shape
    d = dw - ROUTE_W
    fe = w1.shape[3]

    def wmap(t, s, te, tv):
        return (layer, te[2 * t + _pair_slot(t, s)], 0, 0)

    grid_spec = pltpu.PrefetchScalarGridSpec(
        num_scalar_prefetch=2,
        grid=(p // tm, 2),
        in_specs=[pl.BlockSpec((tm, dw), lambda t, s, te, tv: (jnp.minimum(t, tv[tv.shape[0] - 1]), 0)),
                  pl.BlockSpec((None, None, d, fe), wmap),
                  pl.BlockSpec((None, None, d, fe), wmap),
                  pl.BlockSpec((None, None, fe, d), wmap)],
        out_specs=pl.BlockSpec((tm, d), lambda t, s, te, tv: (t, 0)),
    )
    return pl.pallas_call(
        _moe_kernel,
        grid_spec=grid_spec,
        out_shape=jax.ShapeDtypeStruct((p, d), F32),
        compiler_params=_cparams(("arbitrary", "arbitrary")),
        name="moe_ffn",
    )(tile_expert, tile_valid, f_sorted, w1, w3, w2)


def _combine_kernel(pos_cur, pos_nxt, y_hbm, x_ref, gate_ref, o_ref, ybuf, sem, *, tm):
    i = pl.program_id(0)
    n_steps = pl.num_programs(0)

    def issue(pos_ref, slot):
        for j in range(tm):
            _row_copy(y_hbm, pos_ref[0, 0, j], ybuf.at[slot], j, sem.at[slot]).start()

    @pl.when(i == 0)
    def _():
        issue(pos_cur, 0)

    @pl.when(i + 1 < n_steps)
    def _():
        issue(pos_nxt, (i + 1) % 2)

    slot = i % 2

    def wbody(j, carry):
        _row_copy(y_hbm, 0, ybuf.at[slot], 0, sem.at[slot]).wait()
        return carry
    lax.fori_loop(0, tm, wbody, 0, unroll=True)
    o_ref[...] = x_ref[...] + gate_ref[0] * ybuf[slot]


def _combine(y_sorted, pos, x2, mods, rows_per_mod, mod_base, tm=256):
    r, d = x2.shape
    n_steps = r // tm
    tpm = rows_per_mod // tm
    pos3 = pos.reshape(n_steps, 1, tm)
    return pl.pallas_call(
        functools.partial(_combine_kernel, tm=tm),
        grid=(n_steps,),
        in_specs=[pl.BlockSpec((1, 1, tm), lambda i: (i, 0, 0), memory_space=pltpu.SMEM),
                  pl.BlockSpec((1, 1, tm), lambda i: (jnp.minimum(i + 1, n_steps - 1), 0, 0),
                               memory_space=pltpu.SMEM),
                  pl.BlockSpec(memory_space=pl.ANY),
                  pl.BlockSpec((tm, d), lambda i: (i, 0)),
                  pl.BlockSpec((1, 1, d), lambda i: ((mod_base + i // tpm) * N_MOD + 5, 0, 0))],
        out_specs=pl.BlockSpec((tm, d), lambda i: (i, 0)),
        out_shape=jax.ShapeDtypeStruct((r, d), F32),
        scratch_shapes=[pltpu.VMEM((2, tm, d), F32), pltpu.SemaphoreType.DMA((2,))],
        compiler_params=_cparams(("arbitrary",)),
        name="moe_combine",
    )(pos3, pos3, y_sorted, x2, mods)


def _sorted_positions(gids, tm, p_rows):
    g = jnp.concatenate(gids)
    r = g.shape[0]
    epg = EXPERTS_PER_GROUP
    n_pairs = epg * (epg - 1) // 2
    n_cls = N_GROUPS * n_pairs
    onehot = (g[:, None] == jnp.arange(n_cls, dtype=I32)[None, :]).astype(I32)
    counts = jnp.sum(onehot, axis=0)
    rank = jnp.sum((jnp.cumsum(onehot, axis=0) - 1) * onehot, axis=1)
    padded = ((counts + tm - 1) // tm) * tm
    ends = jnp.cumsum(padded)
    pos = jnp.sum(onehot * (ends - padded)[None, :], axis=1) + rank
    p = r + n_cls * tm if p_rows is None else p_rows
    assert p >= r + n_cls * tm and p % tm == 0
    tile_start = jnp.arange(p // tm, dtype=I32) * tm
    tile_cls = jnp.minimum(jnp.sum((tile_start[:, None] >= ends[None, :]).astype(I32), axis=1), n_cls - 1)
    tile_ok = tile_start < ends[-1]
    tile_valid = jnp.concatenate([tile_ok.astype(I32), (ends[-1:] // tm - 1).astype(I32)])
    pairs = [(i, j) for i in range(epg) for j in range(i + 1, epg)]
    pair_lo = jnp.array([a for a, _ in pairs], I32)
    pair_hi = jnp.array([b for _, b in pairs], I32)
    grp = tile_cls // n_pairs
    tile_expert = jnp.stack([grp * epg + pair_lo[tile_cls % n_pairs],
                             grp * epg + pair_hi[tile_cls % n_pairs]], axis=1)
    tile_expert = jnp.where(tile_ok[:, None], tile_expert, 0).reshape(-1)
    return pos.astype(I32), tile_expert.astype(I32), tile_valid, p


def _moe(fexts, gids, w1, w3, w2, layer, f_sorted=None, tm=512):
    pos, tile_expert, tile_valid, p = _sorted_positions(gids, tm, None if f_sorted is None else f_sorted.shape[0])
    sizes = [f.shape[0] for f in fexts]
    poss, off = [], 0
    for s in sizes:
        poss.append(lax.slice(pos, (off,), (off + s,)))
        off += s
    if f_sorted is None:
        f_sorted = jnp.zeros((p, fexts[0].shape[1]), F32)
    for fext, ps in zip(fexts, poss):
        f_sorted = _dispatch(fext, ps, f_sorted)
    y_sorted = _moe_ffn(tile_expert, tile_valid, f_sorted, w1, w3, w2, layer, tm)
    return y_sorted, poss, f_sorted


def kernel(x, c, ctx, c_ctx, ada_w, ada_b, norm_mix, norm_ffn, even_w_in, even_w_out, hgrn_lb_logits, hgrn_out_norm, diff_qk_norm, diff_lambda, diff_out_norm, odd_w_in, odd_conv_w, odd_conv_b, rg_gate_w, rg_gate_b, rg_lambda, odd_w_out, moe_w_grp, moe_b_grp, moe_w_exp, moe_b_exp, moe_w1, moe_w3, moe_w2):
    bsz, n, d = x.shape
    n_ctx = ctx.shape[1]
    depth = ada_w.shape[0]
    assert depth == 2

    cvec = jnp.zeros((MOD_ROWS, d), F32).at[:bsz].set(c).at[bsz].set(c_ctx)
    mods_all = _ada_all(cvec, ada_w, ada_b).reshape(depth, MOD_ROWS * N_MOD, 1, d)
    x_l = x.reshape(bsz * n, d)
    x_c = ctx.reshape(bsz * n_ctx, d)
    lb_all = jnp.cumsum(jax.nn.softmax(hgrn_lb_logits.astype(F32), axis=1), axis=1)

    def router(l):
        w = jnp.concatenate([moe_w_grp[l], moe_w_exp[l]], axis=1)
        b = jnp.concatenate([moe_b_grp[l], moe_b_exp[l]])
        padw = ROUTE_W - w.shape[1]
        return jnp.pad(w, ((0, 0), (0, padw))), jnp.pad(b, (0, padw)).reshape(1, ROUTE_W)

    l = 0
    mods = mods_all[l]
    lam_init = 0.8 - 0.6 * math.exp(-0.3 * l)
    lv = diff_lambda[0].astype(F32)
    lam = jnp.exp(jnp.sum(lv[0] * lv[1])) - jnp.exp(jnp.sum(lv[2] * lv[3])) + lam_init
    w_in = even_w_in
    proj_l = _norm_mod_mm(x_l, norm_mix[l], mods, w_in, n, 0, tm=1024)
    proj_c = _norm_mod_mm(x_c, norm_mix[l], mods, w_in, bsz * n_ctx, bsz, tm=1024)
    a_l, a_c = _hgrn(proj_l, proj_c, lb_all[0, 0], lb_all[1, 0], hgrn_out_norm[0], bsz)
    b_l, b_c = _attn(proj_l, proj_c, diff_qk_norm[0], diff_out_norm[0], lam, lam_init, bsz)
    w_out = even_w_out[0].astype(BF16)
    wr, br = router(l)
    x_l, f_l, g_l = _out_proj([a_l, b_l], w_out, x_l, norm_ffn[l], mods, wr, br, n, 0)
    x_c, f_c, g_c = _out_proj([a_c, b_c], w_out, x_c, norm_ffn[l], mods, wr, br, bsz * n_ctx, bsz)
    y_sorted, (pos_l, pos_c), f_sorted = _moe([f_l, f_c], [g_l, g_c], moe_w1, moe_w3, moe_w2, l)
    x_l = _combine(y_sorted, pos_l, x_l, mods, n, 0)
    x_c = _combine(y_sorted, pos_c, x_c, mods, bsz * n_ctx, bsz)

    l = 1
    mods = mods_all[l]
    w_in = odd_w_in
    proj_l = _norm_mod_mm(x_l, norm_mix[l], mods, w_in, n, 0, tm=1024)
    rg_w = w_in.shape[2] // 2
    u_c = _norm_mod_mm(x_c, norm_mix[l], mods, w_in, bsz * n_ctx, bsz, tm=1024, cols=(rg_w, rg_w))
    gated = _rglru(proj_l, u_c,odd_conv_w[0], odd_conv_b[0], rg_gate_w[0], rg_gate_b[0], rg_lambda[0], bsz)
    wr, br = router(l)
    x_l, f_l, g_l = _out_proj([gated], odd_w_out[0].astype(BF16), x_l, norm_ffn[l], mods, wr, br, n, 0)
    y_sorted, (pos_l,), _ = _moe([f_l], [g_l], moe_w1, moe_w3, moe_w2, l, f_sorted)
    return _combine(y_sorted, pos_l, x_l, mods, n, 0).reshape(bsz, n, d)
```
